```python
import math
import jax, jax.numpy as jnp
from jax import lax
import numpy as np

D_MODEL = 2048
BATCH = 2
SEQ = 8192
DEPTH = 4

N_MIXERS = 2
MEM_LEN = 256
MIX_W = D_MODEL
MEM_HEADS = 4
MEM_HEAD_DIM = 128
MEM_W = MEM_HEADS * MEM_HEAD_DIM
TOK_W = MIX_W - MEM_W
HEAD_DIM = 64
N_Q_HEADS = TOK_W // HEAD_DIM
N_KV_HEADS = 4
GQA_GROUP = N_Q_HEADS // N_KV_HEADS
KV_W = N_KV_HEADS * HEAD_DIM
WINDOW = 128
ATTN_BLOCK = WINDOW
A_IN_W = TOK_W + 2 * KV_W + MEM_W
CHUNK = 128
GM_GROUP_DIM = 128
GM_GROUPS = TOK_W // GM_GROUP_DIM
B_IN_W = 2 * TOK_W + MEM_W
N_GROUPS = 8
EXPERTS_PER_GROUP = 8
N_EXPERTS = N_GROUPS * EXPERTS_PER_GROUP
TOP_K = 2
D_EXPERT = 512
ROW_BLOCK = 128
ALPHA = (2 * DEPTH) ** 0.25
BETA = (8 * DEPTH) ** -0.25
LN_EPS = 1e-5
NEG_INF = -1e30

kernel_name = "hybrid_swa_sink_gmlp_memxattn_hmoe_deepnorm"


def alibi_slopes(n):
    def pow2(m):
        start = 2.0 ** (-8.0 / m)
        return [start ** (i + 1) for i in range(m)]
    if math.log2(n).is_integer():
        s = pow2(n)
    else:
        c = 2 ** math.floor(math.log2(n))
        s = pow2(c) + pow2(2 * c)[0::2][: n - c]
    return np.asarray(s, dtype=np.float32)


def layer_norm(x, g, b):
    xf = x.astype(jnp.float32)
    mu = jnp.mean(xf, axis=-1, keepdims=True)
    xc = xf - mu
    var = jnp.mean(xc * xc, axis=-1, keepdims=True)
    y = xc * lax.rsqrt(var + LN_EPS) * g.astype(jnp.float32) + b.astype(jnp.float32)
    return y.astype(x.dtype)


def sliding_window_sink_attention(q, k, v, sinks, slopes):
    B, S = q.shape[0], q.shape[1]
    nb = S // ATTN_BLOCK
    qb = q.reshape(B, nb, ATTN_BLOCK, N_KV_HEADS, GQA_GROUP, HEAD_DIM)
    kb = k.reshape(B, nb, ATTN_BLOCK, N_KV_HEADS, HEAD_DIM)
    vb = v.reshape(B, nb, ATTN_BLOCK, N_KV_HEADS, HEAD_DIM)
    pad = ((0, 0), (1, 0), (0, 0), (0, 0), (0, 0))
    kk = jnp.concatenate([jnp.pad(kb[:, :-1], pad), kb], axis=2)
    vv = jnp.concatenate([jnp.pad(vb[:, :-1], pad), vb], axis=2)
    scores = jnp.einsum('bnqkgd,bnskd->bnkgqs', qb, kk).astype(jnp.float32) * (HEAD_DIM ** -0.5)
    qi = jnp.arange(ATTN_BLOCK)[:, None]
    kj = jnp.arange(2 * ATTN_BLOCK)[None, :]
    dist = ATTN_BLOCK + qi - kj
    in_band = (dist >= 0) & (dist < WINDOW)
    has_prev = (jnp.arange(nb) > 0)[:, None, None] | (kj >= ATTN_BLOCK)[None]
    valid = in_band[None] & has_prev
    bias = -slopes.reshape(N_KV_HEADS, GQA_GROUP)[:, :, None, None] * dist.astype(jnp.float32)
    logits = jnp.where(valid[None, :, None, None], scores + bias, NEG_INF)
    sink = sinks.astype(jnp.float32).reshape(1, 1, N_KV_HEADS, GQA_GROUP, 1, 1)
    m = jnp.maximum(jnp.max(logits, axis=-1, keepdims=True), sink)
    p = jnp.exp(logits - m)
    probs = p / (jnp.sum(p, axis=-1, keepdims=True) + jnp.exp(sink - m))
    out = jnp.einsum('bnkgqs,bnskd->bnqkgd', probs.astype(v.dtype), vv)
    return out.reshape(B, S, N_Q_HEADS * HEAD_DIM)


def chunked_spatial_gating(z, norm_g, norm_b, w_s, b_s):
    B, S = z.shape[0], z.shape[1]
    z = jax.nn.gelu(z, approximate=False)
    u, v = z[..., :TOK_W], z[..., TOK_W:]
    v = layer_norm(v, norm_g, norm_b)
    v = v.reshape(B, S // CHUNK, CHUNK, GM_GROUPS, GM_GROUP_DIM)
    sv = jnp.einsum('gts,bcsgd->bctgd', jnp.tril(w_s), v) + b_s.T[:, :, None]
    return u * sv.reshape(B, S, TOK_W)


def memory_attention(qm, km, vm):
    B, S = qm.shape[0], qm.shape[1]
    s = jnp.einsum('bshd,bmhd->bhsm', qm, km).astype(jnp.float32) * (MEM_HEAD_DIM ** -0.5)
    p = jax.nn.softmax(s, axis=-1)
    out = jnp.einsum('bhsm,bmhd->bshd', p.astype(vm.dtype), vm)
    return out.reshape(B, S, MEM_W)


def hierarchical_moe(x2d, w_rg, b_rg, w_re, b_re, w_gate, w_up, w_down):
    T, D = x2d.shape
    g_logits = (x2d @ w_rg).astype(jnp.float32) + b_rg.astype(jnp.float32)
    g_prob = jax.nn.softmax(g_logits, axis=-1)
    g_sel = jnp.argmax(g_logits, axis=-1)
    p_grp = jnp.take_along_axis(g_prob, g_sel[:, None], axis=1)
    e_logits = ((x2d @ w_re).astype(jnp.float32) + b_re.astype(jnp.float32)).reshape(
        T, N_GROUPS, EXPERTS_PER_GROUP)
    e_logits = jnp.take_along_axis(e_logits, g_sel[:, None, None], axis=1)[:, 0]
    top_p, top_i = lax.top_k(jax.nn.softmax(e_logits, axis=-1), TOP_K)
    gate = p_grp * top_p / jnp.sum(top_p, axis=-1, keepdims=True)
    expert = g_sel[:, None].astype(jnp.int32) * EXPERTS_PER_GROUP + top_i.astype(jnp.int32)

    n_assign = T * TOP_K
    n_rows = -(-(n_assign + N_EXPERTS * (ROW_BLOCK - 1)) // ROW_BLOCK) * ROW_BLOCK
    e_flat = expert.reshape(-1)
    tok_flat = jnp.repeat(jnp.arange(T, dtype=jnp.int32), TOP_K)
    counts = jnp.bincount(e_flat, length=N_EXPERTS)
    padded = (counts + ROW_BLOCK - 1) // ROW_BLOCK * ROW_BLOCK
    start = jnp.cumsum(counts) - counts
    pstart = jnp.cumsum(padded) - padded
    order = jnp.argsort(e_flat)
    e_sorted = e_flat[order]
    dest = pstart[e_sorted] + jnp.arange(n_assign, dtype=jnp.int32) - start[e_sorted]
    row_tok = jnp.full((n_rows,), T, jnp.int32).at[dest].set(tok_flat[order])
    row_gate = jnp.zeros((n_rows,), jnp.float32).at[dest].set(gate.reshape(-1)[order])
    blk_start = jnp.arange(n_rows // ROW_BLOCK, dtype=jnp.int32) * ROW_BLOCK
    blk_expert = jnp.minimum(
        jnp.searchsorted(jnp.cumsum(padded), blk_start, side='right'), N_EXPERTS - 1)
    x_rows = jnp.concatenate([x2d, jnp.zeros((1, D), x2d.dtype)], axis=0)[row_tok]
    x_rows = x_rows.reshape(n_rows // ROW_BLOCK, ROW_BLOCK, D)

    def expert_block(args):
        xb, e = args
        h = jax.nn.silu(xb @ w_gate[e]) * (xb @ w_up[e])
        return h @ w_down[e]

    y_rows = lax.map(expert_block, (x_rows, blk_expert)).reshape(n_rows, D)
    y_rows = y_rows * row_gate[:, None].astype(y_rows.dtype)
    return jax.ops.segment_sum(y_rows, row_tok, num_segments=T + 1)[:T]


def setup_inputs(seed: int = 0) -> dict:
    key = jax.random.key(seed)
    ks = jax.random.split(key, 24)
    n_a = (DEPTH + 1) // 2
    n_b = DEPTH // 2
    f32 = jnp.float32

    def nrm(k, shape, fan_in, scale=1.0):
        return jax.random.normal(k, shape, f32) * (scale * fan_in ** -0.5)

    def near_one(k, shape):
        return 1.0 + 0.02 * jax.random.normal(k, shape, f32)

    def small(k, shape, s=0.02):
        return s * jax.random.normal(k, shape, f32)

    x = jax.random.normal(ks[0], (BATCH, SEQ, D_MODEL), f32)
    mem = jax.random.normal(ks[1], (BATCH, MEM_LEN, D_MODEL), f32)
    a_col = jnp.concatenate([jnp.ones((TOK_W + KV_W,), f32), jnp.full((KV_W,), BETA, f32),
                             jnp.ones((MEM_W,), f32)])
    a_w_in = nrm(ks[2], (n_a, D_MODEL, A_IN_W), D_MODEL) * a_col
    a_sinks = 0.5 * jax.random.normal(ks[3], (n_a, N_Q_HEADS), f32)
    b_w_in = nrm(ks[4], (n_b, D_MODEL, B_IN_W), D_MODEL)
    b_norm_g = near_one(ks[5], (n_b, TOK_W))
    b_norm_b = small(ks[6], (n_b, TOK_W))
    b_w_spatial = nrm(ks[7], (n_b, GM_GROUPS, CHUNK, CHUNK), CHUNK)
    b_b_spatial = near_one(ks[8], (n_b, GM_GROUPS, CHUNK))
    m_col = jnp.concatenate([jnp.ones((MEM_W,), f32), jnp.full((MEM_W,), BETA, f32)])
    w_mem_kv = nrm(ks[9], (DEPTH, D_MODEL, 2 * MEM_W), D_MODEL) * m_col
    w_out = nrm(ks[10], (DEPTH, MIX_W, D_MODEL), MIX_W, BETA)
    ln_g = near_one(ks[11], (DEPTH, 2, D_MODEL))
    ln_b = small(ks[12], (DEPTH, 2, D_MODEL))
    w_router_group = nrm(ks[13], (DEPTH, D_MODEL, N_GROUPS), D_MODEL)
    b_router_group = small(ks[14], (DEPTH, N_GROUPS), 0.01)
    w_router_expert = nrm(ks[15], (DEPTH, D_MODEL, N_EXPERTS), D_MODEL)
    b_router_expert = small(ks[16], (DEPTH, N_EXPERTS), 0.01)
    w_gate = nrm(ks[17], (DEPTH, N_EXPERTS, D_MODEL, D_EXPERT), D_MODEL)
    w_up = nrm(ks[18], (DEPTH, N_EXPERTS, D_MODEL, D_EXPERT), D_MODEL)
    w_down = nrm(ks[19], (DEPTH, N_EXPERTS, D_EXPERT, D_MODEL), D_EXPERT, BETA)
    return {"x": x, "mem": mem, "a_w_in": a_w_in, "a_sinks": a_sinks,
            "b_w_in": b_w_in, "b_norm_g": b_norm_g, "b_norm_b": b_norm_b,
            "b_w_spatial": b_w_spatial, "b_b_spatial": b_b_spatial,
            "w_mem_kv": w_mem_kv, "w_out": w_out, "ln_g": ln_g, "ln_b": ln_b,
            "w_router_group": w_router_group, "b_router_group": b_router_group,
            "w_router_expert": w_router_expert, "b_router_expert": b_router_expert,
            "w_gate": w_gate, "w_up": w_up, "w_down": w_down}


def reference(x, mem, a_w_in, a_sinks, b_w_in, b_norm_g, b_norm_b, b_w_spatial, b_b_spatial,
              w_mem_kv, w_out, ln_g, ln_b, w_router_group, b_router_group,
              w_router_expert, b_router_expert, w_gate, w_up, w_down):
    B, S, D = x.shape
    M = mem.shape[1]
    slopes = jnp.asarray(alibi_slopes(N_Q_HEADS))
    for i in range(DEPTH):
        j = i // N_MIXERS
        if i % N_MIXERS == 0:
            proj = x @ a_w_in[j]
            q = proj[..., :TOK_W].reshape(B, S, N_Q_HEADS, HEAD_DIM)
            k = proj[..., TOK_W:TOK_W + KV_W].reshape(B, S, N_KV_HEADS, HEAD_DIM)
            v = proj[..., TOK_W + KV_W:TOK_W + 2 * KV_W].reshape(B, S, N_KV_HEADS, HEAD_DIM)
            qm = proj[..., TOK_W + 2 * KV_W:]
            mix = sliding_window_sink_attention(q, k, v, a_sinks[j], slopes)
        else:
            proj = x @ b_w_in[j]
            qm = proj[..., 2 * TOK_W:]
            mix = chunked_spatial_gating(proj[..., :2 * TOK_W], b_norm_g[j], b_norm_b[j],
                                         b_w_spatial[j], b_b_spatial[j])
        kv_mem = mem @ w_mem_kv[i]
        mo = memory_attention(qm.reshape(B, S, MEM_HEADS, MEM_HEAD_DIM),
                              kv_mem[..., :MEM_W].reshape(B, M, MEM_HEADS, MEM_HEAD_DIM),
                              kv_mem[..., MEM_W:].reshape(B, M, MEM_HEADS, MEM_HEAD_DIM))
        y = jnp.concatenate([mix, mo], axis=-1) @ w_out[i]
        x = layer_norm(ALPHA * x + y, ln_g[i, 0], ln_b[i, 0])
        y = hierarchical_moe(x.reshape(B * S, D), w_router_group[i], b_router_group[i],
                             w_router_expert[i], b_router_expert[i],
                             w_gate[i], w_up[i], w_down[i]).reshape(B, S, D)
        x = layer_norm(ALPHA * x + y, ln_g[i, 1], ln_b[i, 1])
    return x
```

```python
import functools
import math

import jax
import jax.numpy as jnp
import numpy as np
from jax import lax
from jax.experimental import pallas as pl
from jax.experimental.pallas import tpu as pltpu

MEM_HEADS = 4
MEM_HEAD_DIM = 128
MEM_W = MEM_HEADS * MEM_HEAD_DIM
HEAD_DIM = 64
N_KV_HEADS = 4
KV_W = N_KV_HEADS * HEAD_DIM
WINDOW = 128
CHUNK = 128
GM_GROUP_DIM = 128
N_GROUPS = 8
EXPERTS_PER_GROUP = 8
N_EXPERTS = N_GROUPS * EXPERTS_PER_GROUP
TOP_K = 2
LN_EPS = 1e-5
NEG_INF = -1e30

LANES = 128
VMEM_LIMIT_BYTES = 56 * 1024 * 1024

ROW_BLOCK = 256
BF16 = jnp.bfloat16
F32 = jnp.float32


def _alibi_slopes(n):
    def pow2(m):
        start = 2.0 ** (-8.0 / m)
        return [start ** (i + 1) for i in range(m)]
    if math.log2(n).is_integer():
        s = pow2(n)
    else:
        c = 2 ** math.floor(math.log2(n))
        s = pow2(c) + pow2(2 * c)[0::2][: n - c]
    return [float(v) for v in np.asarray(s, dtype=np.float32)]


def _params(n_axes=1):
    return pltpu.CompilerParams(dimension_semantics=("arbitrary",) * n_axes,
                                vmem_limit_bytes=VMEM_LIMIT_BYTES)


def _resident(shape):
    nd = len(shape)
    return pl.BlockSpec(shape, lambda *_: (0,) * nd, pipeline_mode=pl.Buffered(1))


def _layer_norm(x, g, b):
    mu = jnp.mean(x, axis=-1, keepdims=True)
    xc = x - mu
    var = jnp.mean(xc * xc, axis=-1, keepdims=True)
    return xc * lax.rsqrt(var + LN_EPS) * g + b


def _proj_kernel(x_ref, w_ref, o_ref):
    o_ref[...] = jnp.dot(x_ref[...].astype(BF16), w_ref[...],
                         preferred_element_type=F32).astype(o_ref.dtype)


def _proj(x2d, w_bf, out_dtype, tm):
    m, k = x2d.shape
    n = w_bf.shape[1]
    return pl.pallas_call(
        _proj_kernel,
        grid=(m // tm,),
        in_specs=[pl.BlockSpec((tm, k), lambda i: (i, 0)), _resident((k, n))],
        out_specs=pl.BlockSpec((tm, n), lambda i: (i, 0)),
        out_shape=jax.ShapeDtypeStruct((m, n), out_dtype),
        compiler_params=_params(),
        name="proj",
    )(x2d, w_bf)


def _gelu(x):
    return 0.5 * x * (1.0 + lax.erf(x * (2.0 ** -0.5)))


def _gmlp_proj_kernel(x_ref, w_ref, g_ref, b_ref, o_ref, *, tok_w):
    z = jnp.dot(x_ref[...].astype(BF16), w_ref[...], preferred_element_type=F32)
    u = _gelu(z[:, :tok_w])
    v = _gelu(z[:, tok_w:2 * tok_w])
    v = _layer_norm(v, g_ref[...], b_ref[...])
    o_ref[:, :tok_w] = u.astype(o_ref.dtype)
    o_ref[:, tok_w:2 * tok_w] = v.astype(o_ref.dtype)
    o_ref[:, 2 * tok_w:] = z[:, 2 * tok_w:].astype(o_ref.dtype)


def _gmlp_proj(x2d, w_bf, norm_g, norm_b, tm):
    m, k = x2d.shape
    n = w_bf.shape[1]
    tok_w = norm_g.shape[-1]
    return pl.pallas_call(
        functools.partial(_gmlp_proj_kernel, tok_w=tok_w),
        grid=(m // tm,),
        in_specs=[pl.BlockSpec((tm, k), lambda i: (i, 0)), _resident((k, n)),
                  _resident((1, tok_w)), _resident((1, tok_w))],
        out_specs=pl.BlockSpec((tm, n), lambda i: (i, 0)),
        out_shape=jax.ShapeDtypeStruct((m, n), BF16),
        compiler_params=_params(),
        name="gmlp_proj",
    )(x2d, w_bf, norm_g.reshape(1, tok_w), norm_b.reshape(1, tok_w))


def _memory_attention(qm, kvm):
    outs = []
    for h in range(MEM_HEADS):
        q = qm[:, h * MEM_HEAD_DIM:(h + 1) * MEM_HEAD_DIM]
        k = kvm[:, h * MEM_HEAD_DIM:(h + 1) * MEM_HEAD_DIM]
        v = kvm[:, MEM_W + h * MEM_HEAD_DIM:MEM_W + (h + 1) * MEM_HEAD_DIM]
        s = lax.dot_general(q, k, (((1,), (1,)), ((), ())),
                            preferred_element_type=F32) * (MEM_HEAD_DIM ** -0.5)
        m = jnp.max(s, axis=-1, keepdims=True)
        p = jnp.exp(s - m)
        p = p / jnp.sum(p, axis=-1, keepdims=True)
        outs.append(jnp.dot(p.astype(BF16), v, preferred_element_type=F32))
    return jnp.concatenate(outs, axis=-1)


def _attn_kernel(sinks_ref, q_ref, kv_ref, kvp_ref, qm_ref, kvm_ref, o_ref, *,
                 tq, seq, slopes, tok_w):
    i = pl.program_id(0)
    n_sub = tq // WINDOW
    gqa = len(slopes) // N_KV_HEADS
    qi = lax.broadcasted_iota(jnp.int32, (WINDOW, 2 * WINDOW), 0)
    kj = lax.broadcasted_iota(jnp.int32, (WINDOW, 2 * WINDOW), 1)
    dist = WINDOW + qi - kj
    in_band = (dist >= 0) & (dist < WINDOW)
    dist_f = dist.astype(F32)
    for sb in range(n_sub):
        r0 = sb * WINDOW
        if sb == 0:
            kv_prev = kvp_ref[...]
            first_key = jnp.where(((i * tq) % seq) == 0, WINDOW, 0)
            valid = in_band & (kj >= first_key)
        else:
            kv_prev = kv_ref[r0 - WINDOW:r0, :]
            valid = in_band
        kv_win = jnp.concatenate([kv_prev, kv_ref[r0:r0 + WINDOW, :]], axis=0)
        for kh in range(N_KV_HEADS):
            k = kv_win[:, kh * HEAD_DIM:(kh + 1) * HEAD_DIM]
            v = kv_win[:, KV_W + kh * HEAD_DIM:KV_W + (kh + 1) * HEAD_DIM]
            for g in range(gqa):
                h = kh * gqa + g
                q = q_ref[r0:r0 + WINDOW, h * HEAD_DIM:(h + 1) * HEAD_DIM]
                s = lax.dot_general(q, k, (((1,), (1,)), ((), ())),
                                    preferred_element_type=F32) * (HEAD_DIM ** -0.5)
                logits = jnp.where(valid, s - slopes[h] * dist_f, NEG_INF)
                sink = sinks_ref[h]
                m = jnp.maximum(jnp.max(logits, axis=-1, keepdims=True), sink)
                p = jnp.exp(logits - m)
                probs = p / (jnp.sum(p, axis=-1, keepdims=True) + jnp.exp(sink - m))
                out = jnp.dot(probs.astype(BF16), v, preferred_element_type=F32)
                o_ref[r0:r0 + WINDOW, h * HEAD_DIM:(h + 1) * HEAD_DIM] = out.astype(o_ref.dtype)
    o_ref[:, tok_w:] = _memory_attention(qm_ref[...], kvm_ref[...]).astype(o_ref.dtype)


def _attn_mixer(proj, kvm, sinks, *, seq, mem_len, tq):
    t = proj.shape[0]
    tok_w = proj.shape[1] - 2 * KV_W - MEM_W
    n_heads = tok_w // HEAD_DIM
    assert tok_w % (2 * KV_W) == 0 and tok_w % MEM_W == 0 and seq % tq == 0
    kv_col = tok_w // (2 * KV_W)
    qm_col = (tok_w + 2 * KV_W) // MEM_W
    sub = tq // WINDOW
    kern = functools.partial(_attn_kernel, tq=tq, seq=seq, slopes=_alibi_slopes(n_heads),
                             tok_w=tok_w)
    return pl.pallas_call(
        kern,
        grid=(t // tq,),
        in_specs=[
            pl.BlockSpec(memory_space=pltpu.SMEM),
            pl.BlockSpec((tq, tok_w), lambda i: (i, 0)),
            pl.BlockSpec((tq, 2 * KV_W), lambda i: (i, kv_col)),
            pl.BlockSpec((WINDOW, 2 * KV_W), lambda i: (jnp.maximum(i * sub - 1, 0), kv_col)),
            pl.BlockSpec((tq, MEM_W), lambda i: (i, qm_col)),
            pl.BlockSpec((mem_len, 2 * MEM_W), lambda i: ((i * tq) // seq, 0)),
        ],
        out_specs=pl.BlockSpec((tq, tok_w + MEM_W), lambda i: (i, 0)),
        out_shape=jax.ShapeDtypeStruct((t, tok_w + MEM_W), BF16),
        compiler_params=_params(),
        name="attn_mixer",
    )(sinks, proj, proj, proj, proj, kvm)


def _gmlp_kernel(u_ref, v_ref, qm_ref, kvm_ref, ws_ref, bs_ref, o_ref, *, tq, tok_w):
    n_sub = tq // CHUNK
    n_grp = tok_w // GM_GROUP_DIM
    row = lax.broadcasted_iota(jnp.int32, (CHUNK, CHUNK), 0)
    col = lax.broadcasted_iota(jnp.int32, (CHUNK, CHUNK), 1)
    lower = row >= col
    for g in range(n_grp):
        w = jnp.where(lower, ws_ref[g], 0.0).astype(BF16)
        bias = bs_ref[:, g:g + 1]
        c0 = g * GM_GROUP_DIM
        for sb in range(n_sub):
            r0 = sb * CHUNK
            sv = jnp.dot(w, v_ref[r0:r0 + CHUNK, c0:c0 + GM_GROUP_DIM],
                         preferred_element_type=F32) + bias
            u = u_ref[r0:r0 + CHUNK, c0:c0 + GM_GROUP_DIM].astype(F32)
            o_ref[r0:r0 + CHUNK, c0:c0 + GM_GROUP_DIM] = (u * sv).astype(o_ref.dtype)
    o_ref[:, tok_w:] = _memory_attention(qm_ref[...], kvm_ref[...]).astype(o_ref.dtype)


def _gmlp_mixer(proj, kvm, w_s, b_s, *, seq, mem_len, tq):
    t = proj.shape[0]
    tok_w = (proj.shape[1] - MEM_W) // 2
    n_grp = tok_w // GM_GROUP_DIM
    assert (2 * tok_w) % MEM_W == 0 and seq % tq == 0
    kern = functools.partial(_gmlp_kernel, tq=tq, tok_w=tok_w)
    return pl.pallas_call(
        kern,
        grid=(t // tq,),
        in_specs=[
            pl.BlockSpec((tq, tok_w), lambda i: (i, 0)),
            pl.BlockSpec((tq, tok_w), lambda i: (i, 1)),
            pl.BlockSpec((tq, MEM_W), lambda i: (i, (2 * tok_w) // MEM_W)),
            pl.BlockSpec((mem_len, 2 * MEM_W), lambda i: ((i * tq) // seq, 0)),
            _resident((n_grp, CHUNK, CHUNK)),
            _resident((CHUNK, n_grp)),
        ],
        out_specs=pl.BlockSpec((tq, tok_w + MEM_W), lambda i: (i, 0)),
        out_shape=jax.ShapeDtypeStruct((t, tok_w + MEM_W), BF16),
        compiler_params=_params(),
        name="gmlp_mixer",
    )(proj, proj, proj, kvm, w_s, b_s.T)


def _post_kernel(mix_ref, x_ref, wo_ref, g_ref, b_ref, wrh_ref, wrl_ref, br_ref,
                 x1_ref, route_ref, cnt_ref, carry_ref, *, tm, alpha):
    i = pl.program_id(0)

    @pl.when(i == 0)
    def _():
        carry_ref[...] = jnp.zeros_like(carry_ref)

    y = jnp.dot(mix_ref[...], wo_ref[...], preferred_element_type=F32)
    x1 = _layer_norm(alpha * x_ref[...] + y, g_ref[...], b_ref[...])
    x1_ref[...] = x1

    xh = x1.astype(BF16)
    xl = (x1 - xh.astype(F32)).astype(BF16)
    logits = (jnp.dot(xh, wrh_ref[...], preferred_element_type=F32)
              + (jnp.dot(xh, wrl_ref[...], preferred_element_type=F32)
                 + jnp.dot(xl, wrh_ref[...], preferred_element_type=F32))) + br_ref[...]

    lane = lax.broadcasted_iota(jnp.int32, (tm, LANES), 1)
    lane_f = lane.astype(F32)
    big = float(LANES)
    is_grp = lane < N_GROUPS
    gl = jnp.where(is_grp, logits, NEG_INF)
    gmax = jnp.max(gl, axis=-1, keepdims=True)
    g_sel = jnp.min(jnp.where(gl == gmax, lane_f, big), axis=-1, keepdims=True)
    p_grp = 1.0 / jnp.sum(jnp.where(is_grp, jnp.exp(gl - gmax), 0.0), axis=-1, keepdims=True)

    e_lo = N_GROUPS + g_sel * EXPERTS_PER_GROUP
    in_grp = (lane_f >= e_lo) & (lane_f < e_lo + EXPERTS_PER_GROUP)
    el = jnp.where(in_grp, logits, NEG_INF)
    m1 = jnp.max(el, axis=-1, keepdims=True)
    i1 = jnp.min(jnp.where(el == m1, lane_f, big), axis=-1, keepdims=True)
    el2 = jnp.where(lane_f == i1, NEG_INF, el)
    m2 = jnp.max(el2, axis=-1, keepdims=True)
    i2 = jnp.min(jnp.where(el2 == m2, lane_f, big), axis=-1, keepdims=True)
    z = jnp.sum(jnp.where(in_grp, jnp.exp(el - m1), 0.0), axis=-1, keepdims=True)
    tp1 = 1.0 / z
    tp2 = jnp.exp(m2 - m1) / z
    gate1 = p_grp * tp1 / (tp1 + tp2)
    gate2 = p_grp * tp2 / (tp1 + tp2)
    e1 = i1 - N_GROUPS
    e2 = i2 - N_GROUPS

    oh1 = lane_f == e1
    oh2 = lane_f == e2
    c = jnp.where(oh1 | oh2, 1.0, 0.0)
    r = lax.broadcasted_iota(jnp.int32, (tm, tm), 0)
    s = lax.broadcasted_iota(jnp.int32, (tm, tm), 1)
    tri = jnp.where(r > s, 1.0, 0.0).astype(BF16)
    prefix = jnp.dot(tri, c.astype(BF16), preferred_element_type=F32) + carry_ref[0:1, :]
    rank1 = jnp.sum(jnp.where(oh1, prefix, 0.0), axis=-1, keepdims=True)
    rank2 = jnp.sum(jnp.where(oh2, prefix, 0.0), axis=-1, keepdims=True)
    new_carry = carry_ref[0:1, :] + jnp.sum(c, axis=0, keepdims=True)
    carry_ref[...] = jnp.broadcast_to(new_carry, carry_ref.shape)
    cnt_ref[...] = jnp.broadcast_to(new_carry, cnt_ref.shape)

    route = jnp.where(lane == 0, e1, 0.0)
    route = jnp.where(lane == 1, e2, route)
    route = jnp.where(lane == 2, rank1, route)
    route = jnp.where(lane == 3, rank2, route)
    route = jnp.where(lane == 4, gate1, route)
    route = jnp.where(lane == 5, gate2, route)
    route_ref[...] = route


def _post(mix, x2d, wo_bf, ln_g, ln_b, wr_hi, wr_lo, b_r, *, tm, alpha):
    t, d = x2d.shape
    kern = functools.partial(_post_kernel, tm=tm, alpha=alpha)
    return pl.pallas_call(
        kern,
        grid=(t // tm,),
        in_specs=[
            pl.BlockSpec((tm, d), lambda i: (i, 0)),
            pl.BlockSpec((tm, d), lambda i: (i, 0)),
            _resident((d, d)),
            _resident((1, d)), _resident((1, d)),
            _resident((d, LANES)), _resident((d, LANES)), _resident((1, LANES)),
        ],
        out_specs=[
            pl.BlockSpec((tm, d), lambda i: (i, 0)),
            pl.BlockSpec((tm, LANES), lambda i: (i, 0)),
            pl.BlockSpec((8, LANES), lambda i: (0, 0)),
        ],
        out_shape=[
            jax.ShapeDtypeStruct((t, d), F32),
            jax.ShapeDtypeStruct((t, LANES), F32),
            jax.ShapeDtypeStruct((8, LANES), F32),
        ],
        scratch_shapes=[pltpu.VMEM((8, LANES), F32)],
        compiler_params=_params(),
        name="post",
    )(mix, x2d, wo_bf, ln_g.reshape(1, d), ln_b.reshape(1, d), wr_hi, wr_lo, b_r)


def _dispatch_kernel(pstart_ref, e1_ref, e2_ref, r1_ref, r2_ref, x_hbm, zeros_hbm, xs_hbm, sem,
                     *, tb):
    del zeros_hbm
    i = pl.program_id(0)

    def issue(j, carry):
        src = x_hbm.at[pl.ds(i * tb + j, 1)]
        d1 = pstart_ref[e1_ref[j]] + r1_ref[j]
        d2 = pstart_ref[e2_ref[j]] + r2_ref[j]
        pltpu.make_async_copy(src, xs_hbm.at[pl.ds(d1, 1)], sem).start()
        pltpu.make_async_copy(src, xs_hbm.at[pl.ds(d2, 1)], sem).start()
        return carry

    lax.fori_loop(0, tb, issue, 0)

    def drain(j, carry):
        src = x_hbm.at[pl.ds(0, 1)]
        pltpu.make_async_copy(src, xs_hbm.at[pl.ds(0, 1)], sem).wait()
        pltpu.make_async_copy(src, xs_hbm.at[pl.ds(0, 1)], sem).wait()
        return carry

    lax.fori_loop(0, tb, drain, 0)


def _dispatch(x1, pstart, e1, e2, r1, r2, n_rows, *, tb):
    t, d = x1.shape
    smem_blk = pl.BlockSpec((tb,), lambda i, *_: (i,), memory_space=pltpu.SMEM)
    any_spec = pl.BlockSpec(memory_space=pl.ANY)
    grid_spec = pltpu.PrefetchScalarGridSpec(
        num_scalar_prefetch=1,
        grid=(t // tb,),
        in_specs=[smem_blk, smem_blk, smem_blk, smem_blk, any_spec, any_spec],
        out_specs=any_spec,
        scratch_shapes=[pltpu.SemaphoreType.DMA(())],
    )
    return pl.pallas_call(
        functools.partial(_dispatch_kernel, tb=tb),
        grid_spec=grid_spec,
        out_shape=jax.ShapeDtypeStruct((n_rows, d), x1.dtype),
        input_output_aliases={6: 0},
        compiler_params=_params(),
        name="dispatch",
    )(pstart, e1, e2, r1, r2, x1, jnp.zeros((n_rows, d), x1.dtype))


def _expert_kernel(blk_e_ref, n_used_ref, xs_ref, wg_ref, wu_ref, wd_ref, y_ref,
                   wg_bf, wu_bf, wd_bf):
    b = pl.program_id(0)
    prev_e = blk_e_ref[jnp.maximum(b - 1, 0)]
    new_expert = (b == 0) | (blk_e_ref[b] != prev_e)
    active = b < n_used_ref[0]

    @pl.when(active & new_expert)
    def _():
        wg_bf[...] = wg_ref[0].astype(BF16)
        wu_bf[...] = wu_ref[0].astype(BF16)
        wd_bf[...] = wd_ref[0].astype(BF16)

    @pl.when(active)
    def _():
        xb = xs_ref[...].astype(BF16)
        hg = jnp.dot(xb, wg_bf[...], preferred_element_type=F32)
        hu = jnp.dot(xb, wu_bf[...], preferred_element_type=F32)
        h = (jax.nn.silu(hg) * hu).astype(BF16)
        y_ref[...] = jnp.dot(h, wd_bf[...], preferred_element_type=F32)

    @pl.when(jnp.logical_not(active))
    def _():
        y_ref[...] = jnp.zeros_like(y_ref)


def _experts(xs, blk_e, n_used, w_gate, w_up, w_down, *, rb):
    n_rows, d = xs.shape
    d_e = w_gate.shape[-1]
    nblk = n_rows // rb

    def row_map(b, blk_e_ref, n_used_ref):
        return (jnp.minimum(b, n_used_ref[0] - 1), 0)

    def w_map(b, blk_e_ref, n_used_ref):
        return (blk_e_ref[b], 0, 0)

    grid_spec = pltpu.PrefetchScalarGridSpec(
        num_scalar_prefetch=2,
        grid=(nblk,),
        in_specs=[
            pl.BlockSpec((rb, d), row_map),
            pl.BlockSpec((1, d, d_e), w_map),
            pl.BlockSpec((1, d, d_e), w_map),
            pl.BlockSpec((1, d_e, d), w_map),
        ],
        out_specs=pl.BlockSpec((rb, d), lambda b, *_: (b, 0)),
        scratch_shapes=[pltpu.VMEM((d, d_e), BF16), pltpu.VMEM((d, d_e), BF16),
                        pltpu.VMEM((d_e, d), BF16)],
    )
    return pl.pallas_call(
        _expert_kernel,
        grid_spec=grid_spec,
        out_shape=jax.ShapeDtypeStruct((n_rows, d), F32),
        compiler_params=_params(),
        name="experts",
    )(blk_e, n_used, xs, w_gate, w_up, w_down)


def _combine_kernel(pstart_ref, e1_ref, e2_ref, r1_ref, r2_ref, x1_ref, route_ref, g_ref, b_ref,
                    y_hbm, o_ref, buf, sem, *, tm, alpha):
    def issue(j, carry):
        d1 = pstart_ref[e1_ref[j]] + r1_ref[j]
        d2 = pstart_ref[e2_ref[j]] + r2_ref[j]
        pltpu.make_async_copy(y_hbm.at[pl.ds(d1, 1)], buf.at[0, pl.ds(j, 1)], sem).start()
        pltpu.make_async_copy(y_hbm.at[pl.ds(d2, 1)], buf.at[1, pl.ds(j, 1)], sem).start()
        return carry

    lax.fori_loop(0, tm, issue, 0)

    def drain(j, carry):
        pltpu.make_async_copy(y_hbm.at[pl.ds(0, 1)], buf.at[0, pl.ds(0, 1)], sem).wait()
        pltpu.make_async_copy(y_hbm.at[pl.ds(0, 1)], buf.at[1, pl.ds(0, 1)], sem).wait()
        return carry

    lax.fori_loop(0, tm, drain, 0)

    route = route_ref[...]
    gate1 = route[:, 4:5]
    gate2 = route[:, 5:6]
    y = buf[0] * gate1 + buf[1] * gate2
    o_ref[...] = _layer_norm(alpha * x1_ref[...] + y, g_ref[...], b_ref[...])


def _combine(y_rows, x1, route, pstart, e1, e2, r1, r2, ln_g, ln_b, *, tm, alpha):
    t, d = x1.shape
    smem_blk = pl.BlockSpec((tm,), lambda i, *_: (i,), memory_space=pltpu.SMEM)
    grid_spec = pltpu.PrefetchScalarGridSpec(
        num_scalar_prefetch=1,
        grid=(t // tm,),
        in_specs=[smem_blk, smem_blk, smem_blk, smem_blk,
                  pl.BlockSpec((tm, d), lambda i, *_: (i, 0)),
                  pl.BlockSpec((tm, LANES), lambda i, *_: (i, 0)),
                  pl.BlockSpec((1, d), lambda i, *_: (0, 0)),
                  pl.BlockSpec((1, d), lambda i, *_: (0, 0)),
                  pl.BlockSpec(memory_space=pl.ANY)],
        out_specs=pl.BlockSpec((tm, d), lambda i, *_: (i, 0)),
        scratch_shapes=[pltpu.VMEM((2, tm, d), F32), pltpu.SemaphoreType.DMA(())],
    )
    return pl.pallas_call(
        functools.partial(_combine_kernel, tm=tm, alpha=alpha),
        grid_spec=grid_spec,
        out_shape=jax.ShapeDtypeStruct((t, d), F32),
        compiler_params=_params(),
        name="combine",
    )(pstart, e1, e2, r1, r2, x1, route, ln_g.reshape(1, d), ln_b.reshape(1, d), y_rows)


def _split_bf16(w):
    hi = w.astype(BF16)
    lo = (w - hi.astype(F32)).astype(BF16)
    return hi, lo


def kernel(x, mem, a_w_in, a_sinks, b_w_in, b_norm_g, b_norm_b, b_w_spatial, b_b_spatial,
           w_mem_kv, w_out, ln_g, ln_b, w_router_group, b_router_group,
           w_router_expert, b_router_expert, w_gate, w_up, w_down):
    bsz, seq, d = x.shape
    mem_len = mem.shape[1]
    depth = w_out.shape[0]
    t = bsz * seq
    alpha = (2 * depth) ** 0.25
    rb = ROW_BLOCK
    n_rows = -(-(t * TOP_K + N_EXPERTS * (rb - 1)) // rb) * rb
    nblk = n_rows // rb

    x2d = x.reshape(t, d)
    mem2d = mem.reshape(bsz * mem_len, d)
    for i in range(depth):
        j = i // 2
        kvm = _proj(mem2d, w_mem_kv[i].astype(BF16), BF16, tm=min(256, bsz * mem_len))
        if i % 2 == 0:
            proj = _proj(x2d, a_w_in[j].astype(BF16), BF16, tm=512)
            mix = _attn_mixer(proj, kvm, a_sinks[j], seq=seq, mem_len=mem_len, tq=256)
        else:
            proj = _gmlp_proj(x2d, b_w_in[j].astype(BF16), b_norm_g[j], b_norm_b[j], tm=256)
            mix = _gmlp_mixer(proj, kvm, b_w_spatial[j], b_b_spatial[j],
                              seq=seq, mem_len=mem_len, tq=256)

        w_r = jnp.concatenate([w_router_group[i], w_router_expert[i]], axis=1)
        w_r = jnp.pad(w_r, ((0, 0), (0, LANES - w_r.shape[1])))
        b_r = jnp.concatenate([b_router_group[i], b_router_expert[i]])
        b_r = jnp.pad(b_r, (0, LANES - b_r.shape[0])).reshape(1, LANES)
        wr_hi, wr_lo = _split_bf16(w_r)
        x1, route, cnt = _post(mix, x2d, w_out[i].astype(BF16), ln_g[i, 0], ln_b[i, 0],
                               wr_hi, wr_lo, b_r, tm=256, alpha=alpha)

        counts = cnt[0, :N_EXPERTS].astype(jnp.int32)
        padded = (counts + rb - 1) // rb * rb
        pend = jnp.cumsum(padded)
        pstart = (pend - padded).astype(jnp.int32)
        n_used = (pend[-1] // rb).astype(jnp.int32).reshape(1)
        blk_start = jnp.arange(nblk, dtype=jnp.int32) * rb
        blk_e = jnp.minimum(jnp.searchsorted(pend, blk_start, side='right'),
                            N_EXPERTS - 1).astype(jnp.int32)
        ri = route[:, :4].astype(jnp.int32)
        e1, e2, r1, r2 = ri[:, 0], ri[:, 1], ri[:, 2], ri[:, 3]

        xs = _dispatch(x1, pstart, e1, e2, r1, r2, n_rows, tb=1024)
        y_rows = _experts(xs, blk_e, n_used, w_gate[i], w_up[i], w_down[i], rb=rb)
        x2d = _combine(y_rows, x1, route, pstart, e1, e2, r1, r2, ln_g[i, 1], ln_b[i, 1],
                       tm=256, alpha=alpha)
    return x2d.reshape(bsz, seq, d)
```

```python
import functools
import math

import jax
import jax.numpy as jnp
import numpy as np
from jax import lax
from jax.experimental import pallas as pl
from jax.experimental.pallas import tpu as pltpu

MEM_HEADS = 4
MEM_HEAD_DIM = 128
MEM_W = MEM_HEADS * MEM_HEAD_DIM
HEAD_DIM = 64
N_KV_HEADS = 4
KV_W = N_KV_HEADS * HEAD_DIM
WINDOW = 128
CHUNK = 128
GM_GROUP_DIM = 128
N_GROUPS = 8
EXPERTS_PER_GROUP = 8
N_EXPERTS = N_GROUPS * EXPERTS_PER_GROUP
TOP_K = 2
LN_EPS = 1e-5
NEG_INF = -1e30

LANES = 128
VMEM_LIMIT_BYTES = 56 * 1024 * 1024

ROW_BLOCK = 256
BF16 = jnp.bfloat16
F32 = jnp.float32


def _alibi_slopes(n):
    def pow2(m):
        start = 2.0 ** (-8.0 / m)
        return [start ** (i + 1) for i in range(m)]
    if math.log2(n).is_integer():
        s = pow2(n)
    else:
        c = 2 ** math.floor(math.log2(n))
        s = pow2(c) + pow2(2 * c)[0::2][: n - c]
    return [float(v) for v in np.asarray(s, dtype=np.float32)]


def _params(n_axes=1):
    return pltpu.CompilerParams(dimension_semantics=("arbitrary",) * n_axes,
                                vmem_limit_bytes=VMEM_LIMIT_BYTES)


def _resident(shape):
    nd = len(shape)
    return pl.BlockSpec(shape, lambda *_: (0,) * nd, pipeline_mode=pl.Buffered(1))


def _layer_norm(x, g, b):
    mu = jnp.mean(x, axis=-1, keepdims=True)
    xc = x - mu
    var = jnp.mean(xc * xc, axis=-1, keepdims=True)
    return xc * lax.rsqrt(var + LN_EPS) * g + b


def _proj_kernel(x_ref, w_ref, o_ref):
    o_ref[...] = jnp.dot(x_ref[...].astype(BF16), w_ref[...],
                         preferred_element_type=F32).astype(o_ref.dtype)


def _proj(x2d, w_bf, out_dtype, tm):
    m, k = x2d.shape
    n = w_bf.shape[1]
    return pl.pallas_call(
        _proj_kernel,
        grid=(m // tm,),
        in_specs=[pl.BlockSpec((tm, k), lambda i: (i, 0)), _resident((k, n))],
        out_specs=pl.BlockSpec((tm, n), lambda i: (i, 0)),
        out_shape=jax.ShapeDtypeStruct((m, n), out_dtype),
        compiler_params=_params(),
        name="proj",
    )(x2d, w_bf)


def _gelu(x):
    return 0.5 * x * (1.0 + lax.erf(x * (2.0 ** -0.5)))


def _gmlp_proj_kernel(x_ref, w_ref, g_ref, b_ref, o_ref, *, tok_w):
    z = jnp.dot(x_ref[...].astype(BF16), w_ref[...], preferred_element_type=F32)
    u = _gelu(z[:, :tok_w])
    v = _gelu(z[:, tok_w:2 * tok_w])
    v = _layer_norm(v, g_ref[...], b_ref[...])
    o_ref[:, :tok_w] = u.astype(o_ref.dtype)
    o_ref[:, tok_w:2 * tok_w] = v.astype(o_ref.dtype)
    o_ref[:, 2 * tok_w:] = z[:, 2 * tok_w:].astype(o_ref.dtype)


def _gmlp_proj(x2d, w_bf, norm_g, norm_b, tm):
    m, k = x2d.shape
    n = w_bf.shape[1]
    tok_w = norm_g.shape[-1]
    return pl.pallas_call(
        functools.partial(_gmlp_proj_kernel, tok_w=tok_w),
        grid=(m // tm,),
        in_specs=[pl.BlockSpec((tm, k), lambda i: (i, 0)), _resident((k, n)),
                  _resident((1, tok_w)), _resident((1, tok_w))],
        out_specs=pl.BlockSpec((tm, n), lambda i: (i, 0)),
        out_shape=jax.ShapeDtypeStruct((m, n), BF16),
        compiler_params=_params(),
        name="gmlp_proj",
    )(x2d, w_bf, norm_g.reshape(1, tok_w), norm_b.reshape(1, tok_w))


def _memory_attention(qm, kvm):
    outs = []
    for h in range(MEM_HEADS):
        q = qm[:, h * MEM_HEAD_DIM:(h + 1) * MEM_HEAD_DIM]
        k = kvm[:, h * MEM_HEAD_DIM:(h + 1) * MEM_HEAD_DIM]
        v = kvm[:, MEM_W + h * MEM_HEAD_DIM:MEM_W + (h + 1) * MEM_HEAD_DIM]
        s = lax.dot_general(q, k, (((1,), (1,)), ((), ())),
                            preferred_element_type=F32) * (MEM_HEAD_DIM ** -0.5)
        m = jnp.max(s, axis=-1, keepdims=True)
        p = jnp.exp(s - m)
        p = p * (1.0 / jnp.sum(p, axis=-1, keepdims=True))
        outs.append(jnp.dot(p.astype(BF16), v, preferred_element_type=F32))
    return jnp.concatenate(outs, axis=-1)


def _attn_kernel(sinks_ref, q_ref, kv_ref, kvp_ref, qm_ref, kvm_ref, o_ref, *,
                 tq, seq, slopes, tok_w):
    i = pl.program_id(0)
    n_sub = tq // WINDOW
    gqa = len(slopes) // N_KV_HEADS
    qi = lax.broadcasted_iota(jnp.int32, (WINDOW, WINDOW), 0)
    c = lax.broadcasted_iota(jnp.int32, (WINDOW, WINDOW), 1)
    own = c <= qi
    dist_f = jnp.where(own, qi - c, WINDOW + qi - c).astype(F32)
    nt = (((1,), (1,)), ((), ()))
    for sb in range(n_sub):
        r0 = sb * WINDOW
        if sb == 0:
            kv_prev = kvp_ref[...]
            reach = jnp.where(((i * tq) % seq) == 0, 0, WINDOW)
            valid = c <= qi + reach
        else:
            kv_prev = kv_ref[r0 - WINDOW:r0, :]
            valid = None
        kv_cur = kv_ref[r0:r0 + WINDOW, :]
        for kh in range(N_KV_HEADS):
            ks = slice(kh * HEAD_DIM, (kh + 1) * HEAD_DIM)
            vs = slice(KV_W + kh * HEAD_DIM, KV_W + (kh + 1) * HEAD_DIM)
            heads = range(kh * gqa, (kh + 1) * gqa)
            q = jnp.concatenate(
                [q_ref[r0:r0 + WINDOW, h * HEAD_DIM:(h + 1) * HEAD_DIM] for h in heads], axis=0)
            s_own = lax.dot_general(q, kv_cur[:, ks], nt, preferred_element_type=F32)
            s_prev = lax.dot_general(q, kv_prev[:, ks], nt, preferred_element_type=F32)
            p_own, p_prev = [], []
            for g, h in enumerate(heads):
                rows = slice(g * WINDOW, (g + 1) * WINDOW)
                logits = (jnp.where(own, s_own[rows], s_prev[rows]) * (HEAD_DIM ** -0.5)
                          - slopes[h] * dist_f)
                if valid is not None:
                    logits = jnp.where(valid, logits, NEG_INF)
                sink = sinks_ref[h]
                m = jnp.maximum(jnp.max(logits, axis=-1, keepdims=True), sink)
                p = jnp.exp(logits - m)
                probs = p * (1.0 / (jnp.sum(p, axis=-1, keepdims=True) + jnp.exp(sink - m)))
                p_own.append(jnp.where(own, probs, 0.0).astype(BF16))
                p_prev.append(jnp.where(own, 0.0, probs).astype(BF16))
            out = (jnp.dot(jnp.concatenate(p_own, axis=0), kv_cur[:, vs],
                           preferred_element_type=F32)
                   + jnp.dot(jnp.concatenate(p_prev, axis=0), kv_prev[:, vs],
                             preferred_element_type=F32))
            o_ref[r0:r0 + WINDOW, kh * gqa * HEAD_DIM:(kh + 1) * gqa * HEAD_DIM] = jnp.concatenate(
                [out[g * WINDOW:(g + 1) * WINDOW] for g in range(gqa)], axis=1).astype(o_ref.dtype)
    o_ref[:, tok_w:] = _memory_attention(qm_ref[...], kvm_ref[...]).astype(o_ref.dtype)


def _attn_mixer(proj, kvm, sinks, *, seq, mem_len, tq):
    t = proj.shape[0]
    tok_w = proj.shape[1] - 2 * KV_W - MEM_W
    n_heads = tok_w // HEAD_DIM
    assert tok_w % (2 * KV_W) == 0 and tok_w % MEM_W == 0 and seq % tq == 0
    kv_col = tok_w // (2 * KV_W)
    qm_col = (tok_w + 2 * KV_W) // MEM_W
    sub = tq // WINDOW
    kern = functools.partial(_attn_kernel, tq=tq, seq=seq, slopes=_alibi_slopes(n_heads),
                             tok_w=tok_w)
    return pl.pallas_call(
        kern,
        grid=(t // tq,),
        in_specs=[
            pl.BlockSpec(memory_space=pltpu.SMEM),
            pl.BlockSpec((tq, tok_w), lambda i: (i, 0)),
            pl.BlockSpec((tq, 2 * KV_W), lambda i: (i, kv_col)),
            pl.BlockSpec((WINDOW, 2 * KV_W), lambda i: (jnp.maximum(i * sub - 1, 0), kv_col)),
            pl.BlockSpec((tq, MEM_W), lambda i: (i, qm_col)),
            pl.BlockSpec((mem_len, 2 * MEM_W), lambda i: ((i * tq) // seq, 0)),
        ],
        out_specs=pl.BlockSpec((tq, tok_w + MEM_W), lambda i: (i, 0)),
        out_shape=jax.ShapeDtypeStruct((t, tok_w + MEM_W), BF16),
        compiler_params=_params(),
        name="attn_mixer",
    )(sinks, proj, proj, proj, proj, kvm)


def _gmlp_kernel(u_ref, v_ref, qm_ref, kvm_ref, ws_ref, bs_ref, o_ref, *, tq, tok_w):
    n_sub = tq // CHUNK
    n_grp = tok_w // GM_GROUP_DIM
    row = lax.broadcasted_iota(jnp.int32, (CHUNK, CHUNK), 0)
    col = lax.broadcasted_iota(jnp.int32, (CHUNK, CHUNK), 1)
    lower = row >= col
    for g in range(n_grp):
        w = jnp.where(lower, ws_ref[g], 0.0).astype(BF16)
        bias = bs_ref[:, g:g + 1]
        c0 = g * GM_GROUP_DIM
        for sb in range(n_sub):
            r0 = sb * CHUNK
            sv = jnp.dot(w, v_ref[r0:r0 + CHUNK, c0:c0 + GM_GROUP_DIM],
                         preferred_element_type=F32) + bias
            u = u_ref[r0:r0 + CHUNK, c0:c0 + GM_GROUP_DIM].astype(F32)
            o_ref[r0:r0 + CHUNK, c0:c0 + GM_GROUP_DIM] = (u * sv).astype(o_ref.dtype)
    o_ref[:, tok_w:] = _memory_attention(qm_ref[...], kvm_ref[...]).astype(o_ref.dtype)


def _gmlp_mixer(proj, kvm, w_s, b_s, *, seq, mem_len, tq):
    t = proj.shape[0]
    tok_w = (proj.shape[1] - MEM_W) // 2
    n_grp = tok_w // GM_GROUP_DIM
    assert (2 * tok_w) % MEM_W == 0 and seq % tq == 0
    kern = functools.partial(_gmlp_kernel, tq=tq, tok_w=tok_w)
    return pl.pallas_call(
        kern,
        grid=(t // tq,),
        in_specs=[
            pl.BlockSpec((tq, tok_w), lambda i: (i, 0)),
            pl.BlockSpec((tq, tok_w), lambda i: (i, 1)),
            pl.BlockSpec((tq, MEM_W), lambda i: (i, (2 * tok_w) // MEM_W)),
            pl.BlockSpec((mem_len, 2 * MEM_W), lambda i: ((i * tq) // seq, 0)),
            _resident((n_grp, CHUNK, CHUNK)),
            _resident((CHUNK, n_grp)),
        ],
        out_specs=pl.BlockSpec((tq, tok_w + MEM_W), lambda i: (i, 0)),
        out_shape=jax.ShapeDtypeStruct((t, tok_w + MEM_W), BF16),
        compiler_params=_params(),
        name="gmlp_mixer",
    )(proj, proj, proj, kvm, w_s, b_s.T)


def _post_kernel(mix_ref, x_ref, wo_ref, g_ref, b_ref, wrh_ref, wrl_ref, br_ref,
                 x1_ref, route_ref, cnt_ref, carry_ref, *, tm, alpha):
    i = pl.program_id(0)

    @pl.when(i == 0)
    def _():
        carry_ref[...] = jnp.zeros_like(carry_ref)

    y = jnp.dot(mix_ref[...], wo_ref[...], preferred_element_type=F32)
    x1 = _layer_norm(alpha * x_ref[...] + y, g_ref[...], b_ref[...])
    x1_ref[...] = x1

    xh = x1.astype(BF16)
    xl = (x1 - xh.astype(F32)).astype(BF16)
    logits = (jnp.dot(xh, wrh_ref[...], preferred_element_type=F32)
              + (jnp.dot(xh, wrl_ref[...], preferred_element_type=F32)
                 + jnp.dot(xl, wrh_ref[...], preferred_element_type=F32))) + br_ref[...]

    lane = lax.broadcasted_iota(jnp.int32, (tm, LANES), 1)
    lane_f = lane.astype(F32)
    big = float(LANES)
    is_grp = lane < N_GROUPS
    gl = jnp.where(is_grp, logits, NEG_INF)
    gmax = jnp.max(gl, axis=-1, keepdims=True)
    g_sel = jnp.min(jnp.where(gl == gmax, lane_f, big), axis=-1, keepdims=True)
    p_grp = 1.0 / jnp.sum(jnp.where(is_grp, jnp.exp(gl - gmax), 0.0), axis=-1, keepdims=True)

    e_lo = N_GROUPS + g_sel * EXPERTS_PER_GROUP
    in_grp = (lane_f >= e_lo) & (lane_f < e_lo + EXPERTS_PER_GROUP)
    el = jnp.where(in_grp, logits, NEG_INF)
    m1 = jnp.max(el, axis=-1, keepdims=True)
    i1 = jnp.min(jnp.where(el == m1, lane_f, big), axis=-1, keepdims=True)
    el2 = jnp.where(lane_f == i1, NEG_INF, el)
    m2 = jnp.max(el2, axis=-1, keepdims=True)
    i2 = jnp.min(jnp.where(el2 == m2, lane_f, big), axis=-1, keepdims=True)
    z = jnp.sum(jnp.where(in_grp, jnp.exp(el - m1), 0.0), axis=-1, keepdims=True)
    tp1 = 1.0 / z
    tp2 = jnp.exp(m2 - m1) / z
    gate1 = p_grp * tp1 / (tp1 + tp2)
    gate2 = p_grp * tp2 / (tp1 + tp2)
    e1 = i1 - N_GROUPS
    e2 = i2 - N_GROUPS

    oh1 = lane_f == e1
    oh2 = lane_f == e2
    c = jnp.where(oh1 | oh2, 1.0, 0.0)
    r = lax.broadcasted_iota(jnp.int32, (tm, tm), 0)
    s = lax.broadcasted_iota(jnp.int32, (tm, tm), 1)
    tri = jnp.where(r > s, 1.0, 0.0).astype(BF16)
    prefix = jnp.dot(tri, c.astype(BF16), preferred_element_type=F32) + carry_ref[0:1, :]
    rank1 = jnp.sum(jnp.where(oh1, prefix, 0.0), axis=-1, keepdims=True)
    rank2 = jnp.sum(jnp.where(oh2, prefix, 0.0), axis=-1, keepdims=True)
    new_carry = carry_ref[0:1, :] + jnp.sum(c, axis=0, keepdims=True)
    carry_ref[...] = jnp.broadcast_to(new_carry, carry_ref.shape)
    cnt_ref[...] = jnp.broadcast_to(new_carry, cnt_ref.shape)

    route = jnp.where(lane == 0, e1, 0.0)
    route = jnp.where(lane == 1, e2, route)
    route = jnp.where(lane == 2, rank1, route)
    route = jnp.where(lane == 3, rank2, route)
    route = jnp.where(lane == 4, gate1, route)
    route = jnp.where(lane == 5, gate2, route)
    route_ref[...] = route


def _post(mix, x2d, wo_bf, ln_g, ln_b, wr_hi, wr_lo, b_r, *, tm, alpha):
    t, d = x2d.shape
    kern = functools.partial(_post_kernel, tm=tm, alpha=alpha)
    return pl.pallas_call(
        kern,
        grid=(t // tm,),
        in_specs=[
            pl.BlockSpec((tm, d), lambda i: (i, 0)),
            pl.BlockSpec((tm, d), lambda i: (i, 0)),
            _resident((d, d)),
            _resident((1, d)), _resident((1, d)),
            _resident((d, LANES)), _resident((d, LANES)), _resident((1, LANES)),
        ],
        out_specs=[
            pl.BlockSpec((tm, d), lambda i: (i, 0)),
            pl.BlockSpec((tm, LANES), lambda i: (i, 0)),
            pl.BlockSpec((8, LANES), lambda i: (0, 0)),
        ],
        out_shape=[
            jax.ShapeDtypeStruct((t, d), F32),
            jax.ShapeDtypeStruct((t, LANES), F32),
            jax.ShapeDtypeStruct((8, LANES), F32),
        ],
        scratch_shapes=[pltpu.VMEM((8, LANES), F32)],
        compiler_params=_params(),
        name="post",
    )(mix, x2d, wo_bf, ln_g.reshape(1, d), ln_b.reshape(1, d), wr_hi, wr_lo, b_r)


def _dispatch_kernel(pstart_ref, e1_ref, e2_ref, r1_ref, r2_ref, x_ref, zeros_hbm, xs_hbm, sem,
                     *, tb):
    del zeros_hbm

    def issue(j, carry):
        src = x_ref.at[pl.ds(j, 1)]
        d1 = pstart_ref[e1_ref[j]] + r1_ref[j]
        d2 = pstart_ref[e2_ref[j]] + r2_ref[j]
        pltpu.make_async_copy(src, xs_hbm.at[pl.ds(d1, 1)], sem).start()
        pltpu.make_async_copy(src, xs_hbm.at[pl.ds(d2, 1)], sem).start()
        return carry

    lax.fori_loop(0, tb, issue, 0, unroll=8)
    for _ in range(TOP_K):
        pltpu.make_async_copy(x_ref, xs_hbm.at[pl.ds(0, tb)], sem).wait()


def _dispatch(x1, pstart, e1, e2, r1, r2, n_rows, *, tb):
    t, d = x1.shape
    smem_blk = pl.BlockSpec((tb,), lambda i, *_: (i,), memory_space=pltpu.SMEM)
    any_spec = pl.BlockSpec(memory_space=pl.ANY)
    grid_spec = pltpu.PrefetchScalarGridSpec(
        num_scalar_prefetch=1,
        grid=(t // tb,),
        in_specs=[smem_blk, smem_blk, smem_blk, smem_blk,
                  pl.BlockSpec((tb, d), lambda i, *_: (i, 0)), any_spec],
        out_specs=any_spec,
        scratch_shapes=[pltpu.SemaphoreType.DMA(())],
    )
    return pl.pallas_call(
        functools.partial(_dispatch_kernel, tb=tb),
        grid_spec=grid_spec,
        out_shape=jax.ShapeDtypeStruct((n_rows, d), x1.dtype),
        input_output_aliases={6: 0},
        compiler_params=_params(),
        name="dispatch",
    )(pstart, e1, e2, r1, r2, x1, jnp.zeros((n_rows, d), x1.dtype))


def _expert_kernel(blk_e_ref, n_used_ref, xs_ref, wg_ref, wu_ref, wd_ref, y_ref,
                   wg_bf, wu_bf, wd_bf):
    b = pl.program_id(0)
    prev_e = blk_e_ref[jnp.maximum(b - 1, 0)]
    new_expert = (b == 0) | (blk_e_ref[b] != prev_e)
    active = b < n_used_ref[0]

    @pl.when(active & new_expert)
    def _():
        wg_bf[...] = wg_ref[0, 0].astype(BF16)
        wu_bf[...] = wu_ref[0, 0].astype(BF16)
        wd_bf[...] = wd_ref[0, 0].astype(BF16)

    @pl.when(active)
    def _():
        xb = xs_ref[...].astype(BF16)
        hg = jnp.dot(xb, wg_bf[...], preferred_element_type=F32)
        hu = jnp.dot(xb, wu_bf[...], preferred_element_type=F32)
        h = (jax.nn.silu(hg) * hu).astype(BF16)
        y_ref[...] = jnp.dot(h, wd_bf[...], preferred_element_type=F32)

    @pl.when(jnp.logical_not(active))
    def _():
        y_ref[...] = jnp.zeros_like(y_ref)


def _experts(xs, blk_e, n_used, w_gate, w_up, w_down, *, layer, rb):
    n_rows, d = xs.shape
    d_e = w_gate.shape[-1]
    nblk = n_rows // rb

    def row_map(b, blk_e_ref, n_used_ref):
        return (jnp.minimum(b, n_used_ref[0] - 1), 0)

    def w_map(b, blk_e_ref, n_used_ref):
        return (layer, blk_e_ref[b], 0, 0)

    grid_spec = pltpu.PrefetchScalarGridSpec(
        num_scalar_prefetch=2,
        grid=(nblk,),
        in_specs=[
            pl.BlockSpec((rb, d), row_map),
            pl.BlockSpec((1, 1, d, d_e), w_map),
            pl.BlockSpec((1, 1, d, d_e), w_map),
            pl.BlockSpec((1, 1, d_e, d), w_map),
        ],
        out_specs=pl.BlockSpec((rb, d), lambda b, *_: (b, 0)),
        scratch_shapes=[pltpu.VMEM((d, d_e), BF16), pltpu.VMEM((d, d_e), BF16),
                        pltpu.VMEM((d_e, d), BF16)],
    )
    return pl.pallas_call(
        _expert_kernel,
        grid_spec=grid_spec,
        out_shape=jax.ShapeDtypeStruct((n_rows, d), F32),
        compiler_params=_params(),
        name="experts",
    )(blk_e, n_used, xs, w_gate, w_up, w_down)


def _combine_kernel(pstart_ref, e1_ref, e2_ref, r1_ref, r2_ref, x1_ref, route_ref, g_ref, b_ref,
                    y_hbm, o_ref, buf, sem, *, tm, alpha):
    def issue(j, carry):
        d1 = pstart_ref[e1_ref[j]] + r1_ref[j]
        d2 = pstart_ref[e2_ref[j]] + r2_ref[j]
        pltpu.make_async_copy(y_hbm.at[pl.ds(d1, 1)], buf.at[0, pl.ds(j, 1)], sem).start()
        pltpu.make_async_copy(y_hbm.at[pl.ds(d2, 1)], buf.at[1, pl.ds(j, 1)], sem).start()
        return carry

    lax.fori_loop(0, tm, issue, 0, unroll=8)
    for k in range(TOP_K):
        pltpu.make_async_copy(y_hbm.at[pl.ds(0, tm)], buf.at[k], sem).wait()

    route = route_ref[...]
    gate1 = route[:, 4:5]
    gate2 = route[:, 5:6]
    y = buf[0] * gate1 + buf[1] * gate2
    o_ref[...] = _layer_norm(alpha * x1_ref[...] + y, g_ref[...], b_ref[...])


def _combine(y_rows, x1, route, pstart, e1, e2, r1, r2, ln_g, ln_b, *, tm, alpha):
    t, d = x1.shape
    smem_blk = pl.BlockSpec((tm,), lambda i, *_: (i,), memory_space=pltpu.SMEM)
    grid_spec = pltpu.PrefetchScalarGridSpec(
        num_scalar_prefetch=1,
        grid=(t // tm,),
        in_specs=[smem_blk, smem_blk, smem_blk, smem_blk,
                  pl.BlockSpec((tm, d), lambda i, *_: (i, 0)),
                  pl.BlockSpec((tm, LANES), lambda i, *_: (i, 0)),
                  pl.BlockSpec((1, d), lambda i, *_: (0, 0)),
                  pl.BlockSpec((1, d), lambda i, *_: (0, 0)),
                  pl.BlockSpec(memory_space=pl.ANY)],
        out_specs=pl.BlockSpec((tm, d), lambda i, *_: (i, 0)),
        scratch_shapes=[pltpu.VMEM((2, tm, d), F32), pltpu.SemaphoreType.DMA(())],
    )
    return pl.pallas_call(
        functools.partial(_combine_kernel, tm=tm, alpha=alpha),
        grid_spec=grid_spec,
        out_shape=jax.ShapeDtypeStruct((t, d), F32),
        compiler_params=_params(),
        name="combine",
    )(pstart, e1, e2, r1, r2, x1, route, ln_g.reshape(1, d), ln_b.reshape(1, d), y_rows)


def _split_bf16(w):
    hi = w.astype(BF16)
    lo = (w - hi.astype(F32)).astype(BF16)
    return hi, lo


def kernel(x, mem, a_w_in, a_sinks, b_w_in, b_norm_g, b_norm_b, b_w_spatial, b_b_spatial,
           w_mem_kv, w_out, ln_g, ln_b, w_router_group, b_router_group,
           w_router_expert, b_router_expert, w_gate, w_up, w_down):
    bsz, seq, d = x.shape
    mem_len = mem.shape[1]
    depth = w_out.shape[0]
    t = bsz * seq
    alpha = (2 * depth) ** 0.25
    rb = ROW_BLOCK
    n_rows = -(-(t * TOP_K + N_EXPERTS * (rb - 1)) // rb) * rb
    nblk = n_rows // rb

    x2d = x.reshape(t, d)
    mem2d = mem.reshape(bsz * mem_len, d)
    for i in range(depth):
        j = i // 2
        kvm = _proj(mem2d, w_mem_kv[i].astype(BF16), BF16, tm=min(256, bsz * mem_len))
        if i % 2 == 0:
            proj = _proj(x2d, a_w_in[j].astype(BF16), BF16, tm=512)
            mix = _attn_mixer(proj, kvm, a_sinks[j], seq=seq, mem_len=mem_len, tq=256)
        else:
            proj = _gmlp_proj(x2d, b_w_in[j].astype(BF16), b_norm_g[j], b_norm_b[j], tm=256)
            mix = _gmlp_mixer(proj, kvm, b_w_spatial[j], b_b_spatial[j],
                              seq=seq, mem_len=mem_len, tq=256)

        w_r = jnp.concatenate([w_router_group[i], w_router_expert[i]], axis=1)
        w_r = jnp.pad(w_r, ((0, 0), (0, LANES - w_r.shape[1])))
        b_r = jnp.concatenate([b_router_group[i], b_router_expert[i]])
        b_r = jnp.pad(b_r, (0, LANES - b_r.shape[0])).reshape(1, LANES)
        wr_hi, wr_lo = _split_bf16(w_r)
        x1, route, cnt = _post(mix, x2d, w_out[i].astype(BF16), ln_g[i, 0], ln_b[i, 0],
                               wr_hi, wr_lo, b_r, tm=256, alpha=alpha)

        counts = cnt[0, :N_EXPERTS].astype(jnp.int32)
        padded = (counts + rb - 1) // rb * rb
        pend = jnp.cumsum(padded)
        pstart = (pend - padded).astype(jnp.int32)
        n_used = (pend[-1] // rb).astype(jnp.int32).reshape(1)
        blk_start = jnp.arange(nblk, dtype=jnp.int32) * rb
        blk_e = jnp.minimum(jnp.searchsorted(pend, blk_start, side='right'),
                            N_EXPERTS - 1).astype(jnp.int32)
        ri = route[:, :4].astype(jnp.int32)
        e1, e2, r1, r2 = ri[:, 0], ri[:, 1], ri[:, 2], ri[:, 3]

        xs = _dispatch(x1, pstart, e1, e2, r1, r2, n_rows, tb=1024)
        y_rows = _experts(xs, blk_e, n_used, w_gate, w_up, w_down, layer=i, rb=rb)
        x2d = _combine(y_rows, x1, route, pstart, e1, e2, r1, r2, ln_g[i, 1], ln_b[i, 1],
                       tm=256, alpha=alpha)
    return x2d.reshape(bsz, seq, d)
```

```python
import functools
import math

import jax
import jax.numpy as jnp
import numpy as np
from jax import lax
from jax.experimental import pallas as pl
from jax.experimental.pallas import tpu as pltpu

MEM_HEADS = 4
MEM_HEAD_DIM = 128
MEM_W = MEM_HEADS * MEM_HEAD_DIM
HEAD_DIM = 64
N_KV_HEADS = 4
KV_W = N_KV_HEADS * HEAD_DIM
WINDOW = 128
CHUNK = 128
GM_GROUP_DIM = 128
N_GROUPS = 8
EXPERTS_PER_GROUP = 8
N_EXPERTS = N_GROUPS * EXPERTS_PER_GROUP
TOP_K = 2
LN_EPS = 1e-5
NEG_INF = -1e30

LANES = 128
VMEM_LIMIT_BYTES = 56 * 1024 * 1024

ROW_BLOCK = 256
BF16 = jnp.bfloat16
F32 = jnp.float32


def _alibi_slopes(n):
    def pow2(m):
        start = 2.0 ** (-8.0 / m)
        return [start ** (i + 1) for i in range(m)]
    if math.log2(n).is_integer():
        s = pow2(n)
    else:
        c = 2 ** math.floor(math.log2(n))
        s = pow2(c) + pow2(2 * c)[0::2][: n - c]
    return [float(v) for v in np.asarray(s, dtype=np.float32)]


def _params(n_axes=1):
    return pltpu.CompilerParams(dimension_semantics=("arbitrary",) * n_axes,
                                vmem_limit_bytes=VMEM_LIMIT_BYTES)


def _resident(shape):
    nd = len(shape)
    return pl.BlockSpec(shape, lambda *_: (0,) * nd, pipeline_mode=pl.Buffered(1))


def _layer_norm(x, g, b):
    mu = jnp.mean(x, axis=-1, keepdims=True)
    xc = x - mu
    var = jnp.mean(xc * xc, axis=-1, keepdims=True)
    return xc * lax.rsqrt(var + LN_EPS) * g + b


def _proj_kernel(x_ref, w_ref, o_ref):
    o_ref[...] = jnp.dot(x_ref[...].astype(BF16), w_ref[...],
                         preferred_element_type=F32).astype(o_ref.dtype)


def _proj(x2d, w_bf, out_dtype, tm):
    m, k = x2d.shape
    n = w_bf.shape[1]
    return pl.pallas_call(
        _proj_kernel,
        grid=(m // tm,),
        in_specs=[pl.BlockSpec((tm, k), lambda i: (i, 0)), _resident((k, n))],
        out_specs=pl.BlockSpec((tm, n), lambda i: (i, 0)),
        out_shape=jax.ShapeDtypeStruct((m, n), out_dtype),
        compiler_params=_params(),
        name="proj",
    )(x2d, w_bf)


def _gelu(x):
    return 0.5 * x * (1.0 + lax.erf(x * (2.0 ** -0.5)))


def _gmlp_proj_kernel(x_ref, w_ref, g_ref, b_ref, o_ref, *, tok_w):
    z = jnp.dot(x_ref[...].astype(BF16), w_ref[...], preferred_element_type=F32)
    u = _gelu(z[:, :tok_w])
    v = _gelu(z[:, tok_w:2 * tok_w])
    v = _layer_norm(v, g_ref[...], b_ref[...])
    o_ref[:, :tok_w] = u.astype(o_ref.dtype)
    o_ref[:, tok_w:2 * tok_w] = v.astype(o_ref.dtype)
    o_ref[:, 2 * tok_w:] = z[:, 2 * tok_w:].astype(o_ref.dtype)


def _gmlp_proj(x2d, w_bf, norm_g, norm_b, tm):
    m, k = x2d.shape
    n = w_bf.shape[1]
    tok_w = norm_g.shape[-1]
    return pl.pallas_call(
        functools.partial(_gmlp_proj_kernel, tok_w=tok_w),
        grid=(m // tm,),
        in_specs=[pl.BlockSpec((tm, k), lambda i: (i, 0)), _resident((k, n)),
                  _resident((1, tok_w)), _resident((1, tok_w))],
        out_specs=pl.BlockSpec((tm, n), lambda i: (i, 0)),
        out_shape=jax.ShapeDtypeStruct((m, n), BF16),
        compiler_params=_params(),
        name="gmlp_proj",
    )(x2d, w_bf, norm_g.reshape(1, tok_w), norm_b.reshape(1, tok_w))


def _memory_attention(qm, kvm):
    outs = []
    for h in range(MEM_HEADS):
        q = qm[:, h * MEM_HEAD_DIM:(h + 1) * MEM_HEAD_DIM]
        k = kvm[:, h * MEM_HEAD_DIM:(h + 1) * MEM_HEAD_DIM]
        v = kvm[:, MEM_W + h * MEM_HEAD_DIM:MEM_W + (h + 1) * MEM_HEAD_DIM]
        s = lax.dot_general(q, k, (((1,), (1,)), ((), ())),
                            preferred_element_type=F32) * (MEM_HEAD_DIM ** -0.5)
        m = jnp.max(s, axis=-1, keepdims=True)
        p = jnp.exp(s - m)
        p = p * (1.0 / jnp.sum(p, axis=-1, keepdims=True))
        outs.append(jnp.dot(p.astype(BF16), v, preferred_element_type=F32))
    return jnp.concatenate(outs, axis=-1)


def _attn_kernel(sinks_ref, q_ref, kv_ref, kvp_ref, qm_ref, kvm_ref, o_ref, *,
                 tq, seq, slopes, tok_w):
    i = pl.program_id(0)
    n_sub = tq // WINDOW
    gqa = len(slopes) // N_KV_HEADS
    qi = lax.broadcasted_iota(jnp.int32, (WINDOW, WINDOW), 0)
    c = lax.broadcasted_iota(jnp.int32, (WINDOW, WINDOW), 1)
    own = c <= qi
    dist_f = jnp.where(own, qi - c, WINDOW + qi - c).astype(F32)
    nt = (((1,), (1,)), ((), ()))
    for sb in range(n_sub):
        r0 = sb * WINDOW
        if sb == 0:
            kv_prev = kvp_ref[...]
            reach = jnp.where(((i * tq) % seq) == 0, 0, WINDOW)
            valid = c <= qi + reach
        else:
            kv_prev = kv_ref[r0 - WINDOW:r0, :]
            valid = None
        kv_cur = kv_ref[r0:r0 + WINDOW, :]
        for kh in range(N_KV_HEADS):
            ks = slice(kh * HEAD_DIM, (kh + 1) * HEAD_DIM)
            vs = slice(KV_W + kh * HEAD_DIM, KV_W + (kh + 1) * HEAD_DIM)
            heads = range(kh * gqa, (kh + 1) * gqa)
            q = jnp.concatenate(
                [q_ref[r0:r0 + WINDOW, h * HEAD_DIM:(h + 1) * HEAD_DIM] for h in heads], axis=0)
            s_own = lax.dot_general(q, kv_cur[:, ks], nt, preferred_element_type=F32)
            s_prev = lax.dot_general(q, kv_prev[:, ks], nt, preferred_element_type=F32)
            p_own, p_prev = [], []
            for g, h in enumerate(heads):
                rows = slice(g * WINDOW, (g + 1) * WINDOW)
                logits = (jnp.where(own, s_own[rows], s_prev[rows]) * (HEAD_DIM ** -0.5)
                          - slopes[h] * dist_f)
                if valid is not None:
                    logits = jnp.where(valid, logits, NEG_INF)
                sink = sinks_ref[h]
                m = jnp.maximum(jnp.max(logits, axis=-1, keepdims=True), sink)
                p = jnp.exp(logits - m)
                probs = p * (1.0 / (jnp.sum(p, axis=-1, keepdims=True) + jnp.exp(sink - m)))
                p_own.append(jnp.where(own, probs, 0.0).astype(BF16))
                p_prev.append(jnp.where(own, 0.0, probs).astype(BF16))
            out = (jnp.dot(jnp.concatenate(p_own, axis=0), kv_cur[:, vs],
                           preferred_element_type=F32)
                   + jnp.dot(jnp.concatenate(p_prev, axis=0), kv_prev[:, vs],
                             preferred_element_type=F32))
            o_ref[r0:r0 + WINDOW, kh * gqa * HEAD_DIM:(kh + 1) * gqa * HEAD_DIM] = jnp.concatenate(
                [out[g * WINDOW:(g + 1) * WINDOW] for g in range(gqa)], axis=1).astype(o_ref.dtype)
    o_ref[:, tok_w:] = _memory_attention(qm_ref[...], kvm_ref[...]).astype(o_ref.dtype)


def _attn_mixer(proj, kvm, sinks, *, seq, mem_len, tq):
    t = proj.shape[0]
    tok_w = proj.shape[1] - 2 * KV_W - MEM_W
    n_heads = tok_w // HEAD_DIM
    assert tok_w % (2 * KV_W) == 0 and tok_w % MEM_W == 0 and seq % tq == 0
    kv_col = tok_w // (2 * KV_W)
    qm_col = (tok_w + 2 * KV_W) // MEM_W
    sub = tq // WINDOW
    kern = functools.partial(_attn_kernel, tq=tq, seq=seq, slopes=_alibi_slopes(n_heads),
                             tok_w=tok_w)
    return pl.pallas_call(
        kern,
        grid=(t // tq,),
        in_specs=[
            pl.BlockSpec(memory_space=pltpu.SMEM),
            pl.BlockSpec((tq, tok_w), lambda i: (i, 0)),
            pl.BlockSpec((tq, 2 * KV_W), lambda i: (i, kv_col)),
            pl.BlockSpec((WINDOW, 2 * KV_W), lambda i: (jnp.maximum(i * sub - 1, 0), kv_col)),
            pl.BlockSpec((tq, MEM_W), lambda i: (i, qm_col)),
            pl.BlockSpec((mem_len, 2 * MEM_W), lambda i: ((i * tq) // seq, 0)),
        ],
        out_specs=pl.BlockSpec((tq, tok_w + MEM_W), lambda i: (i, 0)),
        out_shape=jax.ShapeDtypeStruct((t, tok_w + MEM_W), BF16),
        compiler_params=_params(),
        name="attn_mixer",
    )(sinks, proj, proj, proj, proj, kvm)


def _gmlp_kernel(u_ref, v_ref, qm_ref, kvm_ref, ws_ref, bs_ref, o_ref, *, tq, tok_w):
    n_sub = tq // CHUNK
    n_grp = tok_w // GM_GROUP_DIM
    row = lax.broadcasted_iota(jnp.int32, (CHUNK, CHUNK), 0)
    col = lax.broadcasted_iota(jnp.int32, (CHUNK, CHUNK), 1)
    lower = row >= col
    for g in range(n_grp):
        w = jnp.where(lower, ws_ref[g], 0.0).astype(BF16)
        bias = bs_ref[:, g:g + 1]
        c0 = g * GM_GROUP_DIM
        for sb in range(n_sub):
            r0 = sb * CHUNK
            sv = jnp.dot(w, v_ref[r0:r0 + CHUNK, c0:c0 + GM_GROUP_DIM],
                         preferred_element_type=F32) + bias
            u = u_ref[r0:r0 + CHUNK, c0:c0 + GM_GROUP_DIM].astype(F32)
            o_ref[r0:r0 + CHUNK, c0:c0 + GM_GROUP_DIM] = (u * sv).astype(o_ref.dtype)
    o_ref[:, tok_w:] = _memory_attention(qm_ref[...], kvm_ref[...]).astype(o_ref.dtype)


def _gmlp_mixer(proj, kvm, w_s, b_s, *, seq, mem_len, tq):
    t = proj.shape[0]
    tok_w = (proj.shape[1] - MEM_W) // 2
    n_grp = tok_w // GM_GROUP_DIM
    assert (2 * tok_w) % MEM_W == 0 and seq % tq == 0
    kern = functools.partial(_gmlp_kernel, tq=tq, tok_w=tok_w)
    return pl.pallas_call(
        kern,
        grid=(t // tq,),
        in_specs=[
            pl.BlockSpec((tq, tok_w), lambda i: (i, 0)),
            pl.BlockSpec((tq, tok_w), lambda i: (i, 1)),
            pl.BlockSpec((tq, MEM_W), lambda i: (i, (2 * tok_w) // MEM_W)),
            pl.BlockSpec((mem_len, 2 * MEM_W), lambda i: ((i * tq) // seq, 0)),
            _resident((n_grp, CHUNK, CHUNK)),
            _resident((CHUNK, n_grp)),
        ],
        out_specs=pl.BlockSpec((tq, tok_w + MEM_W), lambda i: (i, 0)),
        out_shape=jax.ShapeDtypeStruct((t, tok_w + MEM_W), BF16),
        compiler_params=_params(),
        name="gmlp_mixer",
    )(proj, proj, proj, kvm, w_s, b_s.T)


def _post_kernel(mix_ref, x_ref, wo_ref, g_ref, b_ref, wrh_ref, wrl_ref, br_ref,
                 x1_ref, route_ref, cnt_ref, carry_ref, *, tm, alpha):
    i = pl.program_id(0)

    @pl.when(i == 0)
    def _():
        carry_ref[...] = jnp.zeros_like(carry_ref)

    y = jnp.dot(mix_ref[...], wo_ref[...], preferred_element_type=F32)
    x1 = _layer_norm(alpha * x_ref[...] + y, g_ref[...], b_ref[...])
    x1_ref[...] = x1

    xh = x1.astype(BF16)
    xl = (x1 - xh.astype(F32)).astype(BF16)
    logits = (jnp.dot(xh, wrh_ref[...], preferred_element_type=F32)
              + (jnp.dot(xh, wrl_ref[...], preferred_element_type=F32)
                 + jnp.dot(xl, wrh_ref[...], preferred_element_type=F32))) + br_ref[...]

    lane = lax.broadcasted_iota(jnp.int32, (tm, LANES), 1)
    lane_f = lane.astype(F32)
    big = float(LANES)
    is_grp = lane < N_GROUPS
    gl = jnp.where(is_grp, logits, NEG_INF)
    gmax = jnp.max(gl, axis=-1, keepdims=True)
    g_sel = jnp.min(jnp.where(gl == gmax, lane_f, big), axis=-1, keepdims=True)
    p_grp = 1.0 / jnp.sum(jnp.where(is_grp, jnp.exp(gl - gmax), 0.0), axis=-1, keepdims=True)

    e_lo = N_GROUPS + g_sel * EXPERTS_PER_GROUP
    in_grp = (lane_f >= e_lo) & (lane_f < e_lo + EXPERTS_PER_GROUP)
    el = jnp.where(in_grp, logits, NEG_INF)
    m1 = jnp.max(el, axis=-1, keepdims=True)
    i1 = jnp.min(jnp.where(el == m1, lane_f, big), axis=-1, keepdims=True)
    el2 = jnp.where(lane_f == i1, NEG_INF, el)
    m2 = jnp.max(el2, axis=-1, keepdims=True)
    i2 = jnp.min(jnp.where(el2 == m2, lane_f, big), axis=-1, keepdims=True)
    z = jnp.sum(jnp.where(in_grp, jnp.exp(el - m1), 0.0), axis=-1, keepdims=True)
    tp1 = 1.0 / z
    tp2 = jnp.exp(m2 - m1) / z
    gate1 = p_grp * tp1 / (tp1 + tp2)
    gate2 = p_grp * tp2 / (tp1 + tp2)
    e1 = i1 - N_GROUPS
    e2 = i2 - N_GROUPS

    oh1 = lane_f == e1
    oh2 = lane_f == e2
    c = jnp.where(oh1 | oh2, 1.0, 0.0)
    r = lax.broadcasted_iota(jnp.int32, (tm, tm), 0)
    s = lax.broadcasted_iota(jnp.int32, (tm, tm), 1)
    tri = jnp.where(r > s, 1.0, 0.0).astype(BF16)
    prefix = jnp.dot(tri, c.astype(BF16), preferred_element_type=F32) + carry_ref[0:1, :]
    rank1 = jnp.sum(jnp.where(oh1, prefix, 0.0), axis=-1, keepdims=True)
    rank2 = jnp.sum(jnp.where(oh2, prefix, 0.0), axis=-1, keepdims=True)
    new_carry = carry_ref[0:1, :] + jnp.sum(c, axis=0, keepdims=True)
    carry_ref[...] = jnp.broadcast_to(new_carry, carry_ref.shape)
    cnt_ref[...] = jnp.broadcast_to(new_carry, cnt_ref.shape)

    route = jnp.where(lane == 0, e1, 0.0)
    route = jnp.where(lane == 1, e2, route)
    route = jnp.where(lane == 2, rank1, route)
    route = jnp.where(lane == 3, rank2, route)
    route = jnp.where(lane == 4, gate1, route)
    route = jnp.where(lane == 5, gate2, route)
    route_ref[...] = route


def _post(mix, x2d, wo_bf, ln_g, ln_b, wr_hi, wr_lo, b_r, *, tm, alpha):
    t, d = x2d.shape
    kern = functools.partial(_post_kernel, tm=tm, alpha=alpha)
    return pl.pallas_call(
        kern,
        grid=(t // tm,),
        in_specs=[
            pl.BlockSpec((tm, d), lambda i: (i, 0)),
            pl.BlockSpec((tm, d), lambda i: (i, 0)),
            _resident((d, d)),
            _resident((1, d)), _resident((1, d)),
            _resident((d, LANES)), _resident((d, LANES)), _resident((1, LANES)),
        ],
        out_specs=[
            pl.BlockSpec((tm, d), lambda i: (i, 0)),
            pl.BlockSpec((tm, LANES), lambda i: (i, 0)),
            pl.BlockSpec((8, LANES), lambda i: (0, 0)),
        ],
        out_shape=[
            jax.ShapeDtypeStruct((t, d), F32),
            jax.ShapeDtypeStruct((t, LANES), F32),
            jax.ShapeDtypeStruct((8, LANES), F32),
        ],
        scratch_shapes=[pltpu.VMEM((8, LANES), F32)],
        compiler_params=_params(),
        name="post",
    )(mix, x2d, wo_bf, ln_g.reshape(1, d), ln_b.reshape(1, d), wr_hi, wr_lo, b_r)


def _moe_kernel(blk_e_ref, n_used_ref, tok0_ref, tokn_ref, dstp_ref,
                x_hbm, wg_ref, wu_ref, wd_ref, y_hbm,
                xbuf, ybuf, wg_bf, wu_bf, wd_bf, sem_g, sem_s, *, rb, dump_row, nblk):
    s = pl.program_id(0)
    n_used = n_used_ref[0]
    slot = lax.rem(s, 2)
    other = 1 - slot

    def gather_row(tok_ref, j, dst_slot):
        pltpu.make_async_copy(x_hbm.at[pl.ds(tok_ref[j], 1)], xbuf.at[dst_slot, pl.ds(j, 1)],
                              sem_g.at[dst_slot]).start()

    def scatter_row(j, src_slot):
        pltpu.make_async_copy(ybuf.at[src_slot, pl.ds(j, 1)], y_hbm.at[pl.ds(dstp_ref[j], 1)],
                              sem_s).start()

    def wait_gather(sl):
        pltpu.make_async_copy(x_hbm.at[pl.ds(0, rb)], xbuf.at[sl], sem_g.at[sl]).wait()

    def wait_scatter():
        pltpu.make_async_copy(ybuf.at[0], y_hbm.at[pl.ds(0, rb)], sem_s).wait()

    @pl.when(s == 0)
    def _():
        ybuf[...] = jnp.zeros_like(ybuf)
        pltpu.make_async_copy(ybuf.at[1], y_hbm.at[pl.ds(dump_row, rb)], sem_s).start()

        def first(j, carry):
            gather_row(tok0_ref, j, 0)
            return carry
        lax.fori_loop(0, rb, first, 0, unroll=8)

    b_cur = jnp.minimum(s, nblk - 1)
    new_expert = (s == 0) | (blk_e_ref[b_cur] != blk_e_ref[jnp.maximum(b_cur - 1, 0)])

    @pl.when((s < n_used) & new_expert)
    def _():
        wg_bf[...] = wg_ref[0, 0].astype(BF16)
        wu_bf[...] = wu_ref[0, 0].astype(BF16)
        wd_bf[...] = wd_ref[0, 0].astype(BF16)

    @pl.when(s < n_used)
    def _():
        wait_gather(slot)
        wait_scatter()
        for j in range(rb):
            scatter_row(j, other)
            gather_row(tokn_ref, j, other)
        xb = xbuf[slot].astype(BF16)
        hg = jnp.dot(xb, wg_bf[...], preferred_element_type=F32)
        hu = jnp.dot(xb, wu_bf[...], preferred_element_type=F32)
        h = (jax.nn.silu(hg) * hu).astype(BF16)
        ybuf[slot] = jnp.dot(h, wd_bf[...], preferred_element_type=F32)

    @pl.when(s == n_used)
    def _():
        wait_gather(slot)
        wait_scatter()

        def last(j, carry):
            scatter_row(j, other)
            return carry
        lax.fori_loop(0, rb, last, 0, unroll=8)
        wait_scatter()


def _moe(x1, row_tok, row_dst, blk_e, n_used, w_gate, w_up, w_down, *, layer, rb):
    t, d = x1.shape
    d_e = w_gate.shape[-1]
    nblk = row_tok.shape[0] // rb

    def w_map(s, blk_e_ref, n_used_ref):
        return (layer, blk_e_ref[jnp.minimum(s, nblk - 1)], 0, 0)

    def smem(index_map):
        return pl.BlockSpec((rb,), index_map, memory_space=pltpu.SMEM)

    any_spec = pl.BlockSpec(memory_space=pl.ANY)
    grid_spec = pltpu.PrefetchScalarGridSpec(
        num_scalar_prefetch=2,
        grid=(nblk + 1,),
        in_specs=[
            smem(lambda s, *_: (0,)),
            smem(lambda s, *_: (jnp.minimum(s + 1, nblk - 1),)),
            smem(lambda s, *_: (jnp.clip(s - 1, 0, nblk - 1),)),
            any_spec,
            pl.BlockSpec((1, 1, d, d_e), w_map),
            pl.BlockSpec((1, 1, d, d_e), w_map),
            pl.BlockSpec((1, 1, d_e, d), w_map),
        ],
        out_specs=any_spec,
        scratch_shapes=[pltpu.VMEM((2, rb, d), F32), pltpu.VMEM((2, rb, d), F32),
                        pltpu.VMEM((d, d_e), BF16), pltpu.VMEM((d, d_e), BF16),
                        pltpu.VMEM((d_e, d), BF16),
                        pltpu.SemaphoreType.DMA((2,)), pltpu.SemaphoreType.DMA(())],
    )
    return pl.pallas_call(
        functools.partial(_moe_kernel, rb=rb, dump_row=TOP_K * t, nblk=nblk),
        grid_spec=grid_spec,
        out_shape=jax.ShapeDtypeStruct((TOP_K * t + rb, d), F32),
        compiler_params=_params(),
        name="moe",
    )(blk_e, n_used, row_tok, row_tok, row_dst, x1, w_gate, w_up, w_down)


def _combine_kernel(x1_ref, y0_ref, y1_ref, route_ref, g_ref, b_ref, o_ref, *, alpha):
    route = route_ref[...]
    y = y0_ref[...] * route[:, 4:5] + y1_ref[...] * route[:, 5:6]
    o_ref[...] = _layer_norm(alpha * x1_ref[...] + y, g_ref[...], b_ref[...])


def _combine(y2, x1, route, ln_g, ln_b, *, tm, alpha):
    t, d = x1.shape
    return pl.pallas_call(
        functools.partial(_combine_kernel, alpha=alpha),
        grid=(t // tm,),
        in_specs=[pl.BlockSpec((tm, d), lambda i: (i, 0)),
                  pl.BlockSpec((tm, d), lambda i: (i, 0)),
                  pl.BlockSpec((tm, d), lambda i: (t // tm + i, 0)),
                  pl.BlockSpec((tm, LANES), lambda i: (i, 0)),
                  _resident((1, d)), _resident((1, d))],
        out_specs=pl.BlockSpec((tm, d), lambda i: (i, 0)),
        out_shape=jax.ShapeDtypeStruct((t, d), F32),
        compiler_params=_params(),
        name="combine",
    )(x1, y2, y2, route, ln_g.reshape(1, d), ln_b.reshape(1, d))


def _split_bf16(w):
    hi = w.astype(BF16)
    lo = (w - hi.astype(F32)).astype(BF16)
    return hi, lo


def kernel(x, mem, a_w_in, a_sinks, b_w_in, b_norm_g, b_norm_b, b_w_spatial, b_b_spatial,
           w_mem_kv, w_out, ln_g, ln_b, w_router_group, b_router_group,
           w_router_expert, b_router_expert, w_gate, w_up, w_down):
    bsz, seq, d = x.shape
    mem_len = mem.shape[1]
    depth = w_out.shape[0]
    t = bsz * seq
    alpha = (2 * depth) ** 0.25
    rb = ROW_BLOCK
    n_rows = -(-(t * TOP_K + N_EXPERTS * (rb - 1)) // rb) * rb
    nblk = n_rows // rb

    x2d = x.reshape(t, d)
    mem2d = mem.reshape(bsz * mem_len, d)
    for i in range(depth):
        j = i // 2
        kvm = _proj(mem2d, w_mem_kv[i].astype(BF16), BF16, tm=min(256, bsz * mem_len))
        if i % 2 == 0:
            proj = _proj(x2d, a_w_in[j].astype(BF16), BF16, tm=512)
            mix = _attn_mixer(proj, kvm, a_sinks[j], seq=seq, mem_len=mem_len, tq=256)
        else:
            proj = _gmlp_proj(x2d, b_w_in[j].astype(BF16), b_norm_g[j], b_norm_b[j], tm=256)
            mix = _gmlp_mixer(proj, kvm, b_w_spatial[j], b_b_spatial[j],
                              seq=seq, mem_len=mem_len, tq=256)

        w_r = jnp.concatenate([w_router_group[i], w_router_expert[i]], axis=1)
        w_r = jnp.pad(w_r, ((0, 0), (0, LANES - w_r.shape[1])))
        b_r = jnp.concatenate([b_router_group[i], b_router_expert[i]])
        b_r = jnp.pad(b_r, (0, LANES - b_r.shape[0])).reshape(1, LANES)
        wr_hi, wr_lo = _split_bf16(w_r)
        x1, route, cnt = _post(mix, x2d, w_out[i].astype(BF16), ln_g[i, 0], ln_b[i, 0],
                               wr_hi, wr_lo, b_r, tm=256, alpha=alpha)

        counts = cnt[0, :N_EXPERTS].astype(jnp.int32)
        padded = (counts + rb - 1) // rb * rb
        pend = jnp.cumsum(padded)
        pstart = (pend - padded).astype(jnp.int32)
        n_used = (pend[-1] // rb).astype(jnp.int32).reshape(1)
        blk = jnp.arange(nblk, dtype=jnp.int32)
        blk_e = jnp.searchsorted(pend, jnp.minimum(blk, n_used[0] - 1) * rb,
                                 side='right').astype(jnp.int32)
        ri = route[:, :4].astype(jnp.int32)
        dest = (pstart[ri[:, :TOP_K]] + ri[:, TOP_K:]).reshape(-1)
        tok = jnp.arange(t, dtype=jnp.int32)
        row_tok = jnp.zeros((n_rows,), jnp.int32).at[dest].set(
            jnp.repeat(tok, TOP_K), unique_indices=True)
        row_dst = (TOP_K * t + jnp.arange(n_rows, dtype=jnp.int32) % rb).at[dest].set(
            (tok[:, None] + t * jnp.arange(TOP_K, dtype=jnp.int32)[None, :]).reshape(-1),
            unique_indices=True)

        y2 = _moe(x1, row_tok, row_dst, blk_e, n_used, w_gate, w_up, w_down, layer=i, rb=rb)
        x2d = _combine(y2, x1, route, ln_g[i, 1], ln_b[i, 1], tm=256, alpha=alpha)
    return x2d.reshape(bsz, seq, d)
```

```python
import functools
import math

import jax
import jax.numpy as jnp
import numpy as np
from jax import lax
from jax.experimental import pallas as pl
from jax.experimental.pallas import tpu as pltpu

MEM_HEADS = 4
MEM_HEAD_DIM = 128
MEM_W = MEM_HEADS * MEM_HEAD_DIM
HEAD_DIM = 64
N_KV_HEADS = 4
KV_W = N_KV_HEADS * HEAD_DIM
WINDOW = 128
CHUNK = 128
GM_GROUP_DIM = 128
N_GROUPS = 8
EXPERTS_PER_GROUP = 8
N_EXPERTS = N_GROUPS * EXPERTS_PER_GROUP
TOP_K = 2
LN_EPS = 1e-5
NEG_INF = -1e30

LANES = 128
VMEM_LIMIT_BYTES = 56 * 1024 * 1024

ROW_BLOCK = 256
BF16 = jnp.bfloat16
F32 = jnp.float32


def _alibi_slopes(n):
    def pow2(m):
        start = 2.0 ** (-8.0 / m)
        return [start ** (i + 1) for i in range(m)]
    if math.log2(n).is_integer():
        s = pow2(n)
    else:
        c = 2 ** math.floor(math.log2(n))
        s = pow2(c) + pow2(2 * c)[0::2][: n - c]
    return [float(v) for v in np.asarray(s, dtype=np.float32)]


def _params(n_axes=1):
    return pltpu.CompilerParams(dimension_semantics=("arbitrary",) * n_axes,
                                vmem_limit_bytes=VMEM_LIMIT_BYTES)


def _resident(shape):
    nd = len(shape)
    return pl.BlockSpec(shape, lambda *_: (0,) * nd, pipeline_mode=pl.Buffered(1))


def _load_rows(ref, n_chunks, *lead):
    rows = ref.shape[-2] // n_chunks
    return jnp.concatenate(
        [ref[tuple(lead) + (pl.ds(c, rows, stride=n_chunks), slice(None))]
         for c in range(n_chunks)], axis=-1)


def _store_rows(ref, val, *lead):
    rows = val.shape[0]
    n_chunks = ref.shape[-2] // rows
    for c in range(n_chunks):
        ref[tuple(lead) + (pl.ds(c, rows, stride=n_chunks), slice(None))] = (
            val[:, c * LANES:(c + 1) * LANES])


def _layer_norm(x, g, b):
    mu = jnp.mean(x, axis=-1, keepdims=True)
    xc = x - mu
    var = jnp.mean(xc * xc, axis=-1, keepdims=True)
    return xc * lax.rsqrt(var + LN_EPS) * g + b


def _proj_kernel(x_ref, w_ref, o_ref):
    o_ref[...] = jnp.dot(x_ref[...].astype(BF16), w_ref[...],
                         preferred_element_type=F32).astype(o_ref.dtype)


def _proj(x2d, w_bf, out_dtype, tm):
    m, k = x2d.shape
    n = w_bf.shape[1]
    return pl.pallas_call(
        _proj_kernel,
        grid=(m // tm,),
        in_specs=[pl.BlockSpec((tm, k), lambda i: (i, 0)), _resident((k, n))],
        out_specs=pl.BlockSpec((tm, n), lambda i: (i, 0)),
        out_shape=jax.ShapeDtypeStruct((m, n), out_dtype),
        compiler_params=_params(),
        name="proj",
    )(x2d, w_bf)


def _gelu(x):
    return 0.5 * x * (1.0 + lax.erf(x * (2.0 ** -0.5)))


def _gmlp_proj_kernel(x_ref, w_ref, g_ref, b_ref, o_ref, *, tok_w):
    z = jnp.dot(x_ref[...].astype(BF16), w_ref[...], preferred_element_type=F32)
    u = _gelu(z[:, :tok_w])
    v = _gelu(z[:, tok_w:2 * tok_w])
    v = _layer_norm(v, g_ref[...], b_ref[...])
    o_ref[:, :tok_w] = u.astype(o_ref.dtype)
    o_ref[:, tok_w:2 * tok_w] = v.astype(o_ref.dtype)
    o_ref[:, 2 * tok_w:] = z[:, 2 * tok_w:].astype(o_ref.dtype)


def _gmlp_proj(x2d, w_bf, norm_g, norm_b, tm):
    m, k = x2d.shape
    n = w_bf.shape[1]
    tok_w = norm_g.shape[-1]
    return pl.pallas_call(
        functools.partial(_gmlp_proj_kernel, tok_w=tok_w),
        grid=(m // tm,),
        in_specs=[pl.BlockSpec((tm, k), lambda i: (i, 0)), _resident((k, n)),
                  _resident((1, tok_w)), _resident((1, tok_w))],
        out_specs=pl.BlockSpec((tm, n), lambda i: (i, 0)),
        out_shape=jax.ShapeDtypeStruct((m, n), BF16),
        compiler_params=_params(),
        name="gmlp_proj",
    )(x2d, w_bf, norm_g.reshape(1, tok_w), norm_b.reshape(1, tok_w))


def _memory_attention(qm, kvm):
    outs = []
    for h in range(MEM_HEADS):
        q = qm[:, h * MEM_HEAD_DIM:(h + 1) * MEM_HEAD_DIM]
        k = kvm[:, h * MEM_HEAD_DIM:(h + 1) * MEM_HEAD_DIM]
        v = kvm[:, MEM_W + h * MEM_HEAD_DIM:MEM_W + (h + 1) * MEM_HEAD_DIM]
        s = lax.dot_general(q, k, (((1,), (1,)), ((), ())),
                            preferred_element_type=F32) * (MEM_HEAD_DIM ** -0.5)
        m = jnp.max(s, axis=-1, keepdims=True)
        p = jnp.exp(s - m)
        p = p * (1.0 / jnp.sum(p, axis=-1, keepdims=True))
        outs.append(jnp.dot(p.astype(BF16), v, preferred_element_type=F32))
    return jnp.concatenate(outs, axis=-1)


def _attn_kernel(sinks_ref, q_ref, kv_ref, kvp_ref, qm_ref, kvm_ref, o_ref, *,
                 tq, seq, slopes, tok_w):
    i = pl.program_id(0)
    n_sub = tq // WINDOW
    gqa = len(slopes) // N_KV_HEADS
    qi = lax.broadcasted_iota(jnp.int32, (WINDOW, WINDOW), 0)
    c = lax.broadcasted_iota(jnp.int32, (WINDOW, WINDOW), 1)
    own = c <= qi
    dist_f = jnp.where(own, qi - c, WINDOW + qi - c).astype(F32)
    nt = (((1,), (1,)), ((), ()))
    for sb in range(n_sub):
        r0 = sb * WINDOW
        if sb == 0:
            kv_prev = kvp_ref[...]
            reach = jnp.where(((i * tq) % seq) == 0, 0, WINDOW)
            valid = c <= qi + reach
        else:
            kv_prev = kv_ref[r0 - WINDOW:r0, :]
            valid = None
        kv_cur = kv_ref[r0:r0 + WINDOW, :]
        for kh in range(N_KV_HEADS):
            ks = slice(kh * HEAD_DIM, (kh + 1) * HEAD_DIM)
            vs = slice(KV_W + kh * HEAD_DIM, KV_W + (kh + 1) * HEAD_DIM)
            heads = range(kh * gqa, (kh + 1) * gqa)
            q = jnp.concatenate(
                [q_ref[r0:r0 + WINDOW, h * HEAD_DIM:(h + 1) * HEAD_DIM] for h in heads], axis=0)
            s_own = lax.dot_general(q, kv_cur[:, ks], nt, preferred_element_type=F32)
            s_prev = lax.dot_general(q, kv_prev[:, ks], nt, preferred_element_type=F32)
            p_own, p_prev = [], []
            for g, h in enumerate(heads):
                rows = slice(g * WINDOW, (g + 1) * WINDOW)
                logits = (jnp.where(own, s_own[rows], s_prev[rows]) * (HEAD_DIM ** -0.5)
                          - slopes[h] * dist_f)
                if valid is not None:
                    logits = jnp.where(valid, logits, NEG_INF)
                sink = sinks_ref[h]
                m = jnp.maximum(jnp.max(logits, axis=-1, keepdims=True), sink)
                p = jnp.exp(logits - m)
                probs = p * (1.0 / (jnp.sum(p, axis=-1, keepdims=True) + jnp.exp(sink - m)))
                p_own.append(jnp.where(own, probs, 0.0).astype(BF16))
                p_prev.append(jnp.where(own, 0.0, probs).astype(BF16))
            out = (jnp.dot(jnp.concatenate(p_own, axis=0), kv_cur[:, vs],
                           preferred_element_type=F32)
                   + jnp.dot(jnp.concatenate(p_prev, axis=0), kv_prev[:, vs],
                             preferred_element_type=F32))
            o_ref[r0:r0 + WINDOW, kh * gqa * HEAD_DIM:(kh + 1) * gqa * HEAD_DIM] = jnp.concatenate(
                [out[g * WINDOW:(g + 1) * WINDOW] for g in range(gqa)], axis=1).astype(o_ref.dtype)
    o_ref[:, tok_w:] = _memory_attention(qm_ref[...], kvm_ref[...]).astype(o_ref.dtype)


def _attn_mixer(proj, kvm, sinks, *, seq, mem_len, tq):
    t = proj.shape[0]
    tok_w = proj.shape[1] - 2 * KV_W - MEM_W
    n_heads = tok_w // HEAD_DIM
    assert tok_w % (2 * KV_W) == 0 and tok_w % MEM_W == 0 and seq % tq == 0
    kv_col = tok_w // (2 * KV_W)
    qm_col = (tok_w + 2 * KV_W) // MEM_W
    sub = tq // WINDOW
    kern = functools.partial(_attn_kernel, tq=tq, seq=seq, slopes=_alibi_slopes(n_heads),
                             tok_w=tok_w)
    return pl.pallas_call(
        kern,
        grid=(t // tq,),
        in_specs=[
            pl.BlockSpec(memory_space=pltpu.SMEM),
            pl.BlockSpec((tq, tok_w), lambda i: (i, 0)),
            pl.BlockSpec((tq, 2 * KV_W), lambda i: (i, kv_col)),
            pl.BlockSpec((WINDOW, 2 * KV_W), lambda i: (jnp.maximum(i * sub - 1, 0), kv_col)),
            pl.BlockSpec((tq, MEM_W), lambda i: (i, qm_col)),
            pl.BlockSpec((mem_len, 2 * MEM_W), lambda i: ((i * tq) // seq, 0)),
        ],
        out_specs=pl.BlockSpec((tq, tok_w + MEM_W), lambda i: (i, 0)),
        out_shape=jax.ShapeDtypeStruct((t, tok_w + MEM_W), BF16),
        compiler_params=_params(),
        name="attn_mixer",
    )(sinks, proj, proj, proj, proj, kvm)


def _gmlp_kernel(u_ref, v_ref, qm_ref, kvm_ref, ws_ref, bs_ref, o_ref, *, tq, tok_w):
    n_sub = tq // CHUNK
    n_grp = tok_w // GM_GROUP_DIM
    row = lax.broadcasted_iota(jnp.int32, (CHUNK, CHUNK), 0)
    col = lax.broadcasted_iota(jnp.int32, (CHUNK, CHUNK), 1)
    lower = row >= col
    for g in range(n_grp):
        w = jnp.where(lower, ws_ref[g], 0.0).astype(BF16)
        bias = bs_ref[:, g:g + 1]
        c0 = g * GM_GROUP_DIM
        for sb in range(n_sub):
            r0 = sb * CHUNK
            sv = jnp.dot(w, v_ref[r0:r0 + CHUNK, c0:c0 + GM_GROUP_DIM],
                         preferred_element_type=F32) + bias
            u = u_ref[r0:r0 + CHUNK, c0:c0 + GM_GROUP_DIM].astype(F32)
            o_ref[r0:r0 + CHUNK, c0:c0 + GM_GROUP_DIM] = (u * sv).astype(o_ref.dtype)
    o_ref[:, tok_w:] = _memory_attention(qm_ref[...], kvm_ref[...]).astype(o_ref.dtype)


def _gmlp_mixer(proj, kvm, w_s, b_s, *, seq, mem_len, tq):
    t = proj.shape[0]
    tok_w = (proj.shape[1] - MEM_W) // 2
    n_grp = tok_w // GM_GROUP_DIM
    assert (2 * tok_w) % MEM_W == 0 and seq % tq == 0
    kern = functools.partial(_gmlp_kernel, tq=tq, tok_w=tok_w)
    return pl.pallas_call(
        kern,
        grid=(t // tq,),
        in_specs=[
            pl.BlockSpec((tq, tok_w), lambda i: (i, 0)),
            pl.BlockSpec((tq, tok_w), lambda i: (i, 1)),
            pl.BlockSpec((tq, MEM_W), lambda i: (i, (2 * tok_w) // MEM_W)),
            pl.BlockSpec((mem_len, 2 * MEM_W), lambda i: ((i * tq) // seq, 0)),
            _resident((n_grp, CHUNK, CHUNK)),
            _resident((CHUNK, n_grp)),
        ],
        out_specs=pl.BlockSpec((tq, tok_w + MEM_W), lambda i: (i, 0)),
        out_shape=jax.ShapeDtypeStruct((t, tok_w + MEM_W), BF16),
        compiler_params=_params(),
        name="gmlp_mixer",
    )(proj, proj, proj, kvm, w_s, b_s.T)


def _post_kernel(mix_ref, x_ref, wo_ref, g_ref, b_ref, wrh_ref, wrl_ref, br_ref,
                 x1_ref, route_ref, cnt_ref, carry_ref, *, tm, alpha):
    i = pl.program_id(0)

    @pl.when(i == 0)
    def _():
        carry_ref[...] = jnp.zeros_like(carry_ref)

    y = jnp.dot(mix_ref[...], wo_ref[...], preferred_element_type=F32)
    x1 = _layer_norm(alpha * x_ref[...] + y, g_ref[...], b_ref[...])
    _store_rows(x1_ref, x1)

    xh = x1.astype(BF16)
    xl = (x1 - xh.astype(F32)).astype(BF16)
    logits = (jnp.dot(xh, wrh_ref[...], preferred_element_type=F32)
              + (jnp.dot(xh, wrl_ref[...], preferred_element_type=F32)
                 + jnp.dot(xl, wrh_ref[...], preferred_element_type=F32))) + br_ref[...]

    lane = lax.broadcasted_iota(jnp.int32, (tm, LANES), 1)
    lane_f = lane.astype(F32)
    big = float(LANES)
    is_grp = lane < N_GROUPS
    gl = jnp.where(is_grp, logits, NEG_INF)
    gmax = jnp.max(gl, axis=-1, keepdims=True)
    g_sel = jnp.min(jnp.where(gl == gmax, lane_f, big), axis=-1, keepdims=True)
    p_grp = 1.0 / jnp.sum(jnp.where(is_grp, jnp.exp(gl - gmax), 0.0), axis=-1, keepdims=True)

    e_lo = N_GROUPS + g_sel * EXPERTS_PER_GROUP
    in_grp = (lane_f >= e_lo) & (lane_f < e_lo + EXPERTS_PER_GROUP)
    el = jnp.where(in_grp, logits, NEG_INF)
    m1 = jnp.max(el, axis=-1, keepdims=True)
    i1 = jnp.min(jnp.where(el == m1, lane_f, big), axis=-1, keepdims=True)
    el2 = jnp.where(lane_f == i1, NEG_INF, el)
    m2 = jnp.max(el2, axis=-1, keepdims=True)
    i2 = jnp.min(jnp.where(el2 == m2, lane_f, big), axis=-1, keepdims=True)
    z = jnp.sum(jnp.where(in_grp, jnp.exp(el - m1), 0.0), axis=-1, keepdims=True)
    tp1 = 1.0 / z
    tp2 = jnp.exp(m2 - m1) / z
    gate1 = p_grp * tp1 / (tp1 + tp2)
    gate2 = p_grp * tp2 / (tp1 + tp2)
    e1 = i1 - N_GROUPS
    e2 = i2 - N_GROUPS

    oh1 = lane_f == e1
    oh2 = lane_f == e2
    c = jnp.where(oh1 | oh2, 1.0, 0.0)
    r = lax.broadcasted_iota(jnp.int32, (tm, tm), 0)
    s = lax.broadcasted_iota(jnp.int32, (tm, tm), 1)
    tri = jnp.where(r > s, 1.0, 0.0).astype(BF16)
    prefix = jnp.dot(tri, c.astype(BF16), preferred_element_type=F32) + carry_ref[0:1, :]
    rank1 = jnp.sum(jnp.where(oh1, prefix, 0.0), axis=-1, keepdims=True)
    rank2 = jnp.sum(jnp.where(oh2, prefix, 0.0), axis=-1, keepdims=True)
    new_carry = carry_ref[0:1, :] + jnp.sum(c, axis=0, keepdims=True)
    carry_ref[...] = jnp.broadcast_to(new_carry, carry_ref.shape)
    cnt_ref[...] = jnp.broadcast_to(new_carry, cnt_ref.shape)

    route = jnp.where(lane == 0, e1, 0.0)
    route = jnp.where(lane == 1, e2, route)
    route = jnp.where(lane == 2, rank1, route)
    route = jnp.where(lane == 3, rank2, route)
    route = jnp.where(lane == 4, gate1, route)
    route = jnp.where(lane == 5, gate2, route)
    route_ref[...] = route


def _post(mix, x2d, wo_bf, ln_g, ln_b, wr_hi, wr_lo, b_r, *, tm, alpha):
    t, d = x2d.shape
    kern = functools.partial(_post_kernel, tm=tm, alpha=alpha)
    return pl.pallas_call(
        kern,
        grid=(t // tm,),
        in_specs=[
            pl.BlockSpec((tm, d), lambda i: (i, 0)),
            pl.BlockSpec((tm, d), lambda i: (i, 0)),
            _resident((d, d)),
            _resident((1, d)), _resident((1, d)),
            _resident((d, LANES)), _resident((d, LANES)), _resident((1, LANES)),
        ],
        out_specs=[
            pl.BlockSpec((tm * (d // LANES), LANES), lambda i: (i, 0)),
            pl.BlockSpec((tm, LANES), lambda i: (i, 0)),
            pl.BlockSpec((8, LANES), lambda i: (0, 0)),
        ],
        out_shape=[
            jax.ShapeDtypeStruct((t * (d // LANES), LANES), F32),
            jax.ShapeDtypeStruct((t, LANES), F32),
            jax.ShapeDtypeStruct((8, LANES), F32),
        ],
        scratch_shapes=[pltpu.VMEM((8, LANES), F32)],
        compiler_params=_params(),
        name="post",
    )(mix, x2d, wo_bf, ln_g.reshape(1, d), ln_b.reshape(1, d), wr_hi, wr_lo, b_r)


def _moe_kernel(blk_e_ref, n_used_ref, tok0_ref, tokn_ref, dstp_ref,
                x_hbm, wg_ref, wu_ref, wd_ref, y_hbm,
                xbuf, ybuf, wg_bf, wu_bf, wd_bf, sem_g, sem_s, *, rb, nc, dump_row, nblk):
    s = pl.program_id(0)
    n_used = n_used_ref[0]
    slot = lax.rem(s, 2)
    other = 1 - slot

    def gather_row(tok_ref, j, dst_slot):
        src = x_hbm.at[pl.ds(pl.multiple_of(tok_ref[j], nc), nc)]
        pltpu.make_async_copy(src, xbuf.at[dst_slot, pl.ds(j * nc, nc)],
                              sem_g.at[dst_slot]).start()

    def scatter_row(j, src_slot):
        dst = y_hbm.at[pl.ds(pl.multiple_of(dstp_ref[j], nc), nc)]
        pltpu.make_async_copy(ybuf.at[src_slot, pl.ds(j * nc, nc)], dst, sem_s).start()

    def wait_gather(sl):
        pltpu.make_async_copy(x_hbm.at[pl.ds(0, rb * nc)], xbuf.at[sl], sem_g.at[sl]).wait()

    def wait_scatter():
        pltpu.make_async_copy(ybuf.at[0], y_hbm.at[pl.ds(0, rb * nc)], sem_s).wait()

    @pl.when(s == 0)
    def _():
        ybuf[...] = jnp.zeros_like(ybuf)
        pltpu.make_async_copy(ybuf.at[1], y_hbm.at[pl.ds(dump_row * nc, rb * nc)], sem_s).start()

        def first(j, carry):
            gather_row(tok0_ref, j, 0)
            return carry
        lax.fori_loop(0, rb, first, 0, unroll=8)

    b_cur = jnp.minimum(s, nblk - 1)
    new_expert = (s == 0) | (blk_e_ref[b_cur] != blk_e_ref[jnp.maximum(b_cur - 1, 0)])

    @pl.when((s < n_used) & new_expert)
    def _():
        wg_bf[...] = wg_ref[0, 0].astype(BF16)
        wu_bf[...] = wu_ref[0, 0].astype(BF16)
        wd_bf[...] = wd_ref[0, 0].astype(BF16)

    @pl.when(s < n_used)
    def _():
        wait_gather(slot)
        wait_scatter()
        for j in range(rb):
            scatter_row(j, other)
            gather_row(tokn_ref, j, other)
        xb = _load_rows(xbuf, nc, slot).astype(BF16)
        hg = jnp.dot(xb, wg_bf[...], preferred_element_type=F32)
        hu = jnp.dot(xb, wu_bf[...], preferred_element_type=F32)
        h = (jax.nn.silu(hg) * hu).astype(BF16)
        _store_rows(ybuf, jnp.dot(h, wd_bf[...], preferred_element_type=F32), slot)

    @pl.when(s == n_used)
    def _():
        wait_gather(slot)
        wait_scatter()

        def last(j, carry):
            scatter_row(j, other)
            return carry
        lax.fori_loop(0, rb, last, 0, unroll=8)
        wait_scatter()


def _moe(x1, row_tok, row_dst, blk_e, n_used, w_gate, w_up, w_down, *, layer, rb):
    d_e, d = w_down.shape[-2:]
    n_chunks = d // LANES
    t = x1.shape[0] // n_chunks
    nblk = row_tok.shape[0] // rb

    def w_map(s, blk_e_ref, n_used_ref):
        return (layer, blk_e_ref[jnp.minimum(s, nblk - 1)], 0, 0)

    def smem(index_map):
        return pl.BlockSpec((rb,), index_map, memory_space=pltpu.SMEM)

    any_spec = pl.BlockSpec(memory_space=pl.ANY)
    grid_spec = pltpu.PrefetchScalarGridSpec(
        num_scalar_prefetch=2,
        grid=(nblk + 1,),
        in_specs=[
            smem(lambda s, *_: (0,)),
            smem(lambda s, *_: (jnp.minimum(s + 1, nblk - 1),)),
            smem(lambda s, *_: (jnp.clip(s - 1, 0, nblk - 1),)),
            any_spec,
            pl.BlockSpec((1, 1, d, d_e), w_map),
            pl.BlockSpec((1, 1, d, d_e), w_map),
            pl.BlockSpec((1, 1, d_e, d), w_map),
        ],
        out_specs=any_spec,
        scratch_shapes=[pltpu.VMEM((2, rb * n_chunks, LANES), F32),
                        pltpu.VMEM((2, rb * n_chunks, LANES), F32),
                        pltpu.VMEM((d, d_e), BF16), pltpu.VMEM((d, d_e), BF16),
                        pltpu.VMEM((d_e, d), BF16),
                        pltpu.SemaphoreType.DMA((2,)), pltpu.SemaphoreType.DMA(())],
    )
    return pl.pallas_call(
        functools.partial(_moe_kernel, rb=rb, nc=n_chunks, dump_row=TOP_K * t, nblk=nblk),
        grid_spec=grid_spec,
        out_shape=jax.ShapeDtypeStruct(((TOP_K * t + rb) * n_chunks, LANES), F32),
        compiler_params=_params(),
        name="moe",
    )(blk_e, n_used, row_tok, row_tok, row_dst, x1, w_gate, w_up, w_down)


def _combine_kernel(x1_ref, y0_ref, y1_ref, route_ref, g_ref, b_ref, o_ref, *, alpha):
    route = route_ref[...]
    nc = o_ref.shape[1] // LANES
    y = _load_rows(y0_ref, nc) * route[:, 4:5] + _load_rows(y1_ref, nc) * route[:, 5:6]
    o_ref[...] = _layer_norm(alpha * _load_rows(x1_ref, nc) + y, g_ref[...], b_ref[...])


def _combine(y2, x1, route, ln_g, ln_b, *, tm, alpha):
    d = ln_g.shape[-1]
    n_chunks = d // LANES
    t = x1.shape[0] // n_chunks
    return pl.pallas_call(
        functools.partial(_combine_kernel, alpha=alpha),
        grid=(t // tm,),
        in_specs=[pl.BlockSpec((tm * n_chunks, LANES), lambda i: (i, 0)),
                  pl.BlockSpec((tm * n_chunks, LANES), lambda i: (i, 0)),
                  pl.BlockSpec((tm * n_chunks, LANES), lambda i: (t // tm + i, 0)),
                  pl.BlockSpec((tm, LANES), lambda i: (i, 0)),
                  _resident((1, d)), _resident((1, d))],
        out_specs=pl.BlockSpec((tm, d), lambda i: (i, 0)),
        out_shape=jax.ShapeDtypeStruct((t, d), F32),
        compiler_params=_params(),
        name="combine",
    )(x1, y2, y2, route, ln_g.reshape(1, d), ln_b.reshape(1, d))


def _split_bf16(w):
    hi = w.astype(BF16)
    lo = (w - hi.astype(F32)).astype(BF16)
    return hi, lo


def kernel(x, mem, a_w_in, a_sinks, b_w_in, b_norm_g, b_norm_b, b_w_spatial, b_b_spatial,
           w_mem_kv, w_out, ln_g, ln_b, w_router_group, b_router_group,
           w_router_expert, b_router_expert, w_gate, w_up, w_down):
    bsz, seq, d = x.shape
    mem_len = mem.shape[1]
    depth = w_out.shape[0]
    t = bsz * seq
    alpha = (2 * depth) ** 0.25
    rb = ROW_BLOCK
    n_rows = -(-(t * TOP_K + N_EXPERTS * (rb - 1)) // rb) * rb
    nblk = n_rows // rb

    x2d = x.reshape(t, d)
    mem2d = mem.reshape(bsz * mem_len, d)
    for i in range(depth):
        j = i // 2
        kvm = _proj(mem2d, w_mem_kv[i].astype(BF16), BF16, tm=min(256, bsz * mem_len))
        if i % 2 == 0:
            proj = _proj(x2d, a_w_in[j].astype(BF16), BF16, tm=512)
            mix = _attn_mixer(proj, kvm, a_sinks[j], seq=seq, mem_len=mem_len, tq=256)
        else:
            proj = _gmlp_proj(x2d, b_w_in[j].astype(BF16), b_norm_g[j], b_norm_b[j], tm=256)
            mix = _gmlp_mixer(proj, kvm, b_w_spatial[j], b_b_spatial[j],
                              seq=seq, mem_len=mem_len, tq=256)

        w_r = jnp.concatenate([w_router_group[i], w_router_expert[i]], axis=1)
        w_r = jnp.pad(w_r, ((0, 0), (0, LANES - w_r.shape[1])))
        b_r = jnp.concatenate([b_router_group[i], b_router_expert[i]])
        b_r = jnp.pad(b_r, (0, LANES - b_r.shape[0])).reshape(1, LANES)
        wr_hi, wr_lo = _split_bf16(w_r)
        x1, route, cnt = _post(mix, x2d, w_out[i].astype(BF16), ln_g[i, 0], ln_b[i, 0],
                               wr_hi, wr_lo, b_r, tm=256, alpha=alpha)

        counts = cnt[0, :N_EXPERTS].astype(jnp.int32)
        padded = (counts + rb - 1) // rb * rb
        pend = jnp.cumsum(padded)
        pstart = (pend - padded).astype(jnp.int32)
        n_used = (pend[-1] // rb).astype(jnp.int32).reshape(1)
        blk = jnp.arange(nblk, dtype=jnp.int32)
        blk_e = jnp.searchsorted(pend, jnp.minimum(blk, n_used[0] - 1) * rb,
                                 side='right').astype(jnp.int32)
        ri = route[:, :4].astype(jnp.int32)
        dest = (pstart[ri[:, :TOP_K]] + ri[:, TOP_K:]).reshape(-1)
        row_asg = jnp.full((n_rows,), -1, jnp.int32).at[dest].set(
            jnp.arange(TOP_K * t, dtype=jnp.int32), unique_indices=True)
        n_chunks = d // LANES
        row_tok = (jnp.maximum(row_asg, 0) // TOP_K) * n_chunks
        row_dst = jnp.where(row_asg < 0,
                            TOP_K * t + jnp.arange(n_rows, dtype=jnp.int32) % rb,
                            (row_asg % TOP_K) * t + row_asg // TOP_K) * n_chunks

        y2 = _moe(x1, row_tok, row_dst, blk_e, n_used, w_gate, w_up, w_down, layer=i, rb=rb)
        x2d = _combine(y2, x1, route, ln_g[i, 1], ln_b[i, 1], tm=256, alpha=alpha)
    return x2d.reshape(bsz, seq, d)
```

```python
import functools
import math

import jax
import jax.numpy as jnp
import numpy as np
from jax import lax
from jax.experimental import pallas as pl
from jax.experimental.pallas import tpu as pltpu

MEM_HEADS = 4
MEM_HEAD_DIM = 128
MEM_W = MEM_HEADS * MEM_HEAD_DIM
HEAD_DIM = 64
N_KV_HEADS = 4
KV_W = N_KV_HEADS * HEAD_DIM
WINDOW = 128
CHUNK = 128
GM_GROUP_DIM = 128
N_GROUPS = 8
EXPERTS_PER_GROUP = 8
N_EXPERTS = N_GROUPS * EXPERTS_PER_GROUP
TOP_K = 2
LN_EPS = 1e-5
NEG_INF = -1e30

LANES = 128
VMEM_LIMIT_BYTES = 56 * 1024 * 1024

ROW_BLOCK = 256
BF16 = jnp.bfloat16
F32 = jnp.float32
U32 = jnp.uint32


def _alibi_slopes(n):
    def pow2(m):
        start = 2.0 ** (-8.0 / m)
        return [start ** (i + 1) for i in range(m)]
    if math.log2(n).is_integer():
        s = pow2(n)
    else:
        c = 2 ** math.floor(math.log2(n))
        s = pow2(c) + pow2(2 * c)[0::2][: n - c]
    return [float(v) for v in np.asarray(s, dtype=np.float32)]


def _params(n_axes=1):
    return pltpu.CompilerParams(dimension_semantics=("arbitrary",) * n_axes,
                                vmem_limit_bytes=VMEM_LIMIT_BYTES)


def _resident(shape):
    nd = len(shape)
    return pl.BlockSpec(shape, lambda *_: (0,) * nd, pipeline_mode=pl.Buffered(1))


def _pack_rows(ref, val, *lead):
    rows, d = val.shape
    nw = d // (2 * LANES)
    for c in range(nw):
        lo = val[:, c * LANES:(c + 1) * LANES].astype(BF16).astype(F32)
        hi = val[:, (c + nw) * LANES:(c + nw + 1) * LANES].astype(BF16).astype(F32)
        word = (lax.bitcast_convert_type(lo, U32) >> 16) | lax.bitcast_convert_type(hi, U32)
        ref[tuple(lead) + (pl.ds(c, rows, stride=nw), slice(None))] = word


def _unpack_rows(ref, nw, *lead):
    rows = ref.shape[-2] // nw
    lo, hi = [], []
    for c in range(nw):
        word = ref[tuple(lead) + (pl.ds(c, rows, stride=nw), slice(None))]
        lo.append(lax.bitcast_convert_type(word << 16, F32))
        hi.append(lax.bitcast_convert_type(word & jnp.uint32(0xFFFF0000), F32))
    return jnp.concatenate(lo + hi, axis=-1)


def _layer_norm(x, g, b):
    mu = jnp.mean(x, axis=-1, keepdims=True)
    xc = x - mu
    var = jnp.mean(xc * xc, axis=-1, keepdims=True)
    return xc * lax.rsqrt(var + LN_EPS) * g + b


def _proj_kernel(x_ref, w_ref, o_ref):
    o_ref[...] = jnp.dot(x_ref[...].astype(BF16), w_ref[...],
                         preferred_element_type=F32).astype(o_ref.dtype)


def _proj(x2d, w_bf, out_dtype, tm):
    m, k = x2d.shape
    n = w_bf.shape[1]
    return pl.pallas_call(
        _proj_kernel,
        grid=(m // tm,),
        in_specs=[pl.BlockSpec((tm, k), lambda i: (i, 0)), _resident((k, n))],
        out_specs=pl.BlockSpec((tm, n), lambda i: (i, 0)),
        out_shape=jax.ShapeDtypeStruct((m, n), out_dtype),
        compiler_params=_params(),
        name="proj",
    )(x2d, w_bf)


def _gelu(x):
    return 0.5 * x * (1.0 + lax.erf(x * (2.0 ** -0.5)))


def _gmlp_proj_kernel(x_ref, w_ref, g_ref, b_ref, o_ref, *, tok_w):
    z = jnp.dot(x_ref[...].astype(BF16), w_ref[...], preferred_element_type=F32)
    u = _gelu(z[:, :tok_w])
    v = _gelu(z[:, tok_w:2 * tok_w])
    v = _layer_norm(v, g_ref[...], b_ref[...])
    o_ref[:, :tok_w] = u.astype(o_ref.dtype)
    o_ref[:, tok_w:2 * tok_w] = v.astype(o_ref.dtype)
    o_ref[:, 2 * tok_w:] = z[:, 2 * tok_w:].astype(o_ref.dtype)


def _gmlp_proj(x2d, w_bf, norm_g, norm_b, tm):
    m, k = x2d.shape
    n = w_bf.shape[1]
    tok_w = norm_g.shape[-1]
    return pl.pallas_call(
        functools.partial(_gmlp_proj_kernel, tok_w=tok_w),
        grid=(m // tm,),
        in_specs=[pl.BlockSpec((tm, k), lambda i: (i, 0)), _resident((k, n)),
                  _resident((1, tok_w)), _resident((1, tok_w))],
        out_specs=pl.BlockSpec((tm, n), lambda i: (i, 0)),
        out_shape=jax.ShapeDtypeStruct((m, n), BF16),
        compiler_params=_params(),
        name="gmlp_proj",
    )(x2d, w_bf, norm_g.reshape(1, tok_w), norm_b.reshape(1, tok_w))


def _memory_attention(qm, kvm):
    outs = []
    for h in range(MEM_HEADS):
        q = qm[:, h * MEM_HEAD_DIM:(h + 1) * MEM_HEAD_DIM]
        k = kvm[:, h * MEM_HEAD_DIM:(h + 1) * MEM_HEAD_DIM]
        v = kvm[:, MEM_W + h * MEM_HEAD_DIM:MEM_W + (h + 1) * MEM_HEAD_DIM]
        s = lax.dot_general(q, k, (((1,), (1,)), ((), ())),
                            preferred_element_type=F32) * (MEM_HEAD_DIM ** -0.5)
        m = jnp.max(s, axis=-1, keepdims=True)
        p = jnp.exp(s - m)
        p = p * (1.0 / jnp.sum(p, axis=-1, keepdims=True))
        outs.append(jnp.dot(p.astype(BF16), v, preferred_element_type=F32))
    return jnp.concatenate(outs, axis=-1)


def _attn_kernel(sinks_ref, q_ref, kv_ref, kvp_ref, qm_ref, kvm_ref, o_ref, *,
                 tq, seq, slopes, tok_w):
    i = pl.program_id(0)
    n_sub = tq // WINDOW
    gqa = len(slopes) // N_KV_HEADS
    qi = lax.broadcasted_iota(jnp.int32, (WINDOW, WINDOW), 0)
    c = lax.broadcasted_iota(jnp.int32, (WINDOW, WINDOW), 1)
    own = c <= qi
    dist_f = jnp.where(own, qi - c, WINDOW + qi - c).astype(F32)
    nt = (((1,), (1,)), ((), ()))
    for sb in range(n_sub):
        r0 = sb * WINDOW
        if sb == 0:
            kv_prev = kvp_ref[...]
            reach = jnp.where(((i * tq) % seq) == 0, 0, WINDOW)
            valid = c <= qi + reach
        else:
            kv_prev = kv_ref[r0 - WINDOW:r0, :]
            valid = None
        kv_cur = kv_ref[r0:r0 + WINDOW, :]
        for kh in range(N_KV_HEADS):
            ks = slice(kh * HEAD_DIM, (kh + 1) * HEAD_DIM)
            vs = slice(KV_W + kh * HEAD_DIM, KV_W + (kh + 1) * HEAD_DIM)
            heads = range(kh * gqa, (kh + 1) * gqa)
            q = jnp.concatenate(
                [q_ref[r0:r0 + WINDOW, h * HEAD_DIM:(h + 1) * HEAD_DIM] for h in heads], axis=0)
            s_own = lax.dot_general(q, kv_cur[:, ks], nt, preferred_element_type=F32)
            s_prev = lax.dot_general(q, kv_prev[:, ks], nt, preferred_element_type=F32)
            p_own, p_prev = [], []
            for g, h in enumerate(heads):
                rows = slice(g * WINDOW, (g + 1) * WINDOW)
                logits = (jnp.where(own, s_own[rows], s_prev[rows]) * (HEAD_DIM ** -0.5)
                          - slopes[h] * dist_f)
                if valid is not None:
                    logits = jnp.where(valid, logits, NEG_INF)
                sink = sinks_ref[h]
                m = jnp.maximum(jnp.max(logits, axis=-1, keepdims=True), sink)
                p = jnp.exp(logits - m)
                probs = p * (1.0 / (jnp.sum(p, axis=-1, keepdims=True) + jnp.exp(sink - m)))
                p_own.append(jnp.where(own, probs, 0.0).astype(BF16))
                p_prev.append(jnp.where(own, 0.0, probs).astype(BF16))
            out = (jnp.dot(jnp.concatenate(p_own, axis=0), kv_cur[:, vs],
                           preferred_element_type=F32)
                   + jnp.dot(jnp.concatenate(p_prev, axis=0), kv_prev[:, vs],
                             preferred_element_type=F32))
            o_ref[r0:r0 + WINDOW, kh * gqa * HEAD_DIM:(kh + 1) * gqa * HEAD_DIM] = jnp.concatenate(
                [out[g * WINDOW:(g + 1) * WINDOW] for g in range(gqa)], axis=1).astype(o_ref.dtype)
    o_ref[:, tok_w:] = _memory_attention(qm_ref[...], kvm_ref[...]).astype(o_ref.dtype)


def _attn_mixer(proj, kvm, sinks, *, seq, mem_len, tq):
    t = proj.shape[0]
    tok_w = proj.shape[1] - 2 * KV_W - MEM_W
    n_heads = tok_w // HEAD_DIM
    assert tok_w % (2 * KV_W) == 0 and tok_w % MEM_W == 0 and seq % tq == 0
    kv_col = tok_w // (2 * KV_W)
    qm_col = (tok_w + 2 * KV_W) // MEM_W
    sub = tq // WINDOW
    kern = functools.partial(_attn_kernel, tq=tq, seq=seq, slopes=_alibi_slopes(n_heads),
                             tok_w=tok_w)
    return pl.pallas_call(
        kern,
        grid=(t // tq,),
        in_specs=[
            pl.BlockSpec(memory_space=pltpu.SMEM),
            pl.BlockSpec((tq, tok_w), lambda i: (i, 0)),
            pl.BlockSpec((tq, 2 * KV_W), lambda i: (i, kv_col)),
            pl.BlockSpec((WINDOW, 2 * KV_W), lambda i: (jnp.maximum(i * sub - 1, 0), kv_col)),
            pl.BlockSpec((tq, MEM_W), lambda i: (i, qm_col)),
            pl.BlockSpec((mem_len, 2 * MEM_W), lambda i: ((i * tq) // seq, 0)),
        ],
        out_specs=pl.BlockSpec((tq, tok_w + MEM_W), lambda i: (i, 0)),
        out_shape=jax.ShapeDtypeStruct((t, tok_w + MEM_W), BF16),
        compiler_params=_params(),
        name="attn_mixer",
    )(sinks, proj, proj, proj, proj, kvm)


def _gmlp_kernel(u_ref, v_ref, qm_ref, kvm_ref, ws_ref, bs_ref, o_ref, *, tq, tok_w):
    n_sub = tq // CHUNK
    n_grp = tok_w // GM_GROUP_DIM
    row = lax.broadcasted_iota(jnp.int32, (CHUNK, CHUNK), 0)
    col = lax.broadcasted_iota(jnp.int32, (CHUNK, CHUNK), 1)
    lower = row >= col
    for g in range(n_grp):
        w = jnp.where(lower, ws_ref[g], 0.0).astype(BF16)
        bias = bs_ref[:, g:g + 1]
        c0 = g * GM_GROUP_DIM
        for sb in range(n_sub):
            r0 = sb * CHUNK
            sv = jnp.dot(w, v_ref[r0:r0 + CHUNK, c0:c0 + GM_GROUP_DIM],
                         preferred_element_type=F32) + bias
            u = u_ref[r0:r0 + CHUNK, c0:c0 + GM_GROUP_DIM].astype(F32)
            o_ref[r0:r0 + CHUNK, c0:c0 + GM_GROUP_DIM] = (u * sv).astype(o_ref.dtype)
    o_ref[:, tok_w:] = _memory_attention(qm_ref[...], kvm_ref[...]).astype(o_ref.dtype)


def _gmlp_mixer(proj, kvm, w_s, b_s, *, seq, mem_len, tq):
    t = proj.shape[0]
    tok_w = (proj.shape[1] - MEM_W) // 2
    n_grp = tok_w // GM_GROUP_DIM
    assert (2 * tok_w) % MEM_W == 0 and seq % tq == 0
    kern = functools.partial(_gmlp_kernel, tq=tq, tok_w=tok_w)
    return pl.pallas_call(
        kern,
        grid=(t // tq,),
        in_specs=[
            pl.BlockSpec((tq, tok_w), lambda i: (i, 0)),
            pl.BlockSpec((tq, tok_w), lambda i: (i, 1)),
            pl.BlockSpec((tq, MEM_W), lambda i: (i, (2 * tok_w) // MEM_W)),
            pl.BlockSpec((mem_len, 2 * MEM_W), lambda i: ((i * tq) // seq, 0)),
            _resident((n_grp, CHUNK, CHUNK)),
            _resident((CHUNK, n_grp)),
        ],
        out_specs=pl.BlockSpec((tq, tok_w + MEM_W), lambda i: (i, 0)),
        out_shape=jax.ShapeDtypeStruct((t, tok_w + MEM_W), BF16),
        compiler_params=_params(),
        name="gmlp_mixer",
    )(proj, proj, proj, kvm, w_s, b_s.T)


def _post_kernel(mix_ref, x_ref, wo_ref, g_ref, b_ref, wrh_ref, wrl_ref, br_ref,
                 x1_ref, xp_ref, route_ref, cnt_ref, carry_ref, *, tm, alpha):
    i = pl.program_id(0)

    @pl.when(i == 0)
    def _():
        carry_ref[...] = jnp.zeros_like(carry_ref)

    y = jnp.dot(mix_ref[...], wo_ref[...], preferred_element_type=F32)
    x1 = _layer_norm(alpha * x_ref[...] + y, g_ref[...], b_ref[...])
    x1_ref[...] = x1
    _pack_rows(xp_ref, x1)

    xh = x1.astype(BF16)
    xl = (x1 - xh.astype(F32)).astype(BF16)
    logits = (jnp.dot(xh, wrh_ref[...], preferred_element_type=F32)
              + (jnp.dot(xh, wrl_ref[...], preferred_element_type=F32)
                 + jnp.dot(xl, wrh_ref[...], preferred_element_type=F32))) + br_ref[...]

    lane = lax.broadcasted_iota(jnp.int32, (tm, LANES), 1)
    lane_f = lane.astype(F32)
    big = float(LANES)
    is_grp = lane < N_GROUPS
    gl = jnp.where(is_grp, logits, NEG_INF)
    gmax = jnp.max(gl, axis=-1, keepdims=True)
    g_sel = jnp.min(jnp.where(gl == gmax, lane_f, big), axis=-1, keepdims=True)
    p_grp = 1.0 / jnp.sum(jnp.where(is_grp, jnp.exp(gl - gmax), 0.0), axis=-1, keepdims=True)

    e_lo = N_GROUPS + g_sel * EXPERTS_PER_GROUP
    in_grp = (lane_f >= e_lo) & (lane_f < e_lo + EXPERTS_PER_GROUP)
    el = jnp.where(in_grp, logits, NEG_INF)
    m1 = jnp.max(el, axis=-1, keepdims=True)
    i1 = jnp.min(jnp.where(el == m1, lane_f, big), axis=-1, keepdims=True)
    el2 = jnp.where(lane_f == i1, NEG_INF, el)
    m2 = jnp.max(el2, axis=-1, keepdims=True)
    i2 = jnp.min(jnp.where(el2 == m2, lane_f, big), axis=-1, keepdims=True)
    z = jnp.sum(jnp.where(in_grp, jnp.exp(el - m1), 0.0), axis=-1, keepdims=True)
    tp1 = 1.0 / z
    tp2 = jnp.exp(m2 - m1) / z
    gate1 = p_grp * tp1 / (tp1 + tp2)
    gate2 = p_grp * tp2 / (tp1 + tp2)
    e1 = i1 - N_GROUPS
    e2 = i2 - N_GROUPS

    oh1 = lane_f == e1
    oh2 = lane_f == e2
    c = jnp.where(oh1 | oh2, 1.0, 0.0)
    r = lax.broadcasted_iota(jnp.int32, (tm, tm), 0)
    s = lax.broadcasted_iota(jnp.int32, (tm, tm), 1)
    tri = jnp.where(r > s, 1.0, 0.0).astype(BF16)
    prefix = jnp.dot(tri, c.astype(BF16), preferred_element_type=F32) + carry_ref[0:1, :]
    rank1 = jnp.sum(jnp.where(oh1, prefix, 0.0), axis=-1, keepdims=True)
    rank2 = jnp.sum(jnp.where(oh2, prefix, 0.0), axis=-1, keepdims=True)
    new_carry = carry_ref[0:1, :] + jnp.sum(c, axis=0, keepdims=True)
    carry_ref[...] = jnp.broadcast_to(new_carry, carry_ref.shape)
    cnt_ref[...] = jnp.broadcast_to(new_carry, cnt_ref.shape)

    route = jnp.where(lane == 0, e1, 0.0)
    route = jnp.where(lane == 1, e2, route)
    route = jnp.where(lane == 2, rank1, route)
    route = jnp.where(lane == 3, rank2, route)
    route = jnp.where(lane == 4, gate1, route)
    route = jnp.where(lane == 5, gate2, route)
    route_ref[...] = route


def _post(mix, x2d, wo_bf, ln_g, ln_b, wr_hi, wr_lo, b_r, *, tm, alpha):
    t, d = x2d.shape
    kern = functools.partial(_post_kernel, tm=tm, alpha=alpha)
    return pl.pallas_call(
        kern,
        grid=(t // tm,),
        in_specs=[
            pl.BlockSpec((tm, d), lambda i: (i, 0)),
            pl.BlockSpec((tm, d), lambda i: (i, 0)),
            _resident((d, d)),
            _resident((1, d)), _resident((1, d)),
            _resident((d, LANES)), _resident((d, LANES)), _resident((1, LANES)),
        ],
        out_specs=[
            pl.BlockSpec((tm, d), lambda i: (i, 0)),
            pl.BlockSpec((tm * (d // (2 * LANES)), LANES), lambda i: (i, 0)),
            pl.BlockSpec((tm, LANES), lambda i: (i, 0)),
            pl.BlockSpec((8, LANES), lambda i: (0, 0)),
        ],
        out_shape=[
            jax.ShapeDtypeStruct((t, d), F32),
            jax.ShapeDtypeStruct((t * (d // (2 * LANES)), LANES), U32),
            jax.ShapeDtypeStruct((t, LANES), F32),
            jax.ShapeDtypeStruct((8, LANES), F32),
        ],
        scratch_shapes=[pltpu.VMEM((8, LANES), F32)],
        compiler_params=_params(),
        name="post",
    )(mix, x2d, wo_bf, ln_g.reshape(1, d), ln_b.reshape(1, d), wr_hi, wr_lo, b_r)


def _moe_kernel(blk_e_ref, n_used_ref, tok0_ref, tokn_ref, dstp_ref,
                x_hbm, wg_ref, wu_ref, wd_ref, y_hbm,
                xbuf, ybuf, wg_bf, wu_bf, wd_bf, sem_g, sem_s, *, rb, nc, dump_row, nblk):
    s = pl.program_id(0)
    n_used = n_used_ref[0]
    slot = lax.rem(s, 2)
    other = 1 - slot

    def gather_row(tok_ref, j, dst_slot):
        src = x_hbm.at[pl.ds(pl.multiple_of(tok_ref[j], nc), nc)]
        pltpu.make_async_copy(src, xbuf.at[dst_slot, pl.ds(j * nc, nc)],
                              sem_g.at[dst_slot]).start()

    def scatter_row(j, src_slot):
        dst = y_hbm.at[pl.ds(pl.multiple_of(dstp_ref[j], nc), nc)]
        pltpu.make_async_copy(ybuf.at[src_slot, pl.ds(j * nc, nc)], dst, sem_s).start()

    def wait_gather(sl):
        pltpu.make_async_copy(x_hbm.at[pl.ds(0, rb * nc)], xbuf.at[sl], sem_g.at[sl]).wait()

    def wait_scatter():
        pltpu.make_async_copy(ybuf.at[0], y_hbm.at[pl.ds(0, rb * nc)], sem_s).wait()

    @pl.when(s == 0)
    def _():
        ybuf[...] = jnp.zeros_like(ybuf)
        pltpu.make_async_copy(ybuf.at[1], y_hbm.at[pl.ds(dump_row * nc, rb * nc)], sem_s).start()

        def first(j, carry):
            gather_row(tok0_ref, j, 0)
            return carry
        lax.fori_loop(0, rb, first, 0, unroll=8)

    b_cur = jnp.minimum(s, nblk - 1)
    new_expert = (s == 0) | (blk_e_ref[b_cur] != blk_e_ref[jnp.maximum(b_cur - 1, 0)])

    @pl.when((s < n_used) & new_expert)
    def _():
        wg_bf[...] = wg_ref[0, 0].astype(BF16)
        wu_bf[...] = wu_ref[0, 0].astype(BF16)
        wd_bf[...] = wd_ref[0, 0].astype(BF16)

    @pl.when(s < n_used)
    def _():
        wait_gather(slot)
        wait_scatter()
        for j in range(rb):
            scatter_row(j, other)
            gather_row(tokn_ref, j, other)
        xb = _unpack_rows(xbuf, nc, slot).astype(BF16)
        hg = jnp.dot(xb, wg_bf[...], preferred_element_type=F32)
        hu = jnp.dot(xb, wu_bf[...], preferred_element_type=F32)
        h = (jax.nn.silu(hg) * hu).astype(BF16)
        _pack_rows(ybuf, jnp.dot(h, wd_bf[...], preferred_element_type=F32), slot)

    @pl.when(s == n_used)
    def _():
        wait_gather(slot)
        wait_scatter()

        def last(j, carry):
            scatter_row(j, other)
            return carry
        lax.fori_loop(0, rb, last, 0, unroll=8)
        wait_scatter()


def _moe(x1, row_tok, row_dst, blk_e, n_used, w_gate, w_up, w_down, *, layer, rb):
    d_e, d = w_down.shape[-2:]
    n_chunks = d // (2 * LANES)
    t = x1.shape[0] // n_chunks
    nblk = row_tok.shape[0] // rb

    def w_map(s, blk_e_ref, n_used_ref):
        return (layer, blk_e_ref[jnp.minimum(s, nblk - 1)], 0, 0)

    def smem(index_map):
        return pl.BlockSpec((rb,), index_map, memory_space=pltpu.SMEM)

    any_spec = pl.BlockSpec(memory_space=pl.ANY)
    grid_spec = pltpu.PrefetchScalarGridSpec(
        num_scalar_prefetch=2,
        grid=(nblk + 1,),
        in_specs=[
            smem(lambda s, *_: (0,)),
            smem(lambda s, *_: (jnp.minimum(s + 1, nblk - 1),)),
            smem(lambda s, *_: (jnp.clip(s - 1, 0, nblk - 1),)),
            any_spec,
            pl.BlockSpec((1, 1, d, d_e), w_map),
            pl.BlockSpec((1, 1, d, d_e), w_map),
            pl.BlockSpec((1, 1, d_e, d), w_map),
        ],
        out_specs=any_spec,
        scratch_shapes=[pltpu.VMEM((2, rb * n_chunks, LANES), U32),
                        pltpu.VMEM((2, rb * n_chunks, LANES), U32),
                        pltpu.VMEM((d, d_e), BF16), pltpu.VMEM((d, d_e), BF16),
                        pltpu.VMEM((d_e, d), BF16),
                        pltpu.SemaphoreType.DMA((2,)), pltpu.SemaphoreType.DMA(())],
    )
    return pl.pallas_call(
        functools.partial(_moe_kernel, rb=rb, nc=n_chunks, dump_row=TOP_K * t, nblk=nblk),
        grid_spec=grid_spec,
        out_shape=jax.ShapeDtypeStruct(((TOP_K * t + rb) * n_chunks, LANES), U32),
        compiler_params=_params(),
        name="moe",
    )(blk_e, n_used, row_tok, row_tok, row_dst, x1, w_gate, w_up, w_down)


def _combine_kernel(x1_ref, y0_ref, y1_ref, route_ref, g_ref, b_ref, o_ref, *, alpha):
    route = route_ref[...]
    nw = o_ref.shape[1] // (2 * LANES)
    y = _unpack_rows(y0_ref, nw) * route[:, 4:5] + _unpack_rows(y1_ref, nw) * route[:, 5:6]
    o_ref[...] = _layer_norm(alpha * x1_ref[...] + y, g_ref[...], b_ref[...])


def _combine(y2, x1, route, ln_g, ln_b, *, tm, alpha):
    t, d = x1.shape
    n_chunks = d // (2 * LANES)
    return pl.pallas_call(
        functools.partial(_combine_kernel, alpha=alpha),
        grid=(t // tm,),
        in_specs=[pl.BlockSpec((tm, d), lambda i: (i, 0)),
                  pl.BlockSpec((tm * n_chunks, LANES), lambda i: (i, 0)),
                  pl.BlockSpec((tm * n_chunks, LANES), lambda i: (t // tm + i, 0)),
                  pl.BlockSpec((tm, LANES), lambda i: (i, 0)),
                  _resident((1, d)), _resident((1, d))],
        out_specs=pl.BlockSpec((tm, d), lambda i: (i, 0)),
        out_shape=jax.ShapeDtypeStruct((t, d), F32),
        compiler_params=_params(),
        name="combine",
    )(x1, y2, y2, route, ln_g.reshape(1, d), ln_b.reshape(1, d))


def _split_bf16(w):
    hi = w.astype(BF16)
    lo = (w - hi.astype(F32)).astype(BF16)
    return hi, lo


def kernel(x, mem, a_w_in, a_sinks, b_w_in, b_norm_g, b_norm_b, b_w_spatial, b_b_spatial,
           w_mem_kv, w_out, ln_g, ln_b, w_router_group, b_router_group,
           w_router_expert, b_router_expert, w_gate, w_up, w_down):
    bsz, seq, d = x.shape
    mem_len = mem.shape[1]
    depth = w_out.shape[0]
    t = bsz * seq
    alpha = (2 * depth) ** 0.25
    rb = ROW_BLOCK
    n_rows = -(-(t * TOP_K + N_EXPERTS * (rb - 1)) // rb) * rb
    nblk = n_rows // rb

    x2d = x.reshape(t, d)
    mem2d = mem.reshape(bsz * mem_len, d)
    for i in range(depth):
        j = i // 2
        kvm = _proj(mem2d, w_mem_kv[i].astype(BF16), BF16, tm=min(256, bsz * mem_len))
        if i % 2 == 0:
            proj = _proj(x2d, a_w_in[j].astype(BF16), BF16, tm=512)
            mix = _attn_mixer(proj, kvm, a_sinks[j], seq=seq, mem_len=mem_len, tq=256)
        else:
            proj = _gmlp_proj(x2d, b_w_in[j].astype(BF16), b_norm_g[j], b_norm_b[j], tm=256)
            mix = _gmlp_mixer(proj, kvm, b_w_spatial[j], b_b_spatial[j],
                              seq=seq, mem_len=mem_len, tq=256)

        w_r = jnp.concatenate([w_router_group[i], w_router_expert[i]], axis=1)
        w_r = jnp.pad(w_r, ((0, 0), (0, LANES - w_r.shape[1])))
        b_r = jnp.concatenate([b_router_group[i], b_router_expert[i]])
        b_r = jnp.pad(b_r, (0, LANES - b_r.shape[0])).reshape(1, LANES)
        wr_hi, wr_lo = _split_bf16(w_r)
        x1, x1p, route, cnt = _post(mix, x2d, w_out[i].astype(BF16), ln_g[i, 0], ln_b[i, 0],
                                    wr_hi, wr_lo, b_r, tm=256, alpha=alpha)

        counts = cnt[0, :N_EXPERTS].astype(jnp.int32)
        padded = (counts + rb - 1) // rb * rb
        pend = jnp.cumsum(padded)
        pstart = (pend - padded).astype(jnp.int32)
        n_used = (pend[-1] // rb).astype(jnp.int32).reshape(1)
        blk = jnp.arange(nblk, dtype=jnp.int32)
        blk_e = jnp.searchsorted(pend, jnp.minimum(blk, n_used[0] - 1) * rb,
                                 side='right').astype(jnp.int32)
        ri = route[:, :4].astype(jnp.int32)
        e_hot = ri[:, :TOP_K, None] == jnp.arange(N_EXPERTS, dtype=jnp.int32)
        dest = (jnp.sum(jnp.where(e_hot, pstart, 0), axis=-1) + ri[:, TOP_K:]).reshape(-1)
        row_asg = jnp.full((n_rows,), -1, jnp.int32).at[dest].set(
            jnp.arange(TOP_K * t, dtype=jnp.int32), unique_indices=True)
        n_chunks = d // (2 * LANES)
        row_tok = (jnp.maximum(row_asg, 0) // TOP_K) * n_chunks
        row_dst = jnp.where(row_asg < 0,
                            TOP_K * t + jnp.arange(n_rows, dtype=jnp.int32) % rb,
                            (row_asg % TOP_K) * t + row_asg // TOP_K) * n_chunks

        y2 = _moe(x1p, row_tok, row_dst, blk_e, n_used, w_gate, w_up, w_down, layer=i, rb=rb)
        x2d = _combine(y2, x1, route, ln_g[i, 1], ln_b[i, 1], tm=256, alpha=alpha)
    return x2d.reshape(bsz, seq, d)
```

```python
import functools
import math

import jax
import jax.numpy as jnp
import numpy as np
from jax import lax
from jax.experimental import pallas as pl
from jax.experimental.pallas import tpu as pltpu

MEM_HEADS = 4
MEM_HEAD_DIM = 128
MEM_W = MEM_HEADS * MEM_HEAD_DIM
HEAD_DIM = 64
N_KV_HEADS = 4
KV_W = N_KV_HEADS * HEAD_DIM
WINDOW = 128
CHUNK = 128
GM_GROUP_DIM = 128
N_GROUPS = 8
EXPERTS_PER_GROUP = 8
N_EXPERTS = N_GROUPS * EXPERTS_PER_GROUP
TOP_K = 2
LN_EPS = 1e-5
NEG_INF = -1e30

LANES = 128
VMEM_LIMIT_BYTES = 56 * 1024 * 1024

ROW_BLOCK = 256
BF16 = jnp.bfloat16
F32 = jnp.float32
U32 = jnp.uint32


def _alibi_slopes(n):
    def pow2(m):
        start = 2.0 ** (-8.0 / m)
        return [start ** (i + 1) for i in range(m)]
    if math.log2(n).is_integer():
        s = pow2(n)
    else:
        c = 2 ** math.floor(math.log2(n))
        s = pow2(c) + pow2(2 * c)[0::2][: n - c]
    return [float(v) for v in np.asarray(s, dtype=np.float32)]


def _params(n_axes=1):
    return pltpu.CompilerParams(dimension_semantics=("arbitrary",) * n_axes,
                                vmem_limit_bytes=VMEM_LIMIT_BYTES)


def _resident(shape):
    nd = len(shape)
    return pl.BlockSpec(shape, lambda *_: (0,) * nd, pipeline_mode=pl.Buffered(1))


def _pack_rows(ref, val, *lead):
    rows, d = val.shape
    nw = d // (2 * LANES)
    for c in range(nw):
        lo = val[:, c * LANES:(c + 1) * LANES].astype(BF16).astype(F32)
        hi = val[:, (c + nw) * LANES:(c + nw + 1) * LANES].astype(BF16).astype(F32)
        word = (lax.bitcast_convert_type(lo, U32) >> 16) | lax.bitcast_convert_type(hi, U32)
        ref[tuple(lead) + (pl.ds(c, rows, stride=nw), slice(None))] = word


def _unpack_rows(ref, nw, *lead):
    rows = ref.shape[-2] // nw
    lo, hi = [], []
    for c in range(nw):
        word = ref[tuple(lead) + (pl.ds(c, rows, stride=nw), slice(None))]
        lo.append(lax.bitcast_convert_type(word << 16, F32))
        hi.append(lax.bitcast_convert_type(word & jnp.uint32(0xFFFF0000), F32))
    return jnp.concatenate(lo + hi, axis=-1)


def _layer_norm(x, g, b):
    mu = jnp.mean(x, axis=-1, keepdims=True)
    xc = x - mu
    var = jnp.mean(xc * xc, axis=-1, keepdims=True)
    return xc * lax.rsqrt(var + LN_EPS) * g + b


def _proj_kernel(x_ref, w_ref, o_ref):
    o_ref[...] = jnp.dot(x_ref[...].astype(BF16), w_ref[...],
                         preferred_element_type=F32).astype(o_ref.dtype)


def _proj(x2d, w_bf, out_dtype, tm):
    m, k = x2d.shape
    n = w_bf.shape[1]
    return pl.pallas_call(
        _proj_kernel,
        grid=(m // tm,),
        in_specs=[pl.BlockSpec((tm, k), lambda i: (i, 0)), _resident((k, n))],
        out_specs=pl.BlockSpec((tm, n), lambda i: (i, 0)),
        out_shape=jax.ShapeDtypeStruct((m, n), out_dtype),
        compiler_params=_params(),
        name="proj",
    )(x2d, w_bf)


def _gelu(x):
    return 0.5 * x * (1.0 + lax.erf(x * (2.0 ** -0.5)))


def _gmlp_proj_kernel(x_ref, w_ref, g_ref, b_ref, o_ref, *, tok_w):
    z = jnp.dot(x_ref[...].astype(BF16), w_ref[...], preferred_element_type=F32)
    u = _gelu(z[:, :tok_w])
    v = _gelu(z[:, tok_w:2 * tok_w])
    v = _layer_norm(v, g_ref[...], b_ref[...])
    o_ref[:, :tok_w] = u.astype(o_ref.dtype)
    o_ref[:, tok_w:2 * tok_w] = v.astype(o_ref.dtype)
    o_ref[:, 2 * tok_w:] = z[:, 2 * tok_w:].astype(o_ref.dtype)


def _gmlp_proj(x2d, w_bf, norm_g, norm_b, tm):
    m, k = x2d.shape
    n = w_bf.shape[1]
    tok_w = norm_g.shape[-1]
    return pl.pallas_call(
        functools.partial(_gmlp_proj_kernel, tok_w=tok_w),
        grid=(m // tm,),
        in_specs=[pl.BlockSpec((tm, k), lambda i: (i, 0)), _resident((k, n)),
                  _resident((1, tok_w)), _resident((1, tok_w))],
        out_specs=pl.BlockSpec((tm, n), lambda i: (i, 0)),
        out_shape=jax.ShapeDtypeStruct((m, n), BF16),
        compiler_params=_params(),
        name="gmlp_proj",
    )(x2d, w_bf, norm_g.reshape(1, tok_w), norm_b.reshape(1, tok_w))


def _memory_attention(qm, kvm):
    outs = []
    for h in range(MEM_HEADS):
        q = qm[:, h * MEM_HEAD_DIM:(h + 1) * MEM_HEAD_DIM]
        k = kvm[:, h * MEM_HEAD_DIM:(h + 1) * MEM_HEAD_DIM]
        v = kvm[:, MEM_W + h * MEM_HEAD_DIM:MEM_W + (h + 1) * MEM_HEAD_DIM]
        s = lax.dot_general(q, k, (((1,), (1,)), ((), ())),
                            preferred_element_type=F32) * (MEM_HEAD_DIM ** -0.5)
        m = jnp.max(s, axis=-1, keepdims=True)
        p = jnp.exp(s - m)
        p = p * (1.0 / jnp.sum(p, axis=-1, keepdims=True))
        outs.append(jnp.dot(p.astype(BF16), v, preferred_element_type=F32))
    return jnp.concatenate(outs, axis=-1)


def _attn_kernel(sinks_ref, q_ref, kv_ref, kvp_ref, qm_ref, kvm_ref, o_ref, *,
                 tq, seq, slopes, tok_w):
    i = pl.program_id(0)
    n_sub = tq // WINDOW
    gqa = len(slopes) // N_KV_HEADS
    qi = lax.broadcasted_iota(jnp.int32, (WINDOW, WINDOW), 0)
    c = lax.broadcasted_iota(jnp.int32, (WINDOW, WINDOW), 1)
    own = c <= qi
    dist_f = jnp.where(own, qi - c, WINDOW + qi - c).astype(F32)
    nt = (((1,), (1,)), ((), ()))
    for sb in range(n_sub):
        r0 = sb * WINDOW
        if sb == 0:
            kv_prev = kvp_ref[...]
            reach = jnp.where(((i * tq) % seq) == 0, 0, WINDOW)
            valid = c <= qi + reach
        else:
            kv_prev = kv_ref[r0 - WINDOW:r0, :]
            valid = None
        kv_cur = kv_ref[r0:r0 + WINDOW, :]
        for kh in range(N_KV_HEADS):
            ks = slice(kh * HEAD_DIM, (kh + 1) * HEAD_DIM)
            vs = slice(KV_W + kh * HEAD_DIM, KV_W + (kh + 1) * HEAD_DIM)
            heads = range(kh * gqa, (kh + 1) * gqa)
            q = jnp.concatenate(
                [q_ref[r0:r0 + WINDOW, h * HEAD_DIM:(h + 1) * HEAD_DIM] for h in heads], axis=0)
            s_own = lax.dot_general(q, kv_cur[:, ks], nt, preferred_element_type=F32)
            s_prev = lax.dot_general(q, kv_prev[:, ks], nt, preferred_element_type=F32)
            p_own, p_prev = [], []
            for g, h in enumerate(heads):
                rows = slice(g * WINDOW, (g + 1) * WINDOW)
                logits = (jnp.where(own, s_own[rows], s_prev[rows]) * (HEAD_DIM ** -0.5)
                          - slopes[h] * dist_f)
                if valid is not None:
                    logits = jnp.where(valid, logits, NEG_INF)
                sink = sinks_ref[h]
                m = jnp.maximum(jnp.max(logits, axis=-1, keepdims=True), sink)
                p = jnp.exp(logits - m)
                probs = p * (1.0 / (jnp.sum(p, axis=-1, keepdims=True) + jnp.exp(sink - m)))
                p_own.append(jnp.where(own, probs, 0.0).astype(BF16))
                p_prev.append(jnp.where(own, 0.0, probs).astype(BF16))
            out = (jnp.dot(jnp.concatenate(p_own, axis=0), kv_cur[:, vs],
                           preferred_element_type=F32)
                   + jnp.dot(jnp.concatenate(p_prev, axis=0), kv_prev[:, vs],
                             preferred_element_type=F32))
            o_ref[r0:r0 + WINDOW, kh * gqa * HEAD_DIM:(kh + 1) * gqa * HEAD_DIM] = jnp.concatenate(
                [out[g * WINDOW:(g + 1) * WINDOW] for g in range(gqa)], axis=1).astype(o_ref.dtype)
    o_ref[:, tok_w:] = _memory_attention(qm_ref[...], kvm_ref[...]).astype(o_ref.dtype)


def _attn_mixer(proj, kvm, sinks, *, seq, mem_len, tq):
    t = proj.shape[0]
    tok_w = proj.shape[1] - 2 * KV_W - MEM_W
    n_heads = tok_w // HEAD_DIM
    assert tok_w % (2 * KV_W) == 0 and tok_w % MEM_W == 0 and seq % tq == 0
    kv_col = tok_w // (2 * KV_W)
    qm_col = (tok_w + 2 * KV_W) // MEM_W
    sub = tq // WINDOW
    kern = functools.partial(_attn_kernel, tq=tq, seq=seq, slopes=_alibi_slopes(n_heads),
                             tok_w=tok_w)
    return pl.pallas_call(
        kern,
        grid=(t // tq,),
        in_specs=[
            pl.BlockSpec(memory_space=pltpu.SMEM),
            pl.BlockSpec((tq, tok_w), lambda i: (i, 0)),
            pl.BlockSpec((tq, 2 * KV_W), lambda i: (i, kv_col)),
            pl.BlockSpec((WINDOW, 2 * KV_W), lambda i: (jnp.maximum(i * sub - 1, 0), kv_col)),
            pl.BlockSpec((tq, MEM_W), lambda i: (i, qm_col)),
            pl.BlockSpec((mem_len, 2 * MEM_W), lambda i: ((i * tq) // seq, 0)),
        ],
        out_specs=pl.BlockSpec((tq, tok_w + MEM_W), lambda i: (i, 0)),
        out_shape=jax.ShapeDtypeStruct((t, tok_w + MEM_W), BF16),
        compiler_params=_params(),
        name="attn_mixer",
    )(sinks, proj, proj, proj, proj, kvm)


def _gmlp_kernel(u_ref, v_ref, qm_ref, kvm_ref, ws_ref, bs_ref, o_ref, *, tq, tok_w):
    n_sub = tq // CHUNK
    n_grp = tok_w // GM_GROUP_DIM
    row = lax.broadcasted_iota(jnp.int32, (CHUNK, CHUNK), 0)
    col = lax.broadcasted_iota(jnp.int32, (CHUNK, CHUNK), 1)
    lower = row >= col
    for g in range(n_grp):
        w = jnp.where(lower, ws_ref[g], 0.0).astype(BF16)
        bias = bs_ref[:, g:g + 1]
        c0 = g * GM_GROUP_DIM
        for sb in range(n_sub):
            r0 = sb * CHUNK
            sv = jnp.dot(w, v_ref[r0:r0 + CHUNK, c0:c0 + GM_GROUP_DIM],
                         preferred_element_type=F32) + bias
            u = u_ref[r0:r0 + CHUNK, c0:c0 + GM_GROUP_DIM].astype(F32)
            o_ref[r0:r0 + CHUNK, c0:c0 + GM_GROUP_DIM] = (u * sv).astype(o_ref.dtype)
    o_ref[:, tok_w:] = _memory_attention(qm_ref[...], kvm_ref[...]).astype(o_ref.dtype)


def _gmlp_mixer(proj, kvm, w_s, b_s, *, seq, mem_len, tq):
    t = proj.shape[0]
    tok_w = (proj.shape[1] - MEM_W) // 2
    n_grp = tok_w // GM_GROUP_DIM
    assert (2 * tok_w) % MEM_W == 0 and seq % tq == 0
    kern = functools.partial(_gmlp_kernel, tq=tq, tok_w=tok_w)
    return pl.pallas_call(
        kern,
        grid=(t // tq,),
        in_specs=[
            pl.BlockSpec((tq, tok_w), lambda i: (i, 0)),
            pl.BlockSpec((tq, tok_w), lambda i: (i, 1)),
            pl.BlockSpec((tq, MEM_W), lambda i: (i, (2 * tok_w) // MEM_W)),
            pl.BlockSpec((mem_len, 2 * MEM_W), lambda i: ((i * tq) // seq, 0)),
            _resident((n_grp, CHUNK, CHUNK)),
            _resident((CHUNK, n_grp)),
        ],
        out_specs=pl.BlockSpec((tq, tok_w + MEM_W), lambda i: (i, 0)),
        out_shape=jax.ShapeDtypeStruct((t, tok_w + MEM_W), BF16),
        compiler_params=_params(),
        name="gmlp_mixer",
    )(proj, proj, proj, kvm, w_s, b_s.T)


def _post_kernel(mix_ref, x_ref, wo_ref, g_ref, b_ref, wrh_ref, wrl_ref, br_ref,
                 x1_ref, xp_ref, route_ref, cnt_ref, carry_ref, *, tm, alpha):
    i = pl.program_id(0)

    @pl.when(i == 0)
    def _():
        carry_ref[...] = jnp.zeros_like(carry_ref)

    y = jnp.dot(mix_ref[...], wo_ref[...], preferred_element_type=F32)
    x1 = _layer_norm(alpha * x_ref[...] + y, g_ref[...], b_ref[...])
    x1_ref[...] = x1
    _pack_rows(xp_ref, x1)

    xh = x1.astype(BF16)
    xl = (x1 - xh.astype(F32)).astype(BF16)
    logits = (jnp.dot(xh, wrh_ref[...], preferred_element_type=F32)
              + (jnp.dot(xh, wrl_ref[...], preferred_element_type=F32)
                 + jnp.dot(xl, wrh_ref[...], preferred_element_type=F32))) + br_ref[...]

    lane = lax.broadcasted_iota(jnp.int32, (tm, LANES), 1)
    lane_f = lane.astype(F32)
    big = float(LANES)
    is_grp = lane < N_GROUPS
    gl = jnp.where(is_grp, logits, NEG_INF)
    gmax = jnp.max(gl, axis=-1, keepdims=True)
    g_sel = jnp.min(jnp.where(gl == gmax, lane_f, big), axis=-1, keepdims=True)
    p_grp = 1.0 / jnp.sum(jnp.where(is_grp, jnp.exp(gl - gmax), 0.0), axis=-1, keepdims=True)

    e_lo = N_GROUPS + g_sel * EXPERTS_PER_GROUP
    in_grp = (lane_f >= e_lo) & (lane_f < e_lo + EXPERTS_PER_GROUP)
    el = jnp.where(in_grp, logits, NEG_INF)
    m1 = jnp.max(el, axis=-1, keepdims=True)
    i1 = jnp.min(jnp.where(el == m1, lane_f, big), axis=-1, keepdims=True)
    el2 = jnp.where(lane_f == i1, NEG_INF, el)
    m2 = jnp.max(el2, axis=-1, keepdims=True)
    i2 = jnp.min(jnp.where(el2 == m2, lane_f, big), axis=-1, keepdims=True)
    z = jnp.sum(jnp.where(in_grp, jnp.exp(el - m1), 0.0), axis=-1, keepdims=True)
    tp1 = 1.0 / z
    tp2 = jnp.exp(m2 - m1) / z
    gate1 = p_grp * tp1 / (tp1 + tp2)
    gate2 = p_grp * tp2 / (tp1 + tp2)
    e1 = i1 - N_GROUPS
    e2 = i2 - N_GROUPS

    oh1 = lane_f == e1
    oh2 = lane_f == e2
    c = jnp.where(oh1 | oh2, 1.0, 0.0)
    r = lax.broadcasted_iota(jnp.int32, (tm, tm), 0)
    s = lax.broadcasted_iota(jnp.int32, (tm, tm), 1)
    tri = jnp.where(r > s, 1.0, 0.0).astype(BF16)
    prefix = jnp.dot(tri, c.astype(BF16), preferred_element_type=F32) + carry_ref[0:1, :]
    rank1 = jnp.sum(jnp.where(oh1, prefix, 0.0), axis=-1, keepdims=True)
    rank2 = jnp.sum(jnp.where(oh2, prefix, 0.0), axis=-1, keepdims=True)
    new_carry = carry_ref[0:1, :] + jnp.sum(c, axis=0, keepdims=True)
    carry_ref[...] = jnp.broadcast_to(new_carry, carry_ref.shape)
    cnt_ref[...] = jnp.broadcast_to(new_carry, cnt_ref.shape)

    route = jnp.where(lane == 0, e1, 0.0)
    route = jnp.where(lane == 1, e2, route)
    route = jnp.where(lane == 2, rank1, route)
    route = jnp.where(lane == 3, rank2, route)
    route = jnp.where(lane == 4, gate1, route)
    route = jnp.where(lane == 5, gate2, route)
    route_ref[...] = route


def _post(mix, x2d, wo_bf, ln_g, ln_b, wr_hi, wr_lo, b_r, *, tm, alpha):
    t, d = x2d.shape
    kern = functools.partial(_post_kernel, tm=tm, alpha=alpha)
    return pl.pallas_call(
        kern,
        grid=(t // tm,),
        in_specs=[
            pl.BlockSpec((tm, d), lambda i: (i, 0)),
            pl.BlockSpec((tm, d), lambda i: (i, 0)),
            _resident((d, d)),
            _resident((1, d)), _resident((1, d)),
            _resident((d, LANES)), _resident((d, LANES)), _resident((1, LANES)),
        ],
        out_specs=[
            pl.BlockSpec((tm, d), lambda i: (i, 0)),
            pl.BlockSpec((tm * (d // (2 * LANES)), LANES), lambda i: (i, 0)),
            pl.BlockSpec((tm, LANES), lambda i: (i, 0)),
            pl.BlockSpec((8, LANES), lambda i: (0, 0)),
        ],
        out_shape=[
            jax.ShapeDtypeStruct((t, d), F32),
            jax.ShapeDtypeStruct((t * (d // (2 * LANES)), LANES), U32),
            jax.ShapeDtypeStruct((t, LANES), F32),
            jax.ShapeDtypeStruct((8, LANES), F32),
        ],
        scratch_shapes=[pltpu.VMEM((8, LANES), F32)],
        compiler_params=_params(),
        name="post",
    )(mix, x2d, wo_bf, ln_g.reshape(1, d), ln_b.reshape(1, d), wr_hi, wr_lo, b_r)


def _moe_kernel(blk_e_ref, nxt_e_ref, n_used_ref, tok0_ref, tokn_ref, dstp_ref,
                x_hbm, wg_hbm, wu_hbm, wd_hbm, y_hbm,
                xbuf, ybuf, wg_st, wu_st, wd_st, wg_bf, wu_bf, wd_bf, sem_g, sem_s, sem_w,
                *, rb, nc, dump_row, nblk, layer):
    s = pl.program_id(0)
    n_used = n_used_ref[0]
    slot = lax.rem(s, 2)
    other = 1 - slot

    def gather_row(tok_ref, j, dst_slot):
        src = x_hbm.at[pl.ds(pl.multiple_of(tok_ref[j], nc), nc)]
        pltpu.make_async_copy(src, xbuf.at[dst_slot, pl.ds(j * nc, nc)],
                              sem_g.at[dst_slot]).start()

    def scatter_row(j, src_slot):
        dst = y_hbm.at[pl.ds(pl.multiple_of(dstp_ref[j], nc), nc)]
        pltpu.make_async_copy(ybuf.at[src_slot, pl.ds(j * nc, nc)], dst, sem_s).start()

    def wait_gather(sl):
        pltpu.make_async_copy(x_hbm.at[pl.ds(0, rb * nc)], xbuf.at[sl], sem_g.at[sl]).wait()

    def wait_scatter():
        pltpu.make_async_copy(ybuf.at[0], y_hbm.at[pl.ds(0, rb * nc)], sem_s).wait()

    weight_copies = ((wg_hbm, wg_st, 0), (wu_hbm, wu_st, 1), (wd_hbm, wd_st, 2))

    def fetch_weights(e):
        for hbm, stage, sem_i in weight_copies:
            pltpu.make_async_copy(hbm.at[layer, e], stage, sem_w.at[sem_i]).start()

    @pl.when(s == 0)
    def _():
        ybuf[...] = jnp.zeros_like(ybuf)
        pltpu.make_async_copy(ybuf.at[1], y_hbm.at[pl.ds(dump_row * nc, rb * nc)], sem_s).start()

        def first(j, carry):
            gather_row(tok0_ref, j, 0)
            return carry
        lax.fori_loop(0, rb, first, 0, unroll=8)

        fetch_weights(blk_e_ref[0])

    b_cur = jnp.minimum(s, nblk - 1)
    e_cur = blk_e_ref[b_cur]
    new_expert = (s == 0) | (e_cur != blk_e_ref[jnp.maximum(b_cur - 1, 0)])

    @pl.when((s < n_used) & new_expert)
    def _():
        for hbm, stage, sem_i in weight_copies:
            pltpu.make_async_copy(hbm.at[layer, 0], stage, sem_w.at[sem_i]).wait()
        wg_bf[...] = wg_st[...].astype(BF16)
        wu_bf[...] = wu_st[...].astype(BF16)
        wd_bf[...] = wd_st[...].astype(BF16)
        e_next = nxt_e_ref[b_cur]

        @pl.when(e_next != e_cur)
        def _():
            fetch_weights(e_next)

    @pl.when(s < n_used)
    def _():
        wait_gather(slot)
        wait_scatter()
        for j in range(rb):
            scatter_row(j, other)
            gather_row(tokn_ref, j, other)
        xb = _unpack_rows(xbuf, nc, slot).astype(BF16)
        hg = jnp.dot(xb, wg_bf[...], preferred_element_type=F32)
        hu = jnp.dot(xb, wu_bf[...], preferred_element_type=F32)
        h = (jax.nn.silu(hg) * hu).astype(BF16)
        _pack_rows(ybuf, jnp.dot(h, wd_bf[...], preferred_element_type=F32), slot)

    @pl.when(s == n_used)
    def _():
        wait_gather(slot)
        wait_scatter()

        def last(j, carry):
            scatter_row(j, other)
            return carry
        lax.fori_loop(0, rb, last, 0, unroll=8)
        wait_scatter()


def _moe(x1, row_tok, row_dst, blk_e, nxt_e, n_used, w_gate, w_up, w_down, *, layer, rb):
    d_e, d = w_down.shape[-2:]
    n_chunks = d // (2 * LANES)
    t = x1.shape[0] // n_chunks
    nblk = row_tok.shape[0] // rb

    def smem(index_map):
        return pl.BlockSpec((rb,), index_map, memory_space=pltpu.SMEM)

    any_spec = pl.BlockSpec(memory_space=pl.ANY)
    grid_spec = pltpu.PrefetchScalarGridSpec(
        num_scalar_prefetch=3,
        grid=(nblk + 1,),
        in_specs=[
            smem(lambda s, *_: (0,)),
            smem(lambda s, *_: (jnp.minimum(s + 1, nblk - 1),)),
            smem(lambda s, *_: (jnp.clip(s - 1, 0, nblk - 1),)),
            any_spec, any_spec, any_spec, any_spec,
        ],
        out_specs=any_spec,
        scratch_shapes=[pltpu.VMEM((2, rb * n_chunks, LANES), U32),
                        pltpu.VMEM((2, rb * n_chunks, LANES), U32),
                        pltpu.VMEM((d, d_e), F32), pltpu.VMEM((d, d_e), F32),
                        pltpu.VMEM((d_e, d), F32),
                        pltpu.VMEM((d, d_e), BF16), pltpu.VMEM((d, d_e), BF16),
                        pltpu.VMEM((d_e, d), BF16),
                        pltpu.SemaphoreType.DMA((2,)), pltpu.SemaphoreType.DMA(()),
                        pltpu.SemaphoreType.DMA((3,))],
    )
    return pl.pallas_call(
        functools.partial(_moe_kernel, rb=rb, nc=n_chunks, dump_row=TOP_K * t, nblk=nblk,
                          layer=layer),
        grid_spec=grid_spec,
        out_shape=jax.ShapeDtypeStruct(((TOP_K * t + rb) * n_chunks, LANES), U32),
        compiler_params=_params(),
        name="moe",
    )(blk_e, nxt_e, n_used, row_tok, row_tok, row_dst, x1, w_gate, w_up, w_down)


def _combine_kernel(x1_ref, y0_ref, y1_ref, route_ref, g_ref, b_ref, o_ref, *, alpha):
    route = route_ref[...]
    nw = o_ref.shape[1] // (2 * LANES)
    y = _unpack_rows(y0_ref, nw) * route[:, 4:5] + _unpack_rows(y1_ref, nw) * route[:, 5:6]
    o_ref[...] = _layer_norm(alpha * x1_ref[...] + y, g_ref[...], b_ref[...])


def _combine(y2, x1, route, ln_g, ln_b, *, tm, alpha):
    t, d = x1.shape
    n_chunks = d // (2 * LANES)
    return pl.pallas_call(
        functools.partial(_combine_kernel, alpha=alpha),
        grid=(t // tm,),
        in_specs=[pl.BlockSpec((tm, d), lambda i: (i, 0)),
                  pl.BlockSpec((tm * n_chunks, LANES), lambda i: (i, 0)),
                  pl.BlockSpec((tm * n_chunks, LANES), lambda i: (t // tm + i, 0)),
                  pl.BlockSpec((tm, LANES), lambda i: (i, 0)),
                  _resident((1, d)), _resident((1, d))],
        out_specs=pl.BlockSpec((tm, d), lambda i: (i, 0)),
        out_shape=jax.ShapeDtypeStruct((t, d), F32),
        compiler_params=_params(),
        name="combine",
    )(x1, y2, y2, route, ln_g.reshape(1, d), ln_b.reshape(1, d))


def _split_bf16(w):
    hi = w.astype(BF16)
    lo = (w - hi.astype(F32)).astype(BF16)
    return hi, lo


def kernel(x, mem, a_w_in, a_sinks, b_w_in, b_norm_g, b_norm_b, b_w_spatial, b_b_spatial,
           w_mem_kv, w_out, ln_g, ln_b, w_router_group, b_router_group,
           w_router_expert, b_router_expert, w_gate, w_up, w_down):
    bsz, seq, d = x.shape
    mem_len = mem.shape[1]
    depth = w_out.shape[0]
    t = bsz * seq
    alpha = (2 * depth) ** 0.25
    rb = ROW_BLOCK
    n_rows = -(-(t * TOP_K + N_EXPERTS * (rb - 1)) // rb) * rb
    nblk = n_rows // rb

    x2d = x.reshape(t, d)
    mem2d = mem.reshape(bsz * mem_len, d)
    for i in range(depth):
        j = i // 2
        kvm = _proj(mem2d, w_mem_kv[i].astype(BF16), BF16, tm=min(256, bsz * mem_len))
        if i % 2 == 0:
            proj = _proj(x2d, a_w_in[j].astype(BF16), BF16, tm=512)
            mix = _attn_mixer(proj, kvm, a_sinks[j], seq=seq, mem_len=mem_len, tq=256)
        else:
            proj = _gmlp_proj(x2d, b_w_in[j].astype(BF16), b_norm_g[j], b_norm_b[j], tm=256)
            mix = _gmlp_mixer(proj, kvm, b_w_spatial[j], b_b_spatial[j],
                              seq=seq, mem_len=mem_len, tq=256)

        w_r = jnp.concatenate([w_router_group[i], w_router_expert[i]], axis=1)
        w_r = jnp.pad(w_r, ((0, 0), (0, LANES - w_r.shape[1])))
        b_r = jnp.concatenate([b_router_group[i], b_router_expert[i]])
        b_r = jnp.pad(b_r, (0, LANES - b_r.shape[0])).reshape(1, LANES)
        wr_hi, wr_lo = _split_bf16(w_r)
        x1, x1p, route, cnt = _post(mix, x2d, w_out[i].astype(BF16), ln_g[i, 0], ln_b[i, 0],
                                    wr_hi, wr_lo, b_r, tm=256, alpha=alpha)

        counts = cnt[0, :N_EXPERTS].astype(jnp.int32)
        padded = (counts + rb - 1) // rb * rb
        pend = jnp.cumsum(padded)
        pstart = (pend - padded).astype(jnp.int32)
        n_used = (pend[-1] // rb).astype(jnp.int32).reshape(1)
        blk = jnp.arange(nblk, dtype=jnp.int32)
        blk_e = jnp.searchsorted(pend, jnp.minimum(blk, n_used[0] - 1) * rb,
                                 side='right').astype(jnp.int32)
        nxt_blk = pend[blk_e] // rb
        nxt_e = jnp.where(nxt_blk < n_used[0], blk_e[jnp.minimum(nxt_blk, nblk - 1)], blk_e)
        ri = route[:, :4].astype(jnp.int32)
        e_hot = ri[:, :TOP_K, None] == jnp.arange(N_EXPERTS, dtype=jnp.int32)
        dest = (jnp.sum(jnp.where(e_hot, pstart, 0), axis=-1) + ri[:, TOP_K:]).reshape(-1)
        row_asg = jnp.full((n_rows,), -1, jnp.int32).at[dest].set(
            jnp.arange(TOP_K * t, dtype=jnp.int32), unique_indices=True)
        n_chunks = d // (2 * LANES)
        row_tok = (jnp.maximum(row_asg, 0) // TOP_K) * n_chunks
        row_dst = jnp.where(row_asg < 0,
                            TOP_K * t + jnp.arange(n_rows, dtype=jnp.int32) % rb,
                            (row_asg % TOP_K) * t + row_asg // TOP_K) * n_chunks

        y2 = _moe(x1p, row_tok, row_dst, blk_e, nxt_e.astype(jnp.int32), n_used,
                  w_gate, w_up, w_down, layer=i, rb=rb)
        x2d = _combine(y2, x1, route, ln_g[i, 1], ln_b[i, 1], tm=256, alpha=alpha)
    return x2d.reshape(bsz, seq, d)
```

```python
import functools
import math

import jax
import jax.numpy as jnp
import numpy as np
from jax import lax
from jax.experimental import pallas as pl
from jax.experimental.pallas import tpu as pltpu

MEM_HEADS = 4
MEM_HEAD_DIM = 128
MEM_W = MEM_HEADS * MEM_HEAD_DIM
HEAD_DIM = 64
N_KV_HEADS = 4
KV_W = N_KV_HEADS * HEAD_DIM
WINDOW = 128
CHUNK = 128
GM_GROUP_DIM = 128
N_GROUPS = 8
EXPERTS_PER_GROUP = 8
N_EXPERTS = N_GROUPS * EXPERTS_PER_GROUP
TOP_K = 2
LN_EPS = 1e-5
NEG_INF = -1e30

LANES = 128
VMEM_LIMIT_BYTES = 56 * 1024 * 1024

ROW_BLOCK = 256
BF16 = jnp.bfloat16
F32 = jnp.float32
U32 = jnp.uint32


def _alibi_slopes(n):
    def pow2(m):
        start = 2.0 ** (-8.0 / m)
        return [start ** (i + 1) for i in range(m)]
    if math.log2(n).is_integer():
        s = pow2(n)
    else:
        c = 2 ** math.floor(math.log2(n))
        s = pow2(c) + pow2(2 * c)[0::2][: n - c]
    return [float(v) for v in np.asarray(s, dtype=np.float32)]


def _params(n_axes=1):
    return pltpu.CompilerParams(dimension_semantics=("arbitrary",) * n_axes,
                                vmem_limit_bytes=VMEM_LIMIT_BYTES)


def _resident(shape):
    nd = len(shape)
    return pl.BlockSpec(shape, lambda *_: (0,) * nd, pipeline_mode=pl.Buffered(1))


def _pack_rows(ref, val, *lead):
    rows, d = val.shape
    nw = d // (2 * LANES)
    for c in range(nw):
        lo = val[:, c * LANES:(c + 1) * LANES].astype(BF16).astype(F32)
        hi = val[:, (c + nw) * LANES:(c + nw + 1) * LANES].astype(BF16).astype(F32)
        word = (lax.bitcast_convert_type(lo, U32) >> 16) | lax.bitcast_convert_type(hi, U32)
        ref[tuple(lead) + (pl.ds(c, rows, stride=nw), slice(None))] = word


def _unpack_rows(ref, nw, *lead):
    rows = ref.shape[-2] // nw
    lo, hi = [], []
    for c in range(nw):
        word = ref[tuple(lead) + (pl.ds(c, rows, stride=nw), slice(None))]
        lo.append(lax.bitcast_convert_type(word << 16, F32))
        hi.append(lax.bitcast_convert_type(word & jnp.uint32(0xFFFF0000), F32))
    return jnp.concatenate(lo + hi, axis=-1)


def _layer_norm(x, g, b):
    mu = jnp.mean(x, axis=-1, keepdims=True)
    xc = x - mu
    var = jnp.mean(xc * xc, axis=-1, keepdims=True)
    return xc * lax.rsqrt(var + LN_EPS) * g + b


def _proj_kernel(x_ref, w_ref, o_ref):
    o_ref[...] = jnp.dot(x_ref[...].astype(BF16), w_ref[...],
                         preferred_element_type=F32).astype(o_ref.dtype)


def _proj(x2d, w_bf, out_dtype, tm):
    m, k = x2d.shape
    n = w_bf.shape[1]
    return pl.pallas_call(
        _proj_kernel,
        grid=(m // tm,),
        in_specs=[pl.BlockSpec((tm, k), lambda i: (i, 0)), _resident((k, n))],
        out_specs=pl.BlockSpec((tm, n), lambda i: (i, 0)),
        out_shape=jax.ShapeDtypeStruct((m, n), out_dtype),
        compiler_params=_params(),
        name="proj",
    )(x2d, w_bf)


def _gelu(x):
    return 0.5 * x * (1.0 + lax.erf(x * (2.0 ** -0.5)))


def _gmlp_proj_kernel(x_ref, w_ref, g_ref, b_ref, o_ref, *, tok_w):
    z = jnp.dot(x_ref[...].astype(BF16), w_ref[...], preferred_element_type=F32)
    u = _gelu(z[:, :tok_w])
    v = _gelu(z[:, tok_w:2 * tok_w])
    v = _layer_norm(v, g_ref[...], b_ref[...])
    o_ref[:, :tok_w] = u.astype(o_ref.dtype)
    o_ref[:, tok_w:2 * tok_w] = v.astype(o_ref.dtype)
    o_ref[:, 2 * tok_w:] = z[:, 2 * tok_w:].astype(o_ref.dtype)


def _gmlp_proj(x2d, w_bf, norm_g, norm_b, tm):
    m, k = x2d.shape
    n = w_bf.shape[1]
    tok_w = norm_g.shape[-1]
    return pl.pallas_call(
        functools.partial(_gmlp_proj_kernel, tok_w=tok_w),
        grid=(m // tm,),
        in_specs=[pl.BlockSpec((tm, k), lambda i: (i, 0)), _resident((k, n)),
                  _resident((1, tok_w)), _resident((1, tok_w))],
        out_specs=pl.BlockSpec((tm, n), lambda i: (i, 0)),
        out_shape=jax.ShapeDtypeStruct((m, n), BF16),
        compiler_params=_params(),
        name="gmlp_proj",
    )(x2d, w_bf, norm_g.reshape(1, tok_w), norm_b.reshape(1, tok_w))


def _memory_attention(qm, kvm):
    outs = []
    for h in range(MEM_HEADS):
        q = qm[:, h * MEM_HEAD_DIM:(h + 1) * MEM_HEAD_DIM]
        k = kvm[:, h * MEM_HEAD_DIM:(h + 1) * MEM_HEAD_DIM]
        v = kvm[:, MEM_W + h * MEM_HEAD_DIM:MEM_W + (h + 1) * MEM_HEAD_DIM]
        s = lax.dot_general(q, k, (((1,), (1,)), ((), ())),
                            preferred_element_type=F32) * (MEM_HEAD_DIM ** -0.5)
        m = jnp.max(s, axis=-1, keepdims=True)
        p = jnp.exp(s - m)
        p = p * (1.0 / jnp.sum(p, axis=-1, keepdims=True))
        outs.append(jnp.dot(p.astype(BF16), v, preferred_element_type=F32))
    return jnp.concatenate(outs, axis=-1)


def _attn_kernel(sinks_ref, q_ref, kv_ref, kvp_ref, qm_ref, kvm_ref, o_ref, *,
                 tq, seq, slopes, tok_w):
    i = pl.program_id(0)
    n_sub = tq // WINDOW
    gqa = len(slopes) // N_KV_HEADS
    qi = lax.broadcasted_iota(jnp.int32, (WINDOW, WINDOW), 0)
    c = lax.broadcasted_iota(jnp.int32, (WINDOW, WINDOW), 1)
    own = c <= qi
    dist_f = jnp.where(own, qi - c, WINDOW + qi - c).astype(F32)
    nt = (((1,), (1,)), ((), ()))
    for sb in range(n_sub):
        r0 = sb * WINDOW
        if sb == 0:
            kv_prev = kvp_ref[...]
            reach = jnp.where(((i * tq) % seq) == 0, 0, WINDOW)
            valid = c <= qi + reach
        else:
            kv_prev = kv_ref[r0 - WINDOW:r0, :]
            valid = None
        kv_cur = kv_ref[r0:r0 + WINDOW, :]
        for kh in range(N_KV_HEADS):
            ks = slice(kh * HEAD_DIM, (kh + 1) * HEAD_DIM)
            vs = slice(KV_W + kh * HEAD_DIM, KV_W + (kh + 1) * HEAD_DIM)
            heads = range(kh * gqa, (kh + 1) * gqa)
            q = jnp.concatenate(
                [q_ref[r0:r0 + WINDOW, h * HEAD_DIM:(h + 1) * HEAD_DIM] for h in heads], axis=0)
            s_own = lax.dot_general(q, kv_cur[:, ks], nt, preferred_element_type=F32)
            s_prev = lax.dot_general(q, kv_prev[:, ks], nt, preferred_element_type=F32)
            p_own, p_prev = [], []
            for g, h in enumerate(heads):
                rows = slice(g * WINDOW, (g + 1) * WINDOW)
                logits = (jnp.where(own, s_own[rows], s_prev[rows]) * (HEAD_DIM ** -0.5)
                          - slopes[h] * dist_f)
                if valid is not None:
                    logits = jnp.where(valid, logits, NEG_INF)
                sink = sinks_ref[h]
                m = jnp.maximum(jnp.max(logits, axis=-1, keepdims=True), sink)
                p = jnp.exp(logits - m)
                probs = p * (1.0 / (jnp.sum(p, axis=-1, keepdims=True) + jnp.exp(sink - m)))
                p_own.append(jnp.where(own, probs, 0.0).astype(BF16))
                p_prev.append(jnp.where(own, 0.0, probs).astype(BF16))
            out = (jnp.dot(jnp.concatenate(p_own, axis=0), kv_cur[:, vs],
                           preferred_element_type=F32)
                   + jnp.dot(jnp.concatenate(p_prev, axis=0), kv_prev[:, vs],
                             preferred_element_type=F32))
            o_ref[r0:r0 + WINDOW, kh * gqa * HEAD_DIM:(kh + 1) * gqa * HEAD_DIM] = jnp.concatenate(
                [out[g * WINDOW:(g + 1) * WINDOW] for g in range(gqa)], axis=1).astype(o_ref.dtype)
    o_ref[:, tok_w:] = _memory_attention(qm_ref[...], kvm_ref[...]).astype(o_ref.dtype)


def _attn_mixer(proj, kvm, sinks, *, seq, mem_len, tq):
    t = proj.shape[0]
    tok_w = proj.shape[1] - 2 * KV_W - MEM_W
    n_heads = tok_w // HEAD_DIM
    assert tok_w % (2 * KV_W) == 0 and tok_w % MEM_W == 0 and seq % tq == 0
    kv_col = tok_w // (2 * KV_W)
    qm_col = (tok_w + 2 * KV_W) // MEM_W
    sub = tq // WINDOW
    kern = functools.partial(_attn_kernel, tq=tq, seq=seq, slopes=_alibi_slopes(n_heads),
                             tok_w=tok_w)
    return pl.pallas_call(
        kern,
        grid=(t // tq,),
        in_specs=[
            pl.BlockSpec(memory_space=pltpu.SMEM),
            pl.BlockSpec((tq, tok_w), lambda i: (i, 0)),
            pl.BlockSpec((tq, 2 * KV_W), lambda i: (i, kv_col)),
            pl.BlockSpec((WINDOW, 2 * KV_W), lambda i: (jnp.maximum(i * sub - 1, 0), kv_col)),
            pl.BlockSpec((tq, MEM_W), lambda i: (i, qm_col)),
            pl.BlockSpec((mem_len, 2 * MEM_W), lambda i: ((i * tq) // seq, 0)),
        ],
        out_specs=pl.BlockSpec((tq, tok_w + MEM_W), lambda i: (i, 0)),
        out_shape=jax.ShapeDtypeStruct((t, tok_w + MEM_W), BF16),
        compiler_params=_params(),
        name="attn_mixer",
    )(sinks, proj, proj, proj, proj, kvm)


def _gmlp_kernel(u_ref, v_ref, qm_ref, kvm_ref, ws_ref, bs_ref, o_ref, *, tq, tok_w):
    n_sub = tq // CHUNK
    n_grp = tok_w // GM_GROUP_DIM
    row = lax.broadcasted_iota(jnp.int32, (CHUNK, CHUNK), 0)
    col = lax.broadcasted_iota(jnp.int32, (CHUNK, CHUNK), 1)
    lower = row >= col
    for g in range(n_grp):
        w = jnp.where(lower, ws_ref[g], 0.0).astype(BF16)
        bias = bs_ref[:, g:g + 1]
        c0 = g * GM_GROUP_DIM
        for sb in range(n_sub):
            r0 = sb * CHUNK
            sv = jnp.dot(w, v_ref[r0:r0 + CHUNK, c0:c0 + GM_GROUP_DIM],
                         preferred_element_type=F32) + bias
            u = u_ref[r0:r0 + CHUNK, c0:c0 + GM_GROUP_DIM].astype(F32)
            o_ref[r0:r0 + CHUNK, c0:c0 + GM_GROUP_DIM] = (u * sv).astype(o_ref.dtype)
    o_ref[:, tok_w:] = _memory_attention(qm_ref[...], kvm_ref[...]).astype(o_ref.dtype)


def _gmlp_mixer(proj, kvm, w_s, b_s, *, seq, mem_len, tq):
    t = proj.shape[0]
    tok_w = (proj.shape[1] - MEM_W) // 2
    n_grp = tok_w // GM_GROUP_DIM
    assert (2 * tok_w) % MEM_W == 0 and seq % tq == 0
    kern = functools.partial(_gmlp_kernel, tq=tq, tok_w=tok_w)
    return pl.pallas_call(
        kern,
        grid=(t // tq,),
        in_specs=[
            pl.BlockSpec((tq, tok_w), lambda i: (i, 0)),
            pl.BlockSpec((tq, tok_w), lambda i: (i, 1)),
            pl.BlockSpec((tq, MEM_W), lambda i: (i, (2 * tok_w) // MEM_W)),
            pl.BlockSpec((mem_len, 2 * MEM_W), lambda i: ((i * tq) // seq, 0)),
            _resident((n_grp, CHUNK, CHUNK)),
            _resident((CHUNK, n_grp)),
        ],
        out_specs=pl.BlockSpec((tq, tok_w + MEM_W), lambda i: (i, 0)),
        out_shape=jax.ShapeDtypeStruct((t, tok_w + MEM_W), BF16),
        compiler_params=_params(),
        name="gmlp_mixer",
    )(proj, proj, proj, kvm, w_s, b_s.T)


def _post_kernel(mix_ref, x_ref, wo_ref, g_ref, b_ref, wrh_ref, wrl_ref, br_ref,
                 x1_ref, xp_ref, route_ref, cnt_ref, carry_ref, *, tm, alpha):
    i = pl.program_id(0)

    @pl.when(i == 0)
    def _():
        carry_ref[...] = jnp.zeros_like(carry_ref)

    y = jnp.dot(mix_ref[...], wo_ref[...], preferred_element_type=F32)
    x1 = _layer_norm(alpha * x_ref[...] + y, g_ref[...], b_ref[...])
    x1_ref[...] = x1
    _pack_rows(xp_ref, x1)

    xh = x1.astype(BF16)
    xl = (x1 - xh.astype(F32)).astype(BF16)
    logits = (jnp.dot(xh, wrh_ref[...], preferred_element_type=F32)
              + (jnp.dot(xh, wrl_ref[...], preferred_element_type=F32)
                 + jnp.dot(xl, wrh_ref[...], preferred_element_type=F32))) + br_ref[...]

    lane = lax.broadcasted_iota(jnp.int32, (tm, LANES), 1)
    lane_f = lane.astype(F32)
    big = float(LANES)
    is_grp = lane < N_GROUPS
    gl = jnp.where(is_grp, logits, NEG_INF)
    gmax = jnp.max(gl, axis=-1, keepdims=True)
    g_sel = jnp.min(jnp.where(gl == gmax, lane_f, big), axis=-1, keepdims=True)
    p_grp = 1.0 / jnp.sum(jnp.where(is_grp, jnp.exp(gl - gmax), 0.0), axis=-1, keepdims=True)

    e_lo = N_GROUPS + g_sel * EXPERTS_PER_GROUP
    in_grp = (lane_f >= e_lo) & (lane_f < e_lo + EXPERTS_PER_GROUP)
    el = jnp.where(in_grp, logits, NEG_INF)
    m1 = jnp.max(el, axis=-1, keepdims=True)
    i1 = jnp.min(jnp.where(el == m1, lane_f, big), axis=-1, keepdims=True)
    el2 = jnp.where(lane_f == i1, NEG_INF, el)
    m2 = jnp.max(el2, axis=-1, keepdims=True)
    i2 = jnp.min(jnp.where(el2 == m2, lane_f, big), axis=-1, keepdims=True)
    z = jnp.sum(jnp.where(in_grp, jnp.exp(el - m1), 0.0), axis=-1, keepdims=True)
    tp1 = 1.0 / z
    tp2 = jnp.exp(m2 - m1) / z
    gate1 = p_grp * tp1 / (tp1 + tp2)
    gate2 = p_grp * tp2 / (tp1 + tp2)
    e1 = i1 - N_GROUPS
    e2 = i2 - N_GROUPS

    oh1 = lane_f == e1
    oh2 = lane_f == e2
    c = jnp.where(oh1 | oh2, 1.0, 0.0)
    r = lax.broadcasted_iota(jnp.int32, (tm, tm), 0)
    s = lax.broadcasted_iota(jnp.int32, (tm, tm), 1)
    tri = jnp.where(r > s, 1.0, 0.0).astype(BF16)
    prefix = jnp.dot(tri, c.astype(BF16), preferred_element_type=F32) + carry_ref[0:1, :]
    rank1 = jnp.sum(jnp.where(oh1, prefix, 0.0), axis=-1, keepdims=True)
    rank2 = jnp.sum(jnp.where(oh2, prefix, 0.0), axis=-1, keepdims=True)
    new_carry = carry_ref[0:1, :] + jnp.sum(c, axis=0, keepdims=True)
    carry_ref[...] = jnp.broadcast_to(new_carry, carry_ref.shape)
    cnt_ref[...] = jnp.broadcast_to(new_carry, cnt_ref.shape)

    route = jnp.where(lane == 0, e1, 0.0)
    route = jnp.where(lane == 1, e2, route)
    route = jnp.where(lane == 2, rank1, route)
    route = jnp.where(lane == 3, rank2, route)
    route = jnp.where(lane == 4, gate1, route)
    route = jnp.where(lane == 5, gate2, route)
    route_ref[...] = route


def _post(mix, x2d, wo_bf, ln_g, ln_b, wr_hi, wr_lo, b_r, *, tm, alpha):
    t, d = x2d.shape
    kern = functools.partial(_post_kernel, tm=tm, alpha=alpha)
    return pl.pallas_call(
        kern,
        grid=(t // tm,),
        in_specs=[
            pl.BlockSpec((tm, d), lambda i: (i, 0)),
            pl.BlockSpec((tm, d), lambda i: (i, 0)),
            _resident((d, d)),
            _resident((1, d)), _resident((1, d)),
            _resident((d, LANES)), _resident((d, LANES)), _resident((1, LANES)),
        ],
        out_specs=[
            pl.BlockSpec((tm, d), lambda i: (i, 0)),
            pl.BlockSpec((tm * (d // (2 * LANES)), LANES), lambda i: (i, 0)),
            pl.BlockSpec((tm, LANES), lambda i: (i, 0)),
            pl.BlockSpec((8, LANES), lambda i: (0, 0)),
        ],
        out_shape=[
            jax.ShapeDtypeStruct((t, d), F32),
            jax.ShapeDtypeStruct((t * (d // (2 * LANES)), LANES), U32),
            jax.ShapeDtypeStruct((t, LANES), F32),
            jax.ShapeDtypeStruct((8, LANES), F32),
        ],
        scratch_shapes=[pltpu.VMEM((8, LANES), F32)],
        compiler_params=_params(),
        name="post",
    )(mix, x2d, wo_bf, ln_g.reshape(1, d), ln_b.reshape(1, d), wr_hi, wr_lo, b_r)


def _moe_kernel(blk_e_ref, nxt_e_ref, nvalid_ref, n_used_ref, tok0_ref, tokn_ref, dst_ref,
                x_hbm, wg_hbm, wu_hbm, wd_hbm, y_hbm,
                xbuf, ybuf, wg_st, wu_st, wd_st, wg_bf, wu_bf, wd_bf, sem_g, sem_s, sem_w,
                *, rb, nc, nblk, layer):
    s = pl.program_id(0)
    n_used = n_used_ref[0]
    slot = lax.rem(s, 2)
    other = 1 - slot
    unroll = 8

    def nvalid(b):
        return jnp.where((b >= 0) & (b < nblk), nvalid_ref[jnp.clip(b, 0, nblk - 1)], 0)

    def gather_row(tok_ref, j, dst_slot):
        src = x_hbm.at[pl.ds(pl.multiple_of(tok_ref[j], nc), nc)]
        pltpu.make_async_copy(src, xbuf.at[dst_slot, pl.ds(j * nc, nc)],
                              sem_g.at[dst_slot]).start()

    def scatter_row(j, src_slot):
        dst = y_hbm.at[pl.ds(pl.multiple_of(dst_ref[j], nc), nc)]
        pltpu.make_async_copy(ybuf.at[src_slot, pl.ds(j * nc, nc)], dst,
                              sem_s.at[src_slot]).start()

    def for_rows(n, row_fn):
        def group(g, carry):
            for u in range(unroll):
                row_fn(g * unroll + u)
            return carry
        lax.fori_loop(0, n // unroll, group, 0)

        def single(j, carry):
            row_fn(j)
            return carry
        lax.fori_loop((n // unroll) * unroll, n, single, 0)

    def wait_rows(n, make_copy):
        p = rb
        while p >= 1:
            @pl.when((n & p) != 0)
            def _(p=p):
                make_copy(p).wait()
            p //= 2

    def wait_gather(n, sl):
        wait_rows(n, lambda p: pltpu.make_async_copy(
            x_hbm.at[pl.ds(0, p * nc)], xbuf.at[sl, pl.ds(0, p * nc)], sem_g.at[sl]))

    def wait_scatter(n, sl):
        wait_rows(n, lambda p: pltpu.make_async_copy(
            ybuf.at[sl, pl.ds(0, p * nc)], y_hbm.at[pl.ds(0, p * nc)], sem_s.at[sl]))

    weight_copies = ((wg_hbm, wg_st, 0), (wu_hbm, wu_st, 1), (wd_hbm, wd_st, 2))

    def fetch_weights(e):
        for hbm, stage, sem_i in weight_copies:
            pltpu.make_async_copy(hbm.at[layer, e], stage, sem_w.at[sem_i]).start()

    @pl.when(s == 0)
    def _():
        xbuf[...] = jnp.zeros_like(xbuf)
        for_rows(nvalid(0), lambda j: gather_row(tok0_ref, j, 0))
        fetch_weights(blk_e_ref[0])

    b_cur = jnp.minimum(s, nblk - 1)
    e_cur = blk_e_ref[b_cur]
    new_expert = (s == 0) | (e_cur != blk_e_ref[jnp.maximum(b_cur - 1, 0)])

    @pl.when((s < n_used) & new_expert)
    def _():
        for hbm, stage, sem_i in weight_copies:
            pltpu.make_async_copy(hbm.at[layer, 0], stage, sem_w.at[sem_i]).wait()
        wg_bf[...] = wg_st[...].astype(BF16)
        wu_bf[...] = wu_st[...].astype(BF16)
        wd_bf[...] = wd_st[...].astype(BF16)
        e_next = nxt_e_ref[b_cur]

        @pl.when(e_next != e_cur)
        def _():
            fetch_weights(e_next)

    @pl.when(s < n_used)
    def _():
        for_rows(nvalid(s + 1), lambda j: gather_row(tokn_ref, j, other))
        wait_gather(nvalid(s), slot)
        wait_scatter(nvalid(s - 2), slot)
        xb = _unpack_rows(xbuf, nc, slot).astype(BF16)
        hg = jnp.dot(xb, wg_bf[...], preferred_element_type=F32)
        hu = jnp.dot(xb, wu_bf[...], preferred_element_type=F32)
        h = (jax.nn.silu(hg) * hu).astype(BF16)
        _pack_rows(ybuf, jnp.dot(h, wd_bf[...], preferred_element_type=F32), slot)
        for_rows(nvalid(s), lambda j: scatter_row(j, slot))

    @pl.when(s == n_used)
    def _():
        wait_scatter(nvalid(s - 2), slot)
        wait_scatter(nvalid(s - 1), other)


def _moe(x1, row_tok, row_dst, blk_e, nxt_e, nvalid, n_used, w_gate, w_up, w_down, *, layer, rb):
    d_e, d = w_down.shape[-2:]
    n_chunks = d // (2 * LANES)
    t = x1.shape[0] // n_chunks
    nblk = row_tok.shape[0] // rb

    def smem(index_map):
        return pl.BlockSpec((rb,), index_map, memory_space=pltpu.SMEM)

    any_spec = pl.BlockSpec(memory_space=pl.ANY)
    grid_spec = pltpu.PrefetchScalarGridSpec(
        num_scalar_prefetch=4,
        grid=(nblk + 1,),
        in_specs=[
            smem(lambda s, *_: (0,)),
            smem(lambda s, *_: (jnp.minimum(s + 1, nblk - 1),)),
            smem(lambda s, *_: (jnp.minimum(s, nblk - 1),)),
            any_spec, any_spec, any_spec, any_spec,
        ],
        out_specs=any_spec,
        scratch_shapes=[pltpu.VMEM((2, rb * n_chunks, LANES), U32),
                        pltpu.VMEM((2, rb * n_chunks, LANES), U32),
                        pltpu.VMEM((d, d_e), F32), pltpu.VMEM((d, d_e), F32),
                        pltpu.VMEM((d_e, d), F32),
                        pltpu.VMEM((d, d_e), BF16), pltpu.VMEM((d, d_e), BF16),
                        pltpu.VMEM((d_e, d), BF16),
                        pltpu.SemaphoreType.DMA((2,)), pltpu.SemaphoreType.DMA((2,)),
                        pltpu.SemaphoreType.DMA((3,))],
    )
    return pl.pallas_call(
        functools.partial(_moe_kernel, rb=rb, nc=n_chunks, nblk=nblk, layer=layer),
        grid_spec=grid_spec,
        out_shape=jax.ShapeDtypeStruct((TOP_K * t * n_chunks, LANES), U32),
        compiler_params=_params(),
        name="moe",
    )(blk_e, nxt_e, nvalid, n_used, row_tok, row_tok, row_dst, x1, w_gate, w_up, w_down)


def _combine_kernel(x1_ref, y0_ref, y1_ref, route_ref, g_ref, b_ref, o_ref, *, alpha):
    route = route_ref[...]
    nw = o_ref.shape[1] // (2 * LANES)
    y = _unpack_rows(y0_ref, nw) * route[:, 4:5] + _unpack_rows(y1_ref, nw) * route[:, 5:6]
    o_ref[...] = _layer_norm(alpha * x1_ref[...] + y, g_ref[...], b_ref[...])


def _combine(y2, x1, route, ln_g, ln_b, *, tm, alpha):
    t, d = x1.shape
    n_chunks = d // (2 * LANES)
    return pl.pallas_call(
        functools.partial(_combine_kernel, alpha=alpha),
        grid=(t // tm,),
        in_specs=[pl.BlockSpec((tm, d), lambda i: (i, 0)),
                  pl.BlockSpec((tm * n_chunks, LANES), lambda i: (i, 0)),
                  pl.BlockSpec((tm * n_chunks, LANES), lambda i: (t // tm + i, 0)),
                  pl.BlockSpec((tm, LANES), lambda i: (i, 0)),
                  _resident((1, d)), _resident((1, d))],
        out_specs=pl.BlockSpec((tm, d), lambda i: (i, 0)),
        out_shape=jax.ShapeDtypeStruct((t, d), F32),
        compiler_params=_params(),
        name="combine",
    )(x1, y2, y2, route, ln_g.reshape(1, d), ln_b.reshape(1, d))


def _split_bf16(w):
    hi = w.astype(BF16)
    lo = (w - hi.astype(F32)).astype(BF16)
    return hi, lo


def kernel(x, mem, a_w_in, a_sinks, b_w_in, b_norm_g, b_norm_b, b_w_spatial, b_b_spatial,
           w_mem_kv, w_out, ln_g, ln_b, w_router_group, b_router_group,
           w_router_expert, b_router_expert, w_gate, w_up, w_down):
    bsz, seq, d = x.shape
    mem_len = mem.shape[1]
    depth = w_out.shape[0]
    t = bsz * seq
    alpha = (2 * depth) ** 0.25
    rb = ROW_BLOCK
    n_rows = -(-(t * TOP_K + N_EXPERTS * (rb - 1)) // rb) * rb
    nblk = n_rows // rb

    x2d = x.reshape(t, d)
    mem2d = mem.reshape(bsz * mem_len, d)
    for i in range(depth):
        j = i // 2
        kvm = _proj(mem2d, w_mem_kv[i].astype(BF16), BF16, tm=min(256, bsz * mem_len))
        if i % 2 == 0:
            proj = _proj(x2d, a_w_in[j].astype(BF16), BF16, tm=512)
            mix = _attn_mixer(proj, kvm, a_sinks[j], seq=seq, mem_len=mem_len, tq=256)
        else:
            proj = _gmlp_proj(x2d, b_w_in[j].astype(BF16), b_norm_g[j], b_norm_b[j], tm=256)
            mix = _gmlp_mixer(proj, kvm, b_w_spatial[j], b_b_spatial[j],
                              seq=seq, mem_len=mem_len, tq=256)

        w_r = jnp.concatenate([w_router_group[i], w_router_expert[i]], axis=1)
        w_r = jnp.pad(w_r, ((0, 0), (0, LANES - w_r.shape[1])))
        b_r = jnp.concatenate([b_router_group[i], b_router_expert[i]])
        b_r = jnp.pad(b_r, (0, LANES - b_r.shape[0])).reshape(1, LANES)
        wr_hi, wr_lo = _split_bf16(w_r)
        x1, x1p, route, cnt = _post(mix, x2d, w_out[i].astype(BF16), ln_g[i, 0], ln_b[i, 0],
                                    wr_hi, wr_lo, b_r, tm=256, alpha=alpha)

        counts = cnt[0, :N_EXPERTS].astype(jnp.int32)
        padded = (counts + rb - 1) // rb * rb
        pend = jnp.cumsum(padded)
        pstart = (pend - padded).astype(jnp.int32)
        n_used = (pend[-1] // rb).astype(jnp.int32).reshape(1)
        blk = jnp.arange(nblk, dtype=jnp.int32)
        blk_e = jnp.searchsorted(pend, jnp.minimum(blk, n_used[0] - 1) * rb,
                                 side='right').astype(jnp.int32)
        nxt_blk = pend[blk_e] // rb
        nxt_e = jnp.where(nxt_blk < n_used[0], blk_e[jnp.minimum(nxt_blk, nblk - 1)], blk_e)
        nvalid = jnp.where(blk < n_used[0],
                           jnp.clip(pstart[blk_e] + counts[blk_e] - blk * rb, 0, rb), 0)
        ri = route[:, :4].astype(jnp.int32)
        e_hot = ri[:, :TOP_K, None] == jnp.arange(N_EXPERTS, dtype=jnp.int32)
        dest = (jnp.sum(jnp.where(e_hot, pstart, 0), axis=-1) + ri[:, TOP_K:]).reshape(-1)
        row_asg = jnp.zeros((n_rows,), jnp.int32).at[dest].set(
            jnp.arange(TOP_K * t, dtype=jnp.int32), unique_indices=True)
        n_chunks = d // (2 * LANES)
        row_tok = (row_asg // TOP_K) * n_chunks
        row_dst = ((row_asg % TOP_K) * t + row_asg // TOP_K) * n_chunks

        y2 = _moe(x1p, row_tok, row_dst, blk_e, nxt_e.astype(jnp.int32),
                  nvalid.astype(jnp.int32), n_used, w_gate, w_up, w_down, layer=i, rb=rb)
        x2d = _combine(y2, x1, route, ln_g[i, 1], ln_b[i, 1], tm=256, alpha=alpha)
    return x2d.reshape(bsz, seq, d)
```

```python
import functools
import math

import jax
import jax.numpy as jnp
import numpy as np
from jax import lax
from jax.experimental import pallas as pl
from jax.experimental.pallas import tpu as pltpu

MEM_HEADS = 4
MEM_HEAD_DIM = 128
MEM_W = MEM_HEADS * MEM_HEAD_DIM
HEAD_DIM = 64
N_KV_HEADS = 4
KV_W = N_KV_HEADS * HEAD_DIM
WINDOW = 128
CHUNK = 128
GM_GROUP_DIM = 128
N_GROUPS = 8
EXPERTS_PER_GROUP = 8
N_EXPERTS = N_GROUPS * EXPERTS_PER_GROUP
TOP_K = 2
LN_EPS = 1e-5
NEG_INF = -1e30

LANES = 128
VMEM_LIMIT_BYTES = 56 * 1024 * 1024

ROW_BLOCK = 256
BF16 = jnp.bfloat16
F32 = jnp.float32
U32 = jnp.uint32


def _alibi_slopes(n):
    def pow2(m):
        start = 2.0 ** (-8.0 / m)
        return [start ** (i + 1) for i in range(m)]
    if math.log2(n).is_integer():
        s = pow2(n)
    else:
        c = 2 ** math.floor(math.log2(n))
        s = pow2(c) + pow2(2 * c)[0::2][: n - c]
    return [float(v) for v in np.asarray(s, dtype=np.float32)]


def _params(n_axes=1):
    return pltpu.CompilerParams(dimension_semantics=("arbitrary",) * n_axes,
                                vmem_limit_bytes=VMEM_LIMIT_BYTES)


def _resident(shape):
    nd = len(shape)
    return pl.BlockSpec(shape, lambda *_: (0,) * nd, pipeline_mode=pl.Buffered(1))


def _pack_rows(ref, val, *lead, row0=0):
    rows, d = val.shape
    nw = d // (2 * LANES)
    for c in range(nw):
        lo = val[:, c * LANES:(c + 1) * LANES].astype(BF16).astype(F32)
        hi = val[:, (c + nw) * LANES:(c + nw + 1) * LANES].astype(BF16).astype(F32)
        word = (lax.bitcast_convert_type(lo, U32) >> 16) | lax.bitcast_convert_type(hi, U32)
        ref[tuple(lead) + (pl.ds(row0 * nw + c, rows, stride=nw), slice(None))] = word


def _unpack_rows(ref, nw, *lead):
    rows = ref.shape[-2] // nw
    lo, hi = [], []
    for c in range(nw):
        word = ref[tuple(lead) + (pl.ds(c, rows, stride=nw), slice(None))]
        lo.append(lax.bitcast_convert_type(word << 16, F32))
        hi.append(lax.bitcast_convert_type(word & jnp.uint32(0xFFFF0000), F32))
    return jnp.concatenate(lo + hi, axis=-1)


def _layer_norm(x, g, b):
    mu = jnp.mean(x, axis=-1, keepdims=True)
    xc = x - mu
    var = jnp.mean(xc * xc, axis=-1, keepdims=True)
    return xc * lax.rsqrt(var + LN_EPS) * g + b


def _proj_kernel(x_ref, w_ref, o_ref):
    o_ref[...] = jnp.dot(x_ref[...].astype(BF16), w_ref[...],
                         preferred_element_type=F32).astype(o_ref.dtype)


def _proj(x2d, w_bf, out_dtype, tm):
    m, k = x2d.shape
    n = w_bf.shape[1]
    return pl.pallas_call(
        _proj_kernel,
        grid=(m // tm,),
        in_specs=[pl.BlockSpec((tm, k), lambda i: (i, 0)), _resident((k, n))],
        out_specs=pl.BlockSpec((tm, n), lambda i: (i, 0)),
        out_shape=jax.ShapeDtypeStruct((m, n), out_dtype),
        compiler_params=_params(),
        name="proj",
    )(x2d, w_bf)


def _gelu(x):
    return 0.5 * x * (1.0 + lax.erf(x * (2.0 ** -0.5)))


def _gmlp_proj_kernel(x_ref, w_ref, g_ref, b_ref, o_ref, *, tok_w):
    z = jnp.dot(x_ref[...].astype(BF16), w_ref[...], preferred_element_type=F32)
    u = _gelu(z[:, :tok_w])
    v = _gelu(z[:, tok_w:2 * tok_w])
    v = _layer_norm(v, g_ref[...], b_ref[...])
    o_ref[:, :tok_w] = u.astype(o_ref.dtype)
    o_ref[:, tok_w:2 * tok_w] = v.astype(o_ref.dtype)
    o_ref[:, 2 * tok_w:] = z[:, 2 * tok_w:].astype(o_ref.dtype)


def _gmlp_proj(x2d, w_bf, norm_g, norm_b, tm):
    m, k = x2d.shape
    n = w_bf.shape[1]
    tok_w = norm_g.shape[-1]
    return pl.pallas_call(
        functools.partial(_gmlp_proj_kernel, tok_w=tok_w),
        grid=(m // tm,),
        in_specs=[pl.BlockSpec((tm, k), lambda i: (i, 0)), _resident((k, n)),
                  _resident((1, tok_w)), _resident((1, tok_w))],
        out_specs=pl.BlockSpec((tm, n), lambda i: (i, 0)),
        out_shape=jax.ShapeDtypeStruct((m, n), BF16),
        compiler_params=_params(),
        name="gmlp_proj",
    )(x2d, w_bf, norm_g.reshape(1, tok_w), norm_b.reshape(1, tok_w))


def _memory_attention(qm, kvm):
    outs = []
    for h in range(MEM_HEADS):
        q = qm[:, h * MEM_HEAD_DIM:(h + 1) * MEM_HEAD_DIM]
        k = kvm[:, h * MEM_HEAD_DIM:(h + 1) * MEM_HEAD_DIM]
        v = kvm[:, MEM_W + h * MEM_HEAD_DIM:MEM_W + (h + 1) * MEM_HEAD_DIM]
        s = lax.dot_general(q, k, (((1,), (1,)), ((), ())),
                            preferred_element_type=F32) * (MEM_HEAD_DIM ** -0.5)
        m = jnp.max(s, axis=-1, keepdims=True)
        p = jnp.exp(s - m)
        p = p * (1.0 / jnp.sum(p, axis=-1, keepdims=True))
        outs.append(jnp.dot(p.astype(BF16), v, preferred_element_type=F32))
    return jnp.concatenate(outs, axis=-1)


def _attn_kernel(sinks_ref, q_ref, kv_ref, kvp_ref, qm_ref, kvm_ref, o_ref, *,
                 tq, seq, slopes, tok_w):
    i = pl.program_id(0)
    n_sub = tq // WINDOW
    gqa = len(slopes) // N_KV_HEADS
    qi = lax.broadcasted_iota(jnp.int32, (WINDOW, WINDOW), 0)
    c = lax.broadcasted_iota(jnp.int32, (WINDOW, WINDOW), 1)
    own = c <= qi
    dist_f = jnp.where(own, qi - c, WINDOW + qi - c).astype(F32)
    nt = (((1,), (1,)), ((), ()))
    for sb in range(n_sub):
        r0 = sb * WINDOW
        if sb == 0:
            kv_prev = kvp_ref[...]
            reach = jnp.where(((i * tq) % seq) == 0, 0, WINDOW)
            valid = c <= qi + reach
        else:
            kv_prev = kv_ref[r0 - WINDOW:r0, :]
            valid = None
        kv_cur = kv_ref[r0:r0 + WINDOW, :]
        for kh in range(N_KV_HEADS):
            ks = slice(kh * HEAD_DIM, (kh + 1) * HEAD_DIM)
            vs = slice(KV_W + kh * HEAD_DIM, KV_W + (kh + 1) * HEAD_DIM)
            heads = range(kh * gqa, (kh + 1) * gqa)
            q = jnp.concatenate(
                [q_ref[r0:r0 + WINDOW, h * HEAD_DIM:(h + 1) * HEAD_DIM] for h in heads], axis=0)
            s_own = lax.dot_general(q, kv_cur[:, ks], nt, preferred_element_type=F32)
            s_prev = lax.dot_general(q, kv_prev[:, ks], nt, preferred_element_type=F32)
            p_own, p_prev = [], []
            for g, h in enumerate(heads):
                rows = slice(g * WINDOW, (g + 1) * WINDOW)
                logits = (jnp.where(own, s_own[rows], s_prev[rows]) * (HEAD_DIM ** -0.5)
                          - slopes[h] * dist_f)
                if valid is not None:
                    logits = jnp.where(valid, logits, NEG_INF)
                sink = sinks_ref[h]
                m = jnp.maximum(jnp.max(logits, axis=-1, keepdims=True), sink)
                p = jnp.exp(logits - m)
                probs = p * (1.0 / (jnp.sum(p, axis=-1, keepdims=True) + jnp.exp(sink - m)))
                p_own.append(jnp.where(own, probs, 0.0).astype(BF16))
                p_prev.append(jnp.where(own, 0.0, probs).astype(BF16))
            out = (jnp.dot(jnp.concatenate(p_own, axis=0), kv_cur[:, vs],
                           preferred_element_type=F32)
                   + jnp.dot(jnp.concatenate(p_prev, axis=0), kv_prev[:, vs],
                             preferred_element_type=F32))
            o_ref[r0:r0 + WINDOW, kh * gqa * HEAD_DIM:(kh + 1) * gqa * HEAD_DIM] = jnp.concatenate(
                [out[g * WINDOW:(g + 1) * WINDOW] for g in range(gqa)], axis=1).astype(o_ref.dtype)
    o_ref[:, tok_w:] = _memory_attention(qm_ref[...], kvm_ref[...]).astype(o_ref.dtype)


def _attn_mixer(proj, kvm, sinks, *, seq, mem_len, tq):
    t = proj.shape[0]
    tok_w = proj.shape[1] - 2 * KV_W - MEM_W
    n_heads = tok_w // HEAD_DIM
    assert tok_w % (2 * KV_W) == 0 and tok_w % MEM_W == 0 and seq % tq == 0
    kv_col = tok_w // (2 * KV_W)
    qm_col = (tok_w + 2 * KV_W) // MEM_W
    sub = tq // WINDOW
    kern = functools.partial(_attn_kernel, tq=tq, seq=seq, slopes=_alibi_slopes(n_heads),
                             tok_w=tok_w)
    return pl.pallas_call(
        kern,
        grid=(t // tq,),
        in_specs=[
            pl.BlockSpec(memory_space=pltpu.SMEM),
            pl.BlockSpec((tq, tok_w), lambda i: (i, 0)),
            pl.BlockSpec((tq, 2 * KV_W), lambda i: (i, kv_col)),
            pl.BlockSpec((WINDOW, 2 * KV_W), lambda i: (jnp.maximum(i * sub - 1, 0), kv_col)),
            pl.BlockSpec((tq, MEM_W), lambda i: (i, qm_col)),
            pl.BlockSpec((mem_len, 2 * MEM_W), lambda i: ((i * tq) // seq, 0)),
        ],
        out_specs=pl.BlockSpec((tq, tok_w + MEM_W), lambda i: (i, 0)),
        out_shape=jax.ShapeDtypeStruct((t, tok_w + MEM_W), BF16),
        compiler_params=_params(),
        name="attn_mixer",
    )(sinks, proj, proj, proj, proj, kvm)


def _gmlp_kernel(u_ref, v_ref, qm_ref, kvm_ref, ws_ref, bs_ref, o_ref, *, tq, tok_w):
    n_sub = tq // CHUNK
    n_grp = tok_w // GM_GROUP_DIM
    row = lax.broadcasted_iota(jnp.int32, (CHUNK, CHUNK), 0)
    col = lax.broadcasted_iota(jnp.int32, (CHUNK, CHUNK), 1)
    lower = row >= col
    for g in range(n_grp):
        w = jnp.where(lower, ws_ref[g], 0.0).astype(BF16)
        bias = bs_ref[:, g:g + 1]
        c0 = g * GM_GROUP_DIM
        for sb in range(n_sub):
            r0 = sb * CHUNK
            sv = jnp.dot(w, v_ref[r0:r0 + CHUNK, c0:c0 + GM_GROUP_DIM],
                         preferred_element_type=F32) + bias
            u = u_ref[r0:r0 + CHUNK, c0:c0 + GM_GROUP_DIM].astype(F32)
            o_ref[r0:r0 + CHUNK, c0:c0 + GM_GROUP_DIM] = (u * sv).astype(o_ref.dtype)
    o_ref[:, tok_w:] = _memory_attention(qm_ref[...], kvm_ref[...]).astype(o_ref.dtype)


def _gmlp_mixer(proj, kvm, w_s, b_s, *, seq, mem_len, tq):
    t = proj.shape[0]
    tok_w = (proj.shape[1] - MEM_W) // 2
    n_grp = tok_w // GM_GROUP_DIM
    assert (2 * tok_w) % MEM_W == 0 and seq % tq == 0
    kern = functools.partial(_gmlp_kernel, tq=tq, tok_w=tok_w)
    return pl.pallas_call(
        kern,
        grid=(t // tq,),
        in_specs=[
            pl.BlockSpec((tq, tok_w), lambda i: (i, 0)),
            pl.BlockSpec((tq, tok_w), lambda i: (i, 1)),
            pl.BlockSpec((tq, MEM_W), lambda i: (i, (2 * tok_w) // MEM_W)),
            pl.BlockSpec((mem_len, 2 * MEM_W), lambda i: ((i * tq) // seq, 0)),
            _resident((n_grp, CHUNK, CHUNK)),
            _resident((CHUNK, n_grp)),
        ],
        out_specs=pl.BlockSpec((tq, tok_w + MEM_W), lambda i: (i, 0)),
        out_shape=jax.ShapeDtypeStruct((t, tok_w + MEM_W), BF16),
        compiler_params=_params(),
        name="gmlp_mixer",
    )(proj, proj, proj, kvm, w_s, b_s.T)


def _post_kernel(mix_ref, x_ref, wo_ref, g_ref, b_ref, wrh_ref, wrl_ref, br_ref,
                 x1_ref, xp_ref, route_ref, cnt_ref, carry_ref, *, tm, sub, alpha):
    i = pl.program_id(0)

    @pl.when(i == 0)
    def _():
        carry_ref[...] = jnp.zeros_like(carry_ref)

    carry = carry_ref[0:1, :]
    for r0 in range(0, tm, sub):
        carry = _post_rows(r0, sub, carry, mix_ref, x_ref, wo_ref, g_ref, b_ref, wrh_ref, wrl_ref,
                           br_ref, x1_ref, xp_ref, route_ref, alpha)
    carry_ref[...] = jnp.broadcast_to(carry, carry_ref.shape)
    cnt_ref[...] = jnp.broadcast_to(carry, cnt_ref.shape)


def _post_rows(r0, tm, carry, mix_ref, x_ref, wo_ref, g_ref, b_ref, wrh_ref, wrl_ref, br_ref,
               x1_ref, xp_ref, route_ref, alpha):
    rows = slice(r0, r0 + tm)
    y = jnp.dot(mix_ref[rows, :], wo_ref[...], preferred_element_type=F32)
    x1 = _layer_norm(alpha * x_ref[rows, :] + y, g_ref[...], b_ref[...])
    x1_ref[rows, :] = x1
    _pack_rows(xp_ref, x1, row0=r0)

    xh = x1.astype(BF16)
    xl = (x1 - xh.astype(F32)).astype(BF16)
    logits = (jnp.dot(xh, wrh_ref[...], preferred_element_type=F32)
              + (jnp.dot(xh, wrl_ref[...], preferred_element_type=F32)
                 + jnp.dot(xl, wrh_ref[...], preferred_element_type=F32))) + br_ref[...]

    lane = lax.broadcasted_iota(jnp.int32, (tm, LANES), 1)
    lane_f = lane.astype(F32)
    big = float(LANES)
    is_grp = lane < N_GROUPS
    gl = jnp.where(is_grp, logits, NEG_INF)
    gmax = jnp.max(gl, axis=-1, keepdims=True)
    g_sel = jnp.min(jnp.where(gl == gmax, lane_f, big), axis=-1, keepdims=True)
    p_grp = 1.0 / jnp.sum(jnp.where(is_grp, jnp.exp(gl - gmax), 0.0), axis=-1, keepdims=True)

    e_lo = N_GROUPS + g_sel * EXPERTS_PER_GROUP
    in_grp = (lane_f >= e_lo) & (lane_f < e_lo + EXPERTS_PER_GROUP)
    el = jnp.where(in_grp, logits, NEG_INF)
    m1 = jnp.max(el, axis=-1, keepdims=True)
    i1 = jnp.min(jnp.where(el == m1, lane_f, big), axis=-1, keepdims=True)
    el2 = jnp.where(lane_f == i1, NEG_INF, el)
    m2 = jnp.max(el2, axis=-1, keepdims=True)
    i2 = jnp.min(jnp.where(el2 == m2, lane_f, big), axis=-1, keepdims=True)
    z = jnp.sum(jnp.where(in_grp, jnp.exp(el - m1), 0.0), axis=-1, keepdims=True)
    tp1 = 1.0 / z
    tp2 = jnp.exp(m2 - m1) / z
    gate1 = p_grp * tp1 / (tp1 + tp2)
    gate2 = p_grp * tp2 / (tp1 + tp2)
    e1 = i1 - N_GROUPS
    e2 = i2 - N_GROUPS

    oh1 = lane_f == e1
    oh2 = lane_f == e2
    c = jnp.where(oh1 | oh2, 1.0, 0.0)
    r = lax.broadcasted_iota(jnp.int32, (tm, tm), 0)
    s = lax.broadcasted_iota(jnp.int32, (tm, tm), 1)
    tri = jnp.where(r > s, 1.0, 0.0).astype(BF16)
    prefix = jnp.dot(tri, c.astype(BF16), preferred_element_type=F32) + carry
    rank1 = jnp.sum(jnp.where(oh1, prefix, 0.0), axis=-1, keepdims=True)
    rank2 = jnp.sum(jnp.where(oh2, prefix, 0.0), axis=-1, keepdims=True)

    route = jnp.where(lane == 0, e1, 0.0)
    route = jnp.where(lane == 1, e2, route)
    route = jnp.where(lane == 2, rank1, route)
    route = jnp.where(lane == 3, rank2, route)
    route = jnp.where(lane == 4, gate1, route)
    route = jnp.where(lane == 5, gate2, route)
    route_ref[rows, :] = route
    return carry + jnp.sum(c, axis=0, keepdims=True)


def _post(mix, x2d, wo_bf, ln_g, ln_b, wr_hi, wr_lo, b_r, *, tm, sub, alpha):
    t, d = x2d.shape
    kern = functools.partial(_post_kernel, tm=tm, sub=sub, alpha=alpha)
    return pl.pallas_call(
        kern,
        grid=(t // tm,),
        in_specs=[
            pl.BlockSpec((tm, d), lambda i: (i, 0)),
            pl.BlockSpec((tm, d), lambda i: (i, 0)),
            _resident((d, d)),
            _resident((1, d)), _resident((1, d)),
            _resident((d, LANES)), _resident((d, LANES)), _resident((1, LANES)),
        ],
        out_specs=[
            pl.BlockSpec((tm, d), lambda i: (i, 0)),
            pl.BlockSpec((tm * (d // (2 * LANES)), LANES), lambda i: (i, 0)),
            pl.BlockSpec((tm, LANES), lambda i: (i, 0)),
            pl.BlockSpec((8, LANES), lambda i: (0, 0)),
        ],
        out_shape=[
            jax.ShapeDtypeStruct((t, d), F32),
            jax.ShapeDtypeStruct((t * (d // (2 * LANES)), LANES), U32),
            jax.ShapeDtypeStruct((t, LANES), F32),
            jax.ShapeDtypeStruct((8, LANES), F32),
        ],
        scratch_shapes=[pltpu.VMEM((8, LANES), F32)],
        compiler_params=_params(),
        name="post",
    )(mix, x2d, wo_bf, ln_g.reshape(1, d), ln_b.reshape(1, d), wr_hi, wr_lo, b_r)


def _invert_kernel(dest_ref, out_ref):
    def clear(r, carry):
        out_ref[r] = 0
        return carry
    lax.fori_loop(0, out_ref.shape[0], clear, 0, unroll=16)

    def place(a, carry):
        out_ref[dest_ref[a]] = a
        return carry
    lax.fori_loop(0, dest_ref.shape[0], place, 0, unroll=16)


def _invert(dest, n_rows):
    smem = pl.BlockSpec(memory_space=pltpu.SMEM)
    return pl.pallas_call(
        _invert_kernel,
        in_specs=[smem], out_specs=smem,
        out_shape=jax.ShapeDtypeStruct((n_rows,), jnp.int32),
        name="invert",
    )(dest)


def _moe_kernel(blk_e_ref, nxt_e_ref, nvalid_ref, n_used_ref, tok0_ref, tokn_ref, dst_ref,
                x_hbm, wg_hbm, wu_hbm, wd_hbm, y_hbm,
                xbuf, ybuf, wg_st, wu_st, wd_st, wg_bf, wu_bf, wd_bf, sem_g, sem_s, sem_w,
                *, rb, nc, nblk, layer):
    s = pl.program_id(0)
    n_used = n_used_ref[0]
    slot = lax.rem(s, 2)
    other = 1 - slot
    unroll = 8

    def nvalid(b):
        return jnp.where((b >= 0) & (b < nblk), nvalid_ref[jnp.clip(b, 0, nblk - 1)], 0)

    def gather_row(tok_ref, j, dst_slot):
        src = x_hbm.at[pl.ds(pl.multiple_of(tok_ref[j], nc), nc)]
        pltpu.make_async_copy(src, xbuf.at[dst_slot, pl.ds(j * nc, nc)],
                              sem_g.at[dst_slot]).start()

    def scatter_row(j, src_slot):
        dst = y_hbm.at[pl.ds(pl.multiple_of(dst_ref[j], nc), nc)]
        pltpu.make_async_copy(ybuf.at[src_slot, pl.ds(j * nc, nc)], dst,
                              sem_s.at[src_slot]).start()

    def for_rows(n, row_fn):
        def group(g, carry):
            for u in range(unroll):
                row_fn(g * unroll + u)
            return carry
        lax.fori_loop(0, n // unroll, group, 0)

        def single(j, carry):
            row_fn(j)
            return carry
        lax.fori_loop((n // unroll) * unroll, n, single, 0)

    def wait_rows(n, make_copy):
        p = rb
        while p >= 1:
            @pl.when((n & p) != 0)
            def _(p=p):
                make_copy(p).wait()
            p //= 2

    def wait_gather(n, sl):
        wait_rows(n, lambda p: pltpu.make_async_copy(
            x_hbm.at[pl.ds(0, p * nc)], xbuf.at[sl, pl.ds(0, p * nc)], sem_g.at[sl]))

    def wait_scatter(n, sl):
        wait_rows(n, lambda p: pltpu.make_async_copy(
            ybuf.at[sl, pl.ds(0, p * nc)], y_hbm.at[pl.ds(0, p * nc)], sem_s.at[sl]))

    weight_copies = ((wg_hbm, wg_st, 0), (wu_hbm, wu_st, 1), (wd_hbm, wd_st, 2))

    def fetch_weights(e):
        for hbm, stage, sem_i in weight_copies:
            pltpu.make_async_copy(hbm.at[layer, e], stage, sem_w.at[sem_i]).start()

    @pl.when(s == 0)
    def _():
        xbuf[...] = jnp.zeros_like(xbuf)
        for_rows(nvalid(0), lambda j: gather_row(tok0_ref, j, 0))
        fetch_weights(blk_e_ref[0])

    b_cur = jnp.minimum(s, nblk - 1)
    e_cur = blk_e_ref[b_cur]
    new_expert = (s == 0) | (e_cur != blk_e_ref[jnp.maximum(b_cur - 1, 0)])

    @pl.when((s < n_used) & new_expert)
    def _():
        for hbm, stage, sem_i in weight_copies:
            pltpu.make_async_copy(hbm.at[layer, 0], stage, sem_w.at[sem_i]).wait()
        wg_bf[...] = wg_st[...].astype(BF16)
        wu_bf[...] = wu_st[...].astype(BF16)
        wd_bf[...] = wd_st[...].astype(BF16)
        e_next = nxt_e_ref[b_cur]

        @pl.when(e_next != e_cur)
        def _():
            fetch_weights(e_next)

    @pl.when(s < n_used)
    def _():
        for_rows(nvalid(s + 1), lambda j: gather_row(tokn_ref, j, other))
        wait_gather(nvalid(s), slot)
        wait_scatter(nvalid(s - 2), slot)
        xb = _unpack_rows(xbuf, nc, slot).astype(BF16)
        hg = jnp.dot(xb, wg_bf[...], preferred_element_type=F32)
        hu = jnp.dot(xb, wu_bf[...], preferred_element_type=F32)
        h = (jax.nn.silu(hg) * hu).astype(BF16)
        _pack_rows(ybuf, jnp.dot(h, wd_bf[...], preferred_element_type=F32), slot)
        for_rows(nvalid(s), lambda j: scatter_row(j, slot))

    @pl.when(s == n_used)
    def _():
        wait_scatter(nvalid(s - 2), slot)
        wait_scatter(nvalid(s - 1), other)


def _moe(x1, row_tok, row_dst, blk_e, nxt_e, nvalid, n_used, w_gate, w_up, w_down, *, layer, rb):
    d_e, d = w_down.shape[-2:]
    n_chunks = d // (2 * LANES)
    t = x1.shape[0] // n_chunks
    nblk = row_tok.shape[0] // rb

    def smem(index_map):
        return pl.BlockSpec((rb,), index_map, memory_space=pltpu.SMEM)

    any_spec = pl.BlockSpec(memory_space=pl.ANY)
    grid_spec = pltpu.PrefetchScalarGridSpec(
        num_scalar_prefetch=4,
        grid=(nblk + 1,),
        in_specs=[
            smem(lambda s, *_: (0,)),
            smem(lambda s, *_: (jnp.minimum(s + 1, nblk - 1),)),
            smem(lambda s, *_: (jnp.minimum(s, nblk - 1),)),
            any_spec, any_spec, any_spec, any_spec,
        ],
        out_specs=any_spec,
        scratch_shapes=[pltpu.VMEM((2, rb * n_chunks, LANES), U32),
                        pltpu.VMEM((2, rb * n_chunks, LANES), U32),
                        pltpu.VMEM((d, d_e), F32), pltpu.VMEM((d, d_e), F32),
                        pltpu.VMEM((d_e, d), F32),
                        pltpu.VMEM((d, d_e), BF16), pltpu.VMEM((d, d_e), BF16),
                        pltpu.VMEM((d_e, d), BF16),
                        pltpu.SemaphoreType.DMA((2,)), pltpu.SemaphoreType.DMA((2,)),
                        pltpu.SemaphoreType.DMA((3,))],
    )
    return pl.pallas_call(
        functools.partial(_moe_kernel, rb=rb, nc=n_chunks, nblk=nblk, layer=layer),
        grid_spec=grid_spec,
        out_shape=jax.ShapeDtypeStruct((TOP_K * t * n_chunks, LANES), U32),
        compiler_params=_params(),
        name="moe",
    )(blk_e, nxt_e, nvalid, n_used, row_tok, row_tok, row_dst, x1, w_gate, w_up, w_down)


def _combine_kernel(x1_ref, y0_ref, y1_ref, route_ref, g_ref, b_ref, o_ref, *, alpha):
    route = route_ref[...]
    nw = o_ref.shape[1] // (2 * LANES)
    y = _unpack_rows(y0_ref, nw) * route[:, 4:5] + _unpack_rows(y1_ref, nw) * route[:, 5:6]
    o_ref[...] = _layer_norm(alpha * x1_ref[...] + y, g_ref[...], b_ref[...])


def _combine(y2, x1, route, ln_g, ln_b, *, tm, alpha):
    t, d = x1.shape
    n_chunks = d // (2 * LANES)
    return pl.pallas_call(
        functools.partial(_combine_kernel, alpha=alpha),
        grid=(t // tm,),
        in_specs=[pl.BlockSpec((tm, d), lambda i: (i, 0)),
                  pl.BlockSpec((tm * n_chunks, LANES), lambda i: (i, 0)),
                  pl.BlockSpec((tm * n_chunks, LANES), lambda i: (t // tm + i, 0)),
                  pl.BlockSpec((tm, LANES), lambda i: (i, 0)),
                  _resident((1, d)), _resident((1, d))],
        out_specs=pl.BlockSpec((tm, d), lambda i: (i, 0)),
        out_shape=jax.ShapeDtypeStruct((t, d), F32),
        compiler_params=_params(),
        name="combine",
    )(x1, y2, y2, route, ln_g.reshape(1, d), ln_b.reshape(1, d))


def _split_bf16(w):
    hi = w.astype(BF16)
    lo = (w - hi.astype(F32)).astype(BF16)
    return hi, lo


def kernel(x, mem, a_w_in, a_sinks, b_w_in, b_norm_g, b_norm_b, b_w_spatial, b_b_spatial,
           w_mem_kv, w_out, ln_g, ln_b, w_router_group, b_router_group,
           w_router_expert, b_router_expert, w_gate, w_up, w_down):
    bsz, seq, d = x.shape
    mem_len = mem.shape[1]
    depth = w_out.shape[0]
    t = bsz * seq
    alpha = (2 * depth) ** 0.25
    rb = ROW_BLOCK
    n_rows = -(-(t * TOP_K + N_EXPERTS * (rb - 1)) // rb) * rb
    nblk = n_rows // rb

    x2d = x.reshape(t, d)
    mem2d = mem.reshape(bsz * mem_len, d)
    for i in range(depth):
        j = i // 2
        kvm = _proj(mem2d, w_mem_kv[i].astype(BF16), BF16, tm=min(256, bsz * mem_len))
        if i % 2 == 0:
            proj = _proj(x2d, a_w_in[j].astype(BF16), BF16, tm=512)
            mix = _attn_mixer(proj, kvm, a_sinks[j], seq=seq, mem_len=mem_len, tq=256)
        else:
            proj = _gmlp_proj(x2d, b_w_in[j].astype(BF16), b_norm_g[j], b_norm_b[j], tm=256)
            mix = _gmlp_mixer(proj, kvm, b_w_spatial[j], b_b_spatial[j],
                              seq=seq, mem_len=mem_len, tq=512)

        w_r = jnp.concatenate([w_router_group[i], w_router_expert[i]], axis=1)
        w_r = jnp.pad(w_r, ((0, 0), (0, LANES - w_r.shape[1])))
        b_r = jnp.concatenate([b_router_group[i], b_router_expert[i]])
        b_r = jnp.pad(b_r, (0, LANES - b_r.shape[0])).reshape(1, LANES)
        wr_hi, wr_lo = _split_bf16(w_r)
        x1, x1p, route, cnt = _post(mix, x2d, w_out[i].astype(BF16), ln_g[i, 0], ln_b[i, 0],
                                    wr_hi, wr_lo, b_r, tm=512, sub=256, alpha=alpha)

        counts = cnt[0, :N_EXPERTS].astype(jnp.int32)
        padded = (counts + rb - 1) // rb * rb
        pend = jnp.cumsum(padded)
        pstart = (pend - padded).astype(jnp.int32)
        n_used = (pend[-1] // rb).astype(jnp.int32).reshape(1)
        blk = jnp.arange(nblk, dtype=jnp.int32)
        blk_e = jnp.searchsorted(pend, jnp.minimum(blk, n_used[0] - 1) * rb,
                                 side='right').astype(jnp.int32)
        nxt_blk = pend[blk_e] // rb
        nxt_e = jnp.where(nxt_blk < n_used[0], blk_e[jnp.minimum(nxt_blk, nblk - 1)], blk_e)
        nvalid = jnp.where(blk < n_used[0],
                           jnp.clip(pstart[blk_e] + counts[blk_e] - blk * rb, 0, rb), 0)
        ri = route[:, :4].astype(jnp.int32)
        e_hot = ri[:, :TOP_K, None] == jnp.arange(N_EXPERTS, dtype=jnp.int32)
        dest = (jnp.sum(jnp.where(e_hot, pstart, 0), axis=-1) + ri[:, TOP_K:]).reshape(-1)
        row_asg = _invert(dest, n_rows)
        n_chunks = d // (2 * LANES)
        row_tok = (row_asg // TOP_K) * n_chunks
        row_dst = ((row_asg % TOP_K) * t + row_asg // TOP_K) * n_chunks

        y2 = _moe(x1p, row_tok, row_dst, blk_e, nxt_e.astype(jnp.int32),
                  nvalid.astype(jnp.int32), n_used, w_gate, w_up, w_down, layer=i, rb=rb)
        x2d = _combine(y2, x1, route, ln_g[i, 1], ln_b[i, 1], tm=512, alpha=alpha)
    return x2d.reshape(bsz, seq, d)
```

```python
import functools
import math

import jax
import jax.numpy as jnp
import numpy as np
from jax import lax
from jax.experimental import pallas as pl
from jax.experimental.pallas import tpu as pltpu

MEM_HEADS = 4
MEM_HEAD_DIM = 128
MEM_W = MEM_HEADS * MEM_HEAD_DIM
HEAD_DIM = 64
N_KV_HEADS = 4
KV_W = N_KV_HEADS * HEAD_DIM
WINDOW = 128
CHUNK = 128
GM_GROUP_DIM = 128
N_GROUPS = 8
EXPERTS_PER_GROUP = 8
N_EXPERTS = N_GROUPS * EXPERTS_PER_GROUP
TOP_K = 2
LN_EPS = 1e-5
NEG_INF = -1e30

LANES = 128
VMEM_LIMIT_BYTES = 56 * 1024 * 1024

ROW_BLOCK = 256
BF16 = jnp.bfloat16
F32 = jnp.float32
U32 = jnp.uint32


def _alibi_slopes(n):
    def pow2(m):
        start = 2.0 ** (-8.0 / m)
        return [start ** (i + 1) for i in range(m)]
    if math.log2(n).is_integer():
        s = pow2(n)
    else:
        c = 2 ** math.floor(math.log2(n))
        s = pow2(c) + pow2(2 * c)[0::2][: n - c]
    return [float(v) for v in np.asarray(s, dtype=np.float32)]


def _params(n_axes=1):
    return pltpu.CompilerParams(dimension_semantics=("arbitrary",) * n_axes,
                                vmem_limit_bytes=VMEM_LIMIT_BYTES)


def _resident(shape):
    nd = len(shape)
    return pl.BlockSpec(shape, lambda *_: (0,) * nd, pipeline_mode=pl.Buffered(1))


def _pack_rows(ref, val, *lead, row0=0):
    rows, d = val.shape
    nw = d // (2 * LANES)
    for c in range(nw):
        lo = val[:, c * LANES:(c + 1) * LANES].astype(BF16).astype(F32)
        hi = val[:, (c + nw) * LANES:(c + nw + 1) * LANES].astype(BF16).astype(F32)
        word = (lax.bitcast_convert_type(lo, U32) >> 16) | lax.bitcast_convert_type(hi, U32)
        ref[tuple(lead) + (pl.ds(row0 * nw + c, rows, stride=nw), slice(None))] = word


def _unpack_rows(ref, nw, *lead):
    rows = ref.shape[-2] // nw
    lo, hi = [], []
    for c in range(nw):
        word = ref[tuple(lead) + (pl.ds(c, rows, stride=nw), slice(None))]
        lo.append(lax.bitcast_convert_type(word << 16, F32))
        hi.append(lax.bitcast_convert_type(word & jnp.uint32(0xFFFF0000), F32))
    return jnp.concatenate(lo + hi, axis=-1)


def _layer_norm(x, g, b):
    mu = jnp.mean(x, axis=-1, keepdims=True)
    xc = x - mu
    var = jnp.mean(xc * xc, axis=-1, keepdims=True)
    return xc * lax.rsqrt(var + LN_EPS) * g + b


def _proj_kernel(x_ref, w_ref, o_ref):
    o_ref[...] = jnp.dot(x_ref[...].astype(BF16), w_ref[...],
                         preferred_element_type=F32).astype(o_ref.dtype)


def _proj(x2d, w_bf, out_dtype, tm):
    m, k = x2d.shape
    n = w_bf.shape[1]
    return pl.pallas_call(
        _proj_kernel,
        grid=(m // tm,),
        in_specs=[pl.BlockSpec((tm, k), lambda i: (i, 0)), _resident((k, n))],
        out_specs=pl.BlockSpec((tm, n), lambda i: (i, 0)),
        out_shape=jax.ShapeDtypeStruct((m, n), out_dtype),
        compiler_params=_params(),
        name="proj",
    )(x2d, w_bf)


def _gelu(x):
    return 0.5 * x * (1.0 + lax.erf(x * (2.0 ** -0.5)))


def _gmlp_epilogue(z, g_ref, b_ref, o_ref, tok_w):
    u = _gelu(z[:, :tok_w])
    v = _gelu(z[:, tok_w:2 * tok_w])
    v = _layer_norm(v, g_ref[...], b_ref[...])
    o_ref[:, :tok_w] = u.astype(o_ref.dtype)
    o_ref[:, tok_w:2 * tok_w] = v.astype(o_ref.dtype)
    o_ref[:, 2 * tok_w:] = z[:, 2 * tok_w:].astype(o_ref.dtype)


def _memory_attention(qm, kvm):
    outs = []
    for h in range(MEM_HEADS):
        q = qm[:, h * MEM_HEAD_DIM:(h + 1) * MEM_HEAD_DIM]
        k = kvm[:, h * MEM_HEAD_DIM:(h + 1) * MEM_HEAD_DIM]
        v = kvm[:, MEM_W + h * MEM_HEAD_DIM:MEM_W + (h + 1) * MEM_HEAD_DIM]
        s = lax.dot_general(q, k, (((1,), (1,)), ((), ())),
                            preferred_element_type=F32) * (MEM_HEAD_DIM ** -0.5)
        m = jnp.max(s, axis=-1, keepdims=True)
        p = jnp.exp(s - m)
        p = p * (1.0 / jnp.sum(p, axis=-1, keepdims=True))
        outs.append(jnp.dot(p.astype(BF16), v, preferred_element_type=F32))
    return jnp.concatenate(outs, axis=-1)


def _attn_kernel(sinks_ref, q_ref, kv_ref, kvp_ref, qm_ref, kvm_ref, o_ref, *,
                 tq, seq, slopes, tok_w):
    i = pl.program_id(0)
    n_sub = tq // WINDOW
    gqa = len(slopes) // N_KV_HEADS
    qi = lax.broadcasted_iota(jnp.int32, (WINDOW, WINDOW), 0)
    c = lax.broadcasted_iota(jnp.int32, (WINDOW, WINDOW), 1)
    own = c <= qi
    dist_f = jnp.where(own, qi - c, WINDOW + qi - c).astype(F32)
    nt = (((1,), (1,)), ((), ()))
    for sb in range(n_sub):
        r0 = sb * WINDOW
        if sb == 0:
            kv_prev = kvp_ref[...]
            reach = jnp.where(((i * tq) % seq) == 0, 0, WINDOW)
            valid = c <= qi + reach
        else:
            kv_prev = kv_ref[r0 - WINDOW:r0, :]
            valid = None
        kv_cur = kv_ref[r0:r0 + WINDOW, :]
        for kh in range(N_KV_HEADS):
            ks = slice(kh * HEAD_DIM, (kh + 1) * HEAD_DIM)
            vs = slice(KV_W + kh * HEAD_DIM, KV_W + (kh + 1) * HEAD_DIM)
            heads = range(kh * gqa, (kh + 1) * gqa)
            q = jnp.concatenate(
                [q_ref[r0:r0 + WINDOW, h * HEAD_DIM:(h + 1) * HEAD_DIM] for h in heads], axis=0)
            s_own = lax.dot_general(q, kv_cur[:, ks], nt, preferred_element_type=F32)
            s_prev = lax.dot_general(q, kv_prev[:, ks], nt, preferred_element_type=F32)
            p_own, p_prev = [], []
            for g, h in enumerate(heads):
                rows = slice(g * WINDOW, (g + 1) * WINDOW)
                logits = (jnp.where(own, s_own[rows], s_prev[rows]) * (HEAD_DIM ** -0.5)
                          - slopes[h] * dist_f)
                if valid is not None:
                    logits = jnp.where(valid, logits, NEG_INF)
                sink = sinks_ref[h]
                m = jnp.maximum(jnp.max(logits, axis=-1, keepdims=True), sink)
                p = jnp.exp(logits - m)
                probs = p * (1.0 / (jnp.sum(p, axis=-1, keepdims=True) + jnp.exp(sink - m)))
                p_own.append(jnp.where(own, probs, 0.0).astype(BF16))
                p_prev.append(jnp.where(own, 0.0, probs).astype(BF16))
            out = (jnp.dot(jnp.concatenate(p_own, axis=0), kv_cur[:, vs],
                           preferred_element_type=F32)
                   + jnp.dot(jnp.concatenate(p_prev, axis=0), kv_prev[:, vs],
                             preferred_element_type=F32))
            o_ref[r0:r0 + WINDOW, kh * gqa * HEAD_DIM:(kh + 1) * gqa * HEAD_DIM] = jnp.concatenate(
                [out[g * WINDOW:(g + 1) * WINDOW] for g in range(gqa)], axis=1).astype(o_ref.dtype)
    o_ref[:, tok_w:] = _memory_attention(qm_ref[...], kvm_ref[...]).astype(o_ref.dtype)


def _attn_mixer(proj, kvm, sinks, *, seq, mem_len, tq):
    t = proj.shape[0]
    tok_w = proj.shape[1] - 2 * KV_W - MEM_W
    n_heads = tok_w // HEAD_DIM
    assert tok_w % (2 * KV_W) == 0 and tok_w % MEM_W == 0 and seq % tq == 0
    kv_col = tok_w // (2 * KV_W)
    qm_col = (tok_w + 2 * KV_W) // MEM_W
    sub = tq // WINDOW
    kern = functools.partial(_attn_kernel, tq=tq, seq=seq, slopes=_alibi_slopes(n_heads),
                             tok_w=tok_w)
    return pl.pallas_call(
        kern,
        grid=(t // tq,),
        in_specs=[
            pl.BlockSpec(memory_space=pltpu.SMEM),
            pl.BlockSpec((tq, tok_w), lambda i: (i, 0)),
            pl.BlockSpec((tq, 2 * KV_W), lambda i: (i, kv_col)),
            pl.BlockSpec((WINDOW, 2 * KV_W), lambda i: (jnp.maximum(i * sub - 1, 0), kv_col)),
            pl.BlockSpec((tq, MEM_W), lambda i: (i, qm_col)),
            pl.BlockSpec((mem_len, 2 * MEM_W), lambda i: ((i * tq) // seq, 0)),
        ],
        out_specs=pl.BlockSpec((tq, tok_w + MEM_W), lambda i: (i, 0)),
        out_shape=jax.ShapeDtypeStruct((t, tok_w + MEM_W), BF16),
        compiler_params=_params(),
        name="attn_mixer",
    )(sinks, proj, proj, proj, proj, kvm)


def _gmlp_kernel(u_ref, v_ref, qm_ref, kvm_ref, ws_ref, bs_ref, o_ref, *, tq, tok_w):
    n_sub = tq // CHUNK
    n_grp = tok_w // GM_GROUP_DIM
    row = lax.broadcasted_iota(jnp.int32, (CHUNK, CHUNK), 0)
    col = lax.broadcasted_iota(jnp.int32, (CHUNK, CHUNK), 1)
    lower = row >= col
    for g in range(n_grp):
        w = jnp.where(lower, ws_ref[g], 0.0).astype(BF16)
        bias = bs_ref[:, g:g + 1]
        c0 = g * GM_GROUP_DIM
        for sb in range(n_sub):
            r0 = sb * CHUNK
            sv = jnp.dot(w, v_ref[r0:r0 + CHUNK, c0:c0 + GM_GROUP_DIM],
                         preferred_element_type=F32) + bias
            u = u_ref[r0:r0 + CHUNK, c0:c0 + GM_GROUP_DIM].astype(F32)
            o_ref[r0:r0 + CHUNK, c0:c0 + GM_GROUP_DIM] = (u * sv).astype(o_ref.dtype)
    o_ref[:, tok_w:] = _memory_attention(qm_ref[...], kvm_ref[...]).astype(o_ref.dtype)


def _gmlp_mixer(proj, kvm, w_s, b_s, *, seq, mem_len, tq):
    t = proj.shape[0]
    tok_w = (proj.shape[1] - MEM_W) // 2
    n_grp = tok_w // GM_GROUP_DIM
    assert (2 * tok_w) % MEM_W == 0 and seq % tq == 0
    kern = functools.partial(_gmlp_kernel, tq=tq, tok_w=tok_w)
    return pl.pallas_call(
        kern,
        grid=(t // tq,),
        in_specs=[
            pl.BlockSpec((tq, tok_w), lambda i: (i, 0)),
            pl.BlockSpec((tq, tok_w), lambda i: (i, 1)),
            pl.BlockSpec((tq, MEM_W), lambda i: (i, (2 * tok_w) // MEM_W)),
            pl.BlockSpec((mem_len, 2 * MEM_W), lambda i: ((i * tq) // seq, 0)),
            _resident((n_grp, CHUNK, CHUNK)),
            _resident((CHUNK, n_grp)),
        ],
        out_specs=pl.BlockSpec((tq, tok_w + MEM_W), lambda i: (i, 0)),
        out_shape=jax.ShapeDtypeStruct((t, tok_w + MEM_W), BF16),
        compiler_params=_params(),
        name="gmlp_mixer",
    )(proj, proj, proj, kvm, w_s, b_s.T)


def _post_kernel(mix_ref, x_ref, wo_ref, g_ref, b_ref, wrh_ref, wrl_ref, br_ref,
                 x1_ref, xp_ref, route_ref, cnt_ref, carry_ref, *, tm, sub, alpha):
    i = pl.program_id(0)

    @pl.when(i == 0)
    def _():
        carry_ref[...] = jnp.zeros_like(carry_ref)

    carry = carry_ref[0:1, :]
    for r0 in range(0, tm, sub):
        carry = _post_rows(r0, sub, carry, mix_ref, x_ref, wo_ref, g_ref, b_ref, wrh_ref, wrl_ref,
                           br_ref, x1_ref, xp_ref, route_ref, alpha)
    carry_ref[...] = jnp.broadcast_to(carry, carry_ref.shape)
    cnt_ref[...] = jnp.broadcast_to(carry, cnt_ref.shape)


def _post_rows(r0, tm, carry, mix_ref, x_ref, wo_ref, g_ref, b_ref, wrh_ref, wrl_ref, br_ref,
               x1_ref, xp_ref, route_ref, alpha):
    rows = slice(r0, r0 + tm)
    y = jnp.dot(mix_ref[rows, :], wo_ref[...], preferred_element_type=F32)
    x1 = _layer_norm(alpha * x_ref[rows, :] + y, g_ref[...], b_ref[...])
    x1_ref[rows, :] = x1
    _pack_rows(xp_ref, x1, row0=r0)

    xh = x1.astype(BF16)
    xl = (x1 - xh.astype(F32)).astype(BF16)
    logits = (jnp.dot(xh, wrh_ref[...], preferred_element_type=F32)
              + (jnp.dot(xh, wrl_ref[...], preferred_element_type=F32)
                 + jnp.dot(xl, wrh_ref[...], preferred_element_type=F32))) + br_ref[...]

    lane = lax.broadcasted_iota(jnp.int32, (tm, LANES), 1)
    lane_f = lane.astype(F32)
    big = float(LANES)
    is_grp = lane < N_GROUPS
    gl = jnp.where(is_grp, logits, NEG_INF)
    gmax = jnp.max(gl, axis=-1, keepdims=True)
    g_sel = jnp.min(jnp.where(gl == gmax, lane_f, big), axis=-1, keepdims=True)
    p_grp = 1.0 / jnp.sum(jnp.where(is_grp, jnp.exp(gl - gmax), 0.0), axis=-1, keepdims=True)

    e_lo = N_GROUPS + g_sel * EXPERTS_PER_GROUP
    in_grp = (lane_f >= e_lo) & (lane_f < e_lo + EXPERTS_PER_GROUP)
    el = jnp.where(in_grp, logits, NEG_INF)
    m1 = jnp.max(el, axis=-1, keepdims=True)
    i1 = jnp.min(jnp.where(el == m1, lane_f, big), axis=-1, keepdims=True)
    el2 = jnp.where(lane_f == i1, NEG_INF, el)
    m2 = jnp.max(el2, axis=-1, keepdims=True)
    i2 = jnp.min(jnp.where(el2 == m2, lane_f, big), axis=-1, keepdims=True)
    z = jnp.sum(jnp.where(in_grp, jnp.exp(el - m1), 0.0), axis=-1, keepdims=True)
    tp1 = 1.0 / z
    tp2 = jnp.exp(m2 - m1) / z
    gate1 = p_grp * tp1 / (tp1 + tp2)
    gate2 = p_grp * tp2 / (tp1 + tp2)
    e1 = i1 - N_GROUPS
    e2 = i2 - N_GROUPS

    oh1 = lane_f == e1
    oh2 = lane_f == e2
    c = jnp.where(oh1 | oh2, 1.0, 0.0)
    r = lax.broadcasted_iota(jnp.int32, (tm, tm), 0)
    s = lax.broadcasted_iota(jnp.int32, (tm, tm), 1)
    tri = jnp.where(r > s, 1.0, 0.0).astype(BF16)
    prefix = jnp.dot(tri, c.astype(BF16), preferred_element_type=F32) + carry
    rank1 = jnp.sum(jnp.where(oh1, prefix, 0.0), axis=-1, keepdims=True)
    rank2 = jnp.sum(jnp.where(oh2, prefix, 0.0), axis=-1, keepdims=True)

    route = jnp.where(lane == 0, e1, 0.0)
    route = jnp.where(lane == 1, e2, route)
    route = jnp.where(lane == 2, rank1, route)
    route = jnp.where(lane == 3, rank2, route)
    route = jnp.where(lane == 4, gate1, route)
    route = jnp.where(lane == 5, gate2, route)
    route_ref[rows, :] = route
    return carry + jnp.sum(c, axis=0, keepdims=True)


def _post(mix, x2d, wo_bf, ln_g, ln_b, wr_hi, wr_lo, b_r, *, tm, sub, alpha):
    t, d = x2d.shape
    kern = functools.partial(_post_kernel, tm=tm, sub=sub, alpha=alpha)
    return pl.pallas_call(
        kern,
        grid=(t // tm,),
        in_specs=[
            pl.BlockSpec((tm, d), lambda i: (i, 0)),
            pl.BlockSpec((tm, d), lambda i: (i, 0)),
            _resident((d, d)),
            _resident((1, d)), _resident((1, d)),
            _resident((d, LANES)), _resident((d, LANES)), _resident((1, LANES)),
        ],
        out_specs=[
            pl.BlockSpec((tm, d), lambda i: (i, 0)),
            pl.BlockSpec((tm * (d // (2 * LANES)), LANES), lambda i: (i, 0)),
            pl.BlockSpec((tm, LANES), lambda i: (i, 0)),
            pl.BlockSpec((8, LANES), lambda i: (0, 0)),
        ],
        out_shape=[
            jax.ShapeDtypeStruct((t, d), F32),
            jax.ShapeDtypeStruct((t * (d // (2 * LANES)), LANES), U32),
            jax.ShapeDtypeStruct((t, LANES), F32),
            jax.ShapeDtypeStruct((8, LANES), F32),
        ],
        scratch_shapes=[pltpu.VMEM((8, LANES), F32)],
        compiler_params=_params(),
        name="post",
    )(mix, x2d, wo_bf, ln_g.reshape(1, d), ln_b.reshape(1, d), wr_hi, wr_lo, b_r)


def _invert_kernel(dest_ref, out_ref):
    def clear(r, carry):
        out_ref[r] = 0
        return carry
    lax.fori_loop(0, out_ref.shape[0], clear, 0, unroll=16)

    def place(a, carry):
        out_ref[dest_ref[a]] = a
        return carry
    lax.fori_loop(0, dest_ref.shape[0], place, 0, unroll=16)


def _invert(dest, n_rows):
    smem = pl.BlockSpec(memory_space=pltpu.SMEM)
    return pl.pallas_call(
        _invert_kernel,
        in_specs=[smem], out_specs=smem,
        out_shape=jax.ShapeDtypeStruct((n_rows,), jnp.int32),
        name="invert",
    )(dest)


def _moe_kernel(blk_e_ref, nxt_e_ref, nvalid_ref, n_used_ref, tok0_ref, tokn_ref,
                x_hbm, wg_hbm, wu_hbm, wd_hbm, y_ref,
                xbuf, wg_st, wu_st, wd_st, wg_bf, wu_bf, wd_bf, sem_g, sem_w,
                *, rb, nc, nblk, layer):
    s = pl.program_id(0)
    n_used = n_used_ref[0]
    slot = lax.rem(s, 2)
    other = 1 - slot
    unroll = 8

    def nvalid(b):
        return jnp.where(b < nblk, nvalid_ref[jnp.minimum(b, nblk - 1)], 0)

    def gather_row(tok_ref, j, dst_slot):
        src = x_hbm.at[pl.ds(pl.multiple_of(tok_ref[j], nc), nc)]
        pltpu.make_async_copy(src, xbuf.at[dst_slot, pl.ds(j * nc, nc)],
                              sem_g.at[dst_slot]).start()

    def for_rows(n, row_fn):
        def group(g, carry):
            for u in range(unroll):
                row_fn(g * unroll + u)
            return carry
        lax.fori_loop(0, n // unroll, group, 0)

        def single(j, carry):
            row_fn(j)
            return carry
        lax.fori_loop((n // unroll) * unroll, n, single, 0)

    def wait_rows(n, make_copy):
        p = rb
        while p >= 1:
            @pl.when((n & p) != 0)
            def _(p=p):
                make_copy(p).wait()
            p //= 2

    def wait_gather(n, sl):
        wait_rows(n, lambda p: pltpu.make_async_copy(
            x_hbm.at[pl.ds(0, p * nc)], xbuf.at[sl, pl.ds(0, p * nc)], sem_g.at[sl]))

    weight_copies = ((wg_hbm, wg_st, 0), (wu_hbm, wu_st, 1), (wd_hbm, wd_st, 2))

    def fetch_weights(e):
        for hbm, stage, sem_i in weight_copies:
            pltpu.make_async_copy(hbm.at[layer, e], stage, sem_w.at[sem_i]).start()

    @pl.when(s == 0)
    def _():
        xbuf[...] = jnp.zeros_like(xbuf)
        for_rows(nvalid(0), lambda j: gather_row(tok0_ref, j, 0))
        fetch_weights(blk_e_ref[0])

    b_cur = jnp.minimum(s, nblk - 1)
    e_cur = blk_e_ref[b_cur]
    new_expert = (s == 0) | (e_cur != blk_e_ref[jnp.maximum(b_cur - 1, 0)])

    @pl.when((s < n_used) & new_expert)
    def _():
        for hbm, stage, sem_i in weight_copies:
            pltpu.make_async_copy(hbm.at[layer, 0], stage, sem_w.at[sem_i]).wait()
        wg_bf[...] = wg_st[...].astype(BF16)
        wu_bf[...] = wu_st[...].astype(BF16)
        wd_bf[...] = wd_st[...].astype(BF16)
        e_next = nxt_e_ref[b_cur]

        @pl.when(e_next != e_cur)
        def _():
            fetch_weights(e_next)

    @pl.when(s < n_used)
    def _():
        for_rows(nvalid(s + 1), lambda j: gather_row(tokn_ref, j, other))
        wait_gather(nvalid(s), slot)
        xb = _unpack_rows(xbuf, nc, slot).astype(BF16)
        hg = jnp.dot(xb, wg_bf[...], preferred_element_type=F32)
        hu = jnp.dot(xb, wu_bf[...], preferred_element_type=F32)
        h = (jax.nn.silu(hg) * hu).astype(BF16)
        _pack_rows(y_ref, jnp.dot(h, wd_bf[...], preferred_element_type=F32))

    @pl.when(s >= n_used)
    def _():
        y_ref[...] = jnp.zeros_like(y_ref)


def _moe(x1, row_tok, blk_e, nxt_e, nvalid, n_used, w_gate, w_up, w_down, *, layer, rb):
    d_e, d = w_down.shape[-2:]
    n_chunks = d // (2 * LANES)
    n_rows = row_tok.shape[0]
    nblk = n_rows // rb

    def smem(index_map):
        return pl.BlockSpec((rb,), index_map, memory_space=pltpu.SMEM)

    any_spec = pl.BlockSpec(memory_space=pl.ANY)
    grid_spec = pltpu.PrefetchScalarGridSpec(
        num_scalar_prefetch=4,
        grid=(nblk,),
        in_specs=[
            smem(lambda s, *_: (0,)),
            smem(lambda s, *_: (jnp.minimum(s + 1, nblk - 1),)),
            any_spec, any_spec, any_spec, any_spec,
        ],
        out_specs=pl.BlockSpec((rb * n_chunks, LANES), lambda s, *_: (s, 0)),
        scratch_shapes=[pltpu.VMEM((2, rb * n_chunks, LANES), U32),
                        pltpu.VMEM((d, d_e), F32), pltpu.VMEM((d, d_e), F32),
                        pltpu.VMEM((d_e, d), F32),
                        pltpu.VMEM((d, d_e), BF16), pltpu.VMEM((d, d_e), BF16),
                        pltpu.VMEM((d_e, d), BF16),
                        pltpu.SemaphoreType.DMA((2,)), pltpu.SemaphoreType.DMA((3,))],
    )
    return pl.pallas_call(
        functools.partial(_moe_kernel, rb=rb, nc=n_chunks, nblk=nblk, layer=layer),
        grid_spec=grid_spec,
        out_shape=jax.ShapeDtypeStruct((n_rows * n_chunks, LANES), U32),
        compiler_params=_params(),
        name="moe",
    )(blk_e, nxt_e, nvalid, n_used, row_tok, row_tok, x1, w_gate, w_up, w_down)


def _combine_kernel(d0a_ref, d0b_ref, dna_ref, dnb_ref, x1_ref, route_ref, g_ref, b_ref, y_hbm,
                    *rest, tm, nc, alpha, mode, tok_w):
    if mode == "attn":
        w_ref, x2_ref, o_ref, gbuf, sem = rest
    elif mode == "gmlp":
        w_ref, ng_ref, nb_ref, x2_ref, o_ref, gbuf, sem = rest
    else:
        x2_ref, gbuf, sem = rest
    i = pl.program_id(0)
    slot = lax.rem(i, 2)
    unroll = 8

    def issue(da_ref, db_ref, sl):
        def group(g, carry):
            for u in range(unroll):
                j = g * unroll + u
                for k, d_ref in enumerate((da_ref, db_ref)):
                    src = y_hbm.at[pl.ds(pl.multiple_of(d_ref[j], nc), nc)]
                    pltpu.make_async_copy(src, gbuf.at[sl, k, pl.ds(j * nc, nc)],
                                          sem.at[sl]).start()
            return carry
        lax.fori_loop(0, tm // unroll, group, 0)

    @pl.when(i == 0)
    def _():
        issue(d0a_ref, d0b_ref, 0)

    @pl.when(i + 1 < pl.num_programs(0))
    def _():
        issue(dna_ref, dnb_ref, 1 - slot)

    for k in range(TOP_K):
        pltpu.make_async_copy(y_hbm.at[pl.ds(0, tm * nc)], gbuf.at[slot, k], sem.at[slot]).wait()

    route = route_ref[...]
    y = (_unpack_rows(gbuf, nc, slot, 0) * route[:, 4:5]
         + _unpack_rows(gbuf, nc, slot, 1) * route[:, 5:6])
    x2 = _layer_norm(alpha * x1_ref[...] + y, g_ref[...], b_ref[...])
    x2_ref[...] = x2
    if mode == "attn":
        o_ref[...] = jnp.dot(x2.astype(BF16), w_ref[...],
                             preferred_element_type=F32).astype(o_ref.dtype)
    elif mode == "gmlp":
        _gmlp_epilogue(jnp.dot(x2.astype(BF16), w_ref[...], preferred_element_type=F32),
                       ng_ref, nb_ref, o_ref, tok_w)


def _combine(y, x1, route, dest_a, dest_b, ln_g, ln_b, *, tm, alpha, w_bf=None, gmlp_norm=None):
    t, d = x1.shape
    n_chunks = d // (2 * LANES)
    mode = "none" if w_bf is None else ("attn" if gmlp_norm is None else "gmlp")
    n_steps = t // tm

    def smem(index_map):
        return pl.BlockSpec((tm,), index_map, memory_space=pltpu.SMEM)

    def first(i):
        return (0,)

    def nxt(i):
        return (jnp.minimum(i + 1, n_steps - 1),)

    in_specs = [smem(first), smem(first), smem(nxt), smem(nxt),
                pl.BlockSpec((tm, d), lambda i: (i, 0)),
                pl.BlockSpec((tm, LANES), lambda i: (i, 0)),
                _resident((1, d)), _resident((1, d)),
                pl.BlockSpec(memory_space=pl.ANY)]
    args = [dest_a, dest_b, dest_a, dest_b, x1, route, ln_g.reshape(1, d), ln_b.reshape(1, d), y]
    out_specs = [pl.BlockSpec((tm, d), lambda i: (i, 0))]
    out_shape = [jax.ShapeDtypeStruct((t, d), F32)]
    tok_w = None
    if w_bf is not None:
        n = w_bf.shape[1]
        in_specs.append(_resident((d, n)))
        args.append(w_bf)
        out_specs.append(pl.BlockSpec((tm, n), lambda i: (i, 0)))
        out_shape.append(jax.ShapeDtypeStruct((t, n), BF16))
    if gmlp_norm is not None:
        tok_w = gmlp_norm[0].shape[-1]
        in_specs += [_resident((1, tok_w)), _resident((1, tok_w))]
        args += [gmlp_norm[0].reshape(1, tok_w), gmlp_norm[1].reshape(1, tok_w)]
    return pl.pallas_call(
        functools.partial(_combine_kernel, tm=tm, nc=n_chunks, alpha=alpha, mode=mode,
                          tok_w=tok_w),
        grid=(n_steps,),
        in_specs=in_specs,
        out_specs=out_specs,
        out_shape=out_shape,
        scratch_shapes=[pltpu.VMEM((2, TOP_K, tm * n_chunks, LANES), U32),
                        pltpu.SemaphoreType.DMA((2,))],
        compiler_params=_params(),
        name="combine_" + mode,
    )(*args)


def _split_bf16(w):
    hi = w.astype(BF16)
    lo = (w - hi.astype(F32)).astype(BF16)
    return hi, lo


def kernel(x, mem, a_w_in, a_sinks, b_w_in, b_norm_g, b_norm_b, b_w_spatial, b_b_spatial,
           w_mem_kv, w_out, ln_g, ln_b, w_router_group, b_router_group,
           w_router_expert, b_router_expert, w_gate, w_up, w_down):
    bsz, seq, d = x.shape
    mem_len = mem.shape[1]
    depth = w_out.shape[0]
    t = bsz * seq
    alpha = (2 * depth) ** 0.25
    rb = ROW_BLOCK
    n_rows = -(-(t * TOP_K + N_EXPERTS * (rb - 1)) // rb) * rb
    nblk = n_rows // rb

    def in_proj_weights(i):
        if i % 2 == 0:
            return a_w_in[i // 2].astype(BF16), None
        return b_w_in[i // 2].astype(BF16), (b_norm_g[i // 2], b_norm_b[i // 2])

    x2d = x.reshape(t, d)
    mem2d = mem.reshape(bsz * mem_len, d)
    proj = _proj(x2d, in_proj_weights(0)[0], BF16, tm=512)
    for i in range(depth):
        j = i // 2
        kvm = _proj(mem2d, w_mem_kv[i].astype(BF16), BF16, tm=min(256, bsz * mem_len))
        if i % 2 == 0:
            mix = _attn_mixer(proj, kvm, a_sinks[j], seq=seq, mem_len=mem_len, tq=256)
        else:
            mix = _gmlp_mixer(proj, kvm, b_w_spatial[j], b_b_spatial[j],
                              seq=seq, mem_len=mem_len, tq=512)

        w_r = jnp.concatenate([w_router_group[i], w_router_expert[i]], axis=1)
        w_r = jnp.pad(w_r, ((0, 0), (0, LANES - w_r.shape[1])))
        b_r = jnp.concatenate([b_router_group[i], b_router_expert[i]])
        b_r = jnp.pad(b_r, (0, LANES - b_r.shape[0])).reshape(1, LANES)
        wr_hi, wr_lo = _split_bf16(w_r)
        x1, x1p, route, cnt = _post(mix, x2d, w_out[i].astype(BF16), ln_g[i, 0], ln_b[i, 0],
                                    wr_hi, wr_lo, b_r, tm=512, sub=256, alpha=alpha)

        counts = cnt[0, :N_EXPERTS].astype(jnp.int32)
        padded = (counts + rb - 1) // rb * rb
        pend = jnp.cumsum(padded)
        pstart = (pend - padded).astype(jnp.int32)
        n_used = (pend[-1] // rb).astype(jnp.int32).reshape(1)
        blk = jnp.arange(nblk, dtype=jnp.int32)
        blk_e = jnp.searchsorted(pend, jnp.minimum(blk, n_used[0] - 1) * rb,
                                 side='right').astype(jnp.int32)
        nxt_blk = pend[blk_e] // rb
        nxt_e = jnp.where(nxt_blk < n_used[0], blk_e[jnp.minimum(nxt_blk, nblk - 1)], blk_e)
        nvalid = jnp.where(blk < n_used[0],
                           jnp.clip(pstart[blk_e] + counts[blk_e] - blk * rb, 0, rb), 0)
        ri = route[:, :4].astype(jnp.int32)
        e_hot = ri[:, :TOP_K, None] == jnp.arange(N_EXPERTS, dtype=jnp.int32)
        dest = jnp.sum(jnp.where(e_hot, pstart, 0), axis=-1) + ri[:, TOP_K:]
        n_chunks = d // (2 * LANES)
        row_tok = (_invert(dest.reshape(-1), n_rows) // TOP_K) * n_chunks

        y = _moe(x1p, row_tok, blk_e, nxt_e.astype(jnp.int32), nvalid.astype(jnp.int32),
                 n_used, w_gate, w_up, w_down, layer=i, rb=rb)
        dest_a, dest_b = dest[:, 0] * n_chunks, dest[:, 1] * n_chunks
        if i + 1 < depth:
            w_next, gmlp_norm = in_proj_weights(i + 1)
            x2d, proj = _combine(y, x1, route, dest_a, dest_b, ln_g[i, 1], ln_b[i, 1],
                                 tm=256, alpha=alpha, w_bf=w_next, gmlp_norm=gmlp_norm)
        else:
            (x2d,) = _combine(y, x1, route, dest_a, dest_b, ln_g[i, 1], ln_b[i, 1],
                              tm=256, alpha=alpha)
    return x2d.reshape(bsz, seq, d)
```

```python
import functools
import math

import jax
import jax.numpy as jnp
import numpy as np
from jax import lax
from jax.experimental import pallas as pl
from jax.experimental.pallas import tpu as pltpu

MEM_HEADS = 4
MEM_HEAD_DIM = 128
MEM_W = MEM_HEADS * MEM_HEAD_DIM
HEAD_DIM = 64
N_KV_HEADS = 4
KV_W = N_KV_HEADS * HEAD_DIM
WINDOW = 128
CHUNK = 128
GM_GROUP_DIM = 128
N_GROUPS = 8
EXPERTS_PER_GROUP = 8
N_EXPERTS = N_GROUPS * EXPERTS_PER_GROUP
TOP_K = 2
LN_EPS = 1e-5
NEG_INF = -1e30

LANES = 128
VMEM_LIMIT_BYTES = 56 * 1024 * 1024

ROW_BLOCK = 256
BF16 = jnp.bfloat16
F32 = jnp.float32
U32 = jnp.uint32


def _alibi_slopes(n):
    def pow2(m):
        start = 2.0 ** (-8.0 / m)
        return [start ** (i + 1) for i in range(m)]
    if math.log2(n).is_integer():
        s = pow2(n)
    else:
        c = 2 ** math.floor(math.log2(n))
        s = pow2(c) + pow2(2 * c)[0::2][: n - c]
    return [float(v) for v in np.asarray(s, dtype=np.float32)]


def _params(n_axes=1):
    return pltpu.CompilerParams(dimension_semantics=("arbitrary",) * n_axes,
                                vmem_limit_bytes=VMEM_LIMIT_BYTES)


def _resident(shape):
    nd = len(shape)
    return pl.BlockSpec(shape, lambda *_: (0,) * nd, pipeline_mode=pl.Buffered(1))


def _pack_rows(ref, val, *lead, row0=0):
    rows, d = val.shape
    nw = d // (2 * LANES)
    for c in range(nw):
        lo = val[:, c * LANES:(c + 1) * LANES].astype(BF16).astype(F32)
        hi = val[:, (c + nw) * LANES:(c + nw + 1) * LANES].astype(BF16).astype(F32)
        word = (lax.bitcast_convert_type(lo, U32) >> 16) | lax.bitcast_convert_type(hi, U32)
        ref[tuple(lead) + (pl.ds(row0 * nw + c, rows, stride=nw), slice(None))] = word


def _unpack_rows(ref, nw, *lead):
    rows = ref.shape[-2] // nw
    lo, hi = [], []
    for c in range(nw):
        word = ref[tuple(lead) + (pl.ds(c, rows, stride=nw), slice(None))]
        lo.append(lax.bitcast_convert_type(word << 16, F32))
        hi.append(lax.bitcast_convert_type(word & jnp.uint32(0xFFFF0000), F32))
    return jnp.concatenate(lo + hi, axis=-1)


def _layer_norm(x, g, b):
    mu = jnp.mean(x, axis=-1, keepdims=True)
    xc = x - mu
    var = jnp.mean(xc * xc, axis=-1, keepdims=True)
    return xc * lax.rsqrt(var + LN_EPS) * g + b


def _proj_kernel(x_ref, w_ref, o_ref):
    o_ref[...] = jnp.dot(x_ref[...].astype(BF16), w_ref[...],
                         preferred_element_type=F32).astype(o_ref.dtype)


def _proj(x2d, w_bf, out_dtype, tm):
    m, k = x2d.shape
    n = w_bf.shape[1]
    return pl.pallas_call(
        _proj_kernel,
        grid=(m // tm,),
        in_specs=[pl.BlockSpec((tm, k), lambda i: (i, 0)), _resident((k, n))],
        out_specs=pl.BlockSpec((tm, n), lambda i: (i, 0)),
        out_shape=jax.ShapeDtypeStruct((m, n), out_dtype),
        compiler_params=_params(),
        name="proj",
    )(x2d, w_bf)


def _gelu(x):
    return 0.5 * x * (1.0 + lax.erf(x * (2.0 ** -0.5)))


def _gmlp_epilogue(z, g_ref, b_ref, o_ref, tok_w):
    u = _gelu(z[:, :tok_w])
    v = _gelu(z[:, tok_w:2 * tok_w])
    v = _layer_norm(v, g_ref[...], b_ref[...])
    o_ref[:, :tok_w] = u.astype(o_ref.dtype)
    o_ref[:, tok_w:2 * tok_w] = v.astype(o_ref.dtype)
    o_ref[:, 2 * tok_w:] = z[:, 2 * tok_w:].astype(o_ref.dtype)


def _memory_attention(qm, kvm):
    outs = []
    for h in range(MEM_HEADS):
        q = qm[:, h * MEM_HEAD_DIM:(h + 1) * MEM_HEAD_DIM]
        k = kvm[:, h * MEM_HEAD_DIM:(h + 1) * MEM_HEAD_DIM]
        v = kvm[:, MEM_W + h * MEM_HEAD_DIM:MEM_W + (h + 1) * MEM_HEAD_DIM]
        s = lax.dot_general(q, k, (((1,), (1,)), ((), ())),
                            preferred_element_type=F32) * (MEM_HEAD_DIM ** -0.5)
        m = jnp.max(s, axis=-1, keepdims=True)
        p = jnp.exp(s - m)
        p = p * (1.0 / jnp.sum(p, axis=-1, keepdims=True))
        outs.append(jnp.dot(p.astype(BF16), v, preferred_element_type=F32))
    return jnp.concatenate(outs, axis=-1)


def _attn_kernel(sinks_ref, q_ref, kv_ref, kvp_ref, qm_ref, kvm_ref, o_ref, *,
                 tq, seq, slopes, tok_w):
    i = pl.program_id(0)
    n_sub = tq // WINDOW
    gqa = len(slopes) // N_KV_HEADS
    qi = lax.broadcasted_iota(jnp.int32, (WINDOW, WINDOW), 0)
    c = lax.broadcasted_iota(jnp.int32, (WINDOW, WINDOW), 1)
    own = c <= qi
    dist_f = jnp.where(own, qi - c, WINDOW + qi - c).astype(F32)
    nt = (((1,), (1,)), ((), ()))
    for sb in range(n_sub):
        r0 = sb * WINDOW
        if sb == 0:
            kv_prev = kvp_ref[...]
            reach = jnp.where(((i * tq) % seq) == 0, 0, WINDOW)
            valid = c <= qi + reach
        else:
            kv_prev = kv_ref[r0 - WINDOW:r0, :]
            valid = None
        kv_cur = kv_ref[r0:r0 + WINDOW, :]
        for kh in range(N_KV_HEADS):
            ks = slice(kh * HEAD_DIM, (kh + 1) * HEAD_DIM)
            vs = slice(KV_W + kh * HEAD_DIM, KV_W + (kh + 1) * HEAD_DIM)
            heads = range(kh * gqa, (kh + 1) * gqa)
            q = jnp.concatenate(
                [q_ref[r0:r0 + WINDOW, h * HEAD_DIM:(h + 1) * HEAD_DIM] for h in heads], axis=0)
            s_own = lax.dot_general(q, kv_cur[:, ks], nt, preferred_element_type=F32)
            s_prev = lax.dot_general(q, kv_prev[:, ks], nt, preferred_element_type=F32)
            p_own, p_prev = [], []
            for g, h in enumerate(heads):
                rows = slice(g * WINDOW, (g + 1) * WINDOW)
                logits = (jnp.where(own, s_own[rows], s_prev[rows]) * (HEAD_DIM ** -0.5)
                          - slopes[h] * dist_f)
                if valid is not None:
                    logits = jnp.where(valid, logits, NEG_INF)
                sink = sinks_ref[h]
                m = jnp.maximum(jnp.max(logits, axis=-1, keepdims=True), sink)
                p = jnp.exp(logits - m)
                probs = p * (1.0 / (jnp.sum(p, axis=-1, keepdims=True) + jnp.exp(sink - m)))
                p_own.append(jnp.where(own, probs, 0.0).astype(BF16))
                p_prev.append(jnp.where(own, 0.0, probs).astype(BF16))
            out = (jnp.dot(jnp.concatenate(p_own, axis=0), kv_cur[:, vs],
                           preferred_element_type=F32)
                   + jnp.dot(jnp.concatenate(p_prev, axis=0), kv_prev[:, vs],
                             preferred_element_type=F32))
            o_ref[r0:r0 + WINDOW, kh * gqa * HEAD_DIM:(kh + 1) * gqa * HEAD_DIM] = jnp.concatenate(
                [out[g * WINDOW:(g + 1) * WINDOW] for g in range(gqa)], axis=1).astype(o_ref.dtype)
    o_ref[:, tok_w:] = _memory_attention(qm_ref[...], kvm_ref[...]).astype(o_ref.dtype)


def _attn_mixer(proj, kvm, sinks, *, seq, mem_len, tq):
    t = proj.shape[0]
    tok_w = proj.shape[1] - 2 * KV_W - MEM_W
    n_heads = tok_w // HEAD_DIM
    assert tok_w % (2 * KV_W) == 0 and tok_w % MEM_W == 0 and seq % tq == 0
    kv_col = tok_w // (2 * KV_W)
    qm_col = (tok_w + 2 * KV_W) // MEM_W
    sub = tq // WINDOW
    kern = functools.partial(_attn_kernel, tq=tq, seq=seq, slopes=_alibi_slopes(n_heads),
                             tok_w=tok_w)
    return pl.pallas_call(
        kern,
        grid=(t // tq,),
        in_specs=[
            pl.BlockSpec(memory_space=pltpu.SMEM),
            pl.BlockSpec((tq, tok_w), lambda i: (i, 0)),
            pl.BlockSpec((tq, 2 * KV_W), lambda i: (i, kv_col)),
            pl.BlockSpec((WINDOW, 2 * KV_W), lambda i: (jnp.maximum(i * sub - 1, 0), kv_col)),
            pl.BlockSpec((tq, MEM_W), lambda i: (i, qm_col)),
            pl.BlockSpec((mem_len, 2 * MEM_W), lambda i: ((i * tq) // seq, 0)),
        ],
        out_specs=pl.BlockSpec((tq, tok_w + MEM_W), lambda i: (i, 0)),
        out_shape=jax.ShapeDtypeStruct((t, tok_w + MEM_W), BF16),
        compiler_params=_params(),
        name="attn_mixer",
    )(sinks, proj, proj, proj, proj, kvm)


def _gmlp_kernel(u_ref, v_ref, qm_ref, kvm_ref, ws_ref, bs_ref, o_ref, *, tq, tok_w):
    n_sub = tq // CHUNK
    n_grp = tok_w // GM_GROUP_DIM
    row = lax.broadcasted_iota(jnp.int32, (CHUNK, CHUNK), 0)
    col = lax.broadcasted_iota(jnp.int32, (CHUNK, CHUNK), 1)
    lower = row >= col
    for g in range(n_grp):
        w = jnp.where(lower, ws_ref[g], 0.0).astype(BF16)
        bias = bs_ref[:, g:g + 1]
        c0 = g * GM_GROUP_DIM
        for sb in range(n_sub):
            r0 = sb * CHUNK
            sv = jnp.dot(w, v_ref[r0:r0 + CHUNK, c0:c0 + GM_GROUP_DIM],
                         preferred_element_type=F32) + bias
            u = u_ref[r0:r0 + CHUNK, c0:c0 + GM_GROUP_DIM].astype(F32)
            o_ref[r0:r0 + CHUNK, c0:c0 + GM_GROUP_DIM] = (u * sv).astype(o_ref.dtype)
    o_ref[:, tok_w:] = _memory_attention(qm_ref[...], kvm_ref[...]).astype(o_ref.dtype)


def _gmlp_mixer(proj, kvm, w_s, b_s, *, seq, mem_len, tq):
    t = proj.shape[0]
    tok_w = (proj.shape[1] - MEM_W) // 2
    n_grp = tok_w // GM_GROUP_DIM
    assert (2 * tok_w) % MEM_W == 0 and seq % tq == 0
    kern = functools.partial(_gmlp_kernel, tq=tq, tok_w=tok_w)
    return pl.pallas_call(
        kern,
        grid=(t // tq,),
        in_specs=[
            pl.BlockSpec((tq, tok_w), lambda i: (i, 0)),
            pl.BlockSpec((tq, tok_w), lambda i: (i, 1)),
            pl.BlockSpec((tq, MEM_W), lambda i: (i, (2 * tok_w) // MEM_W)),
            pl.BlockSpec((mem_len, 2 * MEM_W), lambda i: ((i * tq) // seq, 0)),
            _resident((n_grp, CHUNK, CHUNK)),
            _resident((CHUNK, n_grp)),
        ],
        out_specs=pl.BlockSpec((tq, tok_w + MEM_W), lambda i: (i, 0)),
        out_shape=jax.ShapeDtypeStruct((t, tok_w + MEM_W), BF16),
        compiler_params=_params(),
        name="gmlp_mixer",
    )(proj, proj, proj, kvm, w_s, b_s.T)


def _post_kernel(mix_ref, x_ref, wo_ref, g_ref, b_ref, wrh_ref, wrl_ref, br_ref,
                 x1_ref, xp_ref, route_ref, cnt_ref, carry_ref, *, tm, sub, alpha):
    i = pl.program_id(0)

    @pl.when(i == 0)
    def _():
        carry_ref[...] = jnp.zeros_like(carry_ref)

    carry = carry_ref[0:1, :]
    for r0 in range(0, tm, sub):
        carry = _post_rows(r0, sub, carry, mix_ref, x_ref, wo_ref, g_ref, b_ref, wrh_ref, wrl_ref,
                           br_ref, x1_ref, xp_ref, route_ref, alpha)
    carry_ref[...] = jnp.broadcast_to(carry, carry_ref.shape)
    cnt_ref[...] = jnp.broadcast_to(carry, cnt_ref.shape)


def _post_rows(r0, tm, carry, mix_ref, x_ref, wo_ref, g_ref, b_ref, wrh_ref, wrl_ref, br_ref,
               x1_ref, xp_ref, route_ref, alpha):
    rows = slice(r0, r0 + tm)
    y = jnp.dot(mix_ref[rows, :], wo_ref[...], preferred_element_type=F32)
    x1 = _layer_norm(alpha * x_ref[rows, :] + y, g_ref[...], b_ref[...])
    x1_ref[rows, :] = x1
    _pack_rows(xp_ref, x1, row0=r0)

    xh = x1.astype(BF16)
    xl = (x1 - xh.astype(F32)).astype(BF16)
    logits = (jnp.dot(xh, wrh_ref[...], preferred_element_type=F32)
              + (jnp.dot(xh, wrl_ref[...], preferred_element_type=F32)
                 + jnp.dot(xl, wrh_ref[...], preferred_element_type=F32))) + br_ref[...]

    lane = lax.broadcasted_iota(jnp.int32, (tm, LANES), 1)
    lane_f = lane.astype(F32)
    big = float(LANES)
    is_grp = lane < N_GROUPS
    gl = jnp.where(is_grp, logits, NEG_INF)
    gmax = jnp.max(gl, axis=-1, keepdims=True)
    g_sel = jnp.min(jnp.where(gl == gmax, lane_f, big), axis=-1, keepdims=True)
    p_grp = 1.0 / jnp.sum(jnp.where(is_grp, jnp.exp(gl - gmax), 0.0), axis=-1, keepdims=True)

    e_lo = N_GROUPS + g_sel * EXPERTS_PER_GROUP
    in_grp = (lane_f >= e_lo) & (lane_f < e_lo + EXPERTS_PER_GROUP)
    el = jnp.where(in_grp, logits, NEG_INF)
    m1 = jnp.max(el, axis=-1, keepdims=True)
    i1 = jnp.min(jnp.where(el == m1, lane_f, big), axis=-1, keepdims=True)
    el2 = jnp.where(lane_f == i1, NEG_INF, el)
    m2 = jnp.max(el2, axis=-1, keepdims=True)
    i2 = jnp.min(jnp.where(el2 == m2, lane_f, big), axis=-1, keepdims=True)
    z = jnp.sum(jnp.where(in_grp, jnp.exp(el - m1), 0.0), axis=-1, keepdims=True)
    tp1 = 1.0 / z
    tp2 = jnp.exp(m2 - m1) / z
    gate1 = p_grp * tp1 / (tp1 + tp2)
    gate2 = p_grp * tp2 / (tp1 + tp2)
    e1 = i1 - N_GROUPS
    e2 = i2 - N_GROUPS

    oh1 = lane_f == e1
    oh2 = lane_f == e2
    c = jnp.where(oh1 | oh2, 1.0, 0.0)
    r = lax.broadcasted_iota(jnp.int32, (tm, tm), 0)
    s = lax.broadcasted_iota(jnp.int32, (tm, tm), 1)
    tri = jnp.where(r > s, 1.0, 0.0).astype(BF16)
    prefix = jnp.dot(tri, c.astype(BF16), preferred_element_type=F32) + carry
    rank1 = jnp.sum(jnp.where(oh1, prefix, 0.0), axis=-1, keepdims=True)
    rank2 = jnp.sum(jnp.where(oh2, prefix, 0.0), axis=-1, keepdims=True)

    route = jnp.where(lane == 0, e1, 0.0)
    route = jnp.where(lane == 1, e2, route)
    route = jnp.where(lane == 2, rank1, route)
    route = jnp.where(lane == 3, rank2, route)
    route = jnp.where(lane == 4, gate1, route)
    route = jnp.where(lane == 5, gate2, route)
    route_ref[rows, :] = route
    return carry + jnp.sum(c, axis=0, keepdims=True)


def _post(mix, x2d, wo_bf, ln_g, ln_b, wr_hi, wr_lo, b_r, *, tm, sub, alpha):
    t, d = x2d.shape
    kern = functools.partial(_post_kernel, tm=tm, sub=sub, alpha=alpha)
    return pl.pallas_call(
        kern,
        grid=(t // tm,),
        in_specs=[
            pl.BlockSpec((tm, d), lambda i: (i, 0)),
            pl.BlockSpec((tm, d), lambda i: (i, 0)),
            _resident((d, d)),
            _resident((1, d)), _resident((1, d)),
            _resident((d, LANES)), _resident((d, LANES)), _resident((1, LANES)),
        ],
        out_specs=[
            pl.BlockSpec((tm, d), lambda i: (i, 0)),
            pl.BlockSpec((tm * (d // (2 * LANES)), LANES), lambda i: (i, 0)),
            pl.BlockSpec((tm, LANES), lambda i: (i, 0)),
            pl.BlockSpec((8, LANES), lambda i: (0, 0)),
        ],
        out_shape=[
            jax.ShapeDtypeStruct((t, d), F32),
            jax.ShapeDtypeStruct((t * (d // (2 * LANES)), LANES), U32),
            jax.ShapeDtypeStruct((t, LANES), F32),
            jax.ShapeDtypeStruct((8, LANES), F32),
        ],
        scratch_shapes=[pltpu.VMEM((8, LANES), F32)],
        compiler_params=_params(),
        name="post",
    )(mix, x2d, wo_bf, ln_g.reshape(1, d), ln_b.reshape(1, d), wr_hi, wr_lo, b_r)


def _dispatch_kernel(da_ref, db_ref, xp_ref, stale_hbm, xs_hbm, sem, *, tm, nc):
    del stale_hbm
    unroll = 8

    def group(g, carry):
        for u in range(unroll):
            j = g * unroll + u
            src = xp_ref.at[pl.ds(j * nc, nc)]
            for d_ref in (da_ref, db_ref):
                dst = xs_hbm.at[pl.ds(pl.multiple_of(d_ref[j], nc), nc)]
                pltpu.make_async_copy(src, dst, sem).start()
        return carry
    lax.fori_loop(0, tm // unroll, group, 0)
    for _ in range(TOP_K):
        pltpu.make_async_copy(xp_ref, xs_hbm.at[pl.ds(0, tm * nc)], sem).wait()


def _dispatch(xp, dest_a, dest_b, stale, *, tm):
    nc = xp.shape[0] // dest_a.shape[0]
    t = dest_a.shape[0]
    smem = pl.BlockSpec((tm,), lambda i: (i,), memory_space=pltpu.SMEM)
    any_spec = pl.BlockSpec(memory_space=pl.ANY)
    return pl.pallas_call(
        functools.partial(_dispatch_kernel, tm=tm, nc=nc),
        grid=(t // tm,),
        in_specs=[smem, smem, pl.BlockSpec((tm * nc, LANES), lambda i: (i, 0)), any_spec],
        out_specs=any_spec,
        out_shape=jax.ShapeDtypeStruct(stale.shape, stale.dtype),
        scratch_shapes=[pltpu.SemaphoreType.DMA(())],
        input_output_aliases={3: 0},
        compiler_params=_params(),
        name="dispatch",
    )(dest_a, dest_b, xp, stale)


def _invert_kernel(dest_ref, out_ref):
    def clear(r, carry):
        out_ref[r] = 0
        return carry
    lax.fori_loop(0, out_ref.shape[0], clear, 0, unroll=16)

    def place(a, carry):
        out_ref[dest_ref[a]] = a
        return carry
    lax.fori_loop(0, dest_ref.shape[0], place, 0, unroll=16)


def _invert(dest, n_rows):
    smem = pl.BlockSpec(memory_space=pltpu.SMEM)
    return pl.pallas_call(
        _invert_kernel,
        in_specs=[smem], out_specs=smem,
        out_shape=jax.ShapeDtypeStruct((n_rows,), jnp.int32),
        name="invert",
    )(dest)


def _moe_kernel(blk_e_ref, nxt_e_ref, nvalid_ref, n_used_ref, dst_ref,
                xs_ref, wg_hbm, wu_hbm, wd_hbm, y_hbm,
                ybuf, wg_st, wu_st, wd_st, wg_bf, wu_bf, wd_bf, sem_s, sem_w,
                *, rb, nc, nblk, layer):
    s = pl.program_id(0)
    n_used = n_used_ref[0]
    slot = lax.rem(s, 2)
    other = 1 - slot
    unroll = 8

    def nvalid(b):
        return jnp.where((b >= 0) & (b < nblk), nvalid_ref[jnp.clip(b, 0, nblk - 1)], 0)

    def scatter_row(j, src_slot):
        dst = y_hbm.at[pl.ds(pl.multiple_of(dst_ref[j], nc), nc)]
        pltpu.make_async_copy(ybuf.at[src_slot, pl.ds(j * nc, nc)], dst,
                              sem_s.at[src_slot]).start()

    def for_rows(n, row_fn):
        def group(g, carry):
            for u in range(unroll):
                row_fn(g * unroll + u)
            return carry
        lax.fori_loop(0, n // unroll, group, 0)

        def single(j, carry):
            row_fn(j)
            return carry
        lax.fori_loop((n // unroll) * unroll, n, single, 0)

    def wait_rows(n, make_copy):
        p = rb
        while p >= 1:
            @pl.when((n & p) != 0)
            def _(p=p):
                make_copy(p).wait()
            p //= 2

    def wait_scatter(n, sl):
        wait_rows(n, lambda p: pltpu.make_async_copy(
            ybuf.at[sl, pl.ds(0, p * nc)], y_hbm.at[pl.ds(0, p * nc)], sem_s.at[sl]))

    weight_copies = ((wg_hbm, wg_st, 0), (wu_hbm, wu_st, 1), (wd_hbm, wd_st, 2))

    def fetch_weights(e):
        for hbm, stage, sem_i in weight_copies:
            pltpu.make_async_copy(hbm.at[layer, e], stage, sem_w.at[sem_i]).start()

    @pl.when(s == 0)
    def _():
        fetch_weights(blk_e_ref[0])

    b_cur = jnp.minimum(s, nblk - 1)
    e_cur = blk_e_ref[b_cur]
    new_expert = (s == 0) | (e_cur != blk_e_ref[jnp.maximum(b_cur - 1, 0)])

    @pl.when((s < n_used) & new_expert)
    def _():
        for hbm, stage, sem_i in weight_copies:
            pltpu.make_async_copy(hbm.at[layer, 0], stage, sem_w.at[sem_i]).wait()
        wg_bf[...] = wg_st[...].astype(BF16)
        wu_bf[...] = wu_st[...].astype(BF16)
        wd_bf[...] = wd_st[...].astype(BF16)
        e_next = nxt_e_ref[b_cur]

        @pl.when(e_next != e_cur)
        def _():
            fetch_weights(e_next)

    @pl.when(s < n_used)
    def _():
        wait_scatter(nvalid(s - 2), slot)
        xb = _unpack_rows(xs_ref, nc).astype(BF16)
        hg = jnp.dot(xb, wg_bf[...], preferred_element_type=F32)
        hu = jnp.dot(xb, wu_bf[...], preferred_element_type=F32)
        h = (jax.nn.silu(hg) * hu).astype(BF16)
        _pack_rows(ybuf, jnp.dot(h, wd_bf[...], preferred_element_type=F32), slot)
        for_rows(nvalid(s), lambda j: scatter_row(j, slot))

    @pl.when(s == n_used)
    def _():
        wait_scatter(nvalid(s - 2), slot)
        wait_scatter(nvalid(s - 1), other)


def _moe(xs, row_dst, blk_e, nxt_e, nvalid, n_used, w_gate, w_up, w_down, *, t, layer, rb):
    d_e, d = w_down.shape[-2:]
    n_chunks = d // (2 * LANES)
    nblk = row_dst.shape[0] // rb

    def blk(s, blk_e_ref, nxt_e_ref, nvalid_ref, n_used_ref):
        return jnp.minimum(s, jnp.maximum(n_used_ref[0] - 1, 0))

    any_spec = pl.BlockSpec(memory_space=pl.ANY)
    grid_spec = pltpu.PrefetchScalarGridSpec(
        num_scalar_prefetch=4,
        grid=(nblk + 1,),
        in_specs=[
            pl.BlockSpec((rb,), lambda s, *p: (blk(s, *p),), memory_space=pltpu.SMEM),
            pl.BlockSpec((rb * n_chunks, LANES), lambda s, *p: (blk(s, *p), 0)),
            any_spec, any_spec, any_spec,
        ],
        out_specs=any_spec,
        scratch_shapes=[pltpu.VMEM((2, rb * n_chunks, LANES), U32),
                        pltpu.VMEM((d, d_e), F32), pltpu.VMEM((d, d_e), F32),
                        pltpu.VMEM((d_e, d), F32),
                        pltpu.VMEM((d, d_e), BF16), pltpu.VMEM((d, d_e), BF16),
                        pltpu.VMEM((d_e, d), BF16),
                        pltpu.SemaphoreType.DMA((2,)), pltpu.SemaphoreType.DMA((3,))],
    )
    return pl.pallas_call(
        functools.partial(_moe_kernel, rb=rb, nc=n_chunks, nblk=nblk, layer=layer),
        grid_spec=grid_spec,
        out_shape=jax.ShapeDtypeStruct((TOP_K * t * n_chunks, LANES), U32),
        compiler_params=_params(),
        name="moe",
    )(blk_e, nxt_e, nvalid, n_used, row_dst, xs, w_gate, w_up, w_down)


def _combine_kernel(x1_ref, ya_ref, yb_ref, route_ref, g_ref, b_ref, *rest,
                    nc, alpha, mode, tok_w):
    if mode == "attn":
        w_ref, x2_ref, o_ref = rest
    elif mode == "gmlp":
        w_ref, ng_ref, nb_ref, x2_ref, o_ref = rest
    else:
        (x2_ref,) = rest
    route = route_ref[...]
    y = _unpack_rows(ya_ref, nc) * route[:, 4:5] + _unpack_rows(yb_ref, nc) * route[:, 5:6]
    x2 = _layer_norm(alpha * x1_ref[...] + y, g_ref[...], b_ref[...])
    x2_ref[...] = x2
    if mode == "attn":
        o_ref[...] = jnp.dot(x2.astype(BF16), w_ref[...],
                             preferred_element_type=F32).astype(o_ref.dtype)
    elif mode == "gmlp":
        _gmlp_epilogue(jnp.dot(x2.astype(BF16), w_ref[...], preferred_element_type=F32),
                       ng_ref, nb_ref, o_ref, tok_w)


def _combine(y, x1, route, ln_g, ln_b, *, tm, alpha, w_bf=None, gmlp_norm=None):
    t, d = x1.shape
    n_chunks = d // (2 * LANES)
    mode = "none" if w_bf is None else ("attn" if gmlp_norm is None else "gmlp")
    n_steps = t // tm

    in_specs = [pl.BlockSpec((tm, d), lambda i: (i, 0)),
                pl.BlockSpec((tm * n_chunks, LANES), lambda i: (i, 0)),
                pl.BlockSpec((tm * n_chunks, LANES), lambda i: (n_steps + i, 0)),
                pl.BlockSpec((tm, LANES), lambda i: (i, 0)),
                _resident((1, d)), _resident((1, d))]
    args = [x1, y, y, route, ln_g.reshape(1, d), ln_b.reshape(1, d)]
    out_specs = [pl.BlockSpec((tm, d), lambda i: (i, 0))]
    out_shape = [jax.ShapeDtypeStruct((t, d), F32)]
    tok_w = None
    if w_bf is not None:
        n = w_bf.shape[1]
        in_specs.append(_resident((d, n)))
        args.append(w_bf)
        out_specs.append(pl.BlockSpec((tm, n), lambda i: (i, 0)))
        out_shape.append(jax.ShapeDtypeStruct((t, n), BF16))
    if gmlp_norm is not None:
        tok_w = gmlp_norm[0].shape[-1]
        in_specs += [_resident((1, tok_w)), _resident((1, tok_w))]
        args += [gmlp_norm[0].reshape(1, tok_w), gmlp_norm[1].reshape(1, tok_w)]
    return pl.pallas_call(
        functools.partial(_combine_kernel, nc=n_chunks, alpha=alpha, mode=mode, tok_w=tok_w),
        grid=(n_steps,),
        in_specs=in_specs,
        out_specs=out_specs,
        out_shape=out_shape,
        compiler_params=_params(),
        name="combine_" + mode,
    )(*args)


def _split_bf16(w):
    hi = w.astype(BF16)
    lo = (w - hi.astype(F32)).astype(BF16)
    return hi, lo


def kernel(x, mem, a_w_in, a_sinks, b_w_in, b_norm_g, b_norm_b, b_w_spatial, b_b_spatial,
           w_mem_kv, w_out, ln_g, ln_b, w_router_group, b_router_group,
           w_router_expert, b_router_expert, w_gate, w_up, w_down):
    bsz, seq, d = x.shape
    mem_len = mem.shape[1]
    depth = w_out.shape[0]
    t = bsz * seq
    alpha = (2 * depth) ** 0.25
    rb = ROW_BLOCK
    n_rows = -(-(t * TOP_K + N_EXPERTS * (rb - 1)) // rb) * rb
    nblk = n_rows // rb

    def in_proj_weights(i):
        if i % 2 == 0:
            return a_w_in[i // 2].astype(BF16), None
        return b_w_in[i // 2].astype(BF16), (b_norm_g[i // 2], b_norm_b[i // 2])

    x2d = x.reshape(t, d)
    mem2d = mem.reshape(bsz * mem_len, d)
    proj = _proj(x2d, in_proj_weights(0)[0], BF16, tm=512)
    xs = jnp.zeros((n_rows * (d // (2 * LANES)), LANES), U32)
    for i in range(depth):
        j = i // 2
        kvm = _proj(mem2d, w_mem_kv[i].astype(BF16), BF16, tm=min(256, bsz * mem_len))
        if i % 2 == 0:
            mix = _attn_mixer(proj, kvm, a_sinks[j], seq=seq, mem_len=mem_len, tq=256)
        else:
            mix = _gmlp_mixer(proj, kvm, b_w_spatial[j], b_b_spatial[j],
                              seq=seq, mem_len=mem_len, tq=512)

        w_r = jnp.concatenate([w_router_group[i], w_router_expert[i]], axis=1)
        w_r = jnp.pad(w_r, ((0, 0), (0, LANES - w_r.shape[1])))
        b_r = jnp.concatenate([b_router_group[i], b_router_expert[i]])
        b_r = jnp.pad(b_r, (0, LANES - b_r.shape[0])).reshape(1, LANES)
        wr_hi, wr_lo = _split_bf16(w_r)
        x1, x1p, route, cnt = _post(mix, x2d, w_out[i].astype(BF16), ln_g[i, 0], ln_b[i, 0],
                                    wr_hi, wr_lo, b_r, tm=512, sub=256, alpha=alpha)

        counts = cnt[0, :N_EXPERTS].astype(jnp.int32)
        padded = (counts + rb - 1) // rb * rb
        pend = jnp.cumsum(padded)
        pstart = (pend - padded).astype(jnp.int32)
        n_used = (pend[-1] // rb).astype(jnp.int32).reshape(1)
        blk = jnp.arange(nblk, dtype=jnp.int32)
        blk_e = jnp.searchsorted(pend, jnp.minimum(blk, n_used[0] - 1) * rb,
                                 side='right').astype(jnp.int32)
        nxt_blk = pend[blk_e] // rb
        nxt_e = jnp.where(nxt_blk < n_used[0], blk_e[jnp.minimum(nxt_blk, nblk - 1)], blk_e)
        nvalid = jnp.where(blk < n_used[0],
                           jnp.clip(pstart[blk_e] + counts[blk_e] - blk * rb, 0, rb), 0)
        ri = route[:, :4].astype(jnp.int32)
        e_hot = ri[:, :TOP_K, None] == jnp.arange(N_EXPERTS, dtype=jnp.int32)
        dest = jnp.sum(jnp.where(e_hot, pstart, 0), axis=-1) + ri[:, TOP_K:]
        n_chunks = d // (2 * LANES)
        xs = _dispatch(x1p, dest[:, 0] * n_chunks, dest[:, 1] * n_chunks, xs, tm=512)
        row_asg = _invert(dest.reshape(-1), n_rows)
        row_dst = ((row_asg % TOP_K) * t + row_asg // TOP_K) * n_chunks

        y = _moe(xs, row_dst, blk_e, nxt_e.astype(jnp.int32), nvalid.astype(jnp.int32),
                 n_used, w_gate, w_up, w_down, t=t, layer=i, rb=rb)
        if i + 1 < depth:
            w_next, gmlp_norm = in_proj_weights(i + 1)
            x2d, proj = _combine(y, x1, route, ln_g[i, 1], ln_b[i, 1],
                                 tm=256, alpha=alpha, w_bf=w_next, gmlp_norm=gmlp_norm)
        else:
            (x2d,) = _combine(y, x1, route, ln_g[i, 1], ln_b[i, 1], tm=256, alpha=alpha)
    return x2d.reshape(bsz, seq, d)
```

```python
import functools
import math

import jax
import jax.numpy as jnp
import numpy as np
from jax import lax
from jax.experimental import pallas as pl
from jax.experimental.pallas import tpu as pltpu

MEM_HEADS = 4
MEM_HEAD_DIM = 128
MEM_W = MEM_HEADS * MEM_HEAD_DIM
HEAD_DIM = 64
N_KV_HEADS = 4
KV_W = N_KV_HEADS * HEAD_DIM
WINDOW = 128
CHUNK = 128
GM_GROUP_DIM = 128
N_GROUPS = 8
EXPERTS_PER_GROUP = 8
N_EXPERTS = N_GROUPS * EXPERTS_PER_GROUP
TOP_K = 2
LN_EPS = 1e-5
NEG_INF = -1e30

LANES = 128
VMEM_LIMIT_BYTES = 56 * 1024 * 1024

ROW_BLOCK = 256
BF16 = jnp.bfloat16
F32 = jnp.float32
U32 = jnp.uint32


def _alibi_slopes(n):
    def pow2(m):
        start = 2.0 ** (-8.0 / m)
        return [start ** (i + 1) for i in range(m)]
    if math.log2(n).is_integer():
        s = pow2(n)
    else:
        c = 2 ** math.floor(math.log2(n))
        s = pow2(c) + pow2(2 * c)[0::2][: n - c]
    return [float(v) for v in np.asarray(s, dtype=np.float32)]


def _params(n_axes=1):
    return pltpu.CompilerParams(dimension_semantics=("arbitrary",) * n_axes,
                                vmem_limit_bytes=VMEM_LIMIT_BYTES)


def _resident(shape):
    nd = len(shape)
    return pl.BlockSpec(shape, lambda *_: (0,) * nd, pipeline_mode=pl.Buffered(1))


def _pack_rows(ref, val, *lead, row0=0):
    rows, d = val.shape
    nw = d // (2 * LANES)
    for c in range(nw):
        lo = val[:, c * LANES:(c + 1) * LANES].astype(BF16).astype(F32)
        hi = val[:, (c + nw) * LANES:(c + nw + 1) * LANES].astype(BF16).astype(F32)
        word = (lax.bitcast_convert_type(lo, U32) >> 16) | lax.bitcast_convert_type(hi, U32)
        ref[tuple(lead) + (pl.ds(row0 * nw + c, rows, stride=nw), slice(None))] = word


def _unpack_rows(ref, nw, *lead):
    rows = ref.shape[-2] // nw
    lo, hi = [], []
    for c in range(nw):
        word = ref[tuple(lead) + (pl.ds(c, rows, stride=nw), slice(None))]
        lo.append(lax.bitcast_convert_type(word << 16, F32))
        hi.append(lax.bitcast_convert_type(word & jnp.uint32(0xFFFF0000), F32))
    return jnp.concatenate(lo + hi, axis=-1)


def _layer_norm(x, g, b):
    mu = jnp.mean(x, axis=-1, keepdims=True)
    xc = x - mu
    var = jnp.mean(xc * xc, axis=-1, keepdims=True)
    return xc * lax.rsqrt(var + LN_EPS) * g + b


def _proj_kernel(x_ref, w_ref, o_ref):
    o_ref[...] = jnp.dot(x_ref[...].astype(BF16), w_ref[...],
                         preferred_element_type=F32).astype(o_ref.dtype)


def _proj(x2d, w_bf, out_dtype, tm):
    m, k = x2d.shape
    n = w_bf.shape[1]
    return pl.pallas_call(
        _proj_kernel,
        grid=(m // tm,),
        in_specs=[pl.BlockSpec((tm, k), lambda i: (i, 0)), _resident((k, n))],
        out_specs=pl.BlockSpec((tm, n), lambda i: (i, 0)),
        out_shape=jax.ShapeDtypeStruct((m, n), out_dtype),
        compiler_params=_params(),
        name="proj",
    )(x2d, w_bf)


def _gelu(x):
    return 0.5 * x * (1.0 + lax.erf(x * (2.0 ** -0.5)))


def _gmlp_epilogue(z, g_ref, b_ref, o_ref, tok_w):
    u = _gelu(z[:, :tok_w])
    v = _gelu(z[:, tok_w:2 * tok_w])
    v = _layer_norm(v, g_ref[...], b_ref[...])
    o_ref[:, :tok_w] = u.astype(o_ref.dtype)
    o_ref[:, tok_w:2 * tok_w] = v.astype(o_ref.dtype)
    o_ref[:, 2 * tok_w:] = z[:, 2 * tok_w:].astype(o_ref.dtype)


def _memory_attention(qm, kvm):
    outs = []
    for h in range(MEM_HEADS):
        q = qm[:, h * MEM_HEAD_DIM:(h + 1) * MEM_HEAD_DIM]
        k = kvm[:, h * MEM_HEAD_DIM:(h + 1) * MEM_HEAD_DIM]
        v = kvm[:, MEM_W + h * MEM_HEAD_DIM:MEM_W + (h + 1) * MEM_HEAD_DIM]
        s = lax.dot_general(q, k, (((1,), (1,)), ((), ())),
                            preferred_element_type=F32) * (MEM_HEAD_DIM ** -0.5)
        m = jnp.max(s, axis=-1, keepdims=True)
        p = jnp.exp(s - m)
        p = p * (1.0 / jnp.sum(p, axis=-1, keepdims=True))
        outs.append(jnp.dot(p.astype(BF16), v, preferred_element_type=F32))
    return jnp.concatenate(outs, axis=-1)


def _attn_kernel(sinks_ref, q_ref, kv_ref, kvp_ref, qm_ref, kvm_ref, o_ref, *,
                 tq, seq, slopes, tok_w):
    i = pl.program_id(0)
    n_sub = tq // WINDOW
    gqa = len(slopes) // N_KV_HEADS
    qi = lax.broadcasted_iota(jnp.int32, (WINDOW, WINDOW), 0)
    c = lax.broadcasted_iota(jnp.int32, (WINDOW, WINDOW), 1)
    own = c <= qi
    dist_f = jnp.where(own, qi - c, WINDOW + qi - c).astype(F32)
    nt = (((1,), (1,)), ((), ()))
    for sb in range(n_sub):
        r0 = sb * WINDOW
        if sb == 0:
            kv_prev = kvp_ref[...]
            reach = jnp.where(((i * tq) % seq) == 0, 0, WINDOW)
            valid = c <= qi + reach
        else:
            kv_prev = kv_ref[r0 - WINDOW:r0, :]
            valid = None
        kv_cur = kv_ref[r0:r0 + WINDOW, :]
        for kh in range(N_KV_HEADS):
            ks = slice(kh * HEAD_DIM, (kh + 1) * HEAD_DIM)
            vs = slice(KV_W + kh * HEAD_DIM, KV_W + (kh + 1) * HEAD_DIM)
            heads = range(kh * gqa, (kh + 1) * gqa)
            q = jnp.concatenate(
                [q_ref[r0:r0 + WINDOW, h * HEAD_DIM:(h + 1) * HEAD_DIM] for h in heads], axis=0)
            s_own = lax.dot_general(q, kv_cur[:, ks], nt, preferred_element_type=F32)
            s_prev = lax.dot_general(q, kv_prev[:, ks], nt, preferred_element_type=F32)
            p_own, p_prev = [], []
            for g, h in enumerate(heads):
                rows = slice(g * WINDOW, (g + 1) * WINDOW)
                logits = (jnp.where(own, s_own[rows], s_prev[rows]) * (HEAD_DIM ** -0.5)
                          - slopes[h] * dist_f)
                if valid is not None:
                    logits = jnp.where(valid, logits, NEG_INF)
                sink = sinks_ref[h]
                m = jnp.maximum(jnp.max(logits, axis=-1, keepdims=True), sink)
                p = jnp.exp(logits - m)
                probs = p * (1.0 / (jnp.sum(p, axis=-1, keepdims=True) + jnp.exp(sink - m)))
                p_own.append(jnp.where(own, probs, 0.0).astype(BF16))
                p_prev.append(jnp.where(own, 0.0, probs).astype(BF16))
            out = (jnp.dot(jnp.concatenate(p_own, axis=0), kv_cur[:, vs],
                           preferred_element_type=F32)
                   + jnp.dot(jnp.concatenate(p_prev, axis=0), kv_prev[:, vs],
                             preferred_element_type=F32))
            o_ref[r0:r0 + WINDOW, kh * gqa * HEAD_DIM:(kh + 1) * gqa * HEAD_DIM] = jnp.concatenate(
                [out[g * WINDOW:(g + 1) * WINDOW] for g in range(gqa)], axis=1).astype(o_ref.dtype)
    o_ref[:, tok_w:] = _memory_attention(qm_ref[...], kvm_ref[...]).astype(o_ref.dtype)


def _attn_mixer(proj, kvm, sinks, *, seq, mem_len, tq):
    t = proj.shape[0]
    tok_w = proj.shape[1] - 2 * KV_W - MEM_W
    n_heads = tok_w // HEAD_DIM
    assert tok_w % (2 * KV_W) == 0 and tok_w % MEM_W == 0 and seq % tq == 0
    kv_col = tok_w // (2 * KV_W)
    qm_col = (tok_w + 2 * KV_W) // MEM_W
    sub = tq // WINDOW
    kern = functools.partial(_attn_kernel, tq=tq, seq=seq, slopes=_alibi_slopes(n_heads),
                             tok_w=tok_w)
    return pl.pallas_call(
        kern,
        grid=(t // tq,),
        in_specs=[
            pl.BlockSpec(memory_space=pltpu.SMEM),
            pl.BlockSpec((tq, tok_w), lambda i: (i, 0)),
            pl.BlockSpec((tq, 2 * KV_W), lambda i: (i, kv_col)),
            pl.BlockSpec((WINDOW, 2 * KV_W), lambda i: (jnp.maximum(i * sub - 1, 0), kv_col)),
            pl.BlockSpec((tq, MEM_W), lambda i: (i, qm_col)),
            pl.BlockSpec((mem_len, 2 * MEM_W), lambda i: ((i * tq) // seq, 0)),
        ],
        out_specs=pl.BlockSpec((tq, tok_w + MEM_W), lambda i: (i, 0)),
        out_shape=jax.ShapeDtypeStruct((t, tok_w + MEM_W), BF16),
        compiler_params=_params(),
        name="attn_mixer",
    )(sinks, proj, proj, proj, proj, kvm)


def _gmlp_kernel(u_ref, v_ref, qm_ref, kvm_ref, ws_ref, bs_ref, o_ref, *, tq, tok_w):
    n_sub = tq // CHUNK
    n_grp = tok_w // GM_GROUP_DIM
    row = lax.broadcasted_iota(jnp.int32, (CHUNK, CHUNK), 0)
    col = lax.broadcasted_iota(jnp.int32, (CHUNK, CHUNK), 1)
    lower = row >= col
    for g in range(n_grp):
        w = jnp.where(lower, ws_ref[g], 0.0).astype(BF16)
        bias = bs_ref[:, g:g + 1]
        c0 = g * GM_GROUP_DIM
        for sb in range(n_sub):
            r0 = sb * CHUNK
            sv = jnp.dot(w, v_ref[r0:r0 + CHUNK, c0:c0 + GM_GROUP_DIM],
                         preferred_element_type=F32) + bias
            u = u_ref[r0:r0 + CHUNK, c0:c0 + GM_GROUP_DIM].astype(F32)
            o_ref[r0:r0 + CHUNK, c0:c0 + GM_GROUP_DIM] = (u * sv).astype(o_ref.dtype)
    o_ref[:, tok_w:] = _memory_attention(qm_ref[...], kvm_ref[...]).astype(o_ref.dtype)


def _gmlp_mixer(proj, kvm, w_s, b_s, *, seq, mem_len, tq):
    t = proj.shape[0]
    tok_w = (proj.shape[1] - MEM_W) // 2
    n_grp = tok_w // GM_GROUP_DIM
    assert (2 * tok_w) % MEM_W == 0 and seq % tq == 0
    kern = functools.partial(_gmlp_kernel, tq=tq, tok_w=tok_w)
    return pl.pallas_call(
        kern,
        grid=(t // tq,),
        in_specs=[
            pl.BlockSpec((tq, tok_w), lambda i: (i, 0)),
            pl.BlockSpec((tq, tok_w), lambda i: (i, 1)),
            pl.BlockSpec((tq, MEM_W), lambda i: (i, (2 * tok_w) // MEM_W)),
            pl.BlockSpec((mem_len, 2 * MEM_W), lambda i: ((i * tq) // seq, 0)),
            _resident((n_grp, CHUNK, CHUNK)),
            _resident((CHUNK, n_grp)),
        ],
        out_specs=pl.BlockSpec((tq, tok_w + MEM_W), lambda i: (i, 0)),
        out_shape=jax.ShapeDtypeStruct((t, tok_w + MEM_W), BF16),
        compiler_params=_params(),
        name="gmlp_mixer",
    )(proj, proj, proj, kvm, w_s, b_s.T)


def _post_kernel(mix_ref, x_ref, wo_ref, g_ref, b_ref, wrh_ref, wrl_ref, br_ref,
                 x1_ref, xp_ref, route_ref, cnt_ref, carry_ref, *, tm, sub, alpha):
    i = pl.program_id(0)

    @pl.when(i == 0)
    def _():
        carry_ref[...] = jnp.zeros_like(carry_ref)

    carry = carry_ref[0:1, :]
    for r0 in range(0, tm, sub):
        carry = _post_rows(r0, sub, carry, mix_ref, x_ref, wo_ref, g_ref, b_ref, wrh_ref, wrl_ref,
                           br_ref, x1_ref, xp_ref, route_ref, alpha)
    carry_ref[...] = jnp.broadcast_to(carry, carry_ref.shape)
    cnt_ref[...] = jnp.broadcast_to(carry, cnt_ref.shape)


def _post_rows(r0, tm, carry, mix_ref, x_ref, wo_ref, g_ref, b_ref, wrh_ref, wrl_ref, br_ref,
               x1_ref, xp_ref, route_ref, alpha):
    rows = slice(r0, r0 + tm)
    y = jnp.dot(mix_ref[rows, :], wo_ref[...], preferred_element_type=F32)
    x1 = _layer_norm(alpha * x_ref[rows, :] + y, g_ref[...], b_ref[...])
    x1_ref[rows, :] = x1
    _pack_rows(xp_ref, x1, row0=r0)

    xh = x1.astype(BF16)
    xl = (x1 - xh.astype(F32)).astype(BF16)
    logits = (jnp.dot(xh, wrh_ref[...], preferred_element_type=F32)
              + (jnp.dot(xh, wrl_ref[...], preferred_element_type=F32)
                 + jnp.dot(xl, wrh_ref[...], preferred_element_type=F32))) + br_ref[...]

    lane = lax.broadcasted_iota(jnp.int32, (tm, LANES), 1)
    lane_f = lane.astype(F32)
    big = float(LANES)
    is_grp = lane < N_GROUPS
    gl = jnp.where(is_grp, logits, NEG_INF)
    gmax = jnp.max(gl, axis=-1, keepdims=True)
    g_sel = jnp.min(jnp.where(gl == gmax, lane_f, big), axis=-1, keepdims=True)
    p_grp = 1.0 / jnp.sum(jnp.where(is_grp, jnp.exp(gl - gmax), 0.0), axis=-1, keepdims=True)

    e_lo = N_GROUPS + g_sel * EXPERTS_PER_GROUP
    in_grp = (lane_f >= e_lo) & (lane_f < e_lo + EXPERTS_PER_GROUP)
    el = jnp.where(in_grp, logits, NEG_INF)
    m1 = jnp.max(el, axis=-1, keepdims=True)
    i1 = jnp.min(jnp.where(el == m1, lane_f, big), axis=-1, keepdims=True)
    el2 = jnp.where(lane_f == i1, NEG_INF, el)
    m2 = jnp.max(el2, axis=-1, keepdims=True)
    i2 = jnp.min(jnp.where(el2 == m2, lane_f, big), axis=-1, keepdims=True)
    z = jnp.sum(jnp.where(in_grp, jnp.exp(el - m1), 0.0), axis=-1, keepdims=True)
    tp1 = 1.0 / z
    tp2 = jnp.exp(m2 - m1) / z
    gate1 = p_grp * tp1 / (tp1 + tp2)
    gate2 = p_grp * tp2 / (tp1 + tp2)
    e1 = i1 - N_GROUPS
    e2 = i2 - N_GROUPS

    oh1 = lane_f == e1
    oh2 = lane_f == e2
    c = jnp.where(oh1 | oh2, 1.0, 0.0)
    r = lax.broadcasted_iota(jnp.int32, (tm, tm), 0)
    s = lax.broadcasted_iota(jnp.int32, (tm, tm), 1)
    tri = jnp.where(r > s, 1.0, 0.0).astype(BF16)
    prefix = jnp.dot(tri, c.astype(BF16), preferred_element_type=F32) + carry
    rank1 = jnp.sum(jnp.where(oh1, prefix, 0.0), axis=-1, keepdims=True)
    rank2 = jnp.sum(jnp.where(oh2, prefix, 0.0), axis=-1, keepdims=True)

    route = jnp.where(lane == 0, e1, 0.0)
    route = jnp.where(lane == 1, e2, route)
    route = jnp.where(lane == 2, rank1, route)
    route = jnp.where(lane == 3, rank2, route)
    route = jnp.where(lane == 4, gate1, route)
    route = jnp.where(lane == 5, gate2, route)
    route_ref[rows, :] = route
    return carry + jnp.sum(c, axis=0, keepdims=True)


def _post(mix, x2d, wo_bf, ln_g, ln_b, wr_hi, wr_lo, b_r, *, tm, sub, alpha):
    t, d = x2d.shape
    kern = functools.partial(_post_kernel, tm=tm, sub=sub, alpha=alpha)
    return pl.pallas_call(
        kern,
        grid=(t // tm,),
        in_specs=[
            pl.BlockSpec((tm, d), lambda i: (i, 0)),
            pl.BlockSpec((tm, d), lambda i: (i, 0)),
            _resident((d, d)),
            _resident((1, d)), _resident((1, d)),
            _resident((d, LANES)), _resident((d, LANES)), _resident((1, LANES)),
        ],
        out_specs=[
            pl.BlockSpec((tm, d), lambda i: (i, 0)),
            pl.BlockSpec((tm * (d // (2 * LANES)), LANES), lambda i: (i, 0)),
            pl.BlockSpec((tm, LANES), lambda i: (i, 0)),
            pl.BlockSpec((8, LANES), lambda i: (0, 0)),
        ],
        out_shape=[
            jax.ShapeDtypeStruct((t, d), F32),
            jax.ShapeDtypeStruct((t * (d // (2 * LANES)), LANES), U32),
            jax.ShapeDtypeStruct((t, LANES), F32),
            jax.ShapeDtypeStruct((8, LANES), F32),
        ],
        scratch_shapes=[pltpu.VMEM((8, LANES), F32)],
        compiler_params=_params(),
        name="post",
    )(mix, x2d, wo_bf, ln_g.reshape(1, d), ln_b.reshape(1, d), wr_hi, wr_lo, b_r)


def _dispatch_kernel(real_end_ref, pad_end_ref, da_ref, db_ref, xp_ref, stale_hbm,
                     xs_hbm, inv_ref, sem, *, tm, nc):
    del stale_hbm
    i = pl.program_id(0)
    unroll = 8

    @pl.when(i == 0)
    def _():
        def expert(e, carry):
            def clear(r, c):
                inv_ref[r] = 0
                return c
            lax.fori_loop(real_end_ref[e], pad_end_ref[e], clear, 0)
            return carry
        lax.fori_loop(0, real_end_ref.shape[0], expert, 0)

    def group(g, carry):
        for u in range(unroll):
            j = g * unroll + u
            src = xp_ref.at[pl.ds(j * nc, nc)]
            for k, d_ref in enumerate((da_ref, db_ref)):
                row = d_ref[j]
                pltpu.make_async_copy(src, xs_hbm.at[pl.ds(pl.multiple_of(row * nc, nc), nc)],
                                      sem).start()
                inv_ref[row] = TOP_K * (i * tm + j) + k
        return carry
    lax.fori_loop(0, tm // unroll, group, 0)
    for _ in range(TOP_K):
        pltpu.make_async_copy(xp_ref, xs_hbm.at[pl.ds(0, tm * nc)], sem).wait()


def _dispatch(xp, dest_a, dest_b, real_end, pad_end, stale, *, tm):
    nc = xp.shape[0] // dest_a.shape[0]
    t = dest_a.shape[0]
    n_rows = stale.shape[0] // nc
    smem = pl.BlockSpec((tm,), lambda i, *_: (i,), memory_space=pltpu.SMEM)
    any_spec = pl.BlockSpec(memory_space=pl.ANY)
    grid_spec = pltpu.PrefetchScalarGridSpec(
        num_scalar_prefetch=2,
        grid=(t // tm,),
        in_specs=[smem, smem, pl.BlockSpec((tm * nc, LANES), lambda i, *_: (i, 0)), any_spec],
        out_specs=[any_spec, pl.BlockSpec(memory_space=pltpu.SMEM)],
        scratch_shapes=[pltpu.SemaphoreType.DMA(())],
    )
    return pl.pallas_call(
        functools.partial(_dispatch_kernel, tm=tm, nc=nc),
        grid_spec=grid_spec,
        out_shape=[jax.ShapeDtypeStruct(stale.shape, stale.dtype),
                   jax.ShapeDtypeStruct((n_rows,), jnp.int32)],
        input_output_aliases={5: 0},
        compiler_params=_params(),
        name="dispatch",
    )(real_end, pad_end, dest_a, dest_b, xp, stale)


def _moe_kernel(blk_e_ref, nxt_e_ref, nvalid_ref, n_used_ref, dst_ref,
                xs_ref, wg_hbm, wu_hbm, wd_hbm, y_hbm,
                ybuf, wg_st, wu_st, wd_st, wg_bf, wu_bf, wd_bf, sem_s, sem_w,
                *, rb, nc, nblk, layer):
    s = pl.program_id(0)
    n_used = n_used_ref[0]
    slot = lax.rem(s, 2)
    other = 1 - slot
    unroll = 8

    def nvalid(b):
        return jnp.where((b >= 0) & (b < nblk), nvalid_ref[jnp.clip(b, 0, nblk - 1)], 0)

    def scatter_row(j, src_slot):
        dst = y_hbm.at[pl.ds(pl.multiple_of(dst_ref[j], nc), nc)]
        pltpu.make_async_copy(ybuf.at[src_slot, pl.ds(j * nc, nc)], dst,
                              sem_s.at[src_slot]).start()

    def for_rows(n, row_fn):
        def group(g, carry):
            for u in range(unroll):
                row_fn(g * unroll + u)
            return carry
        lax.fori_loop(0, n // unroll, group, 0)

        def single(j, carry):
            row_fn(j)
            return carry
        lax.fori_loop((n // unroll) * unroll, n, single, 0)

    def wait_rows(n, make_copy):
        p = rb
        while p >= 1:
            @pl.when((n & p) != 0)
            def _(p=p):
                make_copy(p).wait()
            p //= 2

    def wait_scatter(n, sl):
        wait_rows(n, lambda p: pltpu.make_async_copy(
            ybuf.at[sl, pl.ds(0, p * nc)], y_hbm.at[pl.ds(0, p * nc)], sem_s.at[sl]))

    weight_copies = ((wg_hbm, wg_st, 0), (wu_hbm, wu_st, 1), (wd_hbm, wd_st, 2))

    def fetch_weights(e):
        for hbm, stage, sem_i in weight_copies:
            pltpu.make_async_copy(hbm.at[layer, e], stage, sem_w.at[sem_i]).start()

    @pl.when(s == 0)
    def _():
        fetch_weights(blk_e_ref[0])

    b_cur = jnp.minimum(s, nblk - 1)
    e_cur = blk_e_ref[b_cur]
    new_expert = (s == 0) | (e_cur != blk_e_ref[jnp.maximum(b_cur - 1, 0)])

    @pl.when((s < n_used) & new_expert)
    def _():
        for hbm, stage, sem_i in weight_copies:
            pltpu.make_async_copy(hbm.at[layer, 0], stage, sem_w.at[sem_i]).wait()
        wg_bf[...] = wg_st[...].astype(BF16)
        wu_bf[...] = wu_st[...].astype(BF16)
        wd_bf[...] = wd_st[...].astype(BF16)
        e_next = nxt_e_ref[b_cur]

        @pl.when(e_next != e_cur)
        def _():
            fetch_weights(e_next)

    @pl.when(s < n_used)
    def _():
        wait_scatter(nvalid(s - 2), slot)
        xb = _unpack_rows(xs_ref, nc).astype(BF16)
        hg = jnp.dot(xb, wg_bf[...], preferred_element_type=F32)
        hu = jnp.dot(xb, wu_bf[...], preferred_element_type=F32)
        h = (jax.nn.silu(hg) * hu).astype(BF16)
        _pack_rows(ybuf, jnp.dot(h, wd_bf[...], preferred_element_type=F32), slot)
        for_rows(nvalid(s), lambda j: scatter_row(j, slot))

    @pl.when(s == n_used)
    def _():
        wait_scatter(nvalid(s - 2), slot)
        wait_scatter(nvalid(s - 1), other)


def _moe(xs, row_dst, blk_e, nxt_e, nvalid, n_used, w_gate, w_up, w_down, *, t, layer, rb):
    d_e, d = w_down.shape[-2:]
    n_chunks = d // (2 * LANES)
    nblk = row_dst.shape[0] // rb

    def blk(s, blk_e_ref, nxt_e_ref, nvalid_ref, n_used_ref):
        return jnp.minimum(s, jnp.maximum(n_used_ref[0] - 1, 0))

    any_spec = pl.BlockSpec(memory_space=pl.ANY)
    grid_spec = pltpu.PrefetchScalarGridSpec(
        num_scalar_prefetch=4,
        grid=(nblk + 1,),
        in_specs=[
            pl.BlockSpec((rb,), lambda s, *p: (blk(s, *p),), memory_space=pltpu.SMEM),
            pl.BlockSpec((rb * n_chunks, LANES), lambda s, *p: (blk(s, *p), 0)),
            any_spec, any_spec, any_spec,
        ],
        out_specs=any_spec,
        scratch_shapes=[pltpu.VMEM((2, rb * n_chunks, LANES), U32),
                        pltpu.VMEM((d, d_e), F32), pltpu.VMEM((d, d_e), F32),
                        pltpu.VMEM((d_e, d), F32),
                        pltpu.VMEM((d, d_e), BF16), pltpu.VMEM((d, d_e), BF16),
                        pltpu.VMEM((d_e, d), BF16),
                        pltpu.SemaphoreType.DMA((2,)), pltpu.SemaphoreType.DMA((3,))],
    )
    return pl.pallas_call(
        functools.partial(_moe_kernel, rb=rb, nc=n_chunks, nblk=nblk, layer=layer),
        grid_spec=grid_spec,
        out_shape=jax.ShapeDtypeStruct((TOP_K * t * n_chunks, LANES), U32),
        compiler_params=_params(),
        name="moe",
    )(blk_e, nxt_e, nvalid, n_used, row_dst, xs, w_gate, w_up, w_down)


def _combine_kernel(x1_ref, ya_ref, yb_ref, route_ref, g_ref, b_ref, *rest,
                    nc, alpha, mode, tok_w):
    if mode == "attn":
        w_ref, x2_ref, o_ref = rest
    elif mode == "gmlp":
        w_ref, ng_ref, nb_ref, x2_ref, o_ref = rest
    else:
        (x2_ref,) = rest
    route = route_ref[...]
    y = _unpack_rows(ya_ref, nc) * route[:, 4:5] + _unpack_rows(yb_ref, nc) * route[:, 5:6]
    x2 = _layer_norm(alpha * x1_ref[...] + y, g_ref[...], b_ref[...])
    x2_ref[...] = x2
    if mode == "attn":
        o_ref[...] = jnp.dot(x2.astype(BF16), w_ref[...],
                             preferred_element_type=F32).astype(o_ref.dtype)
    elif mode == "gmlp":
        _gmlp_epilogue(jnp.dot(x2.astype(BF16), w_ref[...], preferred_element_type=F32),
                       ng_ref, nb_ref, o_ref, tok_w)


def _combine(y, x1, route, ln_g, ln_b, *, tm, alpha, w_bf=None, gmlp_norm=None):
    t, d = x1.shape
    n_chunks = d // (2 * LANES)
    mode = "none" if w_bf is None else ("attn" if gmlp_norm is None else "gmlp")
    n_steps = t // tm

    in_specs = [pl.BlockSpec((tm, d), lambda i: (i, 0)),
                pl.BlockSpec((tm * n_chunks, LANES), lambda i: (i, 0)),
                pl.BlockSpec((tm * n_chunks, LANES), lambda i: (n_steps + i, 0)),
                pl.BlockSpec((tm, LANES), lambda i: (i, 0)),
                _resident((1, d)), _resident((1, d))]
    args = [x1, y, y, route, ln_g.reshape(1, d), ln_b.reshape(1, d)]
    out_specs = [pl.BlockSpec((tm, d), lambda i: (i, 0))]
    out_shape = [jax.ShapeDtypeStruct((t, d), F32)]
    tok_w = None
    if w_bf is not None:
        n = w_bf.shape[1]
        in_specs.append(_resident((d, n)))
        args.append(w_bf)
        out_specs.append(pl.BlockSpec((tm, n), lambda i: (i, 0)))
        out_shape.append(jax.ShapeDtypeStruct((t, n), BF16))
    if gmlp_norm is not None:
        tok_w = gmlp_norm[0].shape[-1]
        in_specs += [_resident((1, tok_w)), _resident((1, tok_w))]
        args += [gmlp_norm[0].reshape(1, tok_w), gmlp_norm[1].reshape(1, tok_w)]
    return pl.pallas_call(
        functools.partial(_combine_kernel, nc=n_chunks, alpha=alpha, mode=mode, tok_w=tok_w),
        grid=(n_steps,),
        in_specs=in_specs,
        out_specs=out_specs,
        out_shape=out_shape,
        compiler_params=_params(),
        name="combine_" + mode,
    )(*args)


def _split_bf16(w):
    hi = w.astype(BF16)
    lo = (w - hi.astype(F32)).astype(BF16)
    return hi, lo


def kernel(x, mem, a_w_in, a_sinks, b_w_in, b_norm_g, b_norm_b, b_w_spatial, b_b_spatial,
           w_mem_kv, w_out, ln_g, ln_b, w_router_group, b_router_group,
           w_router_expert, b_router_expert, w_gate, w_up, w_down):
    bsz, seq, d = x.shape
    mem_len = mem.shape[1]
    depth = w_out.shape[0]
    t = bsz * seq
    alpha = (2 * depth) ** 0.25
    rb = ROW_BLOCK
    n_rows = -(-(t * TOP_K + N_EXPERTS * (rb - 1)) // rb) * rb
    nblk = n_rows // rb

    def in_proj_weights(i):
        if i % 2 == 0:
            return a_w_in[i // 2].astype(BF16), None
        return b_w_in[i // 2].astype(BF16), (b_norm_g[i // 2], b_norm_b[i // 2])

    x2d = x.reshape(t, d)
    mem2d = mem.reshape(bsz * mem_len, d)
    proj = _proj(x2d, in_proj_weights(0)[0], BF16, tm=512)
    xs = jnp.zeros((n_rows * (d // (2 * LANES)), LANES), U32)
    for i in range(depth):
        j = i // 2
        kvm = _proj(mem2d, w_mem_kv[i].astype(BF16), BF16, tm=min(256, bsz * mem_len))
        if i % 2 == 0:
            mix = _attn_mixer(proj, kvm, a_sinks[j], seq=seq, mem_len=mem_len, tq=256)
        else:
            mix = _gmlp_mixer(proj, kvm, b_w_spatial[j], b_b_spatial[j],
                              seq=seq, mem_len=mem_len, tq=512)

        w_r = jnp.concatenate([w_router_group[i], w_router_expert[i]], axis=1)
        w_r = jnp.pad(w_r, ((0, 0), (0, LANES - w_r.shape[1])))
        b_r = jnp.concatenate([b_router_group[i], b_router_expert[i]])
        b_r = jnp.pad(b_r, (0, LANES - b_r.shape[0])).reshape(1, LANES)
        wr_hi, wr_lo = _split_bf16(w_r)
        x1, x1p, route, cnt = _post(mix, x2d, w_out[i].astype(BF16), ln_g[i, 0], ln_b[i, 0],
                                    wr_hi, wr_lo, b_r, tm=512, sub=256, alpha=alpha)

        counts = cnt[0, :N_EXPERTS].astype(jnp.int32)
        padded = (counts + rb - 1) // rb * rb
        pend = jnp.cumsum(padded)
        pstart = (pend - padded).astype(jnp.int32)
        n_used = (pend[-1] // rb).astype(jnp.int32).reshape(1)
        blk = jnp.arange(nblk, dtype=jnp.int32)
        blk_e = jnp.searchsorted(pend, jnp.minimum(blk, n_used[0] - 1) * rb,
                                 side='right').astype(jnp.int32)
        nxt_blk = pend[blk_e] // rb
        nxt_e = jnp.where(nxt_blk < n_used[0], blk_e[jnp.minimum(nxt_blk, nblk - 1)], blk_e)
        nvalid = jnp.where(blk < n_used[0],
                           jnp.clip(pstart[blk_e] + counts[blk_e] - blk * rb, 0, rb), 0)
        ri = route[:, :4].astype(jnp.int32)
        e_hot = ri[:, :TOP_K, None] == jnp.arange(N_EXPERTS, dtype=jnp.int32)
        dest = jnp.sum(jnp.where(e_hot, pstart, 0), axis=-1) + ri[:, TOP_K:]
        n_chunks = d // (2 * LANES)
        real_end = jnp.concatenate([pstart + counts, pend[-1:]]).astype(jnp.int32)
        pad_end = jnp.concatenate([pend, jnp.full((1,), n_rows)]).astype(jnp.int32)
        xs, row_asg = _dispatch(x1p, dest[:, 0], dest[:, 1], real_end, pad_end, xs, tm=512)
        row_dst = ((row_asg % TOP_K) * t + row_asg // TOP_K) * n_chunks

        y = _moe(xs, row_dst, blk_e, nxt_e.astype(jnp.int32), nvalid.astype(jnp.int32),
                 n_used, w_gate, w_up, w_down, t=t, layer=i, rb=rb)
        if i + 1 < depth:
            w_next, gmlp_norm = in_proj_weights(i + 1)
            x2d, proj = _combine(y, x1, route, ln_g[i, 1], ln_b[i, 1],
                                 tm=256, alpha=alpha, w_bf=w_next, gmlp_norm=gmlp_norm)
        else:
            (x2d,) = _combine(y, x1, route, ln_g[i, 1], ln_b[i, 1], tm=256, alpha=alpha)
    return x2d.reshape(bsz, seq, d)
```

```python
import functools
import math

import jax
import jax.numpy as jnp
import numpy as np
from jax import lax
from jax.experimental import pallas as pl
from jax.experimental.pallas import tpu as pltpu

MEM_HEADS = 4
MEM_HEAD_DIM = 128
MEM_W = MEM_HEADS * MEM_HEAD_DIM
HEAD_DIM = 64
N_KV_HEADS = 4
KV_W = N_KV_HEADS * HEAD_DIM
WINDOW = 128
CHUNK = 128
GM_GROUP_DIM = 128
N_GROUPS = 8
EXPERTS_PER_GROUP = 8
N_EXPERTS = N_GROUPS * EXPERTS_PER_GROUP
TOP_K = 2
LN_EPS = 1e-5
NEG_INF = -1e30

LANES = 128
VMEM_LIMIT_BYTES = 56 * 1024 * 1024

ROW_BLOCK = 256
BF16 = jnp.bfloat16
F32 = jnp.float32
U32 = jnp.uint32


def _alibi_slopes(n):
    def pow2(m):
        start = 2.0 ** (-8.0 / m)
        return [start ** (i + 1) for i in range(m)]
    if math.log2(n).is_integer():
        s = pow2(n)
    else:
        c = 2 ** math.floor(math.log2(n))
        s = pow2(c) + pow2(2 * c)[0::2][: n - c]
    return [float(v) for v in np.asarray(s, dtype=np.float32)]


def _params(n_axes=1):
    return pltpu.CompilerParams(dimension_semantics=("arbitrary",) * n_axes,
                                vmem_limit_bytes=VMEM_LIMIT_BYTES)


def _resident(shape):
    nd = len(shape)
    return pl.BlockSpec(shape, lambda *_: (0,) * nd, pipeline_mode=pl.Buffered(1))


def _pack_rows(ref, val, *lead, row0=0):
    rows, d = val.shape
    nw = d // (2 * LANES)
    for c in range(nw):
        lo = val[:, c * LANES:(c + 1) * LANES].astype(BF16).astype(F32)
        hi = val[:, (c + nw) * LANES:(c + nw + 1) * LANES].astype(BF16).astype(F32)
        word = (lax.bitcast_convert_type(lo, U32) >> 16) | lax.bitcast_convert_type(hi, U32)
        ref[tuple(lead) + (pl.ds(row0 * nw + c, rows, stride=nw), slice(None))] = word


def _unpack_rows(ref, nw, *lead):
    rows = ref.shape[-2] // nw
    lo, hi = [], []
    for c in range(nw):
        word = ref[tuple(lead) + (pl.ds(c, rows, stride=nw), slice(None))]
        lo.append(lax.bitcast_convert_type(word << 16, F32))
        hi.append(lax.bitcast_convert_type(word & jnp.uint32(0xFFFF0000), F32))
    return jnp.concatenate(lo + hi, axis=-1)


def _layer_norm(x, g, b):
    mu = jnp.mean(x, axis=-1, keepdims=True)
    xc = x - mu
    var = jnp.mean(xc * xc, axis=-1, keepdims=True)
    return xc * lax.rsqrt(var + LN_EPS) * g + b


def _proj_kernel(x_ref, w_ref, o_ref):
    o_ref[...] = jnp.dot(x_ref[...].astype(BF16), w_ref[...],
                         preferred_element_type=F32).astype(o_ref.dtype)


def _proj(x2d, w_bf, out_dtype, tm):
    m, k = x2d.shape
    n = w_bf.shape[1]
    return pl.pallas_call(
        _proj_kernel,
        grid=(m // tm,),
        in_specs=[pl.BlockSpec((tm, k), lambda i: (i, 0)), _resident((k, n))],
        out_specs=pl.BlockSpec((tm, n), lambda i: (i, 0)),
        out_shape=jax.ShapeDtypeStruct((m, n), out_dtype),
        compiler_params=_params(),
        name="proj",
    )(x2d, w_bf)


def _gelu(x):
    return 0.5 * x * (1.0 + lax.erf(x * (2.0 ** -0.5)))


def _gmlp_epilogue(z, g_ref, b_ref, o_ref, tok_w):
    u = _gelu(z[:, :tok_w])
    v = _gelu(z[:, tok_w:2 * tok_w])
    v = _layer_norm(v, g_ref[...], b_ref[...])
    o_ref[:, :tok_w] = u.astype(o_ref.dtype)
    o_ref[:, tok_w:2 * tok_w] = v.astype(o_ref.dtype)
    o_ref[:, 2 * tok_w:] = z[:, 2 * tok_w:].astype(o_ref.dtype)


def _memory_attention(qm, kvm):
    outs = []
    for h in range(MEM_HEADS):
        q = qm[:, h * MEM_HEAD_DIM:(h + 1) * MEM_HEAD_DIM]
        k = kvm[:, h * MEM_HEAD_DIM:(h + 1) * MEM_HEAD_DIM]
        v = kvm[:, MEM_W + h * MEM_HEAD_DIM:MEM_W + (h + 1) * MEM_HEAD_DIM]
        s = lax.dot_general(q, k, (((1,), (1,)), ((), ())),
                            preferred_element_type=F32) * (MEM_HEAD_DIM ** -0.5)
        m = jnp.max(s, axis=-1, keepdims=True)
        p = jnp.exp(s - m)
        p = p * (1.0 / jnp.sum(p, axis=-1, keepdims=True))
        outs.append(jnp.dot(p.astype(BF16), v, preferred_element_type=F32))
    return jnp.concatenate(outs, axis=-1)


def _attn_kernel(sinks_ref, q_ref, kv_ref, kvp_ref, qm_ref, kvm_ref, o_ref, *,
                 tq, seq, slopes, tok_w):
    i = pl.program_id(0)
    n_sub = tq // WINDOW
    gqa = len(slopes) // N_KV_HEADS
    qi = lax.broadcasted_iota(jnp.int32, (WINDOW, WINDOW), 0)
    c = lax.broadcasted_iota(jnp.int32, (WINDOW, WINDOW), 1)
    own = c <= qi
    dist_f = jnp.where(own, qi - c, WINDOW + qi - c).astype(F32)
    nt = (((1,), (1,)), ((), ()))
    for sb in range(n_sub):
        r0 = sb * WINDOW
        if sb == 0:
            kv_prev = kvp_ref[...]
            reach = jnp.where(((i * tq) % seq) == 0, 0, WINDOW)
            valid = c <= qi + reach
        else:
            kv_prev = kv_ref[r0 - WINDOW:r0, :]
            valid = None
        kv_cur = kv_ref[r0:r0 + WINDOW, :]
        for kh in range(N_KV_HEADS):
            ks = slice(kh * HEAD_DIM, (kh + 1) * HEAD_DIM)
            vs = slice(KV_W + kh * HEAD_DIM, KV_W + (kh + 1) * HEAD_DIM)
            heads = range(kh * gqa, (kh + 1) * gqa)
            q = jnp.concatenate(
                [q_ref[r0:r0 + WINDOW, h * HEAD_DIM:(h + 1) * HEAD_DIM] for h in heads], axis=0)
            s_own = lax.dot_general(q, kv_cur[:, ks], nt, preferred_element_type=F32)
            s_prev = lax.dot_general(q, kv_prev[:, ks], nt, preferred_element_type=F32)
            p_own, p_prev = [], []
            for g, h in enumerate(heads):
                rows = slice(g * WINDOW, (g + 1) * WINDOW)
                logits = (jnp.where(own, s_own[rows], s_prev[rows]) * (HEAD_DIM ** -0.5)
                          - slopes[h] * dist_f)
                if valid is not None:
                    logits = jnp.where(valid, logits, NEG_INF)
                sink = sinks_ref[h]
                m = jnp.maximum(jnp.max(logits, axis=-1, keepdims=True), sink)
                p = jnp.exp(logits - m)
                probs = p * (1.0 / (jnp.sum(p, axis=-1, keepdims=True) + jnp.exp(sink - m)))
                p_own.append(jnp.where(own, probs, 0.0).astype(BF16))
                p_prev.append(jnp.where(own, 0.0, probs).astype(BF16))
            out = (jnp.dot(jnp.concatenate(p_own, axis=0), kv_cur[:, vs],
                           preferred_element_type=F32)
                   + jnp.dot(jnp.concatenate(p_prev, axis=0), kv_prev[:, vs],
                             preferred_element_type=F32))
            o_ref[r0:r0 + WINDOW, kh * gqa * HEAD_DIM:(kh + 1) * gqa * HEAD_DIM] = jnp.concatenate(
                [out[g * WINDOW:(g + 1) * WINDOW] for g in range(gqa)], axis=1).astype(o_ref.dtype)
    o_ref[:, tok_w:] = _memory_attention(qm_ref[...], kvm_ref[...]).astype(o_ref.dtype)


def _attn_mixer(proj, kvm, sinks, *, seq, mem_len, tq):
    t = proj.shape[0]
    tok_w = proj.shape[1] - 2 * KV_W - MEM_W
    n_heads = tok_w // HEAD_DIM
    assert tok_w % (2 * KV_W) == 0 and tok_w % MEM_W == 0 and seq % tq == 0
    kv_col = tok_w // (2 * KV_W)
    qm_col = (tok_w + 2 * KV_W) // MEM_W
    sub = tq // WINDOW
    kern = functools.partial(_attn_kernel, tq=tq, seq=seq, slopes=_alibi_slopes(n_heads),
                             tok_w=tok_w)
    return pl.pallas_call(
        kern,
        grid=(t // tq,),
        in_specs=[
            pl.BlockSpec(memory_space=pltpu.SMEM),
            pl.BlockSpec((tq, tok_w), lambda i: (i, 0)),
            pl.BlockSpec((tq, 2 * KV_W), lambda i: (i, kv_col)),
            pl.BlockSpec((WINDOW, 2 * KV_W), lambda i: (jnp.maximum(i * sub - 1, 0), kv_col)),
            pl.BlockSpec((tq, MEM_W), lambda i: (i, qm_col)),
            pl.BlockSpec((mem_len, 2 * MEM_W), lambda i: ((i * tq) // seq, 0)),
        ],
        out_specs=pl.BlockSpec((tq, tok_w + MEM_W), lambda i: (i, 0)),
        out_shape=jax.ShapeDtypeStruct((t, tok_w + MEM_W), BF16),
        compiler_params=_params(),
        name="attn_mixer",
    )(sinks, proj, proj, proj, proj, kvm)


def _gmlp_kernel(u_ref, v_ref, qm_ref, kvm_ref, ws_ref, bs_ref, o_ref, *, tq, tok_w):
    n_sub = tq // CHUNK
    n_grp = tok_w // GM_GROUP_DIM
    row = lax.broadcasted_iota(jnp.int32, (CHUNK, CHUNK), 0)
    col = lax.broadcasted_iota(jnp.int32, (CHUNK, CHUNK), 1)
    lower = row >= col
    for g in range(n_grp):
        w = jnp.where(lower, ws_ref[g], 0.0).astype(BF16)
        bias = bs_ref[:, g:g + 1]
        c0 = g * GM_GROUP_DIM
        for sb in range(n_sub):
            r0 = sb * CHUNK
            sv = jnp.dot(w, v_ref[r0:r0 + CHUNK, c0:c0 + GM_GROUP_DIM],
                         preferred_element_type=F32) + bias
            u = u_ref[r0:r0 + CHUNK, c0:c0 + GM_GROUP_DIM].astype(F32)
            o_ref[r0:r0 + CHUNK, c0:c0 + GM_GROUP_DIM] = (u * sv).astype(o_ref.dtype)
    o_ref[:, tok_w:] = _memory_attention(qm_ref[...], kvm_ref[...]).astype(o_ref.dtype)


def _gmlp_mixer(proj, kvm, w_s, b_s, *, seq, mem_len, tq):
    t = proj.shape[0]
    tok_w = (proj.shape[1] - MEM_W) // 2
    n_grp = tok_w // GM_GROUP_DIM
    assert (2 * tok_w) % MEM_W == 0 and seq % tq == 0
    kern = functools.partial(_gmlp_kernel, tq=tq, tok_w=tok_w)
    return pl.pallas_call(
        kern,
        grid=(t // tq,),
        in_specs=[
            pl.BlockSpec((tq, tok_w), lambda i: (i, 0)),
            pl.BlockSpec((tq, tok_w), lambda i: (i, 1)),
            pl.BlockSpec((tq, MEM_W), lambda i: (i, (2 * tok_w) // MEM_W)),
            pl.BlockSpec((mem_len, 2 * MEM_W), lambda i: ((i * tq) // seq, 0)),
            _resident((n_grp, CHUNK, CHUNK)),
            _resident((CHUNK, n_grp)),
        ],
        out_specs=pl.BlockSpec((tq, tok_w + MEM_W), lambda i: (i, 0)),
        out_shape=jax.ShapeDtypeStruct((t, tok_w + MEM_W), BF16),
        compiler_params=_params(),
        name="gmlp_mixer",
    )(proj, proj, proj, kvm, w_s, b_s.T)


def _post_kernel(mix_ref, x_ref, wo_ref, g_ref, b_ref, wr_ref, br_ref,
                 x1_ref, xp_ref, route_ref, cnt_ref, carry_ref, *, tm, sub, alpha):
    i = pl.program_id(0)

    @pl.when(i == 0)
    def _():
        carry_ref[...] = jnp.zeros_like(carry_ref)

    carry = carry_ref[0:1, :]
    for r0 in range(0, tm, sub):
        carry = _post_rows(r0, sub, carry, mix_ref, x_ref, wo_ref, g_ref, b_ref, wr_ref,
                           br_ref, x1_ref, xp_ref, route_ref, alpha)
    carry_ref[...] = jnp.broadcast_to(carry, carry_ref.shape)
    cnt_ref[...] = jnp.broadcast_to(carry, cnt_ref.shape)


def _post_rows(r0, tm, carry, mix_ref, x_ref, wo_ref, g_ref, b_ref, wr_ref, br_ref,
               x1_ref, xp_ref, route_ref, alpha):
    rows = slice(r0, r0 + tm)
    y = jnp.dot(mix_ref[rows, :], wo_ref[...], preferred_element_type=F32)
    x1 = _layer_norm(alpha * x_ref[rows, :] + y, g_ref[...], b_ref[...])
    x1_ref[rows, :] = x1
    _pack_rows(xp_ref, x1, row0=r0)

    xh = x1.astype(BF16)
    xl = (x1 - xh.astype(F32)).astype(BF16)
    hh_hl = jnp.dot(xh, wr_ref[...], preferred_element_type=F32)
    lh = jnp.dot(xl, wr_ref[:, :LANES], preferred_element_type=F32)
    logits = (hh_hl[:, :LANES] + (hh_hl[:, LANES:] + lh)) + br_ref[...]

    lane = lax.broadcasted_iota(jnp.int32, (tm, LANES), 1)
    lane_f = lane.astype(F32)
    big = float(LANES)
    is_grp = lane < N_GROUPS
    gl = jnp.where(is_grp, logits, NEG_INF)
    gmax = jnp.max(gl, axis=-1, keepdims=True)
    g_sel = jnp.min(jnp.where(gl == gmax, lane_f, big), axis=-1, keepdims=True)
    p_grp = 1.0 / jnp.sum(jnp.where(is_grp, jnp.exp(gl - gmax), 0.0), axis=-1, keepdims=True)

    e_lo = N_GROUPS + g_sel * EXPERTS_PER_GROUP
    in_grp = (lane_f >= e_lo) & (lane_f < e_lo + EXPERTS_PER_GROUP)
    el = jnp.where(in_grp, logits, NEG_INF)
    m1 = jnp.max(el, axis=-1, keepdims=True)
    i1 = jnp.min(jnp.where(el == m1, lane_f, big), axis=-1, keepdims=True)
    el2 = jnp.where(lane_f == i1, NEG_INF, el)
    m2 = jnp.max(el2, axis=-1, keepdims=True)
    i2 = jnp.min(jnp.where(el2 == m2, lane_f, big), axis=-1, keepdims=True)
    z = jnp.sum(jnp.where(in_grp, jnp.exp(el - m1), 0.0), axis=-1, keepdims=True)
    tp1 = 1.0 / z
    tp2 = jnp.exp(m2 - m1) / z
    gate1 = p_grp * tp1 / (tp1 + tp2)
    gate2 = p_grp * tp2 / (tp1 + tp2)
    e1 = i1 - N_GROUPS
    e2 = i2 - N_GROUPS

    oh1 = lane_f == e1
    oh2 = lane_f == e2
    c = jnp.where(oh1 | oh2, 1.0, 0.0)
    r = lax.broadcasted_iota(jnp.int32, (tm, tm), 0)
    s = lax.broadcasted_iota(jnp.int32, (tm, tm), 1)
    tri = jnp.where(r > s, 1.0, 0.0).astype(BF16)
    prefix = jnp.dot(tri, c.astype(BF16), preferred_element_type=F32) + carry
    rank1 = jnp.sum(jnp.where(oh1, prefix, 0.0), axis=-1, keepdims=True)
    rank2 = jnp.sum(jnp.where(oh2, prefix, 0.0), axis=-1, keepdims=True)

    route = jnp.where(lane == 0, e1, 0.0)
    route = jnp.where(lane == 1, e2, route)
    route = jnp.where(lane == 2, rank1, route)
    route = jnp.where(lane == 3, rank2, route)
    route = jnp.where(lane == 4, gate1, route)
    route = jnp.where(lane == 5, gate2, route)
    route_ref[rows, :] = route
    return carry + jnp.sum(c, axis=0, keepdims=True)


def _post(mix, x2d, wo_bf, ln_g, ln_b, wr_hi_lo, b_r, *, tm, sub, alpha):
    t, d = x2d.shape
    kern = functools.partial(_post_kernel, tm=tm, sub=sub, alpha=alpha)
    return pl.pallas_call(
        kern,
        grid=(t // tm,),
        in_specs=[
            pl.BlockSpec((tm, d), lambda i: (i, 0)),
            pl.BlockSpec((tm, d), lambda i: (i, 0)),
            _resident((d, d)),
            _resident((1, d)), _resident((1, d)),
            _resident((d, 2 * LANES)), _resident((1, LANES)),
        ],
        out_specs=[
            pl.BlockSpec((tm, d), lambda i: (i, 0)),
            pl.BlockSpec((tm * (d // (2 * LANES)), LANES), lambda i: (i, 0)),
            pl.BlockSpec((tm, LANES), lambda i: (i, 0)),
            pl.BlockSpec((8, LANES), lambda i: (0, 0)),
        ],
        out_shape=[
            jax.ShapeDtypeStruct((t, d), F32),
            jax.ShapeDtypeStruct((t * (d // (2 * LANES)), LANES), U32),
            jax.ShapeDtypeStruct((t, LANES), F32),
            jax.ShapeDtypeStruct((8, LANES), F32),
        ],
        scratch_shapes=[pltpu.VMEM((8, LANES), F32)],
        compiler_params=_params(),
        name="post",
    )(mix, x2d, wo_bf, ln_g.reshape(1, d), ln_b.reshape(1, d), wr_hi_lo, b_r)


def _dispatch_kernel(real_end_ref, pad_end_ref, da_ref, db_ref, xp_ref, stale_hbm,
                     xs_hbm, inv_hbm, inv_ref, sem, sem_inv, *, tm, nc):
    del stale_hbm
    i = pl.program_id(0)
    unroll = 8

    @pl.when(i == 0)
    def _():
        def expert(e, carry):
            def clear(r, c):
                inv_ref[r] = 0
                return c
            lax.fori_loop(real_end_ref[e], pad_end_ref[e], clear, 0)
            return carry
        lax.fori_loop(0, real_end_ref.shape[0], expert, 0)

    def group(g, carry):
        for u in range(unroll):
            j = g * unroll + u
            src = xp_ref.at[pl.ds(j * nc, nc)]
            for k, d_ref in enumerate((da_ref, db_ref)):
                row = d_ref[j]
                pltpu.make_async_copy(src, xs_hbm.at[pl.ds(pl.multiple_of(row * nc, nc), nc)],
                                      sem).start()
                inv_ref[row] = TOP_K * (i * tm + j) + k
        return carry
    lax.fori_loop(0, tm // unroll, group, 0)
    for _ in range(TOP_K):
        pltpu.make_async_copy(xp_ref, xs_hbm.at[pl.ds(0, tm * nc)], sem).wait()

    @pl.when(i == pl.num_programs(0) - 1)
    def _():
        copy = pltpu.make_async_copy(inv_ref, inv_hbm, sem_inv)
        copy.start()
        copy.wait()


def _dispatch(xp, dest_a, dest_b, real_end, pad_end, stale, *, tm):
    nc = xp.shape[0] // dest_a.shape[0]
    t = dest_a.shape[0]
    n_rows = stale.shape[0] // nc
    smem = pl.BlockSpec((tm,), lambda i, *_: (i,), memory_space=pltpu.SMEM)
    any_spec = pl.BlockSpec(memory_space=pl.ANY)
    grid_spec = pltpu.PrefetchScalarGridSpec(
        num_scalar_prefetch=2,
        grid=(t // tm,),
        in_specs=[smem, smem, pl.BlockSpec((tm * nc, LANES), lambda i, *_: (i, 0)), any_spec],
        out_specs=[any_spec, any_spec],
        scratch_shapes=[pltpu.SMEM((n_rows,), jnp.int32), pltpu.SemaphoreType.DMA(()),
                        pltpu.SemaphoreType.DMA(())],
    )
    return pl.pallas_call(
        functools.partial(_dispatch_kernel, tm=tm, nc=nc),
        grid_spec=grid_spec,
        out_shape=[jax.ShapeDtypeStruct(stale.shape, stale.dtype),
                   jax.ShapeDtypeStruct((n_rows,), jnp.int32)],
        input_output_aliases={5: 0},
        compiler_params=_params(),
        name="dispatch",
    )(real_end, pad_end, dest_a, dest_b, xp, stale)


def _moe_kernel(blk_e_ref, nxt_e_ref, nvalid_ref, n_used_ref, dst_ref,
                xs_ref, wg_hbm, wu_hbm, wd_hbm, y_hbm,
                ybuf, wg_st, wu_st, wd_st, wg_bf, wu_bf, wd_bf, sem_s, sem_w,
                *, rb, nc, nblk, layer):
    s = pl.program_id(0)
    n_used = n_used_ref[0]
    slot = lax.rem(s, 2)
    other = 1 - slot
    unroll = 8

    def nvalid(b):
        return jnp.where((b >= 0) & (b < nblk), nvalid_ref[jnp.clip(b, 0, nblk - 1)], 0)

    def scatter_row(j, src_slot):
        dst = y_hbm.at[pl.ds(pl.multiple_of(dst_ref[j], nc), nc)]
        pltpu.make_async_copy(ybuf.at[src_slot, pl.ds(j * nc, nc)], dst,
                              sem_s.at[src_slot]).start()

    def for_rows(n, row_fn):
        def group(g, carry):
            for u in range(unroll):
                row_fn(g * unroll + u)
            return carry
        lax.fori_loop(0, n // unroll, group, 0)

        def single(j, carry):
            row_fn(j)
            return carry
        lax.fori_loop((n // unroll) * unroll, n, single, 0)

    def wait_rows(n, make_copy):
        p = rb
        while p >= 1:
            @pl.when((n & p) != 0)
            def _(p=p):
                make_copy(p).wait()
            p //= 2

    def wait_scatter(n, sl):
        wait_rows(n, lambda p: pltpu.make_async_copy(
            ybuf.at[sl, pl.ds(0, p * nc)], y_hbm.at[pl.ds(0, p * nc)], sem_s.at[sl]))

    weight_copies = ((wg_hbm, wg_st, 0), (wu_hbm, wu_st, 1), (wd_hbm, wd_st, 2))

    def fetch_weights(e):
        for hbm, stage, sem_i in weight_copies:
            pltpu.make_async_copy(hbm.at[layer, e], stage, sem_w.at[sem_i]).start()

    @pl.when(s == 0)
    def _():
        fetch_weights(blk_e_ref[0])

    b_cur = jnp.minimum(s, nblk - 1)
    e_cur = blk_e_ref[b_cur]
    new_expert = (s == 0) | (e_cur != blk_e_ref[jnp.maximum(b_cur - 1, 0)])

    @pl.when((s < n_used) & new_expert)
    def _():
        for hbm, stage, sem_i in weight_copies:
            pltpu.make_async_copy(hbm.at[layer, 0], stage, sem_w.at[sem_i]).wait()
        wg_bf[...] = wg_st[...].astype(BF16)
        wu_bf[...] = wu_st[...].astype(BF16)
        wd_bf[...] = wd_st[...].astype(BF16)
        e_next = nxt_e_ref[b_cur]

        @pl.when(e_next != e_cur)
        def _():
            fetch_weights(e_next)

    @pl.when(s < n_used)
    def _():
        wait_scatter(nvalid(s - 2), slot)
        xb = _unpack_rows(xs_ref, nc).astype(BF16)
        hg = jnp.dot(xb, wg_bf[...], preferred_element_type=F32)
        hu = jnp.dot(xb, wu_bf[...], preferred_element_type=F32)
        h = (jax.nn.silu(hg) * hu).astype(BF16)
        _pack_rows(ybuf, jnp.dot(h, wd_bf[...], preferred_element_type=F32), slot)
        for_rows(nvalid(s), lambda j: scatter_row(j, slot))

    @pl.when(s == n_used)
    def _():
        wait_scatter(nvalid(s - 2), slot)
        wait_scatter(nvalid(s - 1), other)


def _moe(xs, row_dst, blk_e, nxt_e, nvalid, n_used, w_gate, w_up, w_down, *, t, layer, rb):
    d_e, d = w_down.shape[-2:]
    n_chunks = d // (2 * LANES)
    nblk = row_dst.shape[0] // rb

    def blk(s, blk_e_ref, nxt_e_ref, nvalid_ref, n_used_ref):
        return jnp.minimum(s, jnp.maximum(n_used_ref[0] - 1, 0))

    any_spec = pl.BlockSpec(memory_space=pl.ANY)
    grid_spec = pltpu.PrefetchScalarGridSpec(
        num_scalar_prefetch=4,
        grid=(nblk + 1,),
        in_specs=[
            pl.BlockSpec((rb,), lambda s, *p: (blk(s, *p),), memory_space=pltpu.SMEM),
            pl.BlockSpec((rb * n_chunks, LANES), lambda s, *p: (blk(s, *p), 0)),
            any_spec, any_spec, any_spec,
        ],
        out_specs=any_spec,
        scratch_shapes=[pltpu.VMEM((2, rb * n_chunks, LANES), U32),
                        pltpu.VMEM((d, d_e), F32), pltpu.VMEM((d, d_e), F32),
                        pltpu.VMEM((d_e, d), F32),
                        pltpu.VMEM((d, d_e), BF16), pltpu.VMEM((d, d_e), BF16),
                        pltpu.VMEM((d_e, d), BF16),
                        pltpu.SemaphoreType.DMA((2,)), pltpu.SemaphoreType.DMA((3,))],
    )
    return pl.pallas_call(
        functools.partial(_moe_kernel, rb=rb, nc=n_chunks, nblk=nblk, layer=layer),
        grid_spec=grid_spec,
        out_shape=jax.ShapeDtypeStruct((TOP_K * t * n_chunks, LANES), U32),
        compiler_params=_params(),
        name="moe",
    )(blk_e, nxt_e, nvalid, n_used, row_dst, xs, w_gate, w_up, w_down)


def _combine_kernel(x1_ref, ya_ref, yb_ref, route_ref, g_ref, b_ref, *rest,
                    nc, alpha, mode, tok_w):
    if mode == "attn":
        w_ref, x2_ref, o_ref = rest
    elif mode == "gmlp":
        w_ref, ng_ref, nb_ref, x2_ref, o_ref = rest
    else:
        (x2_ref,) = rest
    route = route_ref[...]
    y = _unpack_rows(ya_ref, nc) * route[:, 4:5] + _unpack_rows(yb_ref, nc) * route[:, 5:6]
    x2 = _layer_norm(alpha * x1_ref[...] + y, g_ref[...], b_ref[...])
    x2_ref[...] = x2
    if mode == "attn":
        o_ref[...] = jnp.dot(x2.astype(BF16), w_ref[...],
                             preferred_element_type=F32).astype(o_ref.dtype)
    elif mode == "gmlp":
        _gmlp_epilogue(jnp.dot(x2.astype(BF16), w_ref[...], preferred_element_type=F32),
                       ng_ref, nb_ref, o_ref, tok_w)


def _combine(y, x1, route, ln_g, ln_b, *, tm, alpha, w_bf=None, gmlp_norm=None):
    t, d = x1.shape
    n_chunks = d // (2 * LANES)
    mode = "none" if w_bf is None else ("attn" if gmlp_norm is None else "gmlp")
    n_steps = t // tm

    in_specs = [pl.BlockSpec((tm, d), lambda i: (i, 0)),
                pl.BlockSpec((tm * n_chunks, LANES), lambda i: (i, 0)),
                pl.BlockSpec((tm * n_chunks, LANES), lambda i: (n_steps + i, 0)),
                pl.BlockSpec((tm, LANES), lambda i: (i, 0)),
                _resident((1, d)), _resident((1, d))]
    args = [x1, y, y, route, ln_g.reshape(1, d), ln_b.reshape(1, d)]
    out_specs = [pl.BlockSpec((tm, d), lambda i: (i, 0))]
    out_shape = [jax.ShapeDtypeStruct((t, d), F32)]
    tok_w = None
    if w_bf is not None:
        n = w_bf.shape[1]
        in_specs.append(_resident((d, n)))
        args.append(w_bf)
        out_specs.append(pl.BlockSpec((tm, n), lambda i: (i, 0)))
        out_shape.append(jax.ShapeDtypeStruct((t, n), BF16))
    if gmlp_norm is not None:
        tok_w = gmlp_norm[0].shape[-1]
        in_specs += [_resident((1, tok_w)), _resident((1, tok_w))]
        args += [gmlp_norm[0].reshape(1, tok_w), gmlp_norm[1].reshape(1, tok_w)]
    return pl.pallas_call(
        functools.partial(_combine_kernel, nc=n_chunks, alpha=alpha, mode=mode, tok_w=tok_w),
        grid=(n_steps,),
        in_specs=in_specs,
        out_specs=out_specs,
        out_shape=out_shape,
        compiler_params=_params(),
        name="combine_" + mode,
    )(*args)


def _split_bf16(w):
    hi = w.astype(BF16)
    lo = (w - hi.astype(F32)).astype(BF16)
    return hi, lo


def kernel(x, mem, a_w_in, a_sinks, b_w_in, b_norm_g, b_norm_b, b_w_spatial, b_b_spatial,
           w_mem_kv, w_out, ln_g, ln_b, w_router_group, b_router_group,
           w_router_expert, b_router_expert, w_gate, w_up, w_down):
    bsz, seq, d = x.shape
    mem_len = mem.shape[1]
    depth = w_out.shape[0]
    t = bsz * seq
    alpha = (2 * depth) ** 0.25
    rb = ROW_BLOCK
    n_rows = -(-(t * TOP_K + N_EXPERTS * (rb - 1)) // rb) * rb
    nblk = n_rows // rb

    def in_proj_weights(i):
        if i % 2 == 0:
            return a_w_in[i // 2].astype(BF16), None
        return b_w_in[i // 2].astype(BF16), (b_norm_g[i // 2], b_norm_b[i // 2])

    x2d = x.reshape(t, d)
    mem2d = mem.reshape(bsz * mem_len, d)
    proj = _proj(x2d, in_proj_weights(0)[0], BF16, tm=512)
    xs = jnp.zeros((n_rows * (d // (2 * LANES)), LANES), U32)
    for i in range(depth):
        j = i // 2
        kvm = _proj(mem2d, w_mem_kv[i].astype(BF16), BF16, tm=min(256, bsz * mem_len))
        if i % 2 == 0:
            mix = _attn_mixer(proj, kvm, a_sinks[j], seq=seq, mem_len=mem_len, tq=256)
        else:
            mix = _gmlp_mixer(proj, kvm, b_w_spatial[j], b_b_spatial[j],
                              seq=seq, mem_len=mem_len, tq=512)

        w_r = jnp.concatenate([w_router_group[i], w_router_expert[i]], axis=1)
        w_r = jnp.pad(w_r, ((0, 0), (0, LANES - w_r.shape[1])))
        b_r = jnp.concatenate([b_router_group[i], b_router_expert[i]])
        b_r = jnp.pad(b_r, (0, LANES - b_r.shape[0])).reshape(1, LANES)
        wr_hi_lo = jnp.concatenate(_split_bf16(w_r), axis=1)
        x1, x1p, route, cnt = _post(mix, x2d, w_out[i].astype(BF16), ln_g[i, 0], ln_b[i, 0],
                                    wr_hi_lo, b_r, tm=512, sub=256, alpha=alpha)

        counts = cnt[0, :N_EXPERTS].astype(jnp.int32)
        padded = (counts + rb - 1) // rb * rb
        pend = jnp.cumsum(padded)
        pstart = (pend - padded).astype(jnp.int32)
        n_used = (pend[-1] // rb).astype(jnp.int32).reshape(1)
        blk = jnp.arange(nblk, dtype=jnp.int32)
        blk_e = jnp.searchsorted(pend, jnp.minimum(blk, n_used[0] - 1) * rb,
                                 side='right').astype(jnp.int32)
        nxt_blk = pend[blk_e] // rb
        nxt_e = jnp.where(nxt_blk < n_used[0], blk_e[jnp.minimum(nxt_blk, nblk - 1)], blk_e)
        nvalid = jnp.where(blk < n_used[0],
                           jnp.clip(pstart[blk_e] + counts[blk_e] - blk * rb, 0, rb), 0)
        ri = route[:, :4].astype(jnp.int32)
        e_hot = ri[:, :TOP_K, None] == jnp.arange(N_EXPERTS, dtype=jnp.int32)
        dest = jnp.sum(jnp.where(e_hot, pstart, 0), axis=-1) + ri[:, TOP_K:]
        n_chunks = d // (2 * LANES)
        real_end = jnp.concatenate([pstart + counts, pend[-1:]]).astype(jnp.int32)
        pad_end = jnp.concatenate([pend, jnp.full((1,), n_rows)]).astype(jnp.int32)
        xs, row_asg = _dispatch(x1p, dest[:, 0], dest[:, 1], real_end, pad_end, xs, tm=512)
        row_dst = ((row_asg % TOP_K) * t + row_asg // TOP_K) * n_chunks

        y = _moe(xs, row_dst, blk_e, nxt_e.astype(jnp.int32), nvalid.astype(jnp.int32),
                 n_used, w_gate, w_up, w_down, t=t, layer=i, rb=rb)
        if i + 1 < depth:
            w_next, gmlp_norm = in_proj_weights(i + 1)
            x2d, proj = _combine(y, x1, route, ln_g[i, 1], ln_b[i, 1],
                                 tm=256, alpha=alpha, w_bf=w_next, gmlp_norm=gmlp_norm)
        else:
            (x2d,) = _combine(y, x1, route, ln_g[i, 1], ln_b[i, 1], tm=256, alpha=alpha)
    return x2d.reshape(bsz, seq, d)
```

```python
import functools
import math

import jax
import jax.numpy as jnp
import numpy as np
from jax import lax
from jax.experimental import pallas as pl
from jax.experimental.pallas import tpu as pltpu

MEM_HEADS = 4
MEM_HEAD_DIM = 128
MEM_W = MEM_HEADS * MEM_HEAD_DIM
HEAD_DIM = 64
N_KV_HEADS = 4
KV_W = N_KV_HEADS * HEAD_DIM
WINDOW = 128
CHUNK = 128
GM_GROUP_DIM = 128
N_GROUPS = 8
EXPERTS_PER_GROUP = 8
N_EXPERTS = N_GROUPS * EXPERTS_PER_GROUP
TOP_K = 2
LN_EPS = 1e-5
NEG_INF = -1e30

LANES = 128
VMEM_LIMIT_BYTES = 56 * 1024 * 1024

ROW_BLOCK = 256
BF16 = jnp.bfloat16
F32 = jnp.float32
U32 = jnp.uint32


def _alibi_slopes(n):
    def pow2(m):
        start = 2.0 ** (-8.0 / m)
        return [start ** (i + 1) for i in range(m)]
    if math.log2(n).is_integer():
        s = pow2(n)
    else:
        c = 2 ** math.floor(math.log2(n))
        s = pow2(c) + pow2(2 * c)[0::2][: n - c]
    return [float(v) for v in np.asarray(s, dtype=np.float32)]


def _params(n_axes=1):
    return pltpu.CompilerParams(dimension_semantics=("arbitrary",) * n_axes,
                                vmem_limit_bytes=VMEM_LIMIT_BYTES)


def _resident(shape):
    nd = len(shape)
    return pl.BlockSpec(shape, lambda *_: (0,) * nd, pipeline_mode=pl.Buffered(1))


def _pack_rows(ref, val, *lead, row0=0):
    rows, d = val.shape
    nw = d // (2 * LANES)
    for c in range(nw):
        lo = val[:, c * LANES:(c + 1) * LANES].astype(BF16).astype(F32)
        hi = val[:, (c + nw) * LANES:(c + nw + 1) * LANES].astype(BF16).astype(F32)
        word = (lax.bitcast_convert_type(lo, U32) >> 16) | lax.bitcast_convert_type(hi, U32)
        ref[tuple(lead) + (pl.ds(row0 * nw + c, rows, stride=nw), slice(None))] = word


def _unpack_rows(ref, nw, *lead):
    rows = ref.shape[-2] // nw
    lo, hi = [], []
    for c in range(nw):
        word = ref[tuple(lead) + (pl.ds(c, rows, stride=nw), slice(None))]
        lo.append(lax.bitcast_convert_type(word << 16, F32))
        hi.append(lax.bitcast_convert_type(word & jnp.uint32(0xFFFF0000), F32))
    return jnp.concatenate(lo + hi, axis=-1)


def _layer_norm(x, g, b):
    mu = jnp.mean(x, axis=-1, keepdims=True)
    xc = x - mu
    var = jnp.mean(xc * xc, axis=-1, keepdims=True)
    return xc * lax.rsqrt(var + LN_EPS) * g + b


def _proj_kernel(x_ref, w_ref, o_ref):
    o_ref[...] = jnp.dot(x_ref[...].astype(BF16), w_ref[...],
                         preferred_element_type=F32).astype(o_ref.dtype)


def _proj(x2d, w_bf, out_dtype, tm):
    m, k = x2d.shape
    n = w_bf.shape[1]
    return pl.pallas_call(
        _proj_kernel,
        grid=(m // tm,),
        in_specs=[pl.BlockSpec((tm, k), lambda i: (i, 0)), _resident((k, n))],
        out_specs=pl.BlockSpec((tm, n), lambda i: (i, 0)),
        out_shape=jax.ShapeDtypeStruct((m, n), out_dtype),
        compiler_params=_params(),
        name="proj",
    )(x2d, w_bf)


def _gelu(x):
    return 0.5 * x * (1.0 + lax.erf(x * (2.0 ** -0.5)))


def _gmlp_epilogue(z, g_ref, b_ref, o_ref, tok_w):
    u = _gelu(z[:, :tok_w])
    v = _gelu(z[:, tok_w:2 * tok_w])
    v = _layer_norm(v, g_ref[...], b_ref[...])
    o_ref[:, :tok_w] = u.astype(o_ref.dtype)
    o_ref[:, tok_w:2 * tok_w] = v.astype(o_ref.dtype)
    o_ref[:, 2 * tok_w:] = z[:, 2 * tok_w:].astype(o_ref.dtype)


def _memory_attention(qm, kvm):
    outs = []
    for h in range(MEM_HEADS):
        q = qm[:, h * MEM_HEAD_DIM:(h + 1) * MEM_HEAD_DIM]
        k = kvm[:, h * MEM_HEAD_DIM:(h + 1) * MEM_HEAD_DIM]
        v = kvm[:, MEM_W + h * MEM_HEAD_DIM:MEM_W + (h + 1) * MEM_HEAD_DIM]
        s = lax.dot_general(q, k, (((1,), (1,)), ((), ())),
                            preferred_element_type=F32) * (MEM_HEAD_DIM ** -0.5)
        m = jnp.max(s, axis=-1, keepdims=True)
        p = jnp.exp(s - m)
        p = p * (1.0 / jnp.sum(p, axis=-1, keepdims=True))
        outs.append(jnp.dot(p.astype(BF16), v, preferred_element_type=F32))
    return jnp.concatenate(outs, axis=-1)


def _attn_kernel(sinks_ref, q_ref, kv_ref, kvp_ref, qm_ref, kvm_ref, o_ref, *,
                 tq, seq, slopes, tok_w):
    i = pl.program_id(0)
    n_sub = tq // WINDOW
    gqa = len(slopes) // N_KV_HEADS
    qi = lax.broadcasted_iota(jnp.int32, (WINDOW, WINDOW), 0)
    c = lax.broadcasted_iota(jnp.int32, (WINDOW, WINDOW), 1)
    own = c <= qi
    dist_f = jnp.where(own, qi - c, WINDOW + qi - c).astype(F32)
    nt = (((1,), (1,)), ((), ()))
    for sb in range(n_sub):
        r0 = sb * WINDOW
        if sb == 0:
            kv_prev = kvp_ref[...]
            reach = jnp.where(((i * tq) % seq) == 0, 0, WINDOW)
            valid = c <= qi + reach
        else:
            kv_prev = kv_ref[r0 - WINDOW:r0, :]
            valid = None
        kv_cur = kv_ref[r0:r0 + WINDOW, :]
        for kh in range(N_KV_HEADS):
            ks = slice(kh * HEAD_DIM, (kh + 1) * HEAD_DIM)
            vs = slice(KV_W + kh * HEAD_DIM, KV_W + (kh + 1) * HEAD_DIM)
            heads = range(kh * gqa, (kh + 1) * gqa)
            q = jnp.concatenate(
                [q_ref[r0:r0 + WINDOW, h * HEAD_DIM:(h + 1) * HEAD_DIM] for h in heads], axis=0)
            s_own = lax.dot_general(q, kv_cur[:, ks], nt, preferred_element_type=F32)
            s_prev = lax.dot_general(q, kv_prev[:, ks], nt, preferred_element_type=F32)
            p_own, p_prev = [], []
            for g, h in enumerate(heads):
                rows = slice(g * WINDOW, (g + 1) * WINDOW)
                logits = (jnp.where(own, s_own[rows], s_prev[rows]) * (HEAD_DIM ** -0.5)
                          - slopes[h] * dist_f)
                if valid is not None:
                    logits = jnp.where(valid, logits, NEG_INF)
                sink = sinks_ref[h]
                m = jnp.maximum(jnp.max(logits, axis=-1, keepdims=True), sink)
                p = jnp.exp(logits - m)
                probs = p * (1.0 / (jnp.sum(p, axis=-1, keepdims=True) + jnp.exp(sink - m)))
                p_own.append(jnp.where(own, probs, 0.0).astype(BF16))
                p_prev.append(jnp.where(own, 0.0, probs).astype(BF16))
            out = (jnp.dot(jnp.concatenate(p_own, axis=0), kv_cur[:, vs],
                           preferred_element_type=F32)
                   + jnp.dot(jnp.concatenate(p_prev, axis=0), kv_prev[:, vs],
                             preferred_element_type=F32))
            o_ref[r0:r0 + WINDOW, kh * gqa * HEAD_DIM:(kh + 1) * gqa * HEAD_DIM] = jnp.concatenate(
                [out[g * WINDOW:(g + 1) * WINDOW] for g in range(gqa)], axis=1).astype(o_ref.dtype)
    o_ref[:, tok_w:] = _memory_attention(qm_ref[...], kvm_ref[...]).astype(o_ref.dtype)


def _attn_mixer(proj, kvm, sinks, *, seq, mem_len, tq):
    t = proj.shape[0]
    tok_w = proj.shape[1] - 2 * KV_W - MEM_W
    n_heads = tok_w // HEAD_DIM
    assert tok_w % (2 * KV_W) == 0 and tok_w % MEM_W == 0 and seq % tq == 0
    kv_col = tok_w // (2 * KV_W)
    qm_col = (tok_w + 2 * KV_W) // MEM_W
    sub = tq // WINDOW
    kern = functools.partial(_attn_kernel, tq=tq, seq=seq, slopes=_alibi_slopes(n_heads),
                             tok_w=tok_w)
    return pl.pallas_call(
        kern,
        grid=(t // tq,),
        in_specs=[
            pl.BlockSpec(memory_space=pltpu.SMEM),
            pl.BlockSpec((tq, tok_w), lambda i: (i, 0)),
            pl.BlockSpec((tq, 2 * KV_W), lambda i: (i, kv_col)),
            pl.BlockSpec((WINDOW, 2 * KV_W), lambda i: (jnp.maximum(i * sub - 1, 0), kv_col)),
            pl.BlockSpec((tq, MEM_W), lambda i: (i, qm_col)),
            pl.BlockSpec((mem_len, 2 * MEM_W), lambda i: ((i * tq) // seq, 0)),
        ],
        out_specs=pl.BlockSpec((tq, tok_w + MEM_W), lambda i: (i, 0)),
        out_shape=jax.ShapeDtypeStruct((t, tok_w + MEM_W), BF16),
        compiler_params=_params(),
        name="attn_mixer",
    )(sinks, proj, proj, proj, proj, kvm)


def _gmlp_kernel(u_ref, v_ref, qm_ref, kvm_ref, ws_ref, bs_ref, o_ref, *, tq, tok_w):
    n_sub = tq // CHUNK
    n_grp = tok_w // GM_GROUP_DIM
    row = lax.broadcasted_iota(jnp.int32, (CHUNK, CHUNK), 0)
    col = lax.broadcasted_iota(jnp.int32, (CHUNK, CHUNK), 1)
    lower = row >= col
    for g in range(n_grp):
        w = jnp.where(lower, ws_ref[g], 0.0).astype(BF16)
        bias = bs_ref[:, g:g + 1]
        c0 = g * GM_GROUP_DIM
        for sb in range(n_sub):
            r0 = sb * CHUNK
            sv = jnp.dot(w, v_ref[r0:r0 + CHUNK, c0:c0 + GM_GROUP_DIM],
                         preferred_element_type=F32) + bias
            u = u_ref[r0:r0 + CHUNK, c0:c0 + GM_GROUP_DIM].astype(F32)
            o_ref[r0:r0 + CHUNK, c0:c0 + GM_GROUP_DIM] = (u * sv).astype(o_ref.dtype)
    o_ref[:, tok_w:] = _memory_attention(qm_ref[...], kvm_ref[...]).astype(o_ref.dtype)


def _gmlp_mixer(proj, kvm, w_s, b_s, *, seq, mem_len, tq):
    t = proj.shape[0]
    tok_w = (proj.shape[1] - MEM_W) // 2
    n_grp = tok_w // GM_GROUP_DIM
    assert (2 * tok_w) % MEM_W == 0 and seq % tq == 0
    kern = functools.partial(_gmlp_kernel, tq=tq, tok_w=tok_w)
    return pl.pallas_call(
        kern,
        grid=(t // tq,),
        in_specs=[
            pl.BlockSpec((tq, tok_w), lambda i: (i, 0)),
            pl.BlockSpec((tq, tok_w), lambda i: (i, 1)),
            pl.BlockSpec((tq, MEM_W), lambda i: (i, (2 * tok_w) // MEM_W)),
            pl.BlockSpec((mem_len, 2 * MEM_W), lambda i: ((i * tq) // seq, 0)),
            _resident((n_grp, CHUNK, CHUNK)),
            _resident((CHUNK, n_grp)),
        ],
        out_specs=pl.BlockSpec((tq, tok_w + MEM_W), lambda i: (i, 0)),
        out_shape=jax.ShapeDtypeStruct((t, tok_w + MEM_W), BF16),
        compiler_params=_params(),
        name="gmlp_mixer",
    )(proj, proj, proj, kvm, w_s, b_s.T)


def _post_kernel(mix_ref, x_ref, wo_ref, g_ref, b_ref, wr_ref, br_ref,
                 x1_ref, xp_ref, route_ref, cnt_ref, carry_ref, *, tm, sub, alpha):
    i = pl.program_id(0)

    @pl.when(i == 0)
    def _():
        carry_ref[...] = jnp.zeros_like(carry_ref)

    carry = carry_ref[0:1, :]
    for r0 in range(0, tm, sub):
        carry = _post_rows(r0, sub, carry, mix_ref, x_ref, wo_ref, g_ref, b_ref, wr_ref,
                           br_ref, x1_ref, xp_ref, route_ref, alpha)
    carry_ref[...] = jnp.broadcast_to(carry, carry_ref.shape)
    cnt_ref[...] = jnp.broadcast_to(carry, cnt_ref.shape)


def _post_rows(r0, tm, carry, mix_ref, x_ref, wo_ref, g_ref, b_ref, wr_ref, br_ref,
               x1_ref, xp_ref, route_ref, alpha):
    rows = slice(r0, r0 + tm)
    y = jnp.dot(mix_ref[rows, :], wo_ref[...], preferred_element_type=F32)
    x1 = _layer_norm(alpha * x_ref[rows, :] + y, g_ref[...], b_ref[...])
    x1_ref[rows, :] = x1
    _pack_rows(xp_ref, x1, row0=r0)

    xh = x1.astype(BF16)
    xl = (x1 - xh.astype(F32)).astype(BF16)
    hh_hl = jnp.dot(xh, wr_ref[...], preferred_element_type=F32)
    lh = jnp.dot(xl, wr_ref[:, :LANES], preferred_element_type=F32)
    logits = (hh_hl[:, :LANES] + (hh_hl[:, LANES:] + lh)) + br_ref[...]

    lane = lax.broadcasted_iota(jnp.int32, (tm, LANES), 1)
    lane_f = lane.astype(F32)
    big = float(LANES)
    is_grp = lane < N_GROUPS
    gl = jnp.where(is_grp, logits, NEG_INF)
    gmax = jnp.max(gl, axis=-1, keepdims=True)
    g_sel = jnp.min(jnp.where(gl == gmax, lane_f, big), axis=-1, keepdims=True)
    p_grp = 1.0 / jnp.sum(jnp.where(is_grp, jnp.exp(gl - gmax), 0.0), axis=-1, keepdims=True)

    e_lo = N_GROUPS + g_sel * EXPERTS_PER_GROUP
    in_grp = (lane_f >= e_lo) & (lane_f < e_lo + EXPERTS_PER_GROUP)
    el = jnp.where(in_grp, logits, NEG_INF)
    m1 = jnp.max(el, axis=-1, keepdims=True)
    i1 = jnp.min(jnp.where(el == m1, lane_f, big), axis=-1, keepdims=True)
    el2 = jnp.where(lane_f == i1, NEG_INF, el)
    m2 = jnp.max(el2, axis=-1, keepdims=True)
    i2 = jnp.min(jnp.where(el2 == m2, lane_f, big), axis=-1, keepdims=True)
    z = jnp.sum(jnp.where(in_grp, jnp.exp(el - m1), 0.0), axis=-1, keepdims=True)
    tp1 = 1.0 / z
    tp2 = jnp.exp(m2 - m1) / z
    gate1 = p_grp * tp1 / (tp1 + tp2)
    gate2 = p_grp * tp2 / (tp1 + tp2)
    e1 = i1 - N_GROUPS
    e2 = i2 - N_GROUPS

    oh1 = lane_f == e1
    oh2 = lane_f == e2
    c = jnp.where(oh1 | oh2, 1.0, 0.0)
    r = lax.broadcasted_iota(jnp.int32, (tm, tm), 0)
    s = lax.broadcasted_iota(jnp.int32, (tm, tm), 1)
    tri = jnp.where(r > s, 1.0, 0.0).astype(BF16)
    prefix = jnp.dot(tri, c.astype(BF16), preferred_element_type=F32) + carry
    rank1 = jnp.sum(jnp.where(oh1, prefix, 0.0), axis=-1, keepdims=True)
    rank2 = jnp.sum(jnp.where(oh2, prefix, 0.0), axis=-1, keepdims=True)

    route = jnp.where(lane == 0, e1, 0.0)
    route = jnp.where(lane == 1, e2, route)
    route = jnp.where(lane == 2, rank1, route)
    route = jnp.where(lane == 3, rank2, route)
    route = jnp.where(lane == 4, gate1, route)
    route = jnp.where(lane == 5, gate2, route)
    route_ref[rows, :] = route
    return carry + jnp.sum(c, axis=0, keepdims=True)


def _post(mix, x2d, wo_bf, ln_g, ln_b, wr_hi_lo, b_r, *, tm, sub, alpha):
    t, d = x2d.shape
    kern = functools.partial(_post_kernel, tm=tm, sub=sub, alpha=alpha)
    return pl.pallas_call(
        kern,
        grid=(t // tm,),
        in_specs=[
            pl.BlockSpec((tm, d), lambda i: (i, 0)),
            pl.BlockSpec((tm, d), lambda i: (i, 0)),
            _resident((d, d)),
            _resident((1, d)), _resident((1, d)),
            _resident((d, 2 * LANES)), _resident((1, LANES)),
        ],
        out_specs=[
            pl.BlockSpec((tm, d), lambda i: (i, 0)),
            pl.BlockSpec((tm * (d // (2 * LANES)), LANES), lambda i: (i, 0)),
            pl.BlockSpec((tm, LANES), lambda i: (i, 0)),
            pl.BlockSpec((8, LANES), lambda i: (0, 0)),
        ],
        out_shape=[
            jax.ShapeDtypeStruct((t, d), F32),
            jax.ShapeDtypeStruct((t * (d // (2 * LANES)), LANES), U32),
            jax.ShapeDtypeStruct((t, LANES), F32),
            jax.ShapeDtypeStruct((8, LANES), F32),
        ],
        scratch_shapes=[pltpu.VMEM((8, LANES), F32)],
        compiler_params=_params(),
        name="post",
    )(mix, x2d, wo_bf, ln_g.reshape(1, d), ln_b.reshape(1, d), wr_hi_lo, b_r)


def _dispatch_kernel(real_end_ref, pad_end_ref, da_ref, db_ref, xp_ref, stale_hbm,
                     xs_hbm, inv_hbm, inv_ref, sem, sem_inv, *, tm, nc):
    del stale_hbm
    i = pl.program_id(0)
    unroll = 8

    @pl.when(i == 0)
    def _():
        def expert(e, carry):
            def clear(r, c):
                inv_ref[r] = 0
                return c
            lax.fori_loop(real_end_ref[e], pad_end_ref[e], clear, 0)
            return carry
        lax.fori_loop(0, real_end_ref.shape[0], expert, 0)

    def copy_group(g, carry):
        for u in range(unroll):
            j = g * unroll + u
            src = xp_ref.at[pl.ds(j * nc, nc)]
            for d_ref in (da_ref, db_ref):
                dst = xs_hbm.at[pl.ds(pl.multiple_of(d_ref[j] * nc, nc), nc)]
                pltpu.make_async_copy(src, dst, sem).start()
        return carry
    lax.fori_loop(0, tm // unroll, copy_group, 0)

    def inv_group(g, carry):
        for u in range(unroll):
            j = g * unroll + u
            for k, d_ref in enumerate((da_ref, db_ref)):
                inv_ref[d_ref[j]] = TOP_K * (i * tm + j) + k
        return carry
    lax.fori_loop(0, tm // unroll, inv_group, 0)
    for _ in range(TOP_K):
        pltpu.make_async_copy(xp_ref, xs_hbm.at[pl.ds(0, tm * nc)], sem).wait()

    @pl.when(i == pl.num_programs(0) - 1)
    def _():
        copy = pltpu.make_async_copy(inv_ref, inv_hbm, sem_inv)
        copy.start()
        copy.wait()


def _dispatch(xp, dest_a, dest_b, real_end, pad_end, stale, *, tm):
    nc = xp.shape[0] // dest_a.shape[0]
    t = dest_a.shape[0]
    n_rows = stale.shape[0] // nc
    smem = pl.BlockSpec((tm,), lambda i, *_: (i,), memory_space=pltpu.SMEM)
    any_spec = pl.BlockSpec(memory_space=pl.ANY)
    grid_spec = pltpu.PrefetchScalarGridSpec(
        num_scalar_prefetch=2,
        grid=(t // tm,),
        in_specs=[smem, smem, pl.BlockSpec((tm * nc, LANES), lambda i, *_: (i, 0)), any_spec],
        out_specs=[any_spec, any_spec],
        scratch_shapes=[pltpu.SMEM((n_rows,), jnp.int32), pltpu.SemaphoreType.DMA(()),
                        pltpu.SemaphoreType.DMA(())],
    )
    return pl.pallas_call(
        functools.partial(_dispatch_kernel, tm=tm, nc=nc),
        grid_spec=grid_spec,
        out_shape=[jax.ShapeDtypeStruct(stale.shape, stale.dtype),
                   jax.ShapeDtypeStruct((n_rows,), jnp.int32)],
        input_output_aliases={5: 0},
        compiler_params=_params(),
        name="dispatch",
    )(real_end, pad_end, dest_a, dest_b, xp, stale)


def _moe_kernel(blk_e_ref, nxt_e_ref, nvalid_ref, n_used_ref, dst_ref,
                xs_ref, wg_hbm, wu_hbm, wd_hbm, y_hbm,
                ybuf, wg_st, wu_st, wd_st, wg_bf, wu_bf, wd_bf, sem_s, sem_w,
                *, rb, nc, nblk, layer):
    s = pl.program_id(0)
    n_used = n_used_ref[0]
    slot = lax.rem(s, 2)
    other = 1 - slot
    unroll = 8

    def nvalid(b):
        return jnp.where((b >= 0) & (b < nblk), nvalid_ref[jnp.clip(b, 0, nblk - 1)], 0)

    def scatter_row(j, src_slot):
        dst = y_hbm.at[pl.ds(pl.multiple_of(dst_ref[j], nc), nc)]
        pltpu.make_async_copy(ybuf.at[src_slot, pl.ds(j * nc, nc)], dst,
                              sem_s.at[src_slot]).start()

    def for_rows(n, row_fn):
        def group(g, carry):
            for u in range(unroll):
                row_fn(g * unroll + u)
            return carry
        lax.fori_loop(0, n // unroll, group, 0)

        def single(j, carry):
            row_fn(j)
            return carry
        lax.fori_loop((n // unroll) * unroll, n, single, 0)

    def wait_rows(n, make_copy):
        p = rb
        while p >= 1:
            @pl.when((n & p) != 0)
            def _(p=p):
                make_copy(p).wait()
            p //= 2

    def wait_scatter(n, sl):
        wait_rows(n, lambda p: pltpu.make_async_copy(
            ybuf.at[sl, pl.ds(0, p * nc)], y_hbm.at[pl.ds(0, p * nc)], sem_s.at[sl]))

    weight_copies = ((wg_hbm, wg_st, 0), (wu_hbm, wu_st, 1), (wd_hbm, wd_st, 2))

    def fetch_weights(e):
        for hbm, stage, sem_i in weight_copies:
            pltpu.make_async_copy(hbm.at[layer, e], stage, sem_w.at[sem_i]).start()

    @pl.when(s == 0)
    def _():
        fetch_weights(blk_e_ref[0])

    b_cur = jnp.minimum(s, nblk - 1)
    e_cur = blk_e_ref[b_cur]
    new_expert = (s == 0) | (e_cur != blk_e_ref[jnp.maximum(b_cur - 1, 0)])

    def expert_block(first):
        wait_scatter(nvalid(s - 2), slot)
        xb = _unpack_rows(xs_ref, nc).astype(BF16)
        if first:
            for hbm, stage, sem_i in weight_copies:
                pltpu.make_async_copy(hbm.at[layer, 0], stage, sem_w.at[sem_i]).wait()
            wg_bf[...] = wg_st[...].astype(BF16)
        hg = jnp.dot(xb, wg_bf[...], preferred_element_type=F32)
        if first:
            wu_bf[...] = wu_st[...].astype(BF16)
        hu = jnp.dot(xb, wu_bf[...], preferred_element_type=F32)
        if first:
            wd_bf[...] = wd_st[...].astype(BF16)
        h = (jax.nn.silu(hg) * hu).astype(BF16)
        _pack_rows(ybuf, jnp.dot(h, wd_bf[...], preferred_element_type=F32), slot)
        if first:
            e_next = nxt_e_ref[b_cur]

            @pl.when(e_next != e_cur)
            def _():
                fetch_weights(e_next)
        for_rows(nvalid(s), lambda j: scatter_row(j, slot))

    @pl.when((s < n_used) & new_expert)
    def _():
        expert_block(True)

    @pl.when((s < n_used) & jnp.logical_not(new_expert))
    def _():
        expert_block(False)

    @pl.when(s == n_used)
    def _():
        wait_scatter(nvalid(s - 2), slot)
        wait_scatter(nvalid(s - 1), other)


def _moe(xs, row_dst, blk_e, nxt_e, nvalid, n_used, w_gate, w_up, w_down, *, t, layer, rb):
    d_e, d = w_down.shape[-2:]
    n_chunks = d // (2 * LANES)
    nblk = row_dst.shape[0] // rb

    def blk(s, blk_e_ref, nxt_e_ref, nvalid_ref, n_used_ref):
        return jnp.minimum(s, jnp.maximum(n_used_ref[0] - 1, 0))

    any_spec = pl.BlockSpec(memory_space=pl.ANY)
    grid_spec = pltpu.PrefetchScalarGridSpec(
        num_scalar_prefetch=4,
        grid=(nblk + 1,),
        in_specs=[
            pl.BlockSpec((rb,), lambda s, *p: (blk(s, *p),), memory_space=pltpu.SMEM),
            pl.BlockSpec((rb * n_chunks, LANES), lambda s, *p: (blk(s, *p), 0)),
            any_spec, any_spec, any_spec,
        ],
        out_specs=any_spec,
        scratch_shapes=[pltpu.VMEM((2, rb * n_chunks, LANES), U32),
                        pltpu.VMEM((d, d_e), F32), pltpu.VMEM((d, d_e), F32),
                        pltpu.VMEM((d_e, d), F32),
                        pltpu.VMEM((d, d_e), BF16), pltpu.VMEM((d, d_e), BF16),
                        pltpu.VMEM((d_e, d), BF16),
                        pltpu.SemaphoreType.DMA((2,)), pltpu.SemaphoreType.DMA((3,))],
    )
    return pl.pallas_call(
        functools.partial(_moe_kernel, rb=rb, nc=n_chunks, nblk=nblk, layer=layer),
        grid_spec=grid_spec,
        out_shape=jax.ShapeDtypeStruct((TOP_K * t * n_chunks, LANES), U32),
        compiler_params=_params(),
        name="moe",
    )(blk_e, nxt_e, nvalid, n_used, row_dst, xs, w_gate, w_up, w_down)


def _combine_kernel(x1_ref, ya_ref, yb_ref, route_ref, g_ref, b_ref, *rest,
                    nc, alpha, mode, tok_w):
    if mode == "attn":
        w_ref, x2_ref, o_ref = rest
    elif mode == "gmlp":
        w_ref, ng_ref, nb_ref, x2_ref, o_ref = rest
    else:
        (x2_ref,) = rest
    route = route_ref[...]
    y = _unpack_rows(ya_ref, nc) * route[:, 4:5] + _unpack_rows(yb_ref, nc) * route[:, 5:6]
    x2 = _layer_norm(alpha * x1_ref[...] + y, g_ref[...], b_ref[...])
    x2_ref[...] = x2
    if mode == "attn":
        o_ref[...] = jnp.dot(x2.astype(BF16), w_ref[...],
                             preferred_element_type=F32).astype(o_ref.dtype)
    elif mode == "gmlp":
        _gmlp_epilogue(jnp.dot(x2.astype(BF16), w_ref[...], preferred_element_type=F32),
                       ng_ref, nb_ref, o_ref, tok_w)


def _combine(y, x1, route, ln_g, ln_b, *, tm, alpha, w_bf=None, gmlp_norm=None):
    t, d = x1.shape
    n_chunks = d // (2 * LANES)
    mode = "none" if w_bf is None else ("attn" if gmlp_norm is None else "gmlp")
    n_steps = t // tm

    in_specs = [pl.BlockSpec((tm, d), lambda i: (i, 0)),
                pl.BlockSpec((tm * n_chunks, LANES), lambda i: (i, 0)),
                pl.BlockSpec((tm * n_chunks, LANES), lambda i: (n_steps + i, 0)),
                pl.BlockSpec((tm, LANES), lambda i: (i, 0)),
                _resident((1, d)), _resident((1, d))]
    args = [x1, y, y, route, ln_g.reshape(1, d), ln_b.reshape(1, d)]
    out_specs = [pl.BlockSpec((tm, d), lambda i: (i, 0))]
    out_shape = [jax.ShapeDtypeStruct((t, d), F32)]
    tok_w = None
    if w_bf is not None:
        n = w_bf.shape[1]
        in_specs.append(_resident((d, n)))
        args.append(w_bf)
        out_specs.append(pl.BlockSpec((tm, n), lambda i: (i, 0)))
        out_shape.append(jax.ShapeDtypeStruct((t, n), BF16))
    if gmlp_norm is not None:
        tok_w = gmlp_norm[0].shape[-1]
        in_specs += [_resident((1, tok_w)), _resident((1, tok_w))]
        args += [gmlp_norm[0].reshape(1, tok_w), gmlp_norm[1].reshape(1, tok_w)]
    return pl.pallas_call(
        functools.partial(_combine_kernel, nc=n_chunks, alpha=alpha, mode=mode, tok_w=tok_w),
        grid=(n_steps,),
        in_specs=in_specs,
        out_specs=out_specs,
        out_shape=out_shape,
        compiler_params=_params(),
        name="combine_" + mode,
    )(*args)


def _split_bf16(w):
    hi = w.astype(BF16)
    lo = (w - hi.astype(F32)).astype(BF16)
    return hi, lo


def kernel(x, mem, a_w_in, a_sinks, b_w_in, b_norm_g, b_norm_b, b_w_spatial, b_b_spatial,
           w_mem_kv, w_out, ln_g, ln_b, w_router_group, b_router_group,
           w_router_expert, b_router_expert, w_gate, w_up, w_down):
    bsz, seq, d = x.shape
    mem_len = mem.shape[1]
    depth = w_out.shape[0]
    t = bsz * seq
    alpha = (2 * depth) ** 0.25
    rb = ROW_BLOCK
    n_rows = -(-(t * TOP_K + N_EXPERTS * (rb - 1)) // rb) * rb
    nblk = n_rows // rb

    def in_proj_weights(i):
        if i % 2 == 0:
            return a_w_in[i // 2].astype(BF16), None
        return b_w_in[i // 2].astype(BF16), (b_norm_g[i // 2], b_norm_b[i // 2])

    x2d = x.reshape(t, d)
    mem2d = mem.reshape(bsz * mem_len, d)
    proj = _proj(x2d, in_proj_weights(0)[0], BF16, tm=512)
    xs = jnp.zeros((n_rows * (d // (2 * LANES)), LANES), U32)
    for i in range(depth):
        j = i // 2
        kvm = _proj(mem2d, w_mem_kv[i].astype(BF16), BF16, tm=min(256, bsz * mem_len))
        if i % 2 == 0:
            mix = _attn_mixer(proj, kvm, a_sinks[j], seq=seq, mem_len=mem_len, tq=256)
        else:
            mix = _gmlp_mixer(proj, kvm, b_w_spatial[j], b_b_spatial[j],
                              seq=seq, mem_len=mem_len, tq=512)

        w_r = jnp.concatenate([w_router_group[i], w_router_expert[i]], axis=1)
        w_r = jnp.pad(w_r, ((0, 0), (0, LANES - w_r.shape[1])))
        b_r = jnp.concatenate([b_router_group[i], b_router_expert[i]])
        b_r = jnp.pad(b_r, (0, LANES - b_r.shape[0])).reshape(1, LANES)
        wr_hi_lo = jnp.concatenate(_split_bf16(w_r), axis=1)
        x1, x1p, route, cnt = _post(mix, x2d, w_out[i].astype(BF16), ln_g[i, 0], ln_b[i, 0],
                                    wr_hi_lo, b_r, tm=512, sub=256, alpha=alpha)

        counts = cnt[0, :N_EXPERTS].astype(jnp.int32)
        padded = (counts + rb - 1) // rb * rb
        pend = jnp.cumsum(padded)
        pstart = (pend - padded).astype(jnp.int32)
        n_used = (pend[-1] // rb).astype(jnp.int32).reshape(1)
        blk = jnp.arange(nblk, dtype=jnp.int32)
        blk_e = jnp.searchsorted(pend, jnp.minimum(blk, n_used[0] - 1) * rb,
                                 side='right').astype(jnp.int32)
        nxt_blk = pend[blk_e] // rb
        nxt_e = jnp.where(nxt_blk < n_used[0], blk_e[jnp.minimum(nxt_blk, nblk - 1)], blk_e)
        nvalid = jnp.where(blk < n_used[0],
                           jnp.clip(pstart[blk_e] + counts[blk_e] - blk * rb, 0, rb), 0)
        ri = route[:, :4].astype(jnp.int32)
        e_hot = ri[:, :TOP_K, None] == jnp.arange(N_EXPERTS, dtype=jnp.int32)
        dest = jnp.sum(jnp.where(e_hot, pstart, 0), axis=-1) + ri[:, TOP_K:]
        n_chunks = d // (2 * LANES)
        real_end = jnp.concatenate([pstart + counts, pend[-1:]]).astype(jnp.int32)
        pad_end = jnp.concatenate([pend, jnp.full((1,), n_rows)]).astype(jnp.int32)
        xs, row_asg = _dispatch(x1p, dest[:, 0], dest[:, 1], real_end, pad_end, xs, tm=512)
        row_dst = ((row_asg % TOP_K) * t + row_asg // TOP_K) * n_chunks

        y = _moe(xs, row_dst, blk_e, nxt_e.astype(jnp.int32), nvalid.astype(jnp.int32),
                 n_used, w_gate, w_up, w_down, t=t, layer=i, rb=rb)
        if i + 1 < depth:
            w_next, gmlp_norm = in_proj_weights(i + 1)
            x2d, proj = _combine(y, x1, route, ln_g[i, 1], ln_b[i, 1],
                                 tm=256, alpha=alpha, w_bf=w_next, gmlp_norm=gmlp_norm)
        else:
            (x2d,) = _combine(y, x1, route, ln_g[i, 1], ln_b[i, 1], tm=256, alpha=alpha)
    return x2d.reshape(bsz, seq, d)
```

```python
import functools
import math

import jax
import jax.numpy as jnp
import numpy as np
from jax import lax
from jax.experimental import pallas as pl
from jax.experimental.pallas import tpu as pltpu

MEM_HEADS = 4
MEM_HEAD_DIM = 128
MEM_W = MEM_HEADS * MEM_HEAD_DIM
HEAD_DIM = 64
N_KV_HEADS = 4
KV_W = N_KV_HEADS * HEAD_DIM
WINDOW = 128
CHUNK = 128
GM_GROUP_DIM = 128
N_GROUPS = 8
EXPERTS_PER_GROUP = 8
N_EXPERTS = N_GROUPS * EXPERTS_PER_GROUP
TOP_K = 2
LN_EPS = 1e-5
NEG_INF = -1e30

LANES = 128
VMEM_LIMIT_BYTES = 56 * 1024 * 1024

ROW_BLOCK = 256


class _Rows:
    proj = 512
    mem_proj = 256
    attn = 256
    gmlp = 512
    post = 512
    post_sub = 256
    dispatch = 512
    combine = 256


BF16 = jnp.bfloat16
F32 = jnp.float32
U32 = jnp.uint32


def _alibi_slopes(n):
    def pow2(m):
        start = 2.0 ** (-8.0 / m)
        return [start ** (i + 1) for i in range(m)]
    if math.log2(n).is_integer():
        s = pow2(n)
    else:
        c = 2 ** math.floor(math.log2(n))
        s = pow2(c) + pow2(2 * c)[0::2][: n - c]
    return [float(v) for v in np.asarray(s, dtype=np.float32)]


def _params(n_axes=1):
    return pltpu.CompilerParams(dimension_semantics=("arbitrary",) * n_axes,
                                vmem_limit_bytes=VMEM_LIMIT_BYTES)


def _resident(shape):
    nd = len(shape)
    return pl.BlockSpec(shape, lambda *_: (0,) * nd, pipeline_mode=pl.Buffered(1))


def _pack_rows(ref, val, *lead, row0=0):
    rows, d = val.shape
    nw = d // (2 * LANES)
    for c in range(nw):
        lo = val[:, c * LANES:(c + 1) * LANES].astype(BF16).astype(F32)
        hi = val[:, (c + nw) * LANES:(c + nw + 1) * LANES].astype(BF16).astype(F32)
        word = (lax.bitcast_convert_type(lo, U32) >> 16) | lax.bitcast_convert_type(hi, U32)
        ref[tuple(lead) + (pl.ds(row0 * nw + c, rows, stride=nw), slice(None))] = word


def _unpack_rows(ref, nw, *lead):
    rows = ref.shape[-2] // nw
    lo, hi = [], []
    for c in range(nw):
        word = ref[tuple(lead) + (pl.ds(c, rows, stride=nw), slice(None))]
        lo.append(lax.bitcast_convert_type(word << 16, F32))
        hi.append(lax.bitcast_convert_type(word & jnp.uint32(0xFFFF0000), F32))
    return jnp.concatenate(lo + hi, axis=-1)


def _layer_norm(x, g, b):
    mu = jnp.mean(x, axis=-1, keepdims=True)
    xc = x - mu
    var = jnp.mean(xc * xc, axis=-1, keepdims=True)
    return xc * lax.rsqrt(var + LN_EPS) * g + b


def _proj_kernel(x_ref, w_ref, o_ref):
    o_ref[...] = jnp.dot(x_ref[...].astype(BF16), w_ref[...],
                         preferred_element_type=F32).astype(o_ref.dtype)


def _proj(x2d, w_bf, out_dtype, tm):
    m, k = x2d.shape
    n = w_bf.shape[1]
    return pl.pallas_call(
        _proj_kernel,
        grid=(m // tm,),
        in_specs=[pl.BlockSpec((tm, k), lambda i: (i, 0)), _resident((k, n))],
        out_specs=pl.BlockSpec((tm, n), lambda i: (i, 0)),
        out_shape=jax.ShapeDtypeStruct((m, n), out_dtype),
        compiler_params=_params(),
        name="proj",
    )(x2d, w_bf)


def _gelu(x):
    return 0.5 * x * (1.0 + lax.erf(x * (2.0 ** -0.5)))


def _gmlp_epilogue(z, g_ref, b_ref, o_ref, tok_w):
    u = _gelu(z[:, :tok_w])
    v = _gelu(z[:, tok_w:2 * tok_w])
    v = _layer_norm(v, g_ref[...], b_ref[...])
    o_ref[:, :tok_w] = u.astype(o_ref.dtype)
    o_ref[:, tok_w:2 * tok_w] = v.astype(o_ref.dtype)
    o_ref[:, 2 * tok_w:] = z[:, 2 * tok_w:].astype(o_ref.dtype)


def _memory_attention(qm, kvm):
    outs = []
    for h in range(MEM_HEADS):
        q = qm[:, h * MEM_HEAD_DIM:(h + 1) * MEM_HEAD_DIM]
        k = kvm[:, h * MEM_HEAD_DIM:(h + 1) * MEM_HEAD_DIM]
        v = kvm[:, MEM_W + h * MEM_HEAD_DIM:MEM_W + (h + 1) * MEM_HEAD_DIM]
        s = lax.dot_general(q, k, (((1,), (1,)), ((), ())),
                            preferred_element_type=F32) * (MEM_HEAD_DIM ** -0.5)
        m = jnp.max(s, axis=-1, keepdims=True)
        p = jnp.exp(s - m)
        p = p * (1.0 / jnp.sum(p, axis=-1, keepdims=True))
        outs.append(jnp.dot(p.astype(BF16), v, preferred_element_type=F32))
    return jnp.concatenate(outs, axis=-1)


def _attn_kernel(sinks_ref, q_ref, kv_ref, kvp_ref, qm_ref, kvm_ref, o_ref, *,
                 tq, seq, slopes, tok_w):
    i = pl.program_id(0)
    n_sub = tq // WINDOW
    gqa = len(slopes) // N_KV_HEADS
    qi = lax.broadcasted_iota(jnp.int32, (WINDOW, WINDOW), 0)
    c = lax.broadcasted_iota(jnp.int32, (WINDOW, WINDOW), 1)
    own = c <= qi
    dist_f = jnp.where(own, qi - c, WINDOW + qi - c).astype(F32)
    nt = (((1,), (1,)), ((), ()))
    for sb in range(n_sub):
        r0 = sb * WINDOW
        if sb == 0:
            kv_prev = kvp_ref[...]
            reach = jnp.where(((i * tq) % seq) == 0, 0, WINDOW)
            valid = c <= qi + reach
        else:
            kv_prev = kv_ref[r0 - WINDOW:r0, :]
            valid = None
        kv_cur = kv_ref[r0:r0 + WINDOW, :]
        for kh in range(N_KV_HEADS):
            ks = slice(kh * HEAD_DIM, (kh + 1) * HEAD_DIM)
            vs = slice(KV_W + kh * HEAD_DIM, KV_W + (kh + 1) * HEAD_DIM)
            heads = range(kh * gqa, (kh + 1) * gqa)
            q = jnp.concatenate(
                [q_ref[r0:r0 + WINDOW, h * HEAD_DIM:(h + 1) * HEAD_DIM] for h in heads], axis=0)
            s_own = lax.dot_general(q, kv_cur[:, ks], nt, preferred_element_type=F32)
            s_prev = lax.dot_general(q, kv_prev[:, ks], nt, preferred_element_type=F32)
            p_own, p_prev = [], []
            for g, h in enumerate(heads):
                rows = slice(g * WINDOW, (g + 1) * WINDOW)
                logits = (jnp.where(own, s_own[rows], s_prev[rows]) * (HEAD_DIM ** -0.5)
                          - slopes[h] * dist_f)
                if valid is not None:
                    logits = jnp.where(valid, logits, NEG_INF)
                sink = sinks_ref[h]
                m = jnp.maximum(jnp.max(logits, axis=-1, keepdims=True), sink)
                p = jnp.exp(logits - m)
                probs = p * (1.0 / (jnp.sum(p, axis=-1, keepdims=True) + jnp.exp(sink - m)))
                p_own.append(jnp.where(own, probs, 0.0).astype(BF16))
                p_prev.append(jnp.where(own, 0.0, probs).astype(BF16))
            out = (jnp.dot(jnp.concatenate(p_own, axis=0), kv_cur[:, vs],
                           preferred_element_type=F32)
                   + jnp.dot(jnp.concatenate(p_prev, axis=0), kv_prev[:, vs],
                             preferred_element_type=F32))
            o_ref[r0:r0 + WINDOW, kh * gqa * HEAD_DIM:(kh + 1) * gqa * HEAD_DIM] = jnp.concatenate(
                [out[g * WINDOW:(g + 1) * WINDOW] for g in range(gqa)], axis=1).astype(o_ref.dtype)
    o_ref[:, tok_w:] = _memory_attention(qm_ref[...], kvm_ref[...]).astype(o_ref.dtype)


def _attn_mixer(proj, kvm, sinks, *, seq, mem_len, tq):
    t = proj.shape[0]
    tok_w = proj.shape[1] - 2 * KV_W - MEM_W
    n_heads = tok_w // HEAD_DIM
    assert tok_w % (2 * KV_W) == 0 and tok_w % MEM_W == 0 and seq % tq == 0
    kv_col = tok_w // (2 * KV_W)
    qm_col = (tok_w + 2 * KV_W) // MEM_W
    sub = tq // WINDOW
    kern = functools.partial(_attn_kernel, tq=tq, seq=seq, slopes=_alibi_slopes(n_heads),
                             tok_w=tok_w)
    return pl.pallas_call(
        kern,
        grid=(t // tq,),
        in_specs=[
            pl.BlockSpec(memory_space=pltpu.SMEM),
            pl.BlockSpec((tq, tok_w), lambda i: (i, 0)),
            pl.BlockSpec((tq, 2 * KV_W), lambda i: (i, kv_col)),
            pl.BlockSpec((WINDOW, 2 * KV_W), lambda i: (jnp.maximum(i * sub - 1, 0), kv_col)),
            pl.BlockSpec((tq, MEM_W), lambda i: (i, qm_col)),
            pl.BlockSpec((mem_len, 2 * MEM_W), lambda i: ((i * tq) // seq, 0)),
        ],
        out_specs=pl.BlockSpec((tq, tok_w + MEM_W), lambda i: (i, 0)),
        out_shape=jax.ShapeDtypeStruct((t, tok_w + MEM_W), BF16),
        compiler_params=_params(),
        name="attn_mixer",
    )(sinks, proj, proj, proj, proj, kvm)


def _gmlp_kernel(u_ref, v_ref, qm_ref, kvm_ref, ws_ref, bs_ref, o_ref, *, tq, tok_w):
    n_sub = tq // CHUNK
    n_grp = tok_w // GM_GROUP_DIM
    row = lax.broadcasted_iota(jnp.int32, (CHUNK, CHUNK), 0)
    col = lax.broadcasted_iota(jnp.int32, (CHUNK, CHUNK), 1)
    lower = row >= col
    for g in range(n_grp):
        w = jnp.where(lower, ws_ref[g], 0.0).astype(BF16)
        bias = bs_ref[:, g:g + 1]
        c0 = g * GM_GROUP_DIM
        for sb in range(n_sub):
            r0 = sb * CHUNK
            sv = jnp.dot(w, v_ref[r0:r0 + CHUNK, c0:c0 + GM_GROUP_DIM],
                         preferred_element_type=F32) + bias
            u = u_ref[r0:r0 + CHUNK, c0:c0 + GM_GROUP_DIM].astype(F32)
            o_ref[r0:r0 + CHUNK, c0:c0 + GM_GROUP_DIM] = (u * sv).astype(o_ref.dtype)
    o_ref[:, tok_w:] = _memory_attention(qm_ref[...], kvm_ref[...]).astype(o_ref.dtype)


def _gmlp_mixer(proj, kvm, w_s, b_s, *, seq, mem_len, tq):
    t = proj.shape[0]
    tok_w = (proj.shape[1] - MEM_W) // 2
    n_grp = tok_w // GM_GROUP_DIM
    assert (2 * tok_w) % MEM_W == 0 and seq % tq == 0
    kern = functools.partial(_gmlp_kernel, tq=tq, tok_w=tok_w)
    return pl.pallas_call(
        kern,
        grid=(t // tq,),
        in_specs=[
            pl.BlockSpec((tq, tok_w), lambda i: (i, 0)),
            pl.BlockSpec((tq, tok_w), lambda i: (i, 1)),
            pl.BlockSpec((tq, MEM_W), lambda i: (i, (2 * tok_w) // MEM_W)),
            pl.BlockSpec((mem_len, 2 * MEM_W), lambda i: ((i * tq) // seq, 0)),
            _resident((n_grp, CHUNK, CHUNK)),
            _resident((CHUNK, n_grp)),
        ],
        out_specs=pl.BlockSpec((tq, tok_w + MEM_W), lambda i: (i, 0)),
        out_shape=jax.ShapeDtypeStruct((t, tok_w + MEM_W), BF16),
        compiler_params=_params(),
        name="gmlp_mixer",
    )(proj, proj, proj, kvm, w_s, b_s.T)


def _post_kernel(mix_ref, x_ref, wo_ref, g_ref, b_ref, wr_ref, br_ref,
                 x1_ref, xp_ref, route_ref, cnt_ref, carry_ref, *, tm, sub, alpha):
    i = pl.program_id(0)

    @pl.when(i == 0)
    def _():
        carry_ref[...] = jnp.zeros_like(carry_ref)

    carry = carry_ref[0:1, :]
    for r0 in range(0, tm, sub):
        carry = _post_rows(r0, sub, carry, mix_ref, x_ref, wo_ref, g_ref, b_ref, wr_ref,
                           br_ref, x1_ref, xp_ref, route_ref, alpha)
    carry_ref[...] = jnp.broadcast_to(carry, carry_ref.shape)
    cnt_ref[...] = jnp.broadcast_to(carry, cnt_ref.shape)


def _post_rows(r0, tm, carry, mix_ref, x_ref, wo_ref, g_ref, b_ref, wr_ref, br_ref,
               x1_ref, xp_ref, route_ref, alpha):
    rows = slice(r0, r0 + tm)
    y = jnp.dot(mix_ref[rows, :], wo_ref[...], preferred_element_type=F32)
    x1 = _layer_norm(alpha * x_ref[rows, :] + y, g_ref[...], b_ref[...])
    x1_ref[rows, :] = x1
    _pack_rows(xp_ref, x1, row0=r0)

    xh = x1.astype(BF16)
    xl = (x1 - xh.astype(F32)).astype(BF16)
    hh_hl = jnp.dot(xh, wr_ref[...], preferred_element_type=F32)
    lh = jnp.dot(xl, wr_ref[:, :LANES], preferred_element_type=F32)
    logits = (hh_hl[:, :LANES] + (hh_hl[:, LANES:] + lh)) + br_ref[...]

    lane = lax.broadcasted_iota(jnp.int32, (tm, LANES), 1)
    lane_f = lane.astype(F32)
    big = float(LANES)
    is_grp = lane < N_GROUPS
    gl = jnp.where(is_grp, logits, NEG_INF)
    gmax = jnp.max(gl, axis=-1, keepdims=True)
    g_sel = jnp.min(jnp.where(gl == gmax, lane_f, big), axis=-1, keepdims=True)
    p_grp = 1.0 / jnp.sum(jnp.where(is_grp, jnp.exp(gl - gmax), 0.0), axis=-1, keepdims=True)

    e_lo = N_GROUPS + g_sel * EXPERTS_PER_GROUP
    in_grp = (lane_f >= e_lo) & (lane_f < e_lo + EXPERTS_PER_GROUP)
    el = jnp.where(in_grp, logits, NEG_INF)
    m1 = jnp.max(el, axis=-1, keepdims=True)
    i1 = jnp.min(jnp.where(el == m1, lane_f, big), axis=-1, keepdims=True)
    el2 = jnp.where(lane_f == i1, NEG_INF, el)
    m2 = jnp.max(el2, axis=-1, keepdims=True)
    i2 = jnp.min(jnp.where(el2 == m2, lane_f, big), axis=-1, keepdims=True)
    z = jnp.sum(jnp.where(in_grp, jnp.exp(el - m1), 0.0), axis=-1, keepdims=True)
    tp1 = 1.0 / z
    tp2 = jnp.exp(m2 - m1) / z
    gate1 = p_grp * tp1 / (tp1 + tp2)
    gate2 = p_grp * tp2 / (tp1 + tp2)
    e1 = i1 - N_GROUPS
    e2 = i2 - N_GROUPS

    oh1 = lane_f == e1
    oh2 = lane_f == e2
    c = jnp.where(oh1 | oh2, 1.0, 0.0)
    r = lax.broadcasted_iota(jnp.int32, (tm, tm), 0)
    s = lax.broadcasted_iota(jnp.int32, (tm, tm), 1)
    tri = jnp.where(r > s, 1.0, 0.0).astype(BF16)
    prefix = jnp.dot(tri, c.astype(BF16), preferred_element_type=F32) + carry
    rank1 = jnp.sum(jnp.where(oh1, prefix, 0.0), axis=-1, keepdims=True)
    rank2 = jnp.sum(jnp.where(oh2, prefix, 0.0), axis=-1, keepdims=True)

    route = jnp.where(lane == 0, e1, 0.0)
    route = jnp.where(lane == 1, e2, route)
    route = jnp.where(lane == 2, rank1, route)
    route = jnp.where(lane == 3, rank2, route)
    route = jnp.where(lane == 4, gate1, route)
    route = jnp.where(lane == 5, gate2, route)
    route_ref[rows, :] = route
    return carry + jnp.sum(c, axis=0, keepdims=True)


def _post(mix, x2d, wo_bf, ln_g, ln_b, wr_hi_lo, b_r, *, tm, sub, alpha):
    t, d = x2d.shape
    kern = functools.partial(_post_kernel, tm=tm, sub=sub, alpha=alpha)
    return pl.pallas_call(
        kern,
        grid=(t // tm,),
        in_specs=[
            pl.BlockSpec((tm, d), lambda i: (i, 0)),
            pl.BlockSpec((tm, d), lambda i: (i, 0)),
            _resident((d, d)),
            _resident((1, d)), _resident((1, d)),
            _resident((d, 2 * LANES)), _resident((1, LANES)),
        ],
        out_specs=[
            pl.BlockSpec((tm, d), lambda i: (i, 0)),
            pl.BlockSpec((tm * (d // (2 * LANES)), LANES), lambda i: (i, 0)),
            pl.BlockSpec((tm, LANES), lambda i: (i, 0)),
            pl.BlockSpec((8, LANES), lambda i: (0, 0)),
        ],
        out_shape=[
            jax.ShapeDtypeStruct((t, d), F32),
            jax.ShapeDtypeStruct((t * (d // (2 * LANES)), LANES), U32),
            jax.ShapeDtypeStruct((t, LANES), F32),
            jax.ShapeDtypeStruct((8, LANES), F32),
        ],
        scratch_shapes=[pltpu.VMEM((8, LANES), F32)],
        compiler_params=_params(),
        name="post",
    )(mix, x2d, wo_bf, ln_g.reshape(1, d), ln_b.reshape(1, d), wr_hi_lo, b_r)


def _dispatch_kernel(real_end_ref, pad_end_ref, da_ref, db_ref, xp_ref, stale_hbm,
                     xs_hbm, inv_hbm, inv_ref, sem, sem_inv, *, tm, nc):
    del stale_hbm
    i = pl.program_id(0)
    unroll = 8

    @pl.when(i == 0)
    def _():
        def expert(e, carry):
            def clear(r, c):
                inv_ref[r] = 0
                return c
            lax.fori_loop(real_end_ref[e], pad_end_ref[e], clear, 0)
            return carry
        lax.fori_loop(0, real_end_ref.shape[0], expert, 0)

    def copy_group(g, carry):
        for u in range(unroll):
            j = g * unroll + u
            src = xp_ref.at[pl.ds(j * nc, nc)]
            for d_ref in (da_ref, db_ref):
                dst = xs_hbm.at[pl.ds(pl.multiple_of(d_ref[j] * nc, nc), nc)]
                pltpu.make_async_copy(src, dst, sem).start()
        return carry
    lax.fori_loop(0, tm // unroll, copy_group, 0)

    def inv_group(g, carry):
        for u in range(unroll):
            j = g * unroll + u
            for k, d_ref in enumerate((da_ref, db_ref)):
                inv_ref[d_ref[j]] = TOP_K * (i * tm + j) + k
        return carry
    lax.fori_loop(0, tm // unroll, inv_group, 0)
    for _ in range(TOP_K):
        pltpu.make_async_copy(xp_ref, xs_hbm.at[pl.ds(0, tm * nc)], sem).wait()

    @pl.when(i == pl.num_programs(0) - 1)
    def _():
        copy = pltpu.make_async_copy(inv_ref, inv_hbm, sem_inv)
        copy.start()
        copy.wait()


def _dispatch(xp, dest_a, dest_b, real_end, pad_end, stale, *, tm):
    nc = xp.shape[0] // dest_a.shape[0]
    t = dest_a.shape[0]
    n_rows = stale.shape[0] // nc
    smem = pl.BlockSpec((tm,), lambda i, *_: (i,), memory_space=pltpu.SMEM)
    any_spec = pl.BlockSpec(memory_space=pl.ANY)
    grid_spec = pltpu.PrefetchScalarGridSpec(
        num_scalar_prefetch=2,
        grid=(t // tm,),
        in_specs=[smem, smem, pl.BlockSpec((tm * nc, LANES), lambda i, *_: (i, 0)), any_spec],
        out_specs=[any_spec, any_spec],
        scratch_shapes=[pltpu.SMEM((n_rows,), jnp.int32), pltpu.SemaphoreType.DMA(()),
                        pltpu.SemaphoreType.DMA(())],
    )
    return pl.pallas_call(
        functools.partial(_dispatch_kernel, tm=tm, nc=nc),
        grid_spec=grid_spec,
        out_shape=[jax.ShapeDtypeStruct(stale.shape, stale.dtype),
                   jax.ShapeDtypeStruct((n_rows,), jnp.int32)],
        input_output_aliases={5: 0},
        compiler_params=_params(),
        name="dispatch",
    )(real_end, pad_end, dest_a, dest_b, xp, stale)


def _moe_kernel(blk_e_ref, nxt_e_ref, nvalid_ref, n_used_ref, dst_ref,
                xs_ref, wg_hbm, wu_hbm, wd_hbm, y_hbm,
                ybuf, wg_st, wu_st, wd_st, wg_bf, wu_bf, wd_bf, sem_s, sem_w,
                *, rb, nc, nblk, layer):
    s = pl.program_id(0)
    n_used = n_used_ref[0]
    slot = lax.rem(s, 2)
    other = 1 - slot
    unroll = 8

    def nvalid(b):
        return jnp.where((b >= 0) & (b < nblk), nvalid_ref[jnp.clip(b, 0, nblk - 1)], 0)

    def scatter_row(j, src_slot):
        dst = y_hbm.at[pl.ds(pl.multiple_of(dst_ref[j], nc), nc)]
        pltpu.make_async_copy(ybuf.at[src_slot, pl.ds(j * nc, nc)], dst,
                              sem_s.at[src_slot]).start()

    def for_rows(n, row_fn):
        def group(g, carry):
            for u in range(unroll):
                row_fn(g * unroll + u)
            return carry
        lax.fori_loop(0, n // unroll, group, 0)

        def single(j, carry):
            row_fn(j)
            return carry
        lax.fori_loop((n // unroll) * unroll, n, single, 0)

    def wait_rows(n, make_copy):
        p = rb
        while p >= 1:
            @pl.when((n & p) != 0)
            def _(p=p):
                make_copy(p).wait()
            p //= 2

    def wait_scatter(n, sl):
        wait_rows(n, lambda p: pltpu.make_async_copy(
            ybuf.at[sl, pl.ds(0, p * nc)], y_hbm.at[pl.ds(0, p * nc)], sem_s.at[sl]))

    weight_copies = ((wg_hbm, wg_st, 0), (wu_hbm, wu_st, 1), (wd_hbm, wd_st, 2))

    def fetch_weights(e):
        for hbm, stage, sem_i in weight_copies:
            pltpu.make_async_copy(hbm.at[layer, e], stage, sem_w.at[sem_i]).start()

    @pl.when(s == 0)
    def _():
        fetch_weights(blk_e_ref[0])

    b_cur = jnp.minimum(s, nblk - 1)
    e_cur = blk_e_ref[b_cur]
    new_expert = (s == 0) | (e_cur != blk_e_ref[jnp.maximum(b_cur - 1, 0)])

    def expert_block(first):
        wait_scatter(nvalid(s - 2), slot)
        xb = _unpack_rows(xs_ref, nc).astype(BF16)
        if first:
            for hbm, stage, sem_i in weight_copies:
                pltpu.make_async_copy(hbm.at[layer, 0], stage, sem_w.at[sem_i]).wait()
            wg_bf[...] = wg_st[...].astype(BF16)
        hg = jnp.dot(xb, wg_bf[...], preferred_element_type=F32)
        if first:
            wu_bf[...] = wu_st[...].astype(BF16)
        hu = jnp.dot(xb, wu_bf[...], preferred_element_type=F32)
        if first:
            wd_bf[...] = wd_st[...].astype(BF16)
        h = (jax.nn.silu(hg) * hu).astype(BF16)
        _pack_rows(ybuf, jnp.dot(h, wd_bf[...], preferred_element_type=F32), slot)
        if first:
            e_next = nxt_e_ref[b_cur]

            @pl.when(e_next != e_cur)
            def _():
                fetch_weights(e_next)
        for_rows(nvalid(s), lambda j: scatter_row(j, slot))

    @pl.when((s < n_used) & new_expert)
    def _():
        expert_block(True)

    @pl.when((s < n_used) & jnp.logical_not(new_expert))
    def _():
        expert_block(False)

    @pl.when(s == n_used)
    def _():
        wait_scatter(nvalid(s - 2), slot)
        wait_scatter(nvalid(s - 1), other)


def _moe(xs, row_dst, blk_e, nxt_e, nvalid, n_used, w_gate, w_up, w_down, *, t, layer, rb):
    d_e, d = w_down.shape[-2:]
    n_chunks = d // (2 * LANES)
    nblk = row_dst.shape[0] // rb

    def blk(s, blk_e_ref, nxt_e_ref, nvalid_ref, n_used_ref):
        return jnp.minimum(s, jnp.maximum(n_used_ref[0] - 1, 0))

    any_spec = pl.BlockSpec(memory_space=pl.ANY)
    grid_spec = pltpu.PrefetchScalarGridSpec(
        num_scalar_prefetch=4,
        grid=(nblk + 1,),
        in_specs=[
            pl.BlockSpec((rb,), lambda s, *p: (blk(s, *p),), memory_space=pltpu.SMEM),
            pl.BlockSpec((rb * n_chunks, LANES), lambda s, *p: (blk(s, *p), 0)),
            any_spec, any_spec, any_spec,
        ],
        out_specs=any_spec,
        scratch_shapes=[pltpu.VMEM((2, rb * n_chunks, LANES), U32),
                        pltpu.VMEM((d, d_e), F32), pltpu.VMEM((d, d_e), F32),
                        pltpu.VMEM((d_e, d), F32),
                        pltpu.VMEM((d, d_e), BF16), pltpu.VMEM((d, d_e), BF16),
                        pltpu.VMEM((d_e, d), BF16),
                        pltpu.SemaphoreType.DMA((2,)), pltpu.SemaphoreType.DMA((3,))],
    )
    return pl.pallas_call(
        functools.partial(_moe_kernel, rb=rb, nc=n_chunks, nblk=nblk, layer=layer),
        grid_spec=grid_spec,
        out_shape=jax.ShapeDtypeStruct((TOP_K * t * n_chunks, LANES), U32),
        compiler_params=_params(),
        name="moe",
    )(blk_e, nxt_e, nvalid, n_used, row_dst, xs, w_gate, w_up, w_down)


def _combine_kernel(x1_ref, ya_ref, yb_ref, route_ref, g_ref, b_ref, *rest,
                    nc, alpha, mode, tok_w):
    if mode == "attn":
        w_ref, x2_ref, o_ref = rest
    elif mode == "gmlp":
        w_ref, ng_ref, nb_ref, x2_ref, o_ref = rest
    else:
        (x2_ref,) = rest
    route = route_ref[...]
    y = _unpack_rows(ya_ref, nc) * route[:, 4:5] + _unpack_rows(yb_ref, nc) * route[:, 5:6]
    x2 = _layer_norm(alpha * x1_ref[...] + y, g_ref[...], b_ref[...])
    x2_ref[...] = x2
    if mode == "attn":
        o_ref[...] = jnp.dot(x2.astype(BF16), w_ref[...],
                             preferred_element_type=F32).astype(o_ref.dtype)
    elif mode == "gmlp":
        _gmlp_epilogue(jnp.dot(x2.astype(BF16), w_ref[...], preferred_element_type=F32),
                       ng_ref, nb_ref, o_ref, tok_w)


def _combine(y, x1, route, ln_g, ln_b, *, tm, alpha, w_bf=None, gmlp_norm=None):
    t, d = x1.shape
    n_chunks = d // (2 * LANES)
    mode = "none" if w_bf is None else ("attn" if gmlp_norm is None else "gmlp")
    n_steps = t // tm

    in_specs = [pl.BlockSpec((tm, d), lambda i: (i, 0)),
                pl.BlockSpec((tm * n_chunks, LANES), lambda i: (i, 0)),
                pl.BlockSpec((tm * n_chunks, LANES), lambda i: (n_steps + i, 0)),
                pl.BlockSpec((tm, LANES), lambda i: (i, 0)),
                _resident((1, d)), _resident((1, d))]
    args = [x1, y, y, route, ln_g.reshape(1, d), ln_b.reshape(1, d)]
    out_specs = [pl.BlockSpec((tm, d), lambda i: (i, 0))]
    out_shape = [jax.ShapeDtypeStruct((t, d), F32)]
    tok_w = None
    if w_bf is not None:
        n = w_bf.shape[1]
        in_specs.append(_resident((d, n)))
        args.append(w_bf)
        out_specs.append(pl.BlockSpec((tm, n), lambda i: (i, 0)))
        out_shape.append(jax.ShapeDtypeStruct((t, n), BF16))
    if gmlp_norm is not None:
        tok_w = gmlp_norm[0].shape[-1]
        in_specs += [_resident((1, tok_w)), _resident((1, tok_w))]
        args += [gmlp_norm[0].reshape(1, tok_w), gmlp_norm[1].reshape(1, tok_w)]
    return pl.pallas_call(
        functools.partial(_combine_kernel, nc=n_chunks, alpha=alpha, mode=mode, tok_w=tok_w),
        grid=(n_steps,),
        in_specs=in_specs,
        out_specs=out_specs,
        out_shape=out_shape,
        compiler_params=_params(),
        name="combine_" + mode,
    )(*args)


def _split_bf16(w):
    hi = w.astype(BF16)
    lo = (w - hi.astype(F32)).astype(BF16)
    return hi, lo


def kernel(x, mem, a_w_in, a_sinks, b_w_in, b_norm_g, b_norm_b, b_w_spatial, b_b_spatial,
           w_mem_kv, w_out, ln_g, ln_b, w_router_group, b_router_group,
           w_router_expert, b_router_expert, w_gate, w_up, w_down):
    bsz, seq, d = x.shape
    mem_len = mem.shape[1]
    depth = w_out.shape[0]
    t = bsz * seq
    alpha = (2 * depth) ** 0.25
    rb = ROW_BLOCK
    n_rows = -(-(t * TOP_K + N_EXPERTS * (rb - 1)) // rb) * rb
    nblk = n_rows // rb

    def in_proj_weights(i):
        if i % 2 == 0:
            return a_w_in[i // 2].astype(BF16), None
        return b_w_in[i // 2].astype(BF16), (b_norm_g[i // 2], b_norm_b[i // 2])

    x2d = x.reshape(t, d)
    mem2d = mem.reshape(bsz * mem_len, d)
    proj = _proj(x2d, in_proj_weights(0)[0], BF16, tm=_Rows.proj)
    xs = jnp.zeros((n_rows * (d // (2 * LANES)), LANES), U32)
    for i in range(depth):
        j = i // 2
        kvm = _proj(mem2d, w_mem_kv[i].astype(BF16), BF16,
                    tm=min(_Rows.mem_proj, bsz * mem_len))
        if i % 2 == 0:
            mix = _attn_mixer(proj, kvm, a_sinks[j], seq=seq, mem_len=mem_len,
                              tq=_Rows.attn)
        else:
            mix = _gmlp_mixer(proj, kvm, b_w_spatial[j], b_b_spatial[j],
                              seq=seq, mem_len=mem_len, tq=_Rows.gmlp)

        w_r = jnp.concatenate([w_router_group[i], w_router_expert[i]], axis=1)
        w_r = jnp.pad(w_r, ((0, 0), (0, LANES - w_r.shape[1])))
        b_r = jnp.concatenate([b_router_group[i], b_router_expert[i]])
        b_r = jnp.pad(b_r, (0, LANES - b_r.shape[0])).reshape(1, LANES)
        wr_hi_lo = jnp.concatenate(_split_bf16(w_r), axis=1)
        x1, x1p, route, cnt = _post(mix, x2d, w_out[i].astype(BF16), ln_g[i, 0], ln_b[i, 0],
                                    wr_hi_lo, b_r, tm=_Rows.post, sub=_Rows.post_sub,
                                    alpha=alpha)

        counts = cnt[0, :N_EXPERTS].astype(jnp.int32)
        padded = (counts + rb - 1) // rb * rb
        pend = jnp.cumsum(padded)
        pstart = (pend - padded).astype(jnp.int32)
        n_used = (pend[-1] // rb).astype(jnp.int32).reshape(1)
        blk = jnp.arange(nblk, dtype=jnp.int32)

        def expert_of_block(b):
            row = jnp.minimum(b, n_used[0] - 1) * rb
            return jnp.sum((pend[None, :] <= row[:, None]).astype(jnp.int32), axis=1)

        blk_e = expert_of_block(blk)
        e_sel = blk_e[:, None] == jnp.arange(N_EXPERTS, dtype=jnp.int32)
        seg_end = jnp.sum(jnp.where(e_sel, pend, 0), axis=1)
        real_rows_end = jnp.sum(jnp.where(e_sel, pstart + counts, 0), axis=1)
        nxt_blk = seg_end // rb
        nxt_e = jnp.where(nxt_blk < n_used[0], expert_of_block(nxt_blk), blk_e)
        nvalid = jnp.where(blk < n_used[0], jnp.clip(real_rows_end - blk * rb, 0, rb), 0)
        ri = route[:, :4].astype(jnp.int32)
        e_hot = ri[:, :TOP_K, None] == jnp.arange(N_EXPERTS, dtype=jnp.int32)
        dest = jnp.sum(jnp.where(e_hot, pstart, 0), axis=-1) + ri[:, TOP_K:]
        n_chunks = d // (2 * LANES)
        real_end = jnp.concatenate([pstart + counts, pend[-1:]]).astype(jnp.int32)
        pad_end = jnp.concatenate([pend, jnp.full((1,), n_rows)]).astype(jnp.int32)
        xs, row_asg = _dispatch(x1p, dest[:, 0], dest[:, 1], real_end, pad_end, xs,
                                tm=_Rows.dispatch)
        row_dst = ((row_asg % TOP_K) * t + row_asg // TOP_K) * n_chunks

        y = _moe(xs, row_dst, blk_e, nxt_e.astype(jnp.int32), nvalid.astype(jnp.int32),
                 n_used, w_gate, w_up, w_down, t=t, layer=i, rb=rb)
        if i + 1 < depth:
            w_next, gmlp_norm = in_proj_weights(i + 1)
            x2d, proj = _combine(y, x1, route, ln_g[i, 1], ln_b[i, 1],
                                 tm=_Rows.combine, alpha=alpha, w_bf=w_next,
                                 gmlp_norm=gmlp_norm)
        else:
            (x2d,) = _combine(y, x1, route, ln_g[i, 1], ln_b[i, 1], tm=_Rows.combine,
                              alpha=alpha)
    return x2d.reshape(bsz, seq, d)
```

```python
import functools
import math

import jax
import jax.numpy as jnp
import numpy as np
from jax import lax
from jax.experimental import pallas as pl
from jax.experimental.pallas import tpu as pltpu

MEM_HEADS = 4
MEM_HEAD_DIM = 128
MEM_W = MEM_HEADS * MEM_HEAD_DIM
HEAD_DIM = 64
N_KV_HEADS = 4
KV_W = N_KV_HEADS * HEAD_DIM
WINDOW = 128
CHUNK = 128
GM_GROUP_DIM = 128
N_GROUPS = 8
EXPERTS_PER_GROUP = 8
N_EXPERTS = N_GROUPS * EXPERTS_PER_GROUP
TOP_K = 2
LN_EPS = 1e-5
NEG_INF = -1e30

LANES = 128
VMEM_LIMIT_BYTES = 56 * 1024 * 1024

ROW_BLOCK = 256


class _Rows:
    proj = 512
    mem_proj = 256
    attn = 256
    gmlp = 512
    post = 512
    post_sub = 256
    dispatch = 512
    combine = 256


BF16 = jnp.bfloat16
F32 = jnp.float32
U32 = jnp.uint32


def _alibi_slopes(n):
    def pow2(m):
        start = 2.0 ** (-8.0 / m)
        return [start ** (i + 1) for i in range(m)]
    if math.log2(n).is_integer():
        s = pow2(n)
    else:
        c = 2 ** math.floor(math.log2(n))
        s = pow2(c) + pow2(2 * c)[0::2][: n - c]
    return [float(v) for v in np.asarray(s, dtype=np.float32)]


def _params(n_axes=1):
    return pltpu.CompilerParams(dimension_semantics=("arbitrary",) * n_axes,
                                vmem_limit_bytes=VMEM_LIMIT_BYTES)


def _resident(shape):
    nd = len(shape)
    return pl.BlockSpec(shape, lambda *_: (0,) * nd, pipeline_mode=pl.Buffered(1))


def _pack_rows(ref, val, *lead, row0=0):
    rows, d = val.shape
    nw = d // (2 * LANES)
    for c in range(nw):
        lo = val[:, c * LANES:(c + 1) * LANES].astype(BF16).astype(F32)
        hi = val[:, (c + nw) * LANES:(c + nw + 1) * LANES].astype(BF16).astype(F32)
        word = (lax.bitcast_convert_type(lo, U32) >> 16) | lax.bitcast_convert_type(hi, U32)
        ref[tuple(lead) + (pl.ds(row0 * nw + c, rows, stride=nw), slice(None))] = word


def _unpack_rows(ref, nw, *lead):
    rows = ref.shape[-2] // nw
    lo, hi = [], []
    for c in range(nw):
        word = ref[tuple(lead) + (pl.ds(c, rows, stride=nw), slice(None))]
        lo.append(lax.bitcast_convert_type(word << 16, F32))
        hi.append(lax.bitcast_convert_type(word & jnp.uint32(0xFFFF0000), F32))
    return jnp.concatenate(lo + hi, axis=-1)


def _layer_norm(x, g, b):
    mu = jnp.mean(x, axis=-1, keepdims=True)
    xc = x - mu
    var = jnp.mean(xc * xc, axis=-1, keepdims=True)
    return xc * lax.rsqrt(var + LN_EPS) * g + b


def _proj_kernel(x_ref, w_ref, o_ref):
    o_ref[...] = jnp.dot(x_ref[...].astype(BF16), w_ref[...],
                         preferred_element_type=F32).astype(o_ref.dtype)


def _proj(x2d, w_bf, out_dtype, tm):
    m, k = x2d.shape
    n = w_bf.shape[1]
    return pl.pallas_call(
        _proj_kernel,
        grid=(m // tm,),
        in_specs=[pl.BlockSpec((tm, k), lambda i: (i, 0)), _resident((k, n))],
        out_specs=pl.BlockSpec((tm, n), lambda i: (i, 0)),
        out_shape=jax.ShapeDtypeStruct((m, n), out_dtype),
        compiler_params=_params(),
        name="proj",
    )(x2d, w_bf)


def _gelu(x):
    return 0.5 * x * (1.0 + lax.erf(x * (2.0 ** -0.5)))


def _gmlp_epilogue(z, g_ref, b_ref, o_ref, tok_w):
    u = _gelu(z[:, :tok_w])
    v = _gelu(z[:, tok_w:2 * tok_w])
    v = _layer_norm(v, g_ref[...], b_ref[...])
    o_ref[:, :tok_w] = u.astype(o_ref.dtype)
    o_ref[:, tok_w:2 * tok_w] = v.astype(o_ref.dtype)
    o_ref[:, 2 * tok_w:] = z[:, 2 * tok_w:].astype(o_ref.dtype)


def _memory_attention(qm, kvm):
    outs = []
    for h in range(MEM_HEADS):
        q = qm[:, h * MEM_HEAD_DIM:(h + 1) * MEM_HEAD_DIM]
        k = kvm[:, h * MEM_HEAD_DIM:(h + 1) * MEM_HEAD_DIM]
        v = kvm[:, MEM_W + h * MEM_HEAD_DIM:MEM_W + (h + 1) * MEM_HEAD_DIM]
        s = lax.dot_general(q, k, (((1,), (1,)), ((), ())),
                            preferred_element_type=F32) * (MEM_HEAD_DIM ** -0.5)
        m = jnp.max(s, axis=-1, keepdims=True)
        p = jnp.exp(s - m)
        p = p * (1.0 / jnp.sum(p, axis=-1, keepdims=True))
        outs.append(jnp.dot(p.astype(BF16), v, preferred_element_type=F32))
    return jnp.concatenate(outs, axis=-1)


def _attn_kernel(sinks_ref, q_ref, kv_ref, kvp_ref, qm_ref, kvm_ref, o_ref, *,
                 tq, seq, slopes, tok_w):
    i = pl.program_id(0)
    n_sub = tq // WINDOW
    gqa = len(slopes) // N_KV_HEADS
    qi = lax.broadcasted_iota(jnp.int32, (WINDOW, WINDOW), 0)
    c = lax.broadcasted_iota(jnp.int32, (WINDOW, WINDOW), 1)
    own = c <= qi
    dist_f = jnp.where(own, qi - c, WINDOW + qi - c).astype(F32)
    nt = (((1,), (1,)), ((), ()))
    for sb in range(n_sub):
        r0 = sb * WINDOW
        if sb == 0:
            kv_prev = kvp_ref[...]
            reach = jnp.where(((i * tq) % seq) == 0, 0, WINDOW)
            valid = c <= qi + reach
        else:
            kv_prev = kv_ref[r0 - WINDOW:r0, :]
            valid = None
        kv_cur = kv_ref[r0:r0 + WINDOW, :]
        for kh in range(N_KV_HEADS):
            ks = slice(kh * HEAD_DIM, (kh + 1) * HEAD_DIM)
            vs = slice(KV_W + kh * HEAD_DIM, KV_W + (kh + 1) * HEAD_DIM)
            heads = range(kh * gqa, (kh + 1) * gqa)
            q = jnp.concatenate(
                [q_ref[r0:r0 + WINDOW, h * HEAD_DIM:(h + 1) * HEAD_DIM] for h in heads], axis=0)
            s_own = lax.dot_general(q, kv_cur[:, ks], nt, preferred_element_type=F32)
            s_prev = lax.dot_general(q, kv_prev[:, ks], nt, preferred_element_type=F32)
            p_own, p_prev = [], []
            for g, h in enumerate(heads):
                rows = slice(g * WINDOW, (g + 1) * WINDOW)
                logits = (jnp.where(own, s_own[rows], s_prev[rows]) * (HEAD_DIM ** -0.5)
                          - slopes[h] * dist_f)
                if valid is not None:
                    logits = jnp.where(valid, logits, NEG_INF)
                sink = sinks_ref[h]
                m = jnp.maximum(jnp.max(logits, axis=-1, keepdims=True), sink)
                p = jnp.exp(logits - m)
                probs = p * (1.0 / (jnp.sum(p, axis=-1, keepdims=True) + jnp.exp(sink - m)))
                p_own.append(jnp.where(own, probs, 0.0).astype(BF16))
                p_prev.append(jnp.where(own, 0.0, probs).astype(BF16))
            out = (jnp.dot(jnp.concatenate(p_own, axis=0), kv_cur[:, vs],
                           preferred_element_type=F32)
                   + jnp.dot(jnp.concatenate(p_prev, axis=0), kv_prev[:, vs],
                             preferred_element_type=F32))
            o_ref[r0:r0 + WINDOW, kh * gqa * HEAD_DIM:(kh + 1) * gqa * HEAD_DIM] = jnp.concatenate(
                [out[g * WINDOW:(g + 1) * WINDOW] for g in range(gqa)], axis=1).astype(o_ref.dtype)
    o_ref[:, tok_w:] = _memory_attention(qm_ref[...], kvm_ref[...]).astype(o_ref.dtype)


def _attn_mixer(proj, kvm, sinks, *, seq, mem_len, tq):
    t = proj.shape[0]
    tok_w = proj.shape[1] - 2 * KV_W - MEM_W
    n_heads = tok_w // HEAD_DIM
    assert tok_w % (2 * KV_W) == 0 and tok_w % MEM_W == 0 and seq % tq == 0
    kv_col = tok_w // (2 * KV_W)
    qm_col = (tok_w + 2 * KV_W) // MEM_W
    sub = tq // WINDOW
    kern = functools.partial(_attn_kernel, tq=tq, seq=seq, slopes=_alibi_slopes(n_heads),
                             tok_w=tok_w)
    return pl.pallas_call(
        kern,
        grid=(t // tq,),
        in_specs=[
            pl.BlockSpec(memory_space=pltpu.SMEM),
            pl.BlockSpec((tq, tok_w), lambda i: (i, 0)),
            pl.BlockSpec((tq, 2 * KV_W), lambda i: (i, kv_col)),
            pl.BlockSpec((WINDOW, 2 * KV_W), lambda i: (jnp.maximum(i * sub - 1, 0), kv_col)),
            pl.BlockSpec((tq, MEM_W), lambda i: (i, qm_col)),
            pl.BlockSpec((mem_len, 2 * MEM_W), lambda i: ((i * tq) // seq, 0)),
        ],
        out_specs=pl.BlockSpec((tq, tok_w + MEM_W), lambda i: (i, 0)),
        out_shape=jax.ShapeDtypeStruct((t, tok_w + MEM_W), BF16),
        compiler_params=_params(),
        name="attn_mixer",
    )(sinks, proj, proj, proj, proj, kvm)


def _gmlp_kernel(u_ref, v_ref, qm_ref, kvm_ref, ws_ref, bs_ref, o_ref, *, tq, tok_w):
    n_sub = tq // CHUNK
    n_grp = tok_w // GM_GROUP_DIM
    row = lax.broadcasted_iota(jnp.int32, (CHUNK, CHUNK), 0)
    col = lax.broadcasted_iota(jnp.int32, (CHUNK, CHUNK), 1)
    lower = row >= col
    for g in range(n_grp):
        w = jnp.where(lower, ws_ref[g], 0.0).astype(BF16)
        bias = bs_ref[:, g:g + 1]
        c0 = g * GM_GROUP_DIM
        for sb in range(n_sub):
            r0 = sb * CHUNK
            sv = jnp.dot(w, v_ref[r0:r0 + CHUNK, c0:c0 + GM_GROUP_DIM],
                         preferred_element_type=F32) + bias
            u = u_ref[r0:r0 + CHUNK, c0:c0 + GM_GROUP_DIM].astype(F32)
            o_ref[r0:r0 + CHUNK, c0:c0 + GM_GROUP_DIM] = (u * sv).astype(o_ref.dtype)
    o_ref[:, tok_w:] = _memory_attention(qm_ref[...], kvm_ref[...]).astype(o_ref.dtype)


def _gmlp_mixer(proj, kvm, w_s, b_s, *, seq, mem_len, tq):
    t = proj.shape[0]
    tok_w = (proj.shape[1] - MEM_W) // 2
    n_grp = tok_w // GM_GROUP_DIM
    assert (2 * tok_w) % MEM_W == 0 and seq % tq == 0
    kern = functools.partial(_gmlp_kernel, tq=tq, tok_w=tok_w)
    return pl.pallas_call(
        kern,
        grid=(t // tq,),
        in_specs=[
            pl.BlockSpec((tq, tok_w), lambda i: (i, 0)),
            pl.BlockSpec((tq, tok_w), lambda i: (i, 1)),
            pl.BlockSpec((tq, MEM_W), lambda i: (i, (2 * tok_w) // MEM_W)),
            pl.BlockSpec((mem_len, 2 * MEM_W), lambda i: ((i * tq) // seq, 0)),
            _resident((n_grp, CHUNK, CHUNK)),
            _resident((CHUNK, n_grp)),
        ],
        out_specs=pl.BlockSpec((tq, tok_w + MEM_W), lambda i: (i, 0)),
        out_shape=jax.ShapeDtypeStruct((t, tok_w + MEM_W), BF16),
        compiler_params=_params(),
        name="gmlp_mixer",
    )(proj, proj, proj, kvm, w_s, b_s.T)


def _post_kernel(mix_ref, x_ref, wo_ref, g_ref, b_ref, wr_ref, br_ref,
                 x1_ref, xp_ref, route_ref, cnt_ref, carry_ref, *, tm, sub, alpha):
    i = pl.program_id(0)

    @pl.when(i == 0)
    def _():
        carry_ref[...] = jnp.zeros_like(carry_ref)

    carry = carry_ref[0:1, :]
    for r0 in range(0, tm, sub):
        carry = _post_rows(r0, sub, carry, mix_ref, x_ref, wo_ref, g_ref, b_ref, wr_ref,
                           br_ref, x1_ref, xp_ref, route_ref, alpha)
    carry_ref[...] = jnp.broadcast_to(carry, carry_ref.shape)
    cnt_ref[...] = jnp.broadcast_to(carry, cnt_ref.shape)


def _post_rows(r0, tm, carry, mix_ref, x_ref, wo_ref, g_ref, b_ref, wr_ref, br_ref,
               x1_ref, xp_ref, route_ref, alpha):
    rows = slice(r0, r0 + tm)
    y = jnp.dot(mix_ref[rows, :], wo_ref[...], preferred_element_type=F32)
    x1 = _layer_norm(alpha * x_ref[rows, :] + y, g_ref[...], b_ref[...])
    x1_ref[rows, :] = x1
    _pack_rows(xp_ref, x1, row0=r0)

    xh = x1.astype(BF16)
    xl = (x1 - xh.astype(F32)).astype(BF16)
    hh_hl = jnp.dot(xh, wr_ref[...], preferred_element_type=F32)
    lh = jnp.dot(xl, wr_ref[:, :LANES], preferred_element_type=F32)
    logits = (hh_hl[:, :LANES] + (hh_hl[:, LANES:] + lh)) + br_ref[...]

    lane = lax.broadcasted_iota(jnp.int32, (tm, LANES), 1)
    lane_f = lane.astype(F32)
    big = float(LANES)
    is_grp = lane < N_GROUPS
    gl = jnp.where(is_grp, logits, NEG_INF)
    gmax = jnp.max(gl, axis=-1, keepdims=True)
    g_sel = jnp.min(jnp.where(gl == gmax, lane_f, big), axis=-1, keepdims=True)
    p_grp = 1.0 / jnp.sum(jnp.where(is_grp, jnp.exp(gl - gmax), 0.0), axis=-1, keepdims=True)

    e_lo = N_GROUPS + g_sel * EXPERTS_PER_GROUP
    in_grp = (lane_f >= e_lo) & (lane_f < e_lo + EXPERTS_PER_GROUP)
    el = jnp.where(in_grp, logits, NEG_INF)
    m1 = jnp.max(el, axis=-1, keepdims=True)
    i1 = jnp.min(jnp.where(el == m1, lane_f, big), axis=-1, keepdims=True)
    el2 = jnp.where(lane_f == i1, NEG_INF, el)
    m2 = jnp.max(el2, axis=-1, keepdims=True)
    i2 = jnp.min(jnp.where(el2 == m2, lane_f, big), axis=-1, keepdims=True)
    z = jnp.sum(jnp.where(in_grp, jnp.exp(el - m1), 0.0), axis=-1, keepdims=True)
    tp1 = 1.0 / z
    tp2 = jnp.exp(m2 - m1) / z
    gate1 = p_grp * tp1 / (tp1 + tp2)
    gate2 = p_grp * tp2 / (tp1 + tp2)
    e1 = i1 - N_GROUPS
    e2 = i2 - N_GROUPS

    oh1 = lane_f == e1
    oh2 = lane_f == e2
    c = jnp.where(oh1 | oh2, 1.0, 0.0)
    r = lax.broadcasted_iota(jnp.int32, (tm, tm), 0)
    s = lax.broadcasted_iota(jnp.int32, (tm, tm), 1)
    tri = jnp.where(r > s, 1.0, 0.0).astype(BF16)
    prefix = jnp.dot(tri, c.astype(BF16), preferred_element_type=F32) + carry
    rank1 = jnp.sum(jnp.where(oh1, prefix, 0.0), axis=-1, keepdims=True)
    rank2 = jnp.sum(jnp.where(oh2, prefix, 0.0), axis=-1, keepdims=True)

    route = jnp.where(lane == 0, e1, 0.0)
    route = jnp.where(lane == 1, e2, route)
    route = jnp.where(lane == 2, rank1, route)
    route = jnp.where(lane == 3, rank2, route)
    route = jnp.where(lane == 4, gate1, route)
    route = jnp.where(lane == 5, gate2, route)
    route_ref[rows, :] = route
    return carry + jnp.sum(c, axis=0, keepdims=True)


def _post(mix, x2d, wo_bf, ln_g, ln_b, wr_hi_lo, b_r, *, tm, sub, alpha):
    t, d = x2d.shape
    kern = functools.partial(_post_kernel, tm=tm, sub=sub, alpha=alpha)
    return pl.pallas_call(
        kern,
        grid=(t // tm,),
        in_specs=[
            pl.BlockSpec((tm, d), lambda i: (i, 0)),
            pl.BlockSpec((tm, d), lambda i: (i, 0)),
            _resident((d, d)),
            _resident((1, d)), _resident((1, d)),
            _resident((d, 2 * LANES)), _resident((1, LANES)),
        ],
        out_specs=[
            pl.BlockSpec((tm, d), lambda i: (i, 0)),
            pl.BlockSpec((tm * (d // (2 * LANES)), LANES), lambda i: (i, 0)),
            pl.BlockSpec((tm, LANES), lambda i: (i, 0)),
            pl.BlockSpec((8, LANES), lambda i: (0, 0)),
        ],
        out_shape=[
            jax.ShapeDtypeStruct((t, d), F32),
            jax.ShapeDtypeStruct((t * (d // (2 * LANES)), LANES), U32),
            jax.ShapeDtypeStruct((t, LANES), F32),
            jax.ShapeDtypeStruct((8, LANES), F32),
        ],
        scratch_shapes=[pltpu.VMEM((8, LANES), F32)],
        compiler_params=_params(),
        name="post",
    )(mix, x2d, wo_bf, ln_g.reshape(1, d), ln_b.reshape(1, d), wr_hi_lo, b_r)


def _dispatch_kernel(da_ref, db_ref, xp_ref, stale_hbm, xs_hbm, sem, *, tm, nc):
    del stale_hbm
    unroll = 8

    def group(g, carry):
        for u in range(unroll):
            j = g * unroll + u
            src = xp_ref.at[pl.ds(j * nc, nc)]
            for d_ref in (da_ref, db_ref):
                dst = xs_hbm.at[pl.ds(pl.multiple_of(d_ref[j], nc), nc)]
                pltpu.make_async_copy(src, dst, sem).start()
        return carry
    lax.fori_loop(0, tm // unroll, group, 0)
    for _ in range(TOP_K):
        pltpu.make_async_copy(xp_ref, xs_hbm.at[pl.ds(0, tm * nc)], sem).wait()


def _dispatch(xp, dest_a, dest_b, stale, *, tm):
    nc = xp.shape[0] // dest_a.shape[0]
    t = dest_a.shape[0]
    smem = pl.BlockSpec((tm,), lambda i: (i,), memory_space=pltpu.SMEM)
    any_spec = pl.BlockSpec(memory_space=pl.ANY)
    return pl.pallas_call(
        functools.partial(_dispatch_kernel, tm=tm, nc=nc),
        grid=(t // tm,),
        in_specs=[smem, smem, pl.BlockSpec((tm * nc, LANES), lambda i: (i, 0)), any_spec],
        out_specs=any_spec,
        out_shape=jax.ShapeDtypeStruct(stale.shape, stale.dtype),
        scratch_shapes=[pltpu.SemaphoreType.DMA(())],
        input_output_aliases={3: 0},
        compiler_params=_params(),
        name="dispatch",
    )(dest_a, dest_b, xp, stale)


def _moe_kernel(blk_e_ref, nxt_e_ref, n_used_ref, xs_ref, wg_hbm, wu_hbm, wd_hbm, y_ref,
                wg_st, wu_st, wd_st, wg_bf, wu_bf, wd_bf, sem_w, *, nc, nblk, layer):
    s = pl.program_id(0)
    n_used = n_used_ref[0]

    weight_copies = ((wg_hbm, wg_st, 0), (wu_hbm, wu_st, 1), (wd_hbm, wd_st, 2))

    def fetch_weights(e):
        for hbm, stage, sem_i in weight_copies:
            pltpu.make_async_copy(hbm.at[layer, e], stage, sem_w.at[sem_i]).start()

    @pl.when(s == 0)
    def _():
        fetch_weights(blk_e_ref[0])

    b_cur = jnp.minimum(s, nblk - 1)
    e_cur = blk_e_ref[b_cur]
    new_expert = (s == 0) | (e_cur != blk_e_ref[jnp.maximum(b_cur - 1, 0)])

    def expert_block(first):
        xb = _unpack_rows(xs_ref, nc).astype(BF16)
        if first:
            for hbm, stage, sem_i in weight_copies:
                pltpu.make_async_copy(hbm.at[layer, 0], stage, sem_w.at[sem_i]).wait()
            wg_bf[...] = wg_st[...].astype(BF16)
        hg = jnp.dot(xb, wg_bf[...], preferred_element_type=F32)
        if first:
            wu_bf[...] = wu_st[...].astype(BF16)
        hu = jnp.dot(xb, wu_bf[...], preferred_element_type=F32)
        if first:
            wd_bf[...] = wd_st[...].astype(BF16)
        h = (jax.nn.silu(hg) * hu).astype(BF16)
        _pack_rows(y_ref, jnp.dot(h, wd_bf[...], preferred_element_type=F32))
        if first:
            e_next = nxt_e_ref[b_cur]

            @pl.when(e_next != e_cur)
            def _():
                fetch_weights(e_next)

    @pl.when((s < n_used) & new_expert)
    def _():
        expert_block(True)

    @pl.when((s < n_used) & jnp.logical_not(new_expert))
    def _():
        expert_block(False)

    @pl.when(s >= n_used)
    def _():
        y_ref[...] = jnp.zeros_like(y_ref)


def _moe(xs, blk_e, nxt_e, n_used, w_gate, w_up, w_down, *, layer, rb):
    d_e, d = w_down.shape[-2:]
    n_chunks = d // (2 * LANES)
    nblk = blk_e.shape[0]

    def used_blk(s, blk_e_ref, nxt_e_ref, n_used_ref):
        return (jnp.minimum(s, jnp.maximum(n_used_ref[0] - 1, 0)), 0)

    any_spec = pl.BlockSpec(memory_space=pl.ANY)
    grid_spec = pltpu.PrefetchScalarGridSpec(
        num_scalar_prefetch=3,
        grid=(nblk,),
        in_specs=[pl.BlockSpec((rb * n_chunks, LANES), used_blk), any_spec, any_spec, any_spec],
        out_specs=pl.BlockSpec((rb * n_chunks, LANES), lambda s, *_: (s, 0)),
        scratch_shapes=[pltpu.VMEM((d, d_e), F32), pltpu.VMEM((d, d_e), F32),
                        pltpu.VMEM((d_e, d), F32),
                        pltpu.VMEM((d, d_e), BF16), pltpu.VMEM((d, d_e), BF16),
                        pltpu.VMEM((d_e, d), BF16),
                        pltpu.SemaphoreType.DMA((3,))],
    )
    return pl.pallas_call(
        functools.partial(_moe_kernel, nc=n_chunks, nblk=nblk, layer=layer),
        grid_spec=grid_spec,
        out_shape=jax.ShapeDtypeStruct(xs.shape, U32),
        compiler_params=_params(),
        name="moe",
    )(blk_e, nxt_e, n_used, xs, w_gate, w_up, w_down)


def _combine_kernel(d0a_ref, d0b_ref, dna_ref, dnb_ref, x1_ref, route_ref, g_ref, b_ref, y_hbm,
                    *rest, tm, nc, alpha, mode, tok_w):
    if mode == "attn":
        w_ref, x2_ref, o_ref, gbuf, sem = rest
    elif mode == "gmlp":
        w_ref, ng_ref, nb_ref, x2_ref, o_ref, gbuf, sem = rest
    else:
        x2_ref, gbuf, sem = rest
    i = pl.program_id(0)
    slot = lax.rem(i, 2)
    unroll = 8

    def issue(da_ref, db_ref, sl):
        def group(g, carry):
            for u in range(unroll):
                j = g * unroll + u
                for k, d_ref in enumerate((da_ref, db_ref)):
                    src = y_hbm.at[pl.ds(pl.multiple_of(d_ref[j], nc), nc)]
                    pltpu.make_async_copy(src, gbuf.at[sl, k, pl.ds(j * nc, nc)],
                                          sem.at[sl]).start()
            return carry
        lax.fori_loop(0, tm // unroll, group, 0)

    @pl.when(i == 0)
    def _():
        issue(d0a_ref, d0b_ref, 0)

    @pl.when(i + 1 < pl.num_programs(0))
    def _():
        issue(dna_ref, dnb_ref, 1 - slot)

    for k in range(TOP_K):
        pltpu.make_async_copy(y_hbm.at[pl.ds(0, tm * nc)], gbuf.at[slot, k], sem.at[slot]).wait()

    route = route_ref[...]
    y = (_unpack_rows(gbuf, nc, slot, 0) * route[:, 4:5]
         + _unpack_rows(gbuf, nc, slot, 1) * route[:, 5:6])
    x2 = _layer_norm(alpha * x1_ref[...] + y, g_ref[...], b_ref[...])
    x2_ref[...] = x2
    if mode == "attn":
        o_ref[...] = jnp.dot(x2.astype(BF16), w_ref[...],
                             preferred_element_type=F32).astype(o_ref.dtype)
    elif mode == "gmlp":
        _gmlp_epilogue(jnp.dot(x2.astype(BF16), w_ref[...], preferred_element_type=F32),
                       ng_ref, nb_ref, o_ref, tok_w)


def _combine(y, x1, route, dest_a, dest_b, ln_g, ln_b, *, tm, alpha, w_bf=None, gmlp_norm=None):
    t, d = x1.shape
    n_chunks = d // (2 * LANES)
    mode = "none" if w_bf is None else ("attn" if gmlp_norm is None else "gmlp")
    n_steps = t // tm

    def smem(index_map):
        return pl.BlockSpec((tm,), index_map, memory_space=pltpu.SMEM)

    def first(i):
        return (0,)

    def nxt(i):
        return (jnp.minimum(i + 1, n_steps - 1),)

    in_specs = [smem(first), smem(first), smem(nxt), smem(nxt),
                pl.BlockSpec((tm, d), lambda i: (i, 0)),
                pl.BlockSpec((tm, LANES), lambda i: (i, 0)),
                _resident((1, d)), _resident((1, d)),
                pl.BlockSpec(memory_space=pl.ANY)]
    args = [dest_a, dest_b, dest_a, dest_b, x1, route, ln_g.reshape(1, d), ln_b.reshape(1, d), y]
    out_specs = [pl.BlockSpec((tm, d), lambda i: (i, 0))]
    out_shape = [jax.ShapeDtypeStruct((t, d), F32)]
    tok_w = None
    if w_bf is not None:
        n = w_bf.shape[1]
        in_specs.append(_resident((d, n)))
        args.append(w_bf)
        out_specs.append(pl.BlockSpec((tm, n), lambda i: (i, 0)))
        out_shape.append(jax.ShapeDtypeStruct((t, n), BF16))
    if gmlp_norm is not None:
        tok_w = gmlp_norm[0].shape[-1]
        in_specs += [_resident((1, tok_w)), _resident((1, tok_w))]
        args += [gmlp_norm[0].reshape(1, tok_w), gmlp_norm[1].reshape(1, tok_w)]
    return pl.pallas_call(
        functools.partial(_combine_kernel, tm=tm, nc=n_chunks, alpha=alpha, mode=mode,
                          tok_w=tok_w),
        grid=(n_steps,),
        in_specs=in_specs,
        out_specs=out_specs,
        out_shape=out_shape,
        scratch_shapes=[pltpu.VMEM((2, TOP_K, tm * n_chunks, LANES), U32),
                        pltpu.SemaphoreType.DMA((2,))],
        compiler_params=_params(),
        name="combine_" + mode,
    )(*args)


def _split_bf16(w):
    hi = w.astype(BF16)
    lo = (w - hi.astype(F32)).astype(BF16)
    return hi, lo


def kernel(x, mem, a_w_in, a_sinks, b_w_in, b_norm_g, b_norm_b, b_w_spatial, b_b_spatial,
           w_mem_kv, w_out, ln_g, ln_b, w_router_group, b_router_group,
           w_router_expert, b_router_expert, w_gate, w_up, w_down):
    bsz, seq, d = x.shape
    mem_len = mem.shape[1]
    depth = w_out.shape[0]
    t = bsz * seq
    alpha = (2 * depth) ** 0.25
    rb = ROW_BLOCK
    n_rows = -(-(t * TOP_K + N_EXPERTS * (rb - 1)) // rb) * rb
    nblk = n_rows // rb

    def in_proj_weights(i):
        if i % 2 == 0:
            return a_w_in[i // 2].astype(BF16), None
        return b_w_in[i // 2].astype(BF16), (b_norm_g[i // 2], b_norm_b[i // 2])

    x2d = x.reshape(t, d)
    mem2d = mem.reshape(bsz * mem_len, d)
    proj = _proj(x2d, in_proj_weights(0)[0], BF16, tm=_Rows.proj)
    xs = jnp.zeros((n_rows * (d // (2 * LANES)), LANES), U32)
    for i in range(depth):
        j = i // 2
        kvm = _proj(mem2d, w_mem_kv[i].astype(BF16), BF16,
                    tm=min(_Rows.mem_proj, bsz * mem_len))
        if i % 2 == 0:
            mix = _attn_mixer(proj, kvm, a_sinks[j], seq=seq, mem_len=mem_len,
                              tq=_Rows.attn)
        else:
            mix = _gmlp_mixer(proj, kvm, b_w_spatial[j], b_b_spatial[j],
                              seq=seq, mem_len=mem_len, tq=_Rows.gmlp)

        w_r = jnp.concatenate([w_router_group[i], w_router_expert[i]], axis=1)
        w_r = jnp.pad(w_r, ((0, 0), (0, LANES - w_r.shape[1])))
        b_r = jnp.concatenate([b_router_group[i], b_router_expert[i]])
        b_r = jnp.pad(b_r, (0, LANES - b_r.shape[0])).reshape(1, LANES)
        wr_hi_lo = jnp.concatenate(_split_bf16(w_r), axis=1)
        x1, x1p, route, cnt = _post(mix, x2d, w_out[i].astype(BF16), ln_g[i, 0], ln_b[i, 0],
                                    wr_hi_lo, b_r, tm=_Rows.post, sub=_Rows.post_sub,
                                    alpha=alpha)

        counts = cnt[0, :N_EXPERTS].astype(jnp.int32)
        padded = (counts + rb - 1) // rb * rb
        pend = jnp.cumsum(padded)
        pstart = (pend - padded).astype(jnp.int32)
        n_used = (pend[-1] // rb).astype(jnp.int32).reshape(1)
        blk = jnp.arange(nblk, dtype=jnp.int32)

        def expert_of_block(b):
            row = jnp.minimum(b, n_used[0] - 1) * rb
            return jnp.sum((pend[None, :] <= row[:, None]).astype(jnp.int32), axis=1)

        blk_e = expert_of_block(blk)
        e_sel = blk_e[:, None] == jnp.arange(N_EXPERTS, dtype=jnp.int32)
        seg_end = jnp.sum(jnp.where(e_sel, pend, 0), axis=1)
        nxt_blk = seg_end // rb
        nxt_e = jnp.where(nxt_blk < n_used[0], expert_of_block(nxt_blk), blk_e)
        ri = route[:, :4].astype(jnp.int32)
        e_hot = ri[:, :TOP_K, None] == jnp.arange(N_EXPERTS, dtype=jnp.int32)
        dest = jnp.sum(jnp.where(e_hot, pstart, 0), axis=-1) + ri[:, TOP_K:]
        n_chunks = d // (2 * LANES)
        dest_a, dest_b = dest[:, 0] * n_chunks, dest[:, 1] * n_chunks

        xs = _dispatch(x1p, dest_a, dest_b, xs, tm=_Rows.dispatch)
        y = _moe(xs, blk_e, nxt_e.astype(jnp.int32), n_used, w_gate, w_up, w_down,
                 layer=i, rb=rb)
        if i + 1 < depth:
            w_next, gmlp_norm = in_proj_weights(i + 1)
            x2d, proj = _combine(y, x1, route, dest_a, dest_b, ln_g[i, 1], ln_b[i, 1],
                                 tm=_Rows.combine, alpha=alpha, w_bf=w_next,
                                 gmlp_norm=gmlp_norm)
        else:
            (x2d,) = _combine(y, x1, route, dest_a, dest_b, ln_g[i, 1], ln_b[i, 1],
                              tm=_Rows.combine, alpha=alpha)
    return x2d.reshape(bsz, seq, d)
```

```python
import functools
import math

import jax
import jax.numpy as jnp
import numpy as np
from jax import lax
from jax.experimental import pallas as pl
from jax.experimental.pallas import tpu as pltpu

MEM_HEADS = 4
MEM_HEAD_DIM = 128
MEM_W = MEM_HEADS * MEM_HEAD_DIM
HEAD_DIM = 64
N_KV_HEADS = 4
KV_W = N_KV_HEADS * HEAD_DIM
WINDOW = 128
CHUNK = 128
GM_GROUP_DIM = 128
N_GROUPS = 8
EXPERTS_PER_GROUP = 8
N_EXPERTS = N_GROUPS * EXPERTS_PER_GROUP
TOP_K = 2
LN_EPS = 1e-5
NEG_INF = -1e30

LANES = 128
VMEM_LIMIT_BYTES = 56 * 1024 * 1024

ROW_BLOCK = 256


class _Rows:
    proj = 512
    mem_proj = 256
    attn = 256
    gmlp = 512
    post = 512
    post_sub = 256
    dispatch = 512
    combine = 256


BF16 = jnp.bfloat16
F32 = jnp.float32
U32 = jnp.uint32


def _alibi_slopes(n):
    def pow2(m):
        start = 2.0 ** (-8.0 / m)
        return [start ** (i + 1) for i in range(m)]
    if math.log2(n).is_integer():
        s = pow2(n)
    else:
        c = 2 ** math.floor(math.log2(n))
        s = pow2(c) + pow2(2 * c)[0::2][: n - c]
    return [float(v) for v in np.asarray(s, dtype=np.float32)]


def _params(n_axes=1):
    return pltpu.CompilerParams(dimension_semantics=("arbitrary",) * n_axes,
                                vmem_limit_bytes=VMEM_LIMIT_BYTES)


def _resident(shape):
    nd = len(shape)
    return pl.BlockSpec(shape, lambda *_: (0,) * nd, pipeline_mode=pl.Buffered(1))


def _pack_rows(ref, val, *lead, row0=0):
    rows, d = val.shape
    nw = d // (2 * LANES)
    for c in range(nw):
        lo = val[:, c * LANES:(c + 1) * LANES].astype(BF16).astype(F32)
        hi = val[:, (c + nw) * LANES:(c + nw + 1) * LANES].astype(BF16).astype(F32)
        word = (lax.bitcast_convert_type(lo, U32) >> 16) | lax.bitcast_convert_type(hi, U32)
        ref[tuple(lead) + (pl.ds(row0 * nw + c, rows, stride=nw), slice(None))] = word


def _unpack_rows(ref, nw, *lead):
    rows = ref.shape[-2] // nw
    lo, hi = [], []
    for c in range(nw):
        word = ref[tuple(lead) + (pl.ds(c, rows, stride=nw), slice(None))]
        lo.append(lax.bitcast_convert_type(word << 16, F32))
        hi.append(lax.bitcast_convert_type(word & jnp.uint32(0xFFFF0000), F32))
    return jnp.concatenate(lo + hi, axis=-1)


def _layer_norm(x, g, b):
    mu = jnp.mean(x, axis=-1, keepdims=True)
    xc = x - mu
    var = jnp.mean(xc * xc, axis=-1, keepdims=True)
    return xc * lax.rsqrt(var + LN_EPS) * g + b


def _proj_kernel(x_ref, w_ref, o_ref):
    o_ref[...] = jnp.dot(x_ref[...].astype(BF16), w_ref[...],
                         preferred_element_type=F32).astype(o_ref.dtype)


def _proj(x2d, w_bf, out_dtype, tm):
    m, k = x2d.shape
    n = w_bf.shape[1]
    return pl.pallas_call(
        _proj_kernel,
        grid=(m // tm,),
        in_specs=[pl.BlockSpec((tm, k), lambda i: (i, 0)), _resident((k, n))],
        out_specs=pl.BlockSpec((tm, n), lambda i: (i, 0)),
        out_shape=jax.ShapeDtypeStruct((m, n), out_dtype),
        compiler_params=_params(),
        name="proj",
    )(x2d, w_bf)


def _gelu(x):
    return 0.5 * x * (1.0 + lax.erf(x * (2.0 ** -0.5)))


def _gmlp_epilogue(z, g_ref, b_ref, o_ref, tok_w):
    u = _gelu(z[:, :tok_w])
    v = _gelu(z[:, tok_w:2 * tok_w])
    v = _layer_norm(v, g_ref[...], b_ref[...])
    o_ref[:, :tok_w] = u.astype(o_ref.dtype)
    o_ref[:, tok_w:2 * tok_w] = v.astype(o_ref.dtype)
    o_ref[:, 2 * tok_w:] = z[:, 2 * tok_w:].astype(o_ref.dtype)


def _memory_attention(qm, kvm):
    outs = []
    for h in range(MEM_HEADS):
        q = qm[:, h * MEM_HEAD_DIM:(h + 1) * MEM_HEAD_DIM]
        k = kvm[:, h * MEM_HEAD_DIM:(h + 1) * MEM_HEAD_DIM]
        v = kvm[:, MEM_W + h * MEM_HEAD_DIM:MEM_W + (h + 1) * MEM_HEAD_DIM]
        s = lax.dot_general(q, k, (((1,), (1,)), ((), ())),
                            preferred_element_type=F32) * (MEM_HEAD_DIM ** -0.5)
        m = jnp.max(s, axis=-1, keepdims=True)
        p = jnp.exp(s - m)
        p = p * (1.0 / jnp.sum(p, axis=-1, keepdims=True))
        outs.append(jnp.dot(p.astype(BF16), v, preferred_element_type=F32))
    return jnp.concatenate(outs, axis=-1)


def _attn_kernel(sinks_ref, q_ref, kv_ref, kvp_ref, qm_ref, kvm_ref, o_ref, *,
                 tq, seq, slopes, tok_w):
    i = pl.program_id(0)
    n_sub = tq // WINDOW
    gqa = len(slopes) // N_KV_HEADS
    qi = lax.broadcasted_iota(jnp.int32, (WINDOW, WINDOW), 0)
    c = lax.broadcasted_iota(jnp.int32, (WINDOW, WINDOW), 1)
    own = c <= qi
    dist_f = jnp.where(own, qi - c, WINDOW + qi - c).astype(F32)
    nt = (((1,), (1,)), ((), ()))
    for sb in range(n_sub):
        r0 = sb * WINDOW
        if sb == 0:
            kv_prev = kvp_ref[...]
            reach = jnp.where(((i * tq) % seq) == 0, 0, WINDOW)
            valid = c <= qi + reach
        else:
            kv_prev = kv_ref[r0 - WINDOW:r0, :]
            valid = None
        kv_cur = kv_ref[r0:r0 + WINDOW, :]
        for kh in range(N_KV_HEADS):
            ks = slice(kh * HEAD_DIM, (kh + 1) * HEAD_DIM)
            vs = slice(KV_W + kh * HEAD_DIM, KV_W + (kh + 1) * HEAD_DIM)
            heads = range(kh * gqa, (kh + 1) * gqa)
            q = jnp.concatenate(
                [q_ref[r0:r0 + WINDOW, h * HEAD_DIM:(h + 1) * HEAD_DIM] for h in heads], axis=0)
            s_own = lax.dot_general(q, kv_cur[:, ks], nt, preferred_element_type=F32)
            s_prev = lax.dot_general(q, kv_prev[:, ks], nt, preferred_element_type=F32)
            p_own, p_prev = [], []
            for g, h in enumerate(heads):
                rows = slice(g * WINDOW, (g + 1) * WINDOW)
                logits = (jnp.where(own, s_own[rows], s_prev[rows]) * (HEAD_DIM ** -0.5)
                          - slopes[h] * dist_f)
                if valid is not None:
                    logits = jnp.where(valid, logits, NEG_INF)
                sink = sinks_ref[h]
                m = jnp.maximum(jnp.max(logits, axis=-1, keepdims=True), sink)
                p = jnp.exp(logits - m)
                probs = p * (1.0 / (jnp.sum(p, axis=-1, keepdims=True) + jnp.exp(sink - m)))
                p_own.append(jnp.where(own, probs, 0.0).astype(BF16))
                p_prev.append(jnp.where(own, 0.0, probs).astype(BF16))
            out = (jnp.dot(jnp.concatenate(p_own, axis=0), kv_cur[:, vs],
                           preferred_element_type=F32)
                   + jnp.dot(jnp.concatenate(p_prev, axis=0), kv_prev[:, vs],
                             preferred_element_type=F32))
            o_ref[r0:r0 + WINDOW, kh * gqa * HEAD_DIM:(kh + 1) * gqa * HEAD_DIM] = jnp.concatenate(
                [out[g * WINDOW:(g + 1) * WINDOW] for g in range(gqa)], axis=1).astype(o_ref.dtype)
    o_ref[:, tok_w:] = _memory_attention(qm_ref[...], kvm_ref[...]).astype(o_ref.dtype)


def _attn_mixer(proj, kvm, sinks, *, seq, mem_len, tq):
    t = proj.shape[0]
    tok_w = proj.shape[1] - 2 * KV_W - MEM_W
    n_heads = tok_w // HEAD_DIM
    assert tok_w % (2 * KV_W) == 0 and tok_w % MEM_W == 0 and seq % tq == 0
    kv_col = tok_w // (2 * KV_W)
    qm_col = (tok_w + 2 * KV_W) // MEM_W
    sub = tq // WINDOW
    kern = functools.partial(_attn_kernel, tq=tq, seq=seq, slopes=_alibi_slopes(n_heads),
                             tok_w=tok_w)
    return pl.pallas_call(
        kern,
        grid=(t // tq,),
        in_specs=[
            pl.BlockSpec(memory_space=pltpu.SMEM),
            pl.BlockSpec((tq, tok_w), lambda i: (i, 0)),
            pl.BlockSpec((tq, 2 * KV_W), lambda i: (i, kv_col)),
            pl.BlockSpec((WINDOW, 2 * KV_W), lambda i: (jnp.maximum(i * sub - 1, 0), kv_col)),
            pl.BlockSpec((tq, MEM_W), lambda i: (i, qm_col)),
            pl.BlockSpec((mem_len, 2 * MEM_W), lambda i: ((i * tq) // seq, 0)),
        ],
        out_specs=pl.BlockSpec((tq, tok_w + MEM_W), lambda i: (i, 0)),
        out_shape=jax.ShapeDtypeStruct((t, tok_w + MEM_W), BF16),
        compiler_params=_params(),
        name="attn_mixer",
    )(sinks, proj, proj, proj, proj, kvm)


def _gmlp_kernel(u_ref, v_ref, qm_ref, kvm_ref, ws_ref, bs_ref, o_ref, *, tq, tok_w):
    n_sub = tq // CHUNK
    n_grp = tok_w // GM_GROUP_DIM
    row = lax.broadcasted_iota(jnp.int32, (CHUNK, CHUNK), 0)
    col = lax.broadcasted_iota(jnp.int32, (CHUNK, CHUNK), 1)
    lower = row >= col
    for g in range(n_grp):
        w = jnp.where(lower, ws_ref[g], 0.0).astype(BF16)
        bias = bs_ref[:, g:g + 1]
        c0 = g * GM_GROUP_DIM
        for sb in range(n_sub):
            r0 = sb * CHUNK
            sv = jnp.dot(w, v_ref[r0:r0 + CHUNK, c0:c0 + GM_GROUP_DIM],
                         preferred_element_type=F32) + bias
            u = u_ref[r0:r0 + CHUNK, c0:c0 + GM_GROUP_DIM].astype(F32)
            o_ref[r0:r0 + CHUNK, c0:c0 + GM_GROUP_DIM] = (u * sv).astype(o_ref.dtype)
    o_ref[:, tok_w:] = _memory_attention(qm_ref[...], kvm_ref[...]).astype(o_ref.dtype)


def _gmlp_mixer(proj, kvm, w_s, b_s, *, seq, mem_len, tq):
    t = proj.shape[0]
    tok_w = (proj.shape[1] - MEM_W) // 2
    n_grp = tok_w // GM_GROUP_DIM
    assert (2 * tok_w) % MEM_W == 0 and seq % tq == 0
    kern = functools.partial(_gmlp_kernel, tq=tq, tok_w=tok_w)
    return pl.pallas_call(
        kern,
        grid=(t // tq,),
        in_specs=[
            pl.BlockSpec((tq, tok_w), lambda i: (i, 0)),
            pl.BlockSpec((tq, tok_w), lambda i: (i, 1)),
            pl.BlockSpec((tq, MEM_W), lambda i: (i, (2 * tok_w) // MEM_W)),
            pl.BlockSpec((mem_len, 2 * MEM_W), lambda i: ((i * tq) // seq, 0)),
            _resident((n_grp, CHUNK, CHUNK)),
            _resident((CHUNK, n_grp)),
        ],
        out_specs=pl.BlockSpec((tq, tok_w + MEM_W), lambda i: (i, 0)),
        out_shape=jax.ShapeDtypeStruct((t, tok_w + MEM_W), BF16),
        compiler_params=_params(),
        name="gmlp_mixer",
    )(proj, proj, proj, kvm, w_s, b_s.T)


def _post_kernel(mix_ref, x_ref, wo_ref, g_ref, b_ref, wr_ref, br_ref,
                 x1_ref, xp_ref, route_ref, cnt_ref, carry_ref, *, tm, sub, alpha):
    i = pl.program_id(0)

    @pl.when(i == 0)
    def _():
        carry_ref[...] = jnp.zeros_like(carry_ref)

    carry = carry_ref[0:1, :]
    for r0 in range(0, tm, sub):
        carry = _post_rows(r0, sub, carry, mix_ref, x_ref, wo_ref, g_ref, b_ref, wr_ref,
                           br_ref, x1_ref, xp_ref, route_ref, alpha)
    carry_ref[...] = jnp.broadcast_to(carry, carry_ref.shape)
    cnt_ref[...] = jnp.broadcast_to(carry, cnt_ref.shape)


def _post_rows(r0, tm, carry, mix_ref, x_ref, wo_ref, g_ref, b_ref, wr_ref, br_ref,
               x1_ref, xp_ref, route_ref, alpha):
    rows = slice(r0, r0 + tm)
    y = jnp.dot(mix_ref[rows, :], wo_ref[...], preferred_element_type=F32)
    x1 = _layer_norm(alpha * x_ref[rows, :] + y, g_ref[...], b_ref[...])
    x1_ref[rows, :] = x1
    _pack_rows(xp_ref, x1, row0=r0)

    xh = x1.astype(BF16)
    xl = (x1 - xh.astype(F32)).astype(BF16)
    hh_hl = jnp.dot(xh, wr_ref[...], preferred_element_type=F32)
    lh = jnp.dot(xl, wr_ref[:, :LANES], preferred_element_type=F32)
    logits = (hh_hl[:, :LANES] + (hh_hl[:, LANES:] + lh)) + br_ref[...]

    lane = lax.broadcasted_iota(jnp.int32, (tm, LANES), 1)
    lane_f = lane.astype(F32)
    big = float(LANES)
    is_grp = lane < N_GROUPS
    gl = jnp.where(is_grp, logits, NEG_INF)
    gmax = jnp.max(gl, axis=-1, keepdims=True)
    g_sel = jnp.min(jnp.where(gl == gmax, lane_f, big), axis=-1, keepdims=True)
    p_grp = 1.0 / jnp.sum(jnp.where(is_grp, jnp.exp(gl - gmax), 0.0), axis=-1, keepdims=True)

    e_lo = N_GROUPS + g_sel * EXPERTS_PER_GROUP
    in_grp = (lane_f >= e_lo) & (lane_f < e_lo + EXPERTS_PER_GROUP)
    el = jnp.where(in_grp, logits, NEG_INF)
    m1 = jnp.max(el, axis=-1, keepdims=True)
    i1 = jnp.min(jnp.where(el == m1, lane_f, big), axis=-1, keepdims=True)
    el2 = jnp.where(lane_f == i1, NEG_INF, el)
    m2 = jnp.max(el2, axis=-1, keepdims=True)
    i2 = jnp.min(jnp.where(el2 == m2, lane_f, big), axis=-1, keepdims=True)
    z = jnp.sum(jnp.where(in_grp, jnp.exp(el - m1), 0.0), axis=-1, keepdims=True)
    tp1 = 1.0 / z
    tp2 = jnp.exp(m2 - m1) / z
    gate1 = p_grp * tp1 / (tp1 + tp2)
    gate2 = p_grp * tp2 / (tp1 + tp2)
    e1 = i1 - N_GROUPS
    e2 = i2 - N_GROUPS

    oh1 = lane_f == e1
    oh2 = lane_f == e2
    c = jnp.where(oh1 | oh2, 1.0, 0.0)
    r = lax.broadcasted_iota(jnp.int32, (tm, tm), 0)
    s = lax.broadcasted_iota(jnp.int32, (tm, tm), 1)
    tri = jnp.where(r > s, 1.0, 0.0).astype(BF16)
    prefix = jnp.dot(tri, c.astype(BF16), preferred_element_type=F32) + carry
    rank1 = jnp.sum(jnp.where(oh1, prefix, 0.0), axis=-1, keepdims=True)
    rank2 = jnp.sum(jnp.where(oh2, prefix, 0.0), axis=-1, keepdims=True)

    route = jnp.where(lane == 0, e1, 0.0)
    route = jnp.where(lane == 1, e2, route)
    route = jnp.where(lane == 2, rank1, route)
    route = jnp.where(lane == 3, rank2, route)
    route = jnp.where(lane == 4, gate1, route)
    route = jnp.where(lane == 5, gate2, route)
    route_ref[rows, :] = route
    return carry + jnp.sum(c, axis=0, keepdims=True)


def _post(mix, x2d, wo_bf, ln_g, ln_b, wr_hi_lo, b_r, *, tm, sub, alpha):
    t, d = x2d.shape
    kern = functools.partial(_post_kernel, tm=tm, sub=sub, alpha=alpha)
    return pl.pallas_call(
        kern,
        grid=(t // tm,),
        in_specs=[
            pl.BlockSpec((tm, d), lambda i: (i, 0)),
            pl.BlockSpec((tm, d), lambda i: (i, 0)),
            _resident((d, d)),
            _resident((1, d)), _resident((1, d)),
            _resident((d, 2 * LANES)), _resident((1, LANES)),
        ],
        out_specs=[
            pl.BlockSpec((tm, d), lambda i: (i, 0)),
            pl.BlockSpec((tm * (d // (2 * LANES)), LANES), lambda i: (i, 0)),
            pl.BlockSpec((tm, LANES), lambda i: (i, 0)),
            pl.BlockSpec((8, LANES), lambda i: (0, 0)),
        ],
        out_shape=[
            jax.ShapeDtypeStruct((t, d), F32),
            jax.ShapeDtypeStruct((t * (d // (2 * LANES)), LANES), U32),
            jax.ShapeDtypeStruct((t, LANES), F32),
            jax.ShapeDtypeStruct((8, LANES), F32),
        ],
        scratch_shapes=[pltpu.VMEM((8, LANES), F32)],
        compiler_params=_params(),
        name="post",
    )(mix, x2d, wo_bf, ln_g.reshape(1, d), ln_b.reshape(1, d), wr_hi_lo, b_r)


def _dispatch_kernel(da_ref, db_ref, xp_ref, stale_hbm, xs_hbm, sem, *, tm, nc):
    del stale_hbm
    unroll = 8

    def group(g, carry):
        for u in range(unroll):
            j = g * unroll + u
            src = xp_ref.at[pl.ds(j * nc, nc)]
            for d_ref in (da_ref, db_ref):
                dst = xs_hbm.at[pl.ds(pl.multiple_of(d_ref[j], nc), nc)]
                pltpu.make_async_copy(src, dst, sem).start()
        return carry
    lax.fori_loop(0, tm // unroll, group, 0)
    for _ in range(TOP_K):
        pltpu.make_async_copy(xp_ref, xs_hbm.at[pl.ds(0, tm * nc)], sem).wait()


def _dispatch(xp, dest_a, dest_b, stale, *, tm):
    nc = xp.shape[0] // dest_a.shape[0]
    t = dest_a.shape[0]
    smem = pl.BlockSpec((tm,), lambda i: (i,), memory_space=pltpu.SMEM)
    any_spec = pl.BlockSpec(memory_space=pl.ANY)
    return pl.pallas_call(
        functools.partial(_dispatch_kernel, tm=tm, nc=nc),
        grid=(t // tm,),
        in_specs=[smem, smem, pl.BlockSpec((tm * nc, LANES), lambda i: (i, 0)), any_spec],
        out_specs=any_spec,
        out_shape=jax.ShapeDtypeStruct(stale.shape, stale.dtype),
        scratch_shapes=[pltpu.SemaphoreType.DMA(())],
        input_output_aliases={3: 0},
        compiler_params=_params(),
        name="dispatch",
    )(dest_a, dest_b, xp, stale)


def _moe_kernel(blk_e_ref, nxt_e_ref, n_used_ref, xs_ref, wg_hbm, wu_hbm, wd_hbm, y_ref,
                wg_st, wu_st, wd_st, wg_bf, wu_bf, wd_bf, sem_w, *, nc, nblk, layer):
    s = pl.program_id(0)
    n_used = n_used_ref[0]

    def weight_copy(hbm, stage, sem_i, e):
        return pltpu.make_async_copy(hbm.at[layer, e], stage, sem_w.at[sem_i])

    def round_and_refill(hbm, stage, bf, sem_i, e_next):
        weight_copy(hbm, stage, sem_i, e_next).wait()
        bf[...] = stage[...].astype(BF16)
        weight_copy(hbm, stage, sem_i, e_next).start()

    @pl.when(s == 0)
    def _():
        for sem_i, (hbm, stage) in enumerate(((wg_hbm, wg_st), (wu_hbm, wu_st), (wd_hbm, wd_st))):
            weight_copy(hbm, stage, sem_i, blk_e_ref[0]).start()

    b_cur = jnp.minimum(s, nblk - 1)
    e_cur = blk_e_ref[b_cur]
    new_expert = (s == 0) | (e_cur != blk_e_ref[jnp.maximum(b_cur - 1, 0)])

    def expert_block(first):
        e_next = nxt_e_ref[b_cur]
        xb = _unpack_rows(xs_ref, nc).astype(BF16)
        if first:
            round_and_refill(wg_hbm, wg_st, wg_bf, 0, e_next)
        hg = jnp.dot(xb, wg_bf[...], preferred_element_type=F32)
        if first:
            round_and_refill(wu_hbm, wu_st, wu_bf, 1, e_next)
        hu = jnp.dot(xb, wu_bf[...], preferred_element_type=F32)
        if first:
            round_and_refill(wd_hbm, wd_st, wd_bf, 2, e_next)
        h = (jax.nn.silu(hg) * hu).astype(BF16)
        _pack_rows(y_ref, jnp.dot(h, wd_bf[...], preferred_element_type=F32))

    @pl.when((s < n_used) & new_expert)
    def _():
        expert_block(True)

    @pl.when((s < n_used) & jnp.logical_not(new_expert))
    def _():
        expert_block(False)

    @pl.when((s >= n_used) & (s < nblk))
    def _():
        y_ref[...] = jnp.zeros_like(y_ref)

    @pl.when(s == n_used)
    def _():
        for sem_i, (hbm, stage) in enumerate(((wg_hbm, wg_st), (wu_hbm, wu_st), (wd_hbm, wd_st))):
            weight_copy(hbm, stage, sem_i, 0).wait()


def _moe(xs, blk_e, nxt_e, n_used, w_gate, w_up, w_down, *, layer, rb):
    d_e, d = w_down.shape[-2:]
    n_chunks = d // (2 * LANES)
    nblk = blk_e.shape[0]

    def used_blk(s, blk_e_ref, nxt_e_ref, n_used_ref):
        return (jnp.minimum(s, jnp.maximum(n_used_ref[0] - 1, 0)), 0)

    any_spec = pl.BlockSpec(memory_space=pl.ANY)
    grid_spec = pltpu.PrefetchScalarGridSpec(
        num_scalar_prefetch=3,
        grid=(nblk + 1,),
        in_specs=[pl.BlockSpec((rb * n_chunks, LANES), used_blk), any_spec, any_spec, any_spec],
        out_specs=pl.BlockSpec((rb * n_chunks, LANES),
                               lambda s, *_: (jnp.minimum(s, nblk - 1), 0)),
        scratch_shapes=[pltpu.VMEM((d, d_e), F32), pltpu.VMEM((d, d_e), F32),
                        pltpu.VMEM((d_e, d), F32),
                        pltpu.VMEM((d, d_e), BF16), pltpu.VMEM((d, d_e), BF16),
                        pltpu.VMEM((d_e, d), BF16),
                        pltpu.SemaphoreType.DMA((3,))],
    )
    return pl.pallas_call(
        functools.partial(_moe_kernel, nc=n_chunks, nblk=nblk, layer=layer),
        grid_spec=grid_spec,
        out_shape=jax.ShapeDtypeStruct(xs.shape, U32),
        compiler_params=_params(),
        name="moe",
    )(blk_e, nxt_e, n_used, xs, w_gate, w_up, w_down)


def _combine_kernel(d0a_ref, d0b_ref, dna_ref, dnb_ref, x1_ref, route_ref, g_ref, b_ref, y_hbm,
                    *rest, tm, nc, alpha, mode, tok_w):
    if mode == "attn":
        w_ref, x2_ref, o_ref, gbuf, sem = rest
    elif mode == "gmlp":
        w_ref, ng_ref, nb_ref, x2_ref, o_ref, gbuf, sem = rest
    else:
        x2_ref, gbuf, sem = rest
    i = pl.program_id(0)
    slot = lax.rem(i, 2)
    unroll = 8

    def issue(da_ref, db_ref, sl):
        def group(g, carry):
            for u in range(unroll):
                j = g * unroll + u
                for k, d_ref in enumerate((da_ref, db_ref)):
                    src = y_hbm.at[pl.ds(pl.multiple_of(d_ref[j], nc), nc)]
                    pltpu.make_async_copy(src, gbuf.at[sl, k, pl.ds(j * nc, nc)],
                                          sem.at[sl]).start()
            return carry
        lax.fori_loop(0, tm // unroll, group, 0)

    @pl.when(i == 0)
    def _():
        issue(d0a_ref, d0b_ref, 0)

    @pl.when(i + 1 < pl.num_programs(0))
    def _():
        issue(dna_ref, dnb_ref, 1 - slot)

    for k in range(TOP_K):
        pltpu.make_async_copy(y_hbm.at[pl.ds(0, tm * nc)], gbuf.at[slot, k], sem.at[slot]).wait()

    route = route_ref[...]
    y = (_unpack_rows(gbuf, nc, slot, 0) * route[:, 4:5]
         + _unpack_rows(gbuf, nc, slot, 1) * route[:, 5:6])
    x2 = _layer_norm(alpha * x1_ref[...] + y, g_ref[...], b_ref[...])
    x2_ref[...] = x2
    if mode == "attn":
        o_ref[...] = jnp.dot(x2.astype(BF16), w_ref[...],
                             preferred_element_type=F32).astype(o_ref.dtype)
    elif mode == "gmlp":
        _gmlp_epilogue(jnp.dot(x2.astype(BF16), w_ref[...], preferred_element_type=F32),
                       ng_ref, nb_ref, o_ref, tok_w)


def _combine(y, x1, route, dest_a, dest_b, ln_g, ln_b, *, tm, alpha, w_bf=None, gmlp_norm=None):
    t, d = x1.shape
    n_chunks = d // (2 * LANES)
    mode = "none" if w_bf is None else ("attn" if gmlp_norm is None else "gmlp")
    n_steps = t // tm

    def smem(index_map):
        return pl.BlockSpec((tm,), index_map, memory_space=pltpu.SMEM)

    def first(i):
        return (0,)

    def nxt(i):
        return (jnp.minimum(i + 1, n_steps - 1),)

    in_specs = [smem(first), smem(first), smem(nxt), smem(nxt),
                pl.BlockSpec((tm, d), lambda i: (i, 0)),
                pl.BlockSpec((tm, LANES), lambda i: (i, 0)),
                _resident((1, d)), _resident((1, d)),
                pl.BlockSpec(memory_space=pl.ANY)]
    args = [dest_a, dest_b, dest_a, dest_b, x1, route, ln_g.reshape(1, d), ln_b.reshape(1, d), y]
    out_specs = [pl.BlockSpec((tm, d), lambda i: (i, 0))]
    out_shape = [jax.ShapeDtypeStruct((t, d), F32)]
    tok_w = None
    if w_bf is not None:
        n = w_bf.shape[1]
        in_specs.append(_resident((d, n)))
        args.append(w_bf)
        out_specs.append(pl.BlockSpec((tm, n), lambda i: (i, 0)))
        out_shape.append(jax.ShapeDtypeStruct((t, n), BF16))
    if gmlp_norm is not None:
        tok_w = gmlp_norm[0].shape[-1]
        in_specs += [_resident((1, tok_w)), _resident((1, tok_w))]
        args += [gmlp_norm[0].reshape(1, tok_w), gmlp_norm[1].reshape(1, tok_w)]
    return pl.pallas_call(
        functools.partial(_combine_kernel, tm=tm, nc=n_chunks, alpha=alpha, mode=mode,
                          tok_w=tok_w),
        grid=(n_steps,),
        in_specs=in_specs,
        out_specs=out_specs,
        out_shape=out_shape,
        scratch_shapes=[pltpu.VMEM((2, TOP_K, tm * n_chunks, LANES), U32),
                        pltpu.SemaphoreType.DMA((2,))],
        compiler_params=_params(),
        name="combine_" + mode,
    )(*args)


def _split_bf16(w):
    hi = w.astype(BF16)
    lo = (w - hi.astype(F32)).astype(BF16)
    return hi, lo


def kernel(x, mem, a_w_in, a_sinks, b_w_in, b_norm_g, b_norm_b, b_w_spatial, b_b_spatial,
           w_mem_kv, w_out, ln_g, ln_b, w_router_group, b_router_group,
           w_router_expert, b_router_expert, w_gate, w_up, w_down):
    bsz, seq, d = x.shape
    mem_len = mem.shape[1]
    depth = w_out.shape[0]
    t = bsz * seq
    alpha = (2 * depth) ** 0.25
    rb = ROW_BLOCK
    n_rows = -(-(t * TOP_K + N_EXPERTS * (rb - 1)) // rb) * rb
    nblk = n_rows // rb

    def in_proj_weights(i):
        if i % 2 == 0:
            return a_w_in[i // 2].astype(BF16), None
        return b_w_in[i // 2].astype(BF16), (b_norm_g[i // 2], b_norm_b[i // 2])

    x2d = x.reshape(t, d)
    mem2d = mem.reshape(bsz * mem_len, d)
    proj = _proj(x2d, in_proj_weights(0)[0], BF16, tm=_Rows.proj)
    xs = jnp.zeros((n_rows * (d // (2 * LANES)), LANES), U32)
    for i in range(depth):
        j = i // 2
        kvm = _proj(mem2d, w_mem_kv[i].astype(BF16), BF16,
                    tm=min(_Rows.mem_proj, bsz * mem_len))
        if i % 2 == 0:
            mix = _attn_mixer(proj, kvm, a_sinks[j], seq=seq, mem_len=mem_len,
                              tq=_Rows.attn)
        else:
            mix = _gmlp_mixer(proj, kvm, b_w_spatial[j], b_b_spatial[j],
                              seq=seq, mem_len=mem_len, tq=_Rows.gmlp)

        w_r = jnp.concatenate([w_router_group[i], w_router_expert[i]], axis=1)
        w_r = jnp.pad(w_r, ((0, 0), (0, LANES - w_r.shape[1])))
        b_r = jnp.concatenate([b_router_group[i], b_router_expert[i]])
        b_r = jnp.pad(b_r, (0, LANES - b_r.shape[0])).reshape(1, LANES)
        wr_hi_lo = jnp.concatenate(_split_bf16(w_r), axis=1)
        x1, x1p, route, cnt = _post(mix, x2d, w_out[i].astype(BF16), ln_g[i, 0], ln_b[i, 0],
                                    wr_hi_lo, b_r, tm=_Rows.post, sub=_Rows.post_sub,
                                    alpha=alpha)

        counts = cnt[0, :N_EXPERTS].astype(jnp.int32)
        padded = (counts + rb - 1) // rb * rb
        pend = jnp.cumsum(padded)
        pstart = (pend - padded).astype(jnp.int32)
        n_used = (pend[-1] // rb).astype(jnp.int32).reshape(1)
        blk = jnp.arange(nblk, dtype=jnp.int32)

        def expert_of_block(b):
            row = jnp.minimum(b, n_used[0] - 1) * rb
            return jnp.sum((pend[None, :] <= row[:, None]).astype(jnp.int32), axis=1)

        blk_e = expert_of_block(blk)
        e_sel = blk_e[:, None] == jnp.arange(N_EXPERTS, dtype=jnp.int32)
        seg_end = jnp.sum(jnp.where(e_sel, pend, 0), axis=1)
        nxt_blk = seg_end // rb
        nxt_e = jnp.where(nxt_blk < n_used[0], expert_of_block(nxt_blk), blk_e)
        ri = route[:, :4].astype(jnp.int32)
        e_hot = ri[:, :TOP_K, None] == jnp.arange(N_EXPERTS, dtype=jnp.int32)
        dest = jnp.sum(jnp.where(e_hot, pstart, 0), axis=-1) + ri[:, TOP_K:]
        n_chunks = d // (2 * LANES)
        dest_a, dest_b = dest[:, 0] * n_chunks, dest[:, 1] * n_chunks

        xs = _dispatch(x1p, dest_a, dest_b, xs, tm=_Rows.dispatch)
        y = _moe(xs, blk_e, nxt_e.astype(jnp.int32), n_used, w_gate, w_up, w_down,
                 layer=i, rb=rb)
        if i + 1 < depth:
            w_next, gmlp_norm = in_proj_weights(i + 1)
            x2d, proj = _combine(y, x1, route, dest_a, dest_b, ln_g[i, 1], ln_b[i, 1],
                                 tm=_Rows.combine, alpha=alpha, w_bf=w_next,
                                 gmlp_norm=gmlp_norm)
        else:
            (x2d,) = _combine(y, x1, route, dest_a, dest_b, ln_g[i, 1], ln_b[i, 1],
                              tm=_Rows.combine, alpha=alpha)
    return x2d.reshape(bsz, seq, d)
```

```python
import functools
import math

import jax
import jax.numpy as jnp
import numpy as np
from jax import lax
from jax.experimental import pallas as pl
from jax.experimental.pallas import tpu as pltpu

MEM_HEADS = 4
MEM_HEAD_DIM = 128
MEM_W = MEM_HEADS * MEM_HEAD_DIM
HEAD_DIM = 64
N_KV_HEADS = 4
KV_W = N_KV_HEADS * HEAD_DIM
WINDOW = 128
CHUNK = 128
GM_GROUP_DIM = 128
N_GROUPS = 8
EXPERTS_PER_GROUP = 8
N_EXPERTS = N_GROUPS * EXPERTS_PER_GROUP
TOP_K = 2
LN_EPS = 1e-5
NEG_INF = -1e30

LANES = 128
VMEM_LIMIT_BYTES = 56 * 1024 * 1024

ROW_BLOCK = 256


class _Rows:
    proj = 512
    mem_proj = 256
    attn = 256
    gmlp = 512
    post = 512
    post_sub = 256
    dispatch = 512
    combine = 256


BF16 = jnp.bfloat16
F32 = jnp.float32
U32 = jnp.uint32


def _alibi_slopes(n):
    def pow2(m):
        start = 2.0 ** (-8.0 / m)
        return [start ** (i + 1) for i in range(m)]
    if math.log2(n).is_integer():
        s = pow2(n)
    else:
        c = 2 ** math.floor(math.log2(n))
        s = pow2(c) + pow2(2 * c)[0::2][: n - c]
    return [float(v) for v in np.asarray(s, dtype=np.float32)]


def _params(n_axes=1):
    return pltpu.CompilerParams(dimension_semantics=("arbitrary",) * n_axes,
                                vmem_limit_bytes=VMEM_LIMIT_BYTES)


def _resident(shape):
    nd = len(shape)
    return pl.BlockSpec(shape, lambda *_: (0,) * nd, pipeline_mode=pl.Buffered(1))


def _pack_rows(ref, val, *lead, row0=0):
    rows, d = val.shape
    nw = d // (2 * LANES)
    for c in range(nw):
        lo = val[:, c * LANES:(c + 1) * LANES].astype(BF16).astype(F32)
        hi = val[:, (c + nw) * LANES:(c + nw + 1) * LANES].astype(BF16).astype(F32)
        word = (lax.bitcast_convert_type(lo, U32) >> 16) | lax.bitcast_convert_type(hi, U32)
        ref[tuple(lead) + (pl.ds(row0 * nw + c, rows, stride=nw), slice(None))] = word


def _unpack_rows(ref, nw, *lead):
    rows = ref.shape[-2] // nw
    lo, hi = [], []
    for c in range(nw):
        word = ref[tuple(lead) + (pl.ds(c, rows, stride=nw), slice(None))]
        lo.append(lax.bitcast_convert_type(word << 16, F32))
        hi.append(lax.bitcast_convert_type(word & jnp.uint32(0xFFFF0000), F32))
    return jnp.concatenate(lo + hi, axis=-1)


def _layer_norm(x, g, b):
    mu = jnp.mean(x, axis=-1, keepdims=True)
    xc = x - mu
    var = jnp.mean(xc * xc, axis=-1, keepdims=True)
    return xc * lax.rsqrt(var + LN_EPS) * g + b


def _proj_kernel(x_ref, w_ref, o_ref):
    o_ref[...] = jnp.dot(x_ref[...].astype(BF16), w_ref[...],
                         preferred_element_type=F32).astype(o_ref.dtype)


def _proj(x2d, w_bf, out_dtype, tm):
    m, k = x2d.shape
    n = w_bf.shape[1]
    return pl.pallas_call(
        _proj_kernel,
        grid=(m // tm,),
        in_specs=[pl.BlockSpec((tm, k), lambda i: (i, 0)), _resident((k, n))],
        out_specs=pl.BlockSpec((tm, n), lambda i: (i, 0)),
        out_shape=jax.ShapeDtypeStruct((m, n), out_dtype),
        compiler_params=_params(),
        name="proj",
    )(x2d, w_bf)


def _gelu(x):
    return 0.5 * x * (1.0 + lax.erf(x * (2.0 ** -0.5)))


def _gmlp_epilogue(z, g_ref, b_ref, o_ref, tok_w):
    u = _gelu(z[:, :tok_w])
    v = _gelu(z[:, tok_w:2 * tok_w])
    v = _layer_norm(v, g_ref[...], b_ref[...])
    o_ref[:, :tok_w] = u.astype(o_ref.dtype)
    o_ref[:, tok_w:2 * tok_w] = v.astype(o_ref.dtype)
    o_ref[:, 2 * tok_w:] = z[:, 2 * tok_w:].astype(o_ref.dtype)


def _memory_attention(qm, kvm):
    outs = []
    for h in range(MEM_HEADS):
        q = qm[:, h * MEM_HEAD_DIM:(h + 1) * MEM_HEAD_DIM]
        k = kvm[:, h * MEM_HEAD_DIM:(h + 1) * MEM_HEAD_DIM]
        v = kvm[:, MEM_W + h * MEM_HEAD_DIM:MEM_W + (h + 1) * MEM_HEAD_DIM]
        s = lax.dot_general(q, k, (((1,), (1,)), ((), ())),
                            preferred_element_type=F32) * (MEM_HEAD_DIM ** -0.5)
        m = jnp.max(s, axis=-1, keepdims=True)
        p = jnp.exp(s - m)
        p = p * (1.0 / jnp.sum(p, axis=-1, keepdims=True))
        outs.append(jnp.dot(p.astype(BF16), v, preferred_element_type=F32))
    return jnp.concatenate(outs, axis=-1)


def _attn_kernel(sinks_ref, q_ref, kv_ref, kvp_ref, qm_ref, kvm_ref, o_ref, *,
                 tq, seq, slopes, tok_w):
    i = pl.program_id(0)
    n_sub = tq // WINDOW
    gqa = len(slopes) // N_KV_HEADS
    qi = lax.broadcasted_iota(jnp.int32, (WINDOW, WINDOW), 0)
    c = lax.broadcasted_iota(jnp.int32, (WINDOW, WINDOW), 1)
    own = c <= qi
    dist_f = jnp.where(own, qi - c, WINDOW + qi - c).astype(F32)
    nt = (((1,), (1,)), ((), ()))
    for sb in range(n_sub):
        r0 = sb * WINDOW
        if sb == 0:
            kv_prev = kvp_ref[...]
            reach = jnp.where(((i * tq) % seq) == 0, 0, WINDOW)
            valid = c <= qi + reach
        else:
            kv_prev = kv_ref[r0 - WINDOW:r0, :]
            valid = None
        kv_cur = kv_ref[r0:r0 + WINDOW, :]
        for kh in range(N_KV_HEADS):
            ks = slice(kh * HEAD_DIM, (kh + 1) * HEAD_DIM)
            vs = slice(KV_W + kh * HEAD_DIM, KV_W + (kh + 1) * HEAD_DIM)
            heads = range(kh * gqa, (kh + 1) * gqa)
            q = jnp.concatenate(
                [q_ref[r0:r0 + WINDOW, h * HEAD_DIM:(h + 1) * HEAD_DIM] for h in heads], axis=0)
            s_own = lax.dot_general(q, kv_cur[:, ks], nt, preferred_element_type=F32)
            s_prev = lax.dot_general(q, kv_prev[:, ks], nt, preferred_element_type=F32)
            p_own, p_prev = [], []
            for g, h in enumerate(heads):
                rows = slice(g * WINDOW, (g + 1) * WINDOW)
                logits = (jnp.where(own, s_own[rows], s_prev[rows]) * (HEAD_DIM ** -0.5)
                          - slopes[h] * dist_f)
                if valid is not None:
                    logits = jnp.where(valid, logits, NEG_INF)
                sink = sinks_ref[h]
                m = jnp.maximum(jnp.max(logits, axis=-1, keepdims=True), sink)
                p = jnp.exp(logits - m)
                probs = p * (1.0 / (jnp.sum(p, axis=-1, keepdims=True) + jnp.exp(sink - m)))
                p_own.append(jnp.where(own, probs, 0.0).astype(BF16))
                p_prev.append(jnp.where(own, 0.0, probs).astype(BF16))
            out = (jnp.dot(jnp.concatenate(p_own, axis=0), kv_cur[:, vs],
                           preferred_element_type=F32)
                   + jnp.dot(jnp.concatenate(p_prev, axis=0), kv_prev[:, vs],
                             preferred_element_type=F32))
            o_ref[r0:r0 + WINDOW, kh * gqa * HEAD_DIM:(kh + 1) * gqa * HEAD_DIM] = jnp.concatenate(
                [out[g * WINDOW:(g + 1) * WINDOW] for g in range(gqa)], axis=1).astype(o_ref.dtype)
    o_ref[:, tok_w:] = _memory_attention(qm_ref[...], kvm_ref[...]).astype(o_ref.dtype)


def _attn_mixer(proj, kvm, sinks, *, seq, mem_len, tq):
    t = proj.shape[0]
    tok_w = proj.shape[1] - 2 * KV_W - MEM_W
    n_heads = tok_w // HEAD_DIM
    assert tok_w % (2 * KV_W) == 0 and tok_w % MEM_W == 0 and seq % tq == 0
    kv_col = tok_w // (2 * KV_W)
    qm_col = (tok_w + 2 * KV_W) // MEM_W
    sub = tq // WINDOW
    kern = functools.partial(_attn_kernel, tq=tq, seq=seq, slopes=_alibi_slopes(n_heads),
                             tok_w=tok_w)
    return pl.pallas_call(
        kern,
        grid=(t // tq,),
        in_specs=[
            pl.BlockSpec(memory_space=pltpu.SMEM),
            pl.BlockSpec((tq, tok_w), lambda i: (i, 0)),
            pl.BlockSpec((tq, 2 * KV_W), lambda i: (i, kv_col)),
            pl.BlockSpec((WINDOW, 2 * KV_W), lambda i: (jnp.maximum(i * sub - 1, 0), kv_col)),
            pl.BlockSpec((tq, MEM_W), lambda i: (i, qm_col)),
            pl.BlockSpec((mem_len, 2 * MEM_W), lambda i: ((i * tq) // seq, 0)),
        ],
        out_specs=pl.BlockSpec((tq, tok_w + MEM_W), lambda i: (i, 0)),
        out_shape=jax.ShapeDtypeStruct((t, tok_w + MEM_W), BF16),
        compiler_params=_params(),
        name="attn_mixer",
    )(sinks, proj, proj, proj, proj, kvm)


def _gmlp_kernel(u_ref, v_ref, qm_ref, kvm_ref, ws_ref, bs_ref, o_ref, *, tq, tok_w):
    n_sub = tq // CHUNK
    n_grp = tok_w // GM_GROUP_DIM
    row = lax.broadcasted_iota(jnp.int32, (CHUNK, CHUNK), 0)
    col = lax.broadcasted_iota(jnp.int32, (CHUNK, CHUNK), 1)
    lower = row >= col
    for g in range(n_grp):
        w = jnp.where(lower, ws_ref[g], 0.0).astype(BF16)
        bias = bs_ref[:, g:g + 1]
        c0 = g * GM_GROUP_DIM
        for sb in range(n_sub):
            r0 = sb * CHUNK
            sv = jnp.dot(w, v_ref[r0:r0 + CHUNK, c0:c0 + GM_GROUP_DIM],
                         preferred_element_type=F32) + bias
            u = u_ref[r0:r0 + CHUNK, c0:c0 + GM_GROUP_DIM].astype(F32)
            o_ref[r0:r0 + CHUNK, c0:c0 + GM_GROUP_DIM] = (u * sv).astype(o_ref.dtype)
    o_ref[:, tok_w:] = _memory_attention(qm_ref[...], kvm_ref[...]).astype(o_ref.dtype)


def _gmlp_mixer(proj, kvm, w_s, b_s, *, seq, mem_len, tq):
    t = proj.shape[0]
    tok_w = (proj.shape[1] - MEM_W) // 2
    n_grp = tok_w // GM_GROUP_DIM
    assert (2 * tok_w) % MEM_W == 0 and seq % tq == 0
    kern = functools.partial(_gmlp_kernel, tq=tq, tok_w=tok_w)
    return pl.pallas_call(
        kern,
        grid=(t // tq,),
        in_specs=[
            pl.BlockSpec((tq, tok_w), lambda i: (i, 0)),
            pl.BlockSpec((tq, tok_w), lambda i: (i, 1)),
            pl.BlockSpec((tq, MEM_W), lambda i: (i, (2 * tok_w) // MEM_W)),
            pl.BlockSpec((mem_len, 2 * MEM_W), lambda i: ((i * tq) // seq, 0)),
            _resident((n_grp, CHUNK, CHUNK)),
            _resident((CHUNK, n_grp)),
        ],
        out_specs=pl.BlockSpec((tq, tok_w + MEM_W), lambda i: (i, 0)),
        out_shape=jax.ShapeDtypeStruct((t, tok_w + MEM_W), BF16),
        compiler_params=_params(),
        name="gmlp_mixer",
    )(proj, proj, proj, kvm, w_s, b_s.T)


def _post_kernel(mix_ref, x_ref, wo_ref, g_ref, b_ref, wr_ref, br_ref,
                 x1_ref, xp_ref, route_ref, cnt_ref, carry_ref, *, tm, sub, alpha):
    i = pl.program_id(0)

    @pl.when(i == 0)
    def _():
        carry_ref[...] = jnp.zeros_like(carry_ref)

    carry = carry_ref[0:1, :]
    for r0 in range(0, tm, sub):
        carry = _post_rows(r0, sub, carry, mix_ref, x_ref, wo_ref, g_ref, b_ref, wr_ref,
                           br_ref, x1_ref, xp_ref, route_ref, alpha)
    carry_ref[...] = jnp.broadcast_to(carry, carry_ref.shape)
    cnt_ref[...] = jnp.broadcast_to(carry, cnt_ref.shape)


def _post_rows(r0, tm, carry, mix_ref, x_ref, wo_ref, g_ref, b_ref, wr_ref, br_ref,
               x1_ref, xp_ref, route_ref, alpha):
    rows = slice(r0, r0 + tm)
    y = jnp.dot(mix_ref[rows, :], wo_ref[...], preferred_element_type=F32)
    x1 = _layer_norm(alpha * x_ref[rows, :] + y, g_ref[...], b_ref[...])
    x1_ref[rows, :] = x1
    _pack_rows(xp_ref, x1, row0=r0)

    xh = x1.astype(BF16)
    xl = (x1 - xh.astype(F32)).astype(BF16)
    hh_hl = jnp.dot(xh, wr_ref[...], preferred_element_type=F32)
    lh = jnp.dot(xl, wr_ref[:, :LANES], preferred_element_type=F32)
    logits = (hh_hl[:, :LANES] + (hh_hl[:, LANES:] + lh)) + br_ref[...]

    lane = lax.broadcasted_iota(jnp.int32, (tm, LANES), 1)
    lane_f = lane.astype(F32)
    big = float(LANES)
    is_grp = lane < N_GROUPS
    gl = jnp.where(is_grp, logits, NEG_INF)
    gmax = jnp.max(gl, axis=-1, keepdims=True)
    g_sel = jnp.min(jnp.where(gl == gmax, lane_f, big), axis=-1, keepdims=True)
    p_grp = 1.0 / jnp.sum(jnp.where(is_grp, jnp.exp(gl - gmax), 0.0), axis=-1, keepdims=True)

    e_lo = N_GROUPS + g_sel * EXPERTS_PER_GROUP
    in_grp = (lane_f >= e_lo) & (lane_f < e_lo + EXPERTS_PER_GROUP)
    el = jnp.where(in_grp, logits, NEG_INF)
    m1 = jnp.max(el, axis=-1, keepdims=True)
    i1 = jnp.min(jnp.where(el == m1, lane_f, big), axis=-1, keepdims=True)
    el2 = jnp.where(lane_f == i1, NEG_INF, el)
    m2 = jnp.max(el2, axis=-1, keepdims=True)
    i2 = jnp.min(jnp.where(el2 == m2, lane_f, big), axis=-1, keepdims=True)
    z = jnp.sum(jnp.where(in_grp, jnp.exp(el - m1), 0.0), axis=-1, keepdims=True)
    tp1 = 1.0 / z
    tp2 = jnp.exp(m2 - m1) / z
    gate1 = p_grp * tp1 / (tp1 + tp2)
    gate2 = p_grp * tp2 / (tp1 + tp2)
    e1 = i1 - N_GROUPS
    e2 = i2 - N_GROUPS

    oh1 = lane_f == e1
    oh2 = lane_f == e2
    c = jnp.where(oh1 | oh2, 1.0, 0.0)
    r = lax.broadcasted_iota(jnp.int32, (tm, tm), 0)
    s = lax.broadcasted_iota(jnp.int32, (tm, tm), 1)
    tri = jnp.where(r > s, 1.0, 0.0).astype(BF16)
    prefix = jnp.dot(tri, c.astype(BF16), preferred_element_type=F32) + carry
    rank1 = jnp.sum(jnp.where(oh1, prefix, 0.0), axis=-1, keepdims=True)
    rank2 = jnp.sum(jnp.where(oh2, prefix, 0.0), axis=-1, keepdims=True)

    route = jnp.where(lane == 0, e1, 0.0)
    route = jnp.where(lane == 1, e2, route)
    route = jnp.where(lane == 2, rank1, route)
    route = jnp.where(lane == 3, rank2, route)
    route = jnp.where(lane == 4, gate1, route)
    route = jnp.where(lane == 5, gate2, route)
    route_ref[rows, :] = route
    return carry + jnp.sum(c, axis=0, keepdims=True)


def _post(mix, x2d, wo_bf, ln_g, ln_b, wr_hi_lo, b_r, *, tm, sub, alpha):
    t, d = x2d.shape
    kern = functools.partial(_post_kernel, tm=tm, sub=sub, alpha=alpha)
    return pl.pallas_call(
        kern,
        grid=(t // tm,),
        in_specs=[
            pl.BlockSpec((tm, d), lambda i: (i, 0)),
            pl.BlockSpec((tm, d), lambda i: (i, 0)),
            _resident((d, d)),
            _resident((1, d)), _resident((1, d)),
            _resident((d, 2 * LANES)), _resident((1, LANES)),
        ],
        out_specs=[
            pl.BlockSpec((tm, d), lambda i: (i, 0)),
            pl.BlockSpec((tm * (d // (2 * LANES)), LANES), lambda i: (i, 0)),
            pl.BlockSpec((tm, LANES), lambda i: (i, 0)),
            pl.BlockSpec((8, LANES), lambda i: (0, 0)),
        ],
        out_shape=[
            jax.ShapeDtypeStruct((t, d), F32),
            jax.ShapeDtypeStruct((t * (d // (2 * LANES)), LANES), U32),
            jax.ShapeDtypeStruct((t, LANES), F32),
            jax.ShapeDtypeStruct((8, LANES), F32),
        ],
        scratch_shapes=[pltpu.VMEM((8, LANES), F32)],
        compiler_params=_params(),
        name="post",
    )(mix, x2d, wo_bf, ln_g.reshape(1, d), ln_b.reshape(1, d), wr_hi_lo, b_r)


def _dispatch_kernel(da_ref, db_ref, xp_ref, stale_hbm, xs_hbm, sem, *, tm, nc):
    del stale_hbm
    unroll = 8

    def group(g, carry):
        for u in range(unroll):
            j = g * unroll + u
            src = xp_ref.at[pl.ds(j * nc, nc)]
            for k, d_ref in enumerate((da_ref, db_ref)):
                dst = xs_hbm.at[pl.ds(pl.multiple_of(d_ref[j], nc), nc)]
                pltpu.make_async_copy(src, dst, sem).start(priority=k)
        return carry
    lax.fori_loop(0, tm // unroll, group, 0)
    for _ in range(TOP_K):
        pltpu.make_async_copy(xp_ref, xs_hbm.at[pl.ds(0, tm * nc)], sem).wait()


def _dispatch(xp, dest_a, dest_b, stale, *, tm):
    nc = xp.shape[0] // dest_a.shape[0]
    t = dest_a.shape[0]
    smem = pl.BlockSpec((tm,), lambda i: (i,), memory_space=pltpu.SMEM)
    any_spec = pl.BlockSpec(memory_space=pl.ANY)
    return pl.pallas_call(
        functools.partial(_dispatch_kernel, tm=tm, nc=nc),
        grid=(t // tm,),
        in_specs=[smem, smem, pl.BlockSpec((tm * nc, LANES), lambda i: (i, 0)), any_spec],
        out_specs=any_spec,
        out_shape=jax.ShapeDtypeStruct(stale.shape, stale.dtype),
        scratch_shapes=[pltpu.SemaphoreType.DMA(())],
        input_output_aliases={3: 0},
        compiler_params=_params(),
        name="dispatch",
    )(dest_a, dest_b, xp, stale)


def _moe_kernel(blk_e_ref, nxt_e_ref, n_used_ref, xs_ref, wg_hbm, wu_hbm, wd_hbm, y_ref,
                wg_st, wu_st, wd_st, wg_bf, wu_bf, wd_bf, sem_w, *, nc, nblk, layer):
    s = pl.program_id(0)
    n_used = n_used_ref[0]

    def weight_copy(hbm, stage, sem_i, e):
        return pltpu.make_async_copy(hbm.at[layer, e], stage, sem_w.at[sem_i])

    def round_and_refill(hbm, stage, bf, sem_i, e_next):
        weight_copy(hbm, stage, sem_i, e_next).wait()
        bf[...] = stage[...].astype(BF16)
        weight_copy(hbm, stage, sem_i, e_next).start()

    @pl.when(s == 0)
    def _():
        for sem_i, (hbm, stage) in enumerate(((wg_hbm, wg_st), (wu_hbm, wu_st), (wd_hbm, wd_st))):
            weight_copy(hbm, stage, sem_i, blk_e_ref[0]).start()

    b_cur = jnp.minimum(s, nblk - 1)
    e_cur = blk_e_ref[b_cur]
    new_expert = (s == 0) | (e_cur != blk_e_ref[jnp.maximum(b_cur - 1, 0)])

    def expert_block(first):
        e_next = nxt_e_ref[b_cur]
        xb = _unpack_rows(xs_ref, nc).astype(BF16)
        if first:
            round_and_refill(wg_hbm, wg_st, wg_bf, 0, e_next)
        hg = jnp.dot(xb, wg_bf[...], preferred_element_type=F32)
        if first:
            round_and_refill(wu_hbm, wu_st, wu_bf, 1, e_next)
        hu = jnp.dot(xb, wu_bf[...], preferred_element_type=F32)
        if first:
            round_and_refill(wd_hbm, wd_st, wd_bf, 2, e_next)
        h = (jax.nn.silu(hg) * hu).astype(BF16)
        _pack_rows(y_ref, jnp.dot(h, wd_bf[...], preferred_element_type=F32))

    @pl.when((s < n_used) & new_expert)
    def _():
        expert_block(True)

    @pl.when((s < n_used) & jnp.logical_not(new_expert))
    def _():
        expert_block(False)

    @pl.when((s >= n_used) & (s < nblk))
    def _():
        y_ref[...] = jnp.zeros_like(y_ref)

    @pl.when(s == n_used)
    def _():
        for sem_i, (hbm, stage) in enumerate(((wg_hbm, wg_st), (wu_hbm, wu_st), (wd_hbm, wd_st))):
            weight_copy(hbm, stage, sem_i, 0).wait()


def _moe(xs, blk_e, nxt_e, n_used, w_gate, w_up, w_down, *, layer, rb):
    d_e, d = w_down.shape[-2:]
    n_chunks = d // (2 * LANES)
    nblk = blk_e.shape[0]

    def used_blk(s, blk_e_ref, nxt_e_ref, n_used_ref):
        return (jnp.minimum(s, jnp.maximum(n_used_ref[0] - 1, 0)), 0)

    any_spec = pl.BlockSpec(memory_space=pl.ANY)
    grid_spec = pltpu.PrefetchScalarGridSpec(
        num_scalar_prefetch=3,
        grid=(nblk + 1,),
        in_specs=[pl.BlockSpec((rb * n_chunks, LANES), used_blk), any_spec, any_spec, any_spec],
        out_specs=pl.BlockSpec((rb * n_chunks, LANES),
                               lambda s, *_: (jnp.minimum(s, nblk - 1), 0)),
        scratch_shapes=[pltpu.VMEM((d, d_e), F32), pltpu.VMEM((d, d_e), F32),
                        pltpu.VMEM((d_e, d), F32),
                        pltpu.VMEM((d, d_e), BF16), pltpu.VMEM((d, d_e), BF16),
                        pltpu.VMEM((d_e, d), BF16),
                        pltpu.SemaphoreType.DMA((3,))],
    )
    return pl.pallas_call(
        functools.partial(_moe_kernel, nc=n_chunks, nblk=nblk, layer=layer),
        grid_spec=grid_spec,
        out_shape=jax.ShapeDtypeStruct(xs.shape, U32),
        compiler_params=_params(),
        name="moe",
    )(blk_e, nxt_e, n_used, xs, w_gate, w_up, w_down)


def _combine_kernel(d0a_ref, d0b_ref, dna_ref, dnb_ref, x1_ref, route_ref, g_ref, b_ref, y_hbm,
                    *rest, tm, nc, alpha, mode, tok_w):
    if mode == "attn":
        w_ref, x2_ref, o_ref, gbuf, sem = rest
    elif mode == "gmlp":
        w_ref, ng_ref, nb_ref, x2_ref, o_ref, gbuf, sem = rest
    else:
        x2_ref, gbuf, sem = rest
    i = pl.program_id(0)
    slot = lax.rem(i, 2)
    unroll = 8

    def issue(da_ref, db_ref, sl):
        def group(g, carry):
            for u in range(unroll):
                j = g * unroll + u
                for k, d_ref in enumerate((da_ref, db_ref)):
                    src = y_hbm.at[pl.ds(pl.multiple_of(d_ref[j], nc), nc)]
                    pltpu.make_async_copy(src, gbuf.at[sl, k, pl.ds(j * nc, nc)],
                                          sem.at[sl]).start(priority=k)
            return carry
        lax.fori_loop(0, tm // unroll, group, 0)

    @pl.when(i == 0)
    def _():
        issue(d0a_ref, d0b_ref, 0)

    @pl.when(i + 1 < pl.num_programs(0))
    def _():
        issue(dna_ref, dnb_ref, 1 - slot)

    for k in range(TOP_K):
        pltpu.make_async_copy(y_hbm.at[pl.ds(0, tm * nc)], gbuf.at[slot, k], sem.at[slot]).wait()

    route = route_ref[...]
    y = (_unpack_rows(gbuf, nc, slot, 0) * route[:, 4:5]
         + _unpack_rows(gbuf, nc, slot, 1) * route[:, 5:6])
    x2 = _layer_norm(alpha * x1_ref[...] + y, g_ref[...], b_ref[...])
    x2_ref[...] = x2
    if mode == "attn":
        o_ref[...] = jnp.dot(x2.astype(BF16), w_ref[...],
                             preferred_element_type=F32).astype(o_ref.dtype)
    elif mode == "gmlp":
        _gmlp_epilogue(jnp.dot(x2.astype(BF16), w_ref[...], preferred_element_type=F32),
                       ng_ref, nb_ref, o_ref, tok_w)


def _combine(y, x1, route, dest_a, dest_b, ln_g, ln_b, *, tm, alpha, w_bf=None, gmlp_norm=None):
    t, d = x1.shape
    n_chunks = d // (2 * LANES)
    mode = "none" if w_bf is None else ("attn" if gmlp_norm is None else "gmlp")
    n_steps = t // tm

    def smem(index_map):
        return pl.BlockSpec((tm,), index_map, memory_space=pltpu.SMEM)

    def first(i):
        return (0,)

    def nxt(i):
        return (jnp.minimum(i + 1, n_steps - 1),)

    in_specs = [smem(first), smem(first), smem(nxt), smem(nxt),
                pl.BlockSpec((tm, d), lambda i: (i, 0)),
                pl.BlockSpec((tm, LANES), lambda i: (i, 0)),
                _resident((1, d)), _resident((1, d)),
                pl.BlockSpec(memory_space=pl.ANY)]
    args = [dest_a, dest_b, dest_a, dest_b, x1, route, ln_g.reshape(1, d), ln_b.reshape(1, d), y]
    out_specs = [pl.BlockSpec((tm, d), lambda i: (i, 0))]
    out_shape = [jax.ShapeDtypeStruct((t, d), F32)]
    tok_w = None
    if w_bf is not None:
        n = w_bf.shape[1]
        in_specs.append(_resident((d, n)))
        args.append(w_bf)
        out_specs.append(pl.BlockSpec((tm, n), lambda i: (i, 0)))
        out_shape.append(jax.ShapeDtypeStruct((t, n), BF16))
    if gmlp_norm is not None:
        tok_w = gmlp_norm[0].shape[-1]
        in_specs += [_resident((1, tok_w)), _resident((1, tok_w))]
        args += [gmlp_norm[0].reshape(1, tok_w), gmlp_norm[1].reshape(1, tok_w)]
    return pl.pallas_call(
        functools.partial(_combine_kernel, tm=tm, nc=n_chunks, alpha=alpha, mode=mode,
                          tok_w=tok_w),
        grid=(n_steps,),
        in_specs=in_specs,
        out_specs=out_specs,
        out_shape=out_shape,
        scratch_shapes=[pltpu.VMEM((2, TOP_K, tm * n_chunks, LANES), U32),
                        pltpu.SemaphoreType.DMA((2,))],
        compiler_params=_params(),
        name="combine_" + mode,
    )(*args)


def _split_bf16(w):
    hi = w.astype(BF16)
    lo = (w - hi.astype(F32)).astype(BF16)
    return hi, lo


def kernel(x, mem, a_w_in, a_sinks, b_w_in, b_norm_g, b_norm_b, b_w_spatial, b_b_spatial,
           w_mem_kv, w_out, ln_g, ln_b, w_router_group, b_router_group,
           w_router_expert, b_router_expert, w_gate, w_up, w_down):
    bsz, seq, d = x.shape
    mem_len = mem.shape[1]
    depth = w_out.shape[0]
    t = bsz * seq
    alpha = (2 * depth) ** 0.25
    rb = ROW_BLOCK
    n_rows = -(-(t * TOP_K + N_EXPERTS * (rb - 1)) // rb) * rb
    nblk = n_rows // rb

    def in_proj_weights(i):
        if i % 2 == 0:
            return a_w_in[i // 2].astype(BF16), None
        return b_w_in[i // 2].astype(BF16), (b_norm_g[i // 2], b_norm_b[i // 2])

    x2d = x.reshape(t, d)
    mem2d = mem.reshape(bsz * mem_len, d)
    proj = _proj(x2d, in_proj_weights(0)[0], BF16, tm=_Rows.proj)
    xs = jnp.zeros((n_rows * (d // (2 * LANES)), LANES), U32)
    for i in range(depth):
        j = i // 2
        kvm = _proj(mem2d, w_mem_kv[i].astype(BF16), BF16,
                    tm=min(_Rows.mem_proj, bsz * mem_len))
        if i % 2 == 0:
            mix = _attn_mixer(proj, kvm, a_sinks[j], seq=seq, mem_len=mem_len,
                              tq=_Rows.attn)
        else:
            mix = _gmlp_mixer(proj, kvm, b_w_spatial[j], b_b_spatial[j],
                              seq=seq, mem_len=mem_len, tq=_Rows.gmlp)

        w_r = jnp.concatenate([w_router_group[i], w_router_expert[i]], axis=1)
        w_r = jnp.pad(w_r, ((0, 0), (0, LANES - w_r.shape[1])))
        b_r = jnp.concatenate([b_router_group[i], b_router_expert[i]])
        b_r = jnp.pad(b_r, (0, LANES - b_r.shape[0])).reshape(1, LANES)
        wr_hi_lo = jnp.concatenate(_split_bf16(w_r), axis=1)
        x1, x1p, route, cnt = _post(mix, x2d, w_out[i].astype(BF16), ln_g[i, 0], ln_b[i, 0],
                                    wr_hi_lo, b_r, tm=_Rows.post, sub=_Rows.post_sub,
                                    alpha=alpha)

        counts = cnt[0, :N_EXPERTS].astype(jnp.int32)
        padded = (counts + rb - 1) // rb * rb
        pend = jnp.cumsum(padded)
        pstart = (pend - padded).astype(jnp.int32)
        n_used = (pend[-1] // rb).astype(jnp.int32).reshape(1)
        blk = jnp.arange(nblk, dtype=jnp.int32)

        def expert_of_block(b):
            row = jnp.minimum(b, n_used[0] - 1) * rb
            return jnp.sum((pend[None, :] <= row[:, None]).astype(jnp.int32), axis=1)

        blk_e = expert_of_block(blk)
        e_sel = blk_e[:, None] == jnp.arange(N_EXPERTS, dtype=jnp.int32)
        seg_end = jnp.sum(jnp.where(e_sel, pend, 0), axis=1)
        nxt_blk = seg_end // rb
        nxt_e = jnp.where(nxt_blk < n_used[0], expert_of_block(nxt_blk), blk_e)
        ri = route[:, :4].astype(jnp.int32)
        e_hot = ri[:, :TOP_K, None] == jnp.arange(N_EXPERTS, dtype=jnp.int32)
        dest = jnp.sum(jnp.where(e_hot, pstart, 0), axis=-1) + ri[:, TOP_K:]
        n_chunks = d // (2 * LANES)
        dest_a, dest_b = dest[:, 0] * n_chunks, dest[:, 1] * n_chunks

        xs = _dispatch(x1p, dest_a, dest_b, xs, tm=_Rows.dispatch)
        y = _moe(xs, blk_e, nxt_e.astype(jnp.int32), n_used, w_gate, w_up, w_down,
                 layer=i, rb=rb)
        if i + 1 < depth:
            w_next, gmlp_norm = in_proj_weights(i + 1)
            x2d, proj = _combine(y, x1, route, dest_a, dest_b, ln_g[i, 1], ln_b[i, 1],
                                 tm=_Rows.combine, alpha=alpha, w_bf=w_next,
                                 gmlp_norm=gmlp_norm)
        else:
            (x2d,) = _combine(y, x1, route, dest_a, dest_b, ln_g[i, 1], ln_b[i, 1],
                              tm=_Rows.combine, alpha=alpha)
    return x2d.reshape(bsz, seq, d)
```

```python
import functools
import math

import jax
import jax.numpy as jnp
import numpy as np
from jax import lax
from jax.experimental import pallas as pl
from jax.experimental.pallas import tpu as pltpu

MEM_HEADS = 4
MEM_HEAD_DIM = 128
MEM_W = MEM_HEADS * MEM_HEAD_DIM
HEAD_DIM = 64
N_KV_HEADS = 4
KV_W = N_KV_HEADS * HEAD_DIM
WINDOW = 128
CHUNK = 128
GM_GROUP_DIM = 128
N_GROUPS = 8
EXPERTS_PER_GROUP = 8
N_EXPERTS = N_GROUPS * EXPERTS_PER_GROUP
TOP_K = 2
LN_EPS = 1e-5
NEG_INF = -1e30

LANES = 128
VMEM_LIMIT_BYTES = 56 * 1024 * 1024

ROW_BLOCK = 256


class _Rows:
    proj = 512
    mem_proj = 256
    attn = 256
    gmlp = 512
    post = 512
    post_sub = 256
    dispatch = 1024
    combine = 256


BF16 = jnp.bfloat16
F32 = jnp.float32
U32 = jnp.uint32


def _alibi_slopes(n):
    def pow2(m):
        start = 2.0 ** (-8.0 / m)
        return [start ** (i + 1) for i in range(m)]
    if math.log2(n).is_integer():
        s = pow2(n)
    else:
        c = 2 ** math.floor(math.log2(n))
        s = pow2(c) + pow2(2 * c)[0::2][: n - c]
    return [float(v) for v in np.asarray(s, dtype=np.float32)]


def _params(n_axes=1):
    return pltpu.CompilerParams(dimension_semantics=("arbitrary",) * n_axes,
                                vmem_limit_bytes=VMEM_LIMIT_BYTES)


def _resident(shape):
    nd = len(shape)
    return pl.BlockSpec(shape, lambda *_: (0,) * nd, pipeline_mode=pl.Buffered(1))


def _pack_rows(ref, val, *lead, row0=0):
    rows, d = val.shape
    nw = d // (2 * LANES)
    for c in range(nw):
        lo = val[:, c * LANES:(c + 1) * LANES].astype(BF16).astype(F32)
        hi = val[:, (c + nw) * LANES:(c + nw + 1) * LANES].astype(BF16).astype(F32)
        word = (lax.bitcast_convert_type(lo, U32) >> 16) | lax.bitcast_convert_type(hi, U32)
        ref[tuple(lead) + (pl.ds(row0 * nw + c, rows, stride=nw), slice(None))] = word


def _unpack_rows(ref, nw, *lead):
    rows = ref.shape[-2] // nw
    lo, hi = [], []
    for c in range(nw):
        word = ref[tuple(lead) + (pl.ds(c, rows, stride=nw), slice(None))]
        lo.append(lax.bitcast_convert_type(word << 16, F32))
        hi.append(lax.bitcast_convert_type(word & jnp.uint32(0xFFFF0000), F32))
    return jnp.concatenate(lo + hi, axis=-1)


def _layer_norm(x, g, b):
    mu = jnp.mean(x, axis=-1, keepdims=True)
    xc = x - mu
    var = jnp.mean(xc * xc, axis=-1, keepdims=True)
    return xc * lax.rsqrt(var + LN_EPS) * g + b


def _proj_kernel(x_ref, w_ref, o_ref):
    o_ref[...] = jnp.dot(x_ref[...].astype(BF16), w_ref[...],
                         preferred_element_type=F32).astype(o_ref.dtype)


def _proj(x2d, w_bf, out_dtype, tm):
    m, k = x2d.shape
    n = w_bf.shape[1]
    return pl.pallas_call(
        _proj_kernel,
        grid=(m // tm,),
        in_specs=[pl.BlockSpec((tm, k), lambda i: (i, 0)), _resident((k, n))],
        out_specs=pl.BlockSpec((tm, n), lambda i: (i, 0)),
        out_shape=jax.ShapeDtypeStruct((m, n), out_dtype),
        compiler_params=_params(),
        name="proj",
    )(x2d, w_bf)


def _gelu(x):
    return 0.5 * x * (1.0 + lax.erf(x * (2.0 ** -0.5)))


def _gmlp_epilogue(z, g_ref, b_ref, o_ref, tok_w):
    u = _gelu(z[:, :tok_w])
    v = _gelu(z[:, tok_w:2 * tok_w])
    v = _layer_norm(v, g_ref[...], b_ref[...])
    o_ref[:, :tok_w] = u.astype(o_ref.dtype)
    o_ref[:, tok_w:2 * tok_w] = v.astype(o_ref.dtype)
    o_ref[:, 2 * tok_w:] = z[:, 2 * tok_w:].astype(o_ref.dtype)


def _memory_attention(qm, kvm):
    outs = []
    for h in range(MEM_HEADS):
        q = qm[:, h * MEM_HEAD_DIM:(h + 1) * MEM_HEAD_DIM]
        k = kvm[:, h * MEM_HEAD_DIM:(h + 1) * MEM_HEAD_DIM]
        v = kvm[:, MEM_W + h * MEM_HEAD_DIM:MEM_W + (h + 1) * MEM_HEAD_DIM]
        s = lax.dot_general(q, k, (((1,), (1,)), ((), ())),
                            preferred_element_type=F32) * (MEM_HEAD_DIM ** -0.5)
        m = jnp.max(s, axis=-1, keepdims=True)
        p = jnp.exp(s - m)
        p = p * (1.0 / jnp.sum(p, axis=-1, keepdims=True))
        outs.append(jnp.dot(p.astype(BF16), v, preferred_element_type=F32))
    return jnp.concatenate(outs, axis=-1)


def _attn_kernel(sinks_ref, q_ref, kv_ref, kvp_ref, qm_ref, kvm_ref, o_ref, *,
                 tq, seq, slopes, tok_w):
    i = pl.program_id(0)
    n_sub = tq // WINDOW
    gqa = len(slopes) // N_KV_HEADS
    qi = lax.broadcasted_iota(jnp.int32, (WINDOW, WINDOW), 0)
    c = lax.broadcasted_iota(jnp.int32, (WINDOW, WINDOW), 1)
    own = c <= qi
    dist_f = jnp.where(own, qi - c, WINDOW + qi - c).astype(F32)
    nt = (((1,), (1,)), ((), ()))
    for sb in range(n_sub):
        r0 = sb * WINDOW
        if sb == 0:
            kv_prev = kvp_ref[...]
            reach = jnp.where(((i * tq) % seq) == 0, 0, WINDOW)
            valid = c <= qi + reach
        else:
            kv_prev = kv_ref[r0 - WINDOW:r0, :]
            valid = None
        kv_cur = kv_ref[r0:r0 + WINDOW, :]
        for kh in range(N_KV_HEADS):
            ks = slice(kh * HEAD_DIM, (kh + 1) * HEAD_DIM)
            vs = slice(KV_W + kh * HEAD_DIM, KV_W + (kh + 1) * HEAD_DIM)
            heads = range(kh * gqa, (kh + 1) * gqa)
            q = jnp.concatenate(
                [q_ref[r0:r0 + WINDOW, h * HEAD_DIM:(h + 1) * HEAD_DIM] for h in heads], axis=0)
            s_own = lax.dot_general(q, kv_cur[:, ks], nt, preferred_element_type=F32)
            s_prev = lax.dot_general(q, kv_prev[:, ks], nt, preferred_element_type=F32)
            p_own, p_prev = [], []
            for g, h in enumerate(heads):
                rows = slice(g * WINDOW, (g + 1) * WINDOW)
                logits = (jnp.where(own, s_own[rows], s_prev[rows]) * (HEAD_DIM ** -0.5)
                          - slopes[h] * dist_f)
                if valid is not None:
                    logits = jnp.where(valid, logits, NEG_INF)
                sink = sinks_ref[h]
                m = jnp.maximum(jnp.max(logits, axis=-1, keepdims=True), sink)
                p = jnp.exp(logits - m)
                probs = p * (1.0 / (jnp.sum(p, axis=-1, keepdims=True) + jnp.exp(sink - m)))
                p_own.append(jnp.where(own, probs, 0.0).astype(BF16))
                p_prev.append(jnp.where(own, 0.0, probs).astype(BF16))
            out = (jnp.dot(jnp.concatenate(p_own, axis=0), kv_cur[:, vs],
                           preferred_element_type=F32)
                   + jnp.dot(jnp.concatenate(p_prev, axis=0), kv_prev[:, vs],
                             preferred_element_type=F32))
            o_ref[r0:r0 + WINDOW, kh * gqa * HEAD_DIM:(kh + 1) * gqa * HEAD_DIM] = jnp.concatenate(
                [out[g * WINDOW:(g + 1) * WINDOW] for g in range(gqa)], axis=1).astype(o_ref.dtype)
    o_ref[:, tok_w:] = _memory_attention(qm_ref[...], kvm_ref[...]).astype(o_ref.dtype)


def _attn_mixer(proj, kvm, sinks, *, seq, mem_len, tq):
    t = proj.shape[0]
    tok_w = proj.shape[1] - 2 * KV_W - MEM_W
    n_heads = tok_w // HEAD_DIM
    assert tok_w % (2 * KV_W) == 0 and tok_w % MEM_W == 0 and seq % tq == 0
    kv_col = tok_w // (2 * KV_W)
    qm_col = (tok_w + 2 * KV_W) // MEM_W
    sub = tq // WINDOW
    kern = functools.partial(_attn_kernel, tq=tq, seq=seq, slopes=_alibi_slopes(n_heads),
                             tok_w=tok_w)
    return pl.pallas_call(
        kern,
        grid=(t // tq,),
        in_specs=[
            pl.BlockSpec(memory_space=pltpu.SMEM),
            pl.BlockSpec((tq, tok_w), lambda i: (i, 0)),
            pl.BlockSpec((tq, 2 * KV_W), lambda i: (i, kv_col)),
            pl.BlockSpec((WINDOW, 2 * KV_W), lambda i: (jnp.maximum(i * sub - 1, 0), kv_col)),
            pl.BlockSpec((tq, MEM_W), lambda i: (i, qm_col)),
            pl.BlockSpec((mem_len, 2 * MEM_W), lambda i: ((i * tq) // seq, 0)),
        ],
        out_specs=pl.BlockSpec((tq, tok_w + MEM_W), lambda i: (i, 0)),
        out_shape=jax.ShapeDtypeStruct((t, tok_w + MEM_W), BF16),
        compiler_params=_params(),
        name="attn_mixer",
    )(sinks, proj, proj, proj, proj, kvm)


def _gmlp_kernel(u_ref, v_ref, qm_ref, kvm_ref, ws_ref, bs_ref, o_ref, *, tq, tok_w):
    n_sub = tq // CHUNK
    n_grp = tok_w // GM_GROUP_DIM
    row = lax.broadcasted_iota(jnp.int32, (CHUNK, CHUNK), 0)
    col = lax.broadcasted_iota(jnp.int32, (CHUNK, CHUNK), 1)
    lower = row >= col
    for g in range(n_grp):
        w = jnp.where(lower, ws_ref[g], 0.0).astype(BF16)
        bias = bs_ref[:, g:g + 1]
        c0 = g * GM_GROUP_DIM
        for sb in range(n_sub):
            r0 = sb * CHUNK
            sv = jnp.dot(w, v_ref[r0:r0 + CHUNK, c0:c0 + GM_GROUP_DIM],
                         preferred_element_type=F32) + bias
            u = u_ref[r0:r0 + CHUNK, c0:c0 + GM_GROUP_DIM].astype(F32)
            o_ref[r0:r0 + CHUNK, c0:c0 + GM_GROUP_DIM] = (u * sv).astype(o_ref.dtype)
    o_ref[:, tok_w:] = _memory_attention(qm_ref[...], kvm_ref[...]).astype(o_ref.dtype)


def _gmlp_mixer(proj, kvm, w_s, b_s, *, seq, mem_len, tq):
    t = proj.shape[0]
    tok_w = (proj.shape[1] - MEM_W) // 2
    n_grp = tok_w // GM_GROUP_DIM
    assert (2 * tok_w) % MEM_W == 0 and seq % tq == 0
    kern = functools.partial(_gmlp_kernel, tq=tq, tok_w=tok_w)
    return pl.pallas_call(
        kern,
        grid=(t // tq,),
        in_specs=[
            pl.BlockSpec((tq, tok_w), lambda i: (i, 0)),
            pl.BlockSpec((tq, tok_w), lambda i: (i, 1)),
            pl.BlockSpec((tq, MEM_W), lambda i: (i, (2 * tok_w) // MEM_W)),
            pl.BlockSpec((mem_len, 2 * MEM_W), lambda i: ((i * tq) // seq, 0)),
            _resident((n_grp, CHUNK, CHUNK)),
            _resident((CHUNK, n_grp)),
        ],
        out_specs=pl.BlockSpec((tq, tok_w + MEM_W), lambda i: (i, 0)),
        out_shape=jax.ShapeDtypeStruct((t, tok_w + MEM_W), BF16),
        compiler_params=_params(),
        name="gmlp_mixer",
    )(proj, proj, proj, kvm, w_s, b_s.T)


def _post_kernel(mix_ref, x_ref, wo_ref, g_ref, b_ref, wr_ref, br_ref,
                 x1_ref, xp_ref, route_ref, cnt_ref, carry_ref, *, tm, sub, alpha):
    i = pl.program_id(0)

    @pl.when(i == 0)
    def _():
        carry_ref[...] = jnp.zeros_like(carry_ref)

    carry = carry_ref[0:1, :]
    for r0 in range(0, tm, sub):
        carry = _post_rows(r0, sub, carry, mix_ref, x_ref, wo_ref, g_ref, b_ref, wr_ref,
                           br_ref, x1_ref, xp_ref, route_ref, alpha)
    carry_ref[...] = jnp.broadcast_to(carry, carry_ref.shape)
    cnt_ref[...] = jnp.broadcast_to(carry, cnt_ref.shape)


def _post_rows(r0, tm, carry, mix_ref, x_ref, wo_ref, g_ref, b_ref, wr_ref, br_ref,
               x1_ref, xp_ref, route_ref, alpha):
    rows = slice(r0, r0 + tm)
    y = jnp.dot(mix_ref[rows, :], wo_ref[...], preferred_element_type=F32)
    x1 = _layer_norm(alpha * x_ref[rows, :] + y, g_ref[...], b_ref[...])
    x1_ref[rows, :] = x1
    _pack_rows(xp_ref, x1, row0=r0)

    xh = x1.astype(BF16)
    xl = (x1 - xh.astype(F32)).astype(BF16)
    hh_hl = jnp.dot(xh, wr_ref[...], preferred_element_type=F32)
    lh = jnp.dot(xl, wr_ref[:, :LANES], preferred_element_type=F32)
    logits = (hh_hl[:, :LANES] + (hh_hl[:, LANES:] + lh)) + br_ref[...]

    lane = lax.broadcasted_iota(jnp.int32, (tm, LANES), 1)
    lane_f = lane.astype(F32)
    big = float(LANES)
    is_grp = lane < N_GROUPS
    gl = jnp.where(is_grp, logits, NEG_INF)
    gmax = jnp.max(gl, axis=-1, keepdims=True)
    g_sel = jnp.min(jnp.where(gl == gmax, lane_f, big), axis=-1, keepdims=True)
    p_grp = 1.0 / jnp.sum(jnp.where(is_grp, jnp.exp(gl - gmax), 0.0), axis=-1, keepdims=True)

    e_lo = N_GROUPS + g_sel * EXPERTS_PER_GROUP
    in_grp = (lane_f >= e_lo) & (lane_f < e_lo + EXPERTS_PER_GROUP)
    el = jnp.where(in_grp, logits, NEG_INF)
    m1 = jnp.max(el, axis=-1, keepdims=True)
    i1 = jnp.min(jnp.where(el == m1, lane_f, big), axis=-1, keepdims=True)
    el2 = jnp.where(lane_f == i1, NEG_INF, el)
    m2 = jnp.max(el2, axis=-1, keepdims=True)
    i2 = jnp.min(jnp.where(el2 == m2, lane_f, big), axis=-1, keepdims=True)
    z = jnp.sum(jnp.where(in_grp, jnp.exp(el - m1), 0.0), axis=-1, keepdims=True)
    tp1 = 1.0 / z
    tp2 = jnp.exp(m2 - m1) / z
    gate1 = p_grp * tp1 / (tp1 + tp2)
    gate2 = p_grp * tp2 / (tp1 + tp2)
    e1 = i1 - N_GROUPS
    e2 = i2 - N_GROUPS

    oh1 = lane_f == e1
    oh2 = lane_f == e2
    c = jnp.where(oh1 | oh2, 1.0, 0.0)
    r = lax.broadcasted_iota(jnp.int32, (tm, tm), 0)
    s = lax.broadcasted_iota(jnp.int32, (tm, tm), 1)
    tri = jnp.where(r > s, 1.0, 0.0).astype(BF16)
    prefix = jnp.dot(tri, c.astype(BF16), preferred_element_type=F32) + carry
    rank1 = jnp.sum(jnp.where(oh1, prefix, 0.0), axis=-1, keepdims=True)
    rank2 = jnp.sum(jnp.where(oh2, prefix, 0.0), axis=-1, keepdims=True)

    route = jnp.where(lane == 0, e1, 0.0)
    route = jnp.where(lane == 1, e2, route)
    route = jnp.where(lane == 2, rank1, route)
    route = jnp.where(lane == 3, rank2, route)
    route = jnp.where(lane == 4, gate1, route)
    route = jnp.where(lane == 5, gate2, route)
    route_ref[rows, :] = route
    return carry + jnp.sum(c, axis=0, keepdims=True)


def _post(mix, x2d, wo_bf, ln_g, ln_b, wr_hi_lo, b_r, *, tm, sub, alpha):
    t, d = x2d.shape
    kern = functools.partial(_post_kernel, tm=tm, sub=sub, alpha=alpha)
    return pl.pallas_call(
        kern,
        grid=(t // tm,),
        in_specs=[
            pl.BlockSpec((tm, d), lambda i: (i, 0)),
            pl.BlockSpec((tm, d), lambda i: (i, 0)),
            _resident((d, d)),
            _resident((1, d)), _resident((1, d)),
            _resident((d, 2 * LANES)), _resident((1, LANES)),
        ],
        out_specs=[
            pl.BlockSpec((tm, d), lambda i: (i, 0)),
            pl.BlockSpec((tm * (d // (2 * LANES)), LANES), lambda i: (i, 0)),
            pl.BlockSpec((tm, LANES), lambda i: (i, 0)),
            pl.BlockSpec((8, LANES), lambda i: (0, 0)),
        ],
        out_shape=[
            jax.ShapeDtypeStruct((t, d), F32),
            jax.ShapeDtypeStruct((t * (d // (2 * LANES)), LANES), U32),
            jax.ShapeDtypeStruct((t, LANES), F32),
            jax.ShapeDtypeStruct((8, LANES), F32),
        ],
        scratch_shapes=[pltpu.VMEM((8, LANES), F32)],
        compiler_params=_params(),
        name="post",
    )(mix, x2d, wo_bf, ln_g.reshape(1, d), ln_b.reshape(1, d), wr_hi_lo, b_r)


def _dispatch_kernel(da_ref, db_ref, xp_ref, stale_hbm, xs_hbm, sem, *, tm, nc):
    del stale_hbm
    unroll = 8

    def group(g, carry):
        for u in range(unroll):
            j = g * unroll + u
            src = xp_ref.at[pl.ds(j * nc, nc)]
            for k, d_ref in enumerate((da_ref, db_ref)):
                dst = xs_hbm.at[pl.ds(pl.multiple_of(d_ref[j], nc), nc)]
                pltpu.make_async_copy(src, dst, sem).start(priority=k)
        return carry
    lax.fori_loop(0, tm // unroll, group, 0)
    for _ in range(TOP_K):
        pltpu.make_async_copy(xp_ref, xs_hbm.at[pl.ds(0, tm * nc)], sem).wait()


def _dispatch(xp, dest_a, dest_b, stale, *, tm):
    nc = xp.shape[0] // dest_a.shape[0]
    t = dest_a.shape[0]
    smem = pl.BlockSpec((tm,), lambda i: (i,), memory_space=pltpu.SMEM)
    any_spec = pl.BlockSpec(memory_space=pl.ANY)
    return pl.pallas_call(
        functools.partial(_dispatch_kernel, tm=tm, nc=nc),
        grid=(t // tm,),
        in_specs=[smem, smem, pl.BlockSpec((tm * nc, LANES), lambda i: (i, 0)), any_spec],
        out_specs=any_spec,
        out_shape=jax.ShapeDtypeStruct(stale.shape, stale.dtype),
        scratch_shapes=[pltpu.SemaphoreType.DMA(())],
        input_output_aliases={3: 0},
        compiler_params=_params(),
        name="dispatch",
    )(dest_a, dest_b, xp, stale)


def _moe_kernel(blk_e_ref, nxt_e_ref, n_used_ref, xs_ref, wg_hbm, wu_hbm, wd_hbm, y_ref,
                wg_st, wu_st, wd_st, wg_bf, wu_bf, wd_bf, sem_w, *, nc, nblk, layer):
    s = pl.program_id(0)
    n_used = n_used_ref[0]

    def weight_copy(hbm, stage, sem_i, e):
        return pltpu.make_async_copy(hbm.at[layer, e], stage, sem_w.at[sem_i])

    def round_and_refill(hbm, stage, bf, sem_i, e_next):
        weight_copy(hbm, stage, sem_i, e_next).wait()
        bf[...] = stage[...].astype(BF16)
        weight_copy(hbm, stage, sem_i, e_next).start()

    @pl.when(s == 0)
    def _():
        for sem_i, (hbm, stage) in enumerate(((wg_hbm, wg_st), (wu_hbm, wu_st), (wd_hbm, wd_st))):
            weight_copy(hbm, stage, sem_i, blk_e_ref[0]).start()

    b_cur = jnp.minimum(s, nblk - 1)
    e_cur = blk_e_ref[b_cur]
    new_expert = (s == 0) | (e_cur != blk_e_ref[jnp.maximum(b_cur - 1, 0)])

    def expert_block(first):
        e_next = nxt_e_ref[b_cur]
        xb = _unpack_rows(xs_ref, nc).astype(BF16)
        if first:
            round_and_refill(wg_hbm, wg_st, wg_bf, 0, e_next)
        hg = jnp.dot(xb, wg_bf[...], preferred_element_type=F32)
        if first:
            round_and_refill(wu_hbm, wu_st, wu_bf, 1, e_next)
        hu = jnp.dot(xb, wu_bf[...], preferred_element_type=F32)
        if first:
            round_and_refill(wd_hbm, wd_st, wd_bf, 2, e_next)
        h = (jax.nn.silu(hg) * hu).astype(BF16)
        _pack_rows(y_ref, jnp.dot(h, wd_bf[...], preferred_element_type=F32))

    @pl.when((s < n_used) & new_expert)
    def _():
        expert_block(True)

    @pl.when((s < n_used) & jnp.logical_not(new_expert))
    def _():
        expert_block(False)

    @pl.when((s >= n_used) & (s < nblk))
    def _():
        y_ref[...] = jnp.zeros_like(y_ref)

    @pl.when(s == n_used)
    def _():
        for sem_i, (hbm, stage) in enumerate(((wg_hbm, wg_st), (wu_hbm, wu_st), (wd_hbm, wd_st))):
            weight_copy(hbm, stage, sem_i, 0).wait()


def _moe(xs, blk_e, nxt_e, n_used, w_gate, w_up, w_down, *, layer, rb):
    d_e, d = w_down.shape[-2:]
    n_chunks = d // (2 * LANES)
    nblk = blk_e.shape[0]

    def used_blk(s, blk_e_ref, nxt_e_ref, n_used_ref):
        return (jnp.minimum(s, jnp.maximum(n_used_ref[0] - 1, 0)), 0)

    any_spec = pl.BlockSpec(memory_space=pl.ANY)
    grid_spec = pltpu.PrefetchScalarGridSpec(
        num_scalar_prefetch=3,
        grid=(nblk + 1,),
        in_specs=[pl.BlockSpec((rb * n_chunks, LANES), used_blk), any_spec, any_spec, any_spec],
        out_specs=pl.BlockSpec((rb * n_chunks, LANES),
                               lambda s, *_: (jnp.minimum(s, nblk - 1), 0)),
        scratch_shapes=[pltpu.VMEM((d, d_e), F32), pltpu.VMEM((d, d_e), F32),
                        pltpu.VMEM((d_e, d), F32),
                        pltpu.VMEM((d, d_e), BF16), pltpu.VMEM((d, d_e), BF16),
                        pltpu.VMEM((d_e, d), BF16),
                        pltpu.SemaphoreType.DMA((3,))],
    )
    return pl.pallas_call(
        functools.partial(_moe_kernel, nc=n_chunks, nblk=nblk, layer=layer),
        grid_spec=grid_spec,
        out_shape=jax.ShapeDtypeStruct(xs.shape, U32),
        compiler_params=_params(),
        name="moe",
    )(blk_e, nxt_e, n_used, xs, w_gate, w_up, w_down)


def _combine_kernel(d0a_ref, d0b_ref, dna_ref, dnb_ref, x1_ref, route_ref, g_ref, b_ref, y_hbm,
                    *rest, tm, nc, alpha, mode, tok_w):
    if mode == "attn":
        w_ref, x2_ref, o_ref, gbuf, sem = rest
    elif mode == "gmlp":
        w_ref, ng_ref, nb_ref, x2_ref, o_ref, gbuf, sem = rest
    else:
        x2_ref, gbuf, sem = rest
    i = pl.program_id(0)
    slot = lax.rem(i, 2)
    unroll = 8

    def issue(da_ref, db_ref, sl):
        def group(g, carry):
            for u in range(unroll):
                j = g * unroll + u
                for k, d_ref in enumerate((da_ref, db_ref)):
                    src = y_hbm.at[pl.ds(pl.multiple_of(d_ref[j], nc), nc)]
                    pltpu.make_async_copy(src, gbuf.at[sl, k, pl.ds(j * nc, nc)],
                                          sem.at[sl]).start(priority=k)
            return carry
        lax.fori_loop(0, tm // unroll, group, 0)

    @pl.when(i == 0)
    def _():
        issue(d0a_ref, d0b_ref, 0)

    @pl.when(i + 1 < pl.num_programs(0))
    def _():
        issue(dna_ref, dnb_ref, 1 - slot)

    for k in range(TOP_K):
        pltpu.make_async_copy(y_hbm.at[pl.ds(0, tm * nc)], gbuf.at[slot, k], sem.at[slot]).wait()

    route = route_ref[...]
    y = (_unpack_rows(gbuf, nc, slot, 0) * route[:, 4:5]
         + _unpack_rows(gbuf, nc, slot, 1) * route[:, 5:6])
    x2 = _layer_norm(alpha * x1_ref[...] + y, g_ref[...], b_ref[...])
    x2_ref[...] = x2
    if mode == "attn":
        o_ref[...] = jnp.dot(x2.astype(BF16), w_ref[...],
                             preferred_element_type=F32).astype(o_ref.dtype)
    elif mode == "gmlp":
        _gmlp_epilogue(jnp.dot(x2.astype(BF16), w_ref[...], preferred_element_type=F32),
                       ng_ref, nb_ref, o_ref, tok_w)


def _combine(y, x1, route, dest_a, dest_b, ln_g, ln_b, *, tm, alpha, w_bf=None, gmlp_norm=None):
    t, d = x1.shape
    n_chunks = d // (2 * LANES)
    mode = "none" if w_bf is None else ("attn" if gmlp_norm is None else "gmlp")
    n_steps = t // tm

    def smem(index_map):
        return pl.BlockSpec((tm,), index_map, memory_space=pltpu.SMEM)

    def first(i):
        return (0,)

    def nxt(i):
        return (jnp.minimum(i + 1, n_steps - 1),)

    in_specs = [smem(first), smem(first), smem(nxt), smem(nxt),
                pl.BlockSpec((tm, d), lambda i: (i, 0)),
                pl.BlockSpec((tm, LANES), lambda i: (i, 0)),
                _resident((1, d)), _resident((1, d)),
                pl.BlockSpec(memory_space=pl.ANY)]
    args = [dest_a, dest_b, dest_a, dest_b, x1, route, ln_g.reshape(1, d), ln_b.reshape(1, d), y]
    out_specs = [pl.BlockSpec((tm, d), lambda i: (i, 0))]
    out_shape = [jax.ShapeDtypeStruct((t, d), F32)]
    tok_w = None
    if w_bf is not None:
        n = w_bf.shape[1]
        in_specs.append(_resident((d, n)))
        args.append(w_bf)
        out_specs.append(pl.BlockSpec((tm, n), lambda i: (i, 0)))
        out_shape.append(jax.ShapeDtypeStruct((t, n), BF16))
    if gmlp_norm is not None:
        tok_w = gmlp_norm[0].shape[-1]
        in_specs += [_resident((1, tok_w)), _resident((1, tok_w))]
        args += [gmlp_norm[0].reshape(1, tok_w), gmlp_norm[1].reshape(1, tok_w)]
    return pl.pallas_call(
        functools.partial(_combine_kernel, tm=tm, nc=n_chunks, alpha=alpha, mode=mode,
                          tok_w=tok_w),
        grid=(n_steps,),
        in_specs=in_specs,
        out_specs=out_specs,
        out_shape=out_shape,
        scratch_shapes=[pltpu.VMEM((2, TOP_K, tm * n_chunks, LANES), U32),
                        pltpu.SemaphoreType.DMA((2,))],
        compiler_params=_params(),
        name="combine_" + mode,
    )(*args)


def _split_bf16(w):
    hi = w.astype(BF16)
    lo = (w - hi.astype(F32)).astype(BF16)
    return hi, lo


def kernel(x, mem, a_w_in, a_sinks, b_w_in, b_norm_g, b_norm_b, b_w_spatial, b_b_spatial,
           w_mem_kv, w_out, ln_g, ln_b, w_router_group, b_router_group,
           w_router_expert, b_router_expert, w_gate, w_up, w_down):
    bsz, seq, d = x.shape
    mem_len = mem.shape[1]
    depth = w_out.shape[0]
    t = bsz * seq
    alpha = (2 * depth) ** 0.25
    rb = ROW_BLOCK
    n_rows = -(-(t * TOP_K + N_EXPERTS * (rb - 1)) // rb) * rb
    nblk = n_rows // rb

    def in_proj_weights(i):
        if i % 2 == 0:
            return a_w_in[i // 2].astype(BF16), None
        return b_w_in[i // 2].astype(BF16), (b_norm_g[i // 2], b_norm_b[i // 2])

    x2d = x.reshape(t, d)
    mem2d = mem.reshape(bsz * mem_len, d)
    proj = _proj(x2d, in_proj_weights(0)[0], BF16, tm=_Rows.proj)
    xs = jnp.zeros((n_rows * (d // (2 * LANES)), LANES), U32)
    for i in range(depth):
        j = i // 2
        kvm = _proj(mem2d, w_mem_kv[i].astype(BF16), BF16,
                    tm=min(_Rows.mem_proj, bsz * mem_len))
        if i % 2 == 0:
            mix = _attn_mixer(proj, kvm, a_sinks[j], seq=seq, mem_len=mem_len,
                              tq=_Rows.attn)
        else:
            mix = _gmlp_mixer(proj, kvm, b_w_spatial[j], b_b_spatial[j],
                              seq=seq, mem_len=mem_len, tq=_Rows.gmlp)

        w_r = jnp.concatenate([w_router_group[i], w_router_expert[i]], axis=1)
        w_r = jnp.pad(w_r, ((0, 0), (0, LANES - w_r.shape[1])))
        b_r = jnp.concatenate([b_router_group[i], b_router_expert[i]])
        b_r = jnp.pad(b_r, (0, LANES - b_r.shape[0])).reshape(1, LANES)
        wr_hi_lo = jnp.concatenate(_split_bf16(w_r), axis=1)
        x1, x1p, route, cnt = _post(mix, x2d, w_out[i].astype(BF16), ln_g[i, 0], ln_b[i, 0],
                                    wr_hi_lo, b_r, tm=_Rows.post, sub=_Rows.post_sub,
                                    alpha=alpha)

        counts = cnt[0, :N_EXPERTS].astype(jnp.int32)
        padded = (counts + rb - 1) // rb * rb
        pend = jnp.cumsum(padded)
        pstart = (pend - padded).astype(jnp.int32)
        n_used = (pend[-1] // rb).astype(jnp.int32).reshape(1)
        blk = jnp.arange(nblk, dtype=jnp.int32)

        def expert_of_block(b):
            row = jnp.minimum(b, n_used[0] - 1) * rb
            return jnp.sum((pend[None, :] <= row[:, None]).astype(jnp.int32), axis=1)

        blk_e = expert_of_block(blk)
        e_sel = blk_e[:, None] == jnp.arange(N_EXPERTS, dtype=jnp.int32)
        seg_end = jnp.sum(jnp.where(e_sel, pend, 0), axis=1)
        nxt_blk = seg_end // rb
        nxt_e = jnp.where(nxt_blk < n_used[0], expert_of_block(nxt_blk), blk_e)
        ri = route[:, :4].astype(jnp.int32)
        e_hot = ri[:, :TOP_K, None] == jnp.arange(N_EXPERTS, dtype=jnp.int32)
        dest = jnp.sum(jnp.where(e_hot, pstart, 0), axis=-1) + ri[:, TOP_K:]
        n_chunks = d // (2 * LANES)
        dest_a, dest_b = dest[:, 0] * n_chunks, dest[:, 1] * n_chunks

        xs = _dispatch(x1p, dest_a, dest_b, xs, tm=_Rows.dispatch)
        y = _moe(xs, blk_e, nxt_e.astype(jnp.int32), n_used, w_gate, w_up, w_down,
                 layer=i, rb=rb)
        if i + 1 < depth:
            w_next, gmlp_norm = in_proj_weights(i + 1)
            x2d, proj = _combine(y, x1, route, dest_a, dest_b, ln_g[i, 1], ln_b[i, 1],
                                 tm=_Rows.combine, alpha=alpha, w_bf=w_next,
                                 gmlp_norm=gmlp_norm)
        else:
            (x2d,) = _combine(y, x1, route, dest_a, dest_b, ln_g[i, 1], ln_b[i, 1],
                              tm=_Rows.combine, alpha=alpha)
    return x2d.reshape(bsz, seq, d)
```
